```python
import math
import jax, jax.numpy as jnp
from jax import lax
import numpy as np

D_MODEL = 1024
BATCH = 8
SEQ = 4096
DEPTH = 4

GRID_W = 64
CTX_LEN = 256
N_MIXERS = 4
N_CONV = len(range(0, DEPTH, N_MIXERS))
N_DIFF = len(range(1, DEPTH, N_MIXERS))
N_CHUNK = len(range(2, DEPTH, N_MIXERS))
N_SWA = len(range(3, DEPTH, N_MIXERS))
N_DENSE = len(range(0, DEPTH, 2))
N_MOE = len(range(1, DEPTH, 2))

HEAD_DIM = 64
ROPE_HALF = HEAD_DIM // 2
ROPE_BASE = 10000.0
Q_BLOCK = 128
DIFF_HEADS = D_MODEL // (2 * HEAD_DIM)
SWA_Q_HEADS = D_MODEL // HEAD_DIM
SWA_KV_HEADS = 4
SWA_GROUP = SWA_Q_HEADS // SWA_KV_HEADS
SWA_WINDOW = 128
CHUNK = 128
CM_WIDTH = 2 * D_MODEL
CM_GROUPS = 8
FFN_DIM = 2816
N_EXPERTS = 8
TOP_K = 2
MOE_BLOCK = 256
EPS = 1e-6

kernel_name = "hybrid_interleaved_diffusion_backbone"


def rms_norm(x, g):
    xf = x.astype(jnp.float32)
    y = xf * lax.rsqrt(jnp.mean(xf * xf, axis=-1, keepdims=True) + EPS)
    return (y * g.astype(jnp.float32)).astype(x.dtype)


def adaln(cvec, w, b):
    mod = jax.nn.silu(cvec) @ w + b
    return jnp.split(mod[..., None, :], 6, axis=-1)


def modulate(xn, shift, scale):
    return xn * (1 + scale) + shift


def axial_angles(seq_len):
    rows = seq_len // GRID_W
    row = jnp.repeat(jnp.arange(rows), GRID_W).astype(jnp.float32)
    col = jnp.tile(jnp.arange(GRID_W), rows).astype(jnp.float32)
    n_freq = ROPE_HALF // 2
    inv = ROPE_BASE ** (-jnp.arange(n_freq, dtype=jnp.float32) / n_freq)
    return row[:, None] * inv, col[:, None] * inv


def rope_1d(x, ang):
    cos = jnp.cos(ang)[None, :, None, :]
    sin = jnp.sin(ang)[None, :, None, :]
    x1, x2 = jnp.split(x.astype(jnp.float32), 2, axis=-1)
    return jnp.concatenate([x1 * cos - x2 * sin, x1 * sin + x2 * cos], axis=-1).astype(x.dtype)


def rope_2d(x, angles):
    ang_row, ang_col = angles
    return jnp.concatenate([rope_1d(x[..., :ROPE_HALF], ang_row),
                            rope_1d(x[..., ROPE_HALF:], ang_col)], axis=-1)


def short_conv_seq(a, w_in, w_conv, w_out):
    bg, cg, xv = jnp.split(a @ w_in, 3, axis=-1)
    y = cg * xv
    yp = jnp.pad(y, ((0, 0), (1, 1), (0, 0)))
    conv = w_conv[0] * yp[:, :-2] + w_conv[1] * yp[:, 1:-1] + w_conv[2] * yp[:, 2:]
    return (bg * conv) @ w_out


def short_conv_mixer(a_lat, a_ctx, w_in, w_conv, w_out, ctx_out):
    y_lat = short_conv_seq(a_lat, w_in, w_conv, w_out)
    y_ctx = short_conv_seq(a_ctx, w_in, w_conv, w_out) if ctx_out else None
    return y_lat, y_ctx


def diff_core(q, k, v, lam):
    logits = jnp.einsum('bqhd,bkhd->bhqk', q, k, preferred_element_type=jnp.float32) * (HEAD_DIM ** -0.5)
    p = jax.nn.softmax(logits, axis=-1)
    b_, h2, nq, nk = p.shape
    p = p.reshape(b_, h2 // 2, 2, nq, nk)
    w = p[:, :, 0] - lam * p[:, :, 1]
    return jnp.einsum('bhqk,bkhe->bqhe', w.astype(v.dtype), v)


def diff_attn_mixer(a_lat, a_ctx, w_qkv, w_out, q_g, k_g, lam_p, sub_g, layer, angles, ctx_out):
    lam_init = 0.8 - 0.6 * math.exp(-0.3 * layer)
    lp = lam_p.astype(jnp.float32)
    lam = jnp.exp(jnp.sum(lp[0] * lp[1])) - jnp.exp(jnp.sum(lp[2] * lp[3])) + lam_init

    def proj(a):
        b_, s_, _ = a.shape
        q, k, v = jnp.split(a @ w_qkv, 3, axis=-1)
        q = rms_norm(q.reshape(b_, s_, 2 * DIFF_HEADS, HEAD_DIM), q_g)
        k = rms_norm(k.reshape(b_, s_, 2 * DIFF_HEADS, HEAD_DIM), k_g)
        return q, k, v.reshape(b_, s_, DIFF_HEADS, 2 * HEAD_DIM)

    def finish(o):
        o = rms_norm(o, sub_g) * (1 - lam_init)
        return o.reshape(o.shape[0], o.shape[1], -1) @ w_out

    q_l, k_l, v_l = proj(a_lat)
    q_l = rope_2d(q_l, angles)
    k_l = rope_2d(k_l, angles)
    q_c, k_c, v_c = proj(a_ctx)
    k_all = jnp.concatenate([k_c, k_l], axis=1)
    v_all = jnp.concatenate([v_c, v_l], axis=1)
    b_, s_ = a_lat.shape[:2]
    nb = s_ // Q_BLOCK
    q_blocks = q_l.reshape(b_, nb, Q_BLOCK, 2 * DIFF_HEADS, HEAD_DIM).swapaxes(0, 1)
    o = lax.map(lambda qb: diff_core(qb, k_all, v_all, lam), q_blocks)
    o = o.swapaxes(0, 1).reshape(b_, s_, DIFF_HEADS, 2 * HEAD_DIM)
    y_lat = finish(o)
    y_ctx = finish(diff_core(q_c, k_c, v_c, lam)) if ctx_out else None
    return y_lat, y_ctx


def chunk_mlp_seq(a, w_in, b_in, v_g, w_s, b_s, w_out):
    b_, s_, _ = a.shape
    z = jax.nn.gelu(a @ w_in + b_in)
    u, v = jnp.split(z, 2, axis=-1)
    v = rms_norm(v, v_g).reshape(b_, s_ // CHUNK, CHUNK, CM_GROUPS, CM_WIDTH // CM_GROUPS)
    sv = jnp.einsum('gpq,bnqgc->bnpgc', w_s, v) + b_s.T[:, :, None]
    return (u * sv.reshape(b_, s_, CM_WIDTH)) @ w_out


def chunk_mlp_mixer(a_lat, a_ctx, w_in, b_in, v_g, w_s, b_s, w_out, ctx_out):
    y_lat = chunk_mlp_seq(a_lat, w_in, b_in, v_g, w_s, b_s, w_out)
    y_ctx = chunk_mlp_seq(a_ctx, w_in, b_in, v_g, w_s, b_s, w_out) if ctx_out else None
    return y_lat, y_ctx


def sink_attend(q, ks, vs, masks, sink):
    logits = []
    for k, m in zip(ks, masks):
        l = jnp.einsum('bqhgd,bkhd->bhgqk', q, k, preferred_element_type=jnp.float32) * (HEAD_DIM ** -0.5)
        if m is not None:
            l = jnp.where(m, l, -jnp.inf)
        logits.append(l)
    b_, nq = q.shape[0], q.shape[1]
    sink_col = jnp.broadcast_to(sink.astype(jnp.float32)[None, :, :, None, None],
                                (b_, SWA_KV_HEADS, SWA_GROUP, nq, 1))
    p = jax.nn.softmax(jnp.concatenate(logits + [sink_col], axis=-1), axis=-1)
    outs = []
    off = 0
    for v in vs:
        n = v.shape[1]
        outs.append(jnp.einsum('bhgqk,bkhd->bqhgd', p[..., off:off + n].astype(v.dtype), v))
        off += n
    return sum(outs)


def swa_mixer(a_lat, a_ctx, w_qkv, w_out, q_g, k_g, sink, angles, ctx_out):
    qd = SWA_Q_HEADS * HEAD_DIM
    kvd = SWA_KV_HEADS * HEAD_DIM
    sink_kg = sink.reshape(SWA_KV_HEADS, SWA_GROUP)

    def proj(a, rotate):
        b_, s_, _ = a.shape
        qkv = a @ w_qkv
        q = rms_norm(qkv[..., :qd].reshape(b_, s_, SWA_Q_HEADS, HEAD_DIM), q_g)
        k = rms_norm(qkv[..., qd:qd + kvd].reshape(b_, s_, SWA_KV_HEADS, HEAD_DIM), k_g)
        v = qkv[..., qd + kvd:].reshape(b_, s_, SWA_KV_HEADS, HEAD_DIM)
        if rotate:
            q = rope_2d(q, angles)
            k = rope_2d(k, angles)
        return q.reshape(b_, s_, SWA_KV_HEADS, SWA_GROUP, HEAD_DIM), k, v

    q_l, k_l, v_l = proj(a_lat, True)
    q_c, k_c, v_c = proj(a_ctx, False)
    b_, s_ = a_lat.shape[:2]
    nb = s_ // Q_BLOCK
    band = Q_BLOCK + 2 * SWA_WINDOW
    pad = ((0, 0), (SWA_WINDOW, SWA_WINDOW), (0, 0), (0, 0))
    k_pad = jnp.pad(k_l, pad)
    v_pad = jnp.pad(v_l, pad)
    rel = jnp.arange(band)[None, :] - jnp.arange(Q_BLOCK)[:, None]
    in_band = (rel >= 0) & (rel <= 2 * SWA_WINDOW)
    q_blocks = q_l.reshape(b_, nb, Q_BLOCK, SWA_KV_HEADS, SWA_GROUP, HEAD_DIM).swapaxes(0, 1)
    starts = jnp.arange(nb) * Q_BLOCK

    def block(args):
        qb, st = args
        kb = lax.dynamic_slice_in_dim(k_pad, st, band, axis=1)
        vb = lax.dynamic_slice_in_dim(v_pad, st, band, axis=1)
        key_pos = st - SWA_WINDOW + jnp.arange(band)
        mask = in_band & ((key_pos >= 0) & (key_pos < s_))[None, :]
        return sink_attend(qb, [k_c, kb], [v_c, vb], [None, mask], sink_kg)

    o = lax.map(block, (q_blocks, starts)).swapaxes(0, 1).reshape(b_, s_, qd)
    y_lat = o @ w_out
    y_ctx = None
    if ctx_out:
        oc = sink_attend(q_c, [k_c], [v_c], [None], sink_kg)
        y_ctx = oc.reshape(oc.shape[0], oc.shape[1], qd) @ w_out
    return y_lat, y_ctx


def swiglu(a, w1, w3, w2):
    return (jax.nn.silu(a @ w1) * (a @ w3)) @ w2


def moe_ffn(a, w_r, b_r, w1, w3, w2):
    t, d = a.shape
    logits = (a @ w_r).astype(jnp.float32) + b_r.astype(jnp.float32)
    top_v, top_i = lax.top_k(logits, TOP_K)
    gates = jax.nn.softmax(top_v, axis=-1)
    n_assign = t * TOP_K
    expert = top_i.reshape(-1).astype(jnp.int32)
    token = jnp.arange(n_assign, dtype=jnp.int32) // TOP_K
    gate = gates.reshape(-1)
    order = jnp.argsort(expert)
    s_exp, s_tok, s_gate = expert[order], token[order], gate[order]
    counts = jnp.bincount(expert, length=N_EXPERTS).astype(jnp.int32)
    padded = (counts + MOE_BLOCK - 1) // MOE_BLOCK * MOE_BLOCK
    pad_end = jnp.cumsum(padded)
    pad_start = pad_end - padded
    grp_start = jnp.cumsum(counts) - counts
    dest = pad_start[s_exp] + jnp.arange(n_assign, dtype=jnp.int32) - grp_start[s_exp]
    n_blocks = n_assign // MOE_BLOCK + N_EXPERTS
    n_slots = n_blocks * MOE_BLOCK
    slot_tok = jnp.full((n_slots,), t, jnp.int32).at[dest].set(s_tok)
    slot_gate = jnp.zeros((n_slots,), jnp.float32).at[dest].set(s_gate)
    blk_exp = jnp.minimum(jnp.searchsorted(pad_end, jnp.arange(n_blocks, dtype=jnp.int32) * MOE_BLOCK, side='right'),
                          N_EXPERTS - 1)
    a_pad = jnp.concatenate([a, jnp.zeros((1, d), a.dtype)], axis=0)
    xs = a_pad[slot_tok].reshape(n_blocks, MOE_BLOCK, d)
    ys = lax.map(lambda args: swiglu(args[0], w1[args[1]], w3[args[1]], w2[args[1]]), (xs, blk_exp))
    ys = ys.reshape(n_slots, d) * slot_gate[:, None].astype(a.dtype)
    return jnp.zeros_like(a_pad).at[slot_tok].add(ys)[:t]


def setup_inputs(seed: int = 0) -> dict:
    key = jax.random.key(seed)
    ks = iter(jax.random.split(key, 48))
    f32 = jnp.float32
    D = D_MODEL

    def nrm(shape, fan_in):
        return jax.random.normal(next(ks), shape, f32) * (fan_in ** -0.5)

    def gain(shape):
        return 1.0 + 0.1 * jax.random.normal(next(ks), shape, f32)

    def small(shape, s):
        return s * jax.random.normal(next(ks), shape, f32)

    inp = {}
    inp["x"] = jax.random.normal(next(ks), (BATCH, SEQ, D), f32)
    inp["c"] = jax.random.normal(next(ks), (BATCH, D), f32)
    inp["ctx"] = jax.random.normal(next(ks), (BATCH, CTX_LEN, D), f32)
    inp["c_ctx"] = jax.random.normal(next(ks), (D,), f32)
    inp["ada_w"] = 0.5 * nrm((DEPTH, D, 6 * D), D)
    inp["ada_b"] = small((DEPTH, 6 * D), 0.02)
    inp["norm_mix_g"] = gain((DEPTH, D))
    inp["norm_ffn_g"] = gain((DEPTH, D))
    inp["sc_in_w"] = nrm((N_CONV, D, 3 * D), D)
    inp["sc_conv_w"] = nrm((N_CONV, 3, D), 3)
    inp["sc_out_w"] = nrm((N_CONV, D, D), D)
    inp["da_qkv_w"] = nrm((N_DIFF, D, 3 * D), D)
    inp["da_out_w"] = nrm((N_DIFF, D, D), D)
    inp["da_q_norm_g"] = gain((N_DIFF, HEAD_DIM))
    inp["da_k_norm_g"] = gain((N_DIFF, HEAD_DIM))
    inp["da_lambda"] = small((N_DIFF, 4, HEAD_DIM), 0.1)
    inp["da_sub_norm_g"] = gain((N_DIFF, 2 * HEAD_DIM))
    inp["cm_in_w"] = nrm((N_CHUNK, D, 2 * CM_WIDTH), D)
    inp["cm_in_b"] = small((N_CHUNK, 2 * CM_WIDTH), 0.02)
    inp["cm_v_norm_g"] = gain((N_CHUNK, CM_WIDTH))
    inp["cm_ws"] = nrm((N_CHUNK, CM_GROUPS, CHUNK, CHUNK), CHUNK)
    inp["cm_bs"] = small((N_CHUNK, CM_GROUPS, CHUNK), 0.1)
    inp["cm_out_w"] = nrm((N_CHUNK, CM_WIDTH, D), CM_WIDTH)
    inp["sw_qkv_w"] = nrm((N_SWA, D, (SWA_Q_HEADS + 2 * SWA_KV_HEADS) * HEAD_DIM), D)
    inp["sw_out_w"] = nrm((N_SWA, SWA_Q_HEADS * HEAD_DIM, D), SWA_Q_HEADS * HEAD_DIM)
    inp["sw_q_norm_g"] = gain((N_SWA, HEAD_DIM))
    inp["sw_k_norm_g"] = gain((N_SWA, HEAD_DIM))
    inp["sw_sink"] = small((N_SWA, SWA_Q_HEADS), 1.0)
    inp["ffn_w1"] = nrm((N_DENSE, D, FFN_DIM), D)
    inp["ffn_w3"] = nrm((N_DENSE, D, FFN_DIM), D)
    inp["ffn_w2"] = nrm((N_DENSE, FFN_DIM, D), FFN_DIM)
    inp["moe_router_w"] = nrm((N_MOE, D, N_EXPERTS), D)
    inp["moe_router_b"] = small((N_MOE, N_EXPERTS), 0.01)
    inp["moe_w1"] = nrm((N_MOE, N_EXPERTS, D, FFN_DIM), D)
    inp["moe_w3"] = nrm((N_MOE, N_EXPERTS, D, FFN_DIM), D)
    inp["moe_w2"] = nrm((N_MOE, N_EXPERTS, FFN_DIM, D), FFN_DIM)
    return inp


def reference(x, c, ctx, c_ctx, ada_w, ada_b, norm_mix_g, norm_ffn_g,
              sc_in_w, sc_conv_w, sc_out_w,
              da_qkv_w, da_out_w, da_q_norm_g, da_k_norm_g, da_lambda, da_sub_norm_g,
              cm_in_w, cm_in_b, cm_v_norm_g, cm_ws, cm_bs, cm_out_w,
              sw_qkv_w, sw_out_w, sw_q_norm_g, sw_k_norm_g, sw_sink,
              ffn_w1, ffn_w3, ffn_w2,
              moe_router_w, moe_router_b, moe_w1, moe_w3, moe_w2):
    angles = axial_angles(x.shape[1])
    h, hc = x, ctx
    for i in range(DEPTH):
        ctx_out = i < DEPTH - 1
        sh1, sc1, g1, sh2, sc2, g2 = adaln(c, ada_w[i], ada_b[i])
        csh1, csc1, cg1, csh2, csc2, cg2 = adaln(c_ctx, ada_w[i], ada_b[i])
        a = modulate(rms_norm(h, norm_mix_g[i]), sh1, sc1)
        ac = modulate(rms_norm(hc, norm_mix_g[i]), csh1, csc1)
        kind, j = i % N_MIXERS, i // N_MIXERS
        if kind == 0:
            y, yc = short_conv_mixer(a, ac, sc_in_w[j], sc_conv_w[j], sc_out_w[j], ctx_out)
        elif kind == 1:
            y, yc = diff_attn_mixer(a, ac, da_qkv_w[j], da_out_w[j], da_q_norm_g[j], da_k_norm_g[j],
                                    da_lambda[j], da_sub_norm_g[j], i, angles, ctx_out)
        elif kind == 2:
            y, yc = chunk_mlp_mixer(a, ac, cm_in_w[j], cm_in_b[j], cm_v_norm_g[j], cm_ws[j], cm_bs[j],
                                    cm_out_w[j], ctx_out)
        else:
            y, yc = swa_mixer(a, ac, sw_qkv_w[j], sw_out_w[j], sw_q_norm_g[j], sw_k_norm_g[j],
                              sw_sink[j], angles, ctx_out)
        h = h + g1 * y
        if ctx_out:
            hc = hc + cg1 * yc
        a = modulate(rms_norm(h, norm_ffn_g[i]), sh2, sc2)
        f = i // 2
        if i % 2 == 0:
            h = h + g2 * swiglu(a, ffn_w1[f], ffn_w3[f], ffn_w2[f])
            if ctx_out:
                ac = modulate(rms_norm(hc, norm_ffn_g[i]), csh2, csc2)
                hc = hc + cg2 * swiglu(ac, ffn_w1[f], ffn_w3[f], ffn_w2[f])
        else:
            b_, s_, d_ = a.shape
            if ctx_out:
                ac = modulate(rms_norm(hc, norm_ffn_g[i]), csh2, csc2)
                flat = jnp.concatenate([a.reshape(-1, d_), ac.reshape(-1, d_)], axis=0)
                out = moe_ffn(flat, moe_router_w[f], moe_router_b[f], moe_w1[f], moe_w3[f], moe_w2[f])
                h = h + g2 * out[:b_ * s_].reshape(b_, s_, d_)
                hc = hc + cg2 * out[b_ * s_:].reshape(hc.shape)
            else:
                out = moe_ffn(a.reshape(-1, d_), moe_router_w[f], moe_router_b[f], moe_w1[f], moe_w3[f], moe_w2[f])
                h = h + g2 * out.reshape(b_, s_, d_)
    return h
```

```python
import functools
import math

import jax
import jax.numpy as jnp
from jax import lax
from jax.experimental import pallas as pl
from jax.experimental.pallas import tpu as pltpu

D = 1024
HEAD_DIM = 64
GRID_W = 64
ROPE_HALF = HEAD_DIM // 2
ROPE_BASE = 10000.0
DIFF_HEADS = D // (2 * HEAD_DIM)
SWA_Q_HEADS = D // HEAD_DIM
SWA_KV_HEADS = 4
SWA_GROUP = SWA_Q_HEADS // SWA_KV_HEADS
SWA_WINDOW = 128
CHUNK = 128
CM_WIDTH = 2 * D
CM_GROUPS = 8
CM_GW = CM_WIDTH // CM_GROUPS
FFN_DIM = 2816
N_EXPERTS = 8
EPS = 1e-6

F32 = jnp.float32
BF16 = jnp.bfloat16
HIGHEST = lax.Precision.HIGHEST

TM = 512
CN = 256
MOE_RB = 512
ATT_TQ = 512
ATT_TK = 512
SWA_TQ = 256
ROUTE_W = 128
VMEM_LIMIT = 56 << 20


def _params(n_grid):
    return pltpu.CompilerParams(dimension_semantics=("arbitrary",) * n_grid, vmem_limit_bytes=VMEM_LIMIT)


def _resident(shape):
    zeros = (0,) * len(shape)
    return pl.BlockSpec(shape, lambda *_: zeros, pipeline_mode=pl.Buffered(1))


def _sigmoid(x):
    return 1.0 / (1.0 + jnp.exp(-x))


def _gelu_tanh(x):
    return 0.5 * x * (1.0 + jnp.tanh(math.sqrt(2.0 / math.pi) * (x + 0.044715 * (x * x * x))))


def _modnorm(x, g, shift, scale):
    y = x * lax.rsqrt(jnp.mean(x * x, axis=-1, keepdims=True) + EPS)
    return (y * g) * (1.0 + scale) + shift


def _dot(a, b):
    return jnp.dot(a, b, preferred_element_type=F32)


def _mix_mod(m):
    return m[6:7], m[0:1], m[1:2]


def _ffn_mod(m):
    return m[7:8], m[3:4], m[4:5]


def _ada_kernel(c_ref, w_ref, b_ref, o_ref):
    c = c_ref[...]
    s = c * _sigmoid(c)
    o_ref[0] = jnp.dot(s, w_ref[0], precision=HIGHEST, preferred_element_type=F32) + b_ref[0]


def _ada_all(cvec, ada_w, ada_b):
    depth, _, n = ada_w.shape
    tn = 1536
    rows = cvec.shape[0]
    return pl.pallas_call(
        _ada_kernel,
        grid=(depth, n // tn),
        in_specs=[pl.BlockSpec((rows, D), lambda l, j: (0, 0)),
                  pl.BlockSpec((1, D, tn), lambda l, j: (l, 0, j)),
                  pl.BlockSpec((1, 1, tn), lambda l, j: (l, 0, j))],
        out_specs=pl.BlockSpec((1, rows, tn), lambda l, j: (l, 0, j)),
        out_shape=jax.ShapeDtypeStruct((depth, rows, n), F32),
        compiler_params=_params(2),
        name="adaln",
    )(cvec, ada_w, ada_b.reshape(depth, 1, n))


class _Rows:
    def __init__(self, batch, seq, ctx_len):
        self.batch, self.seq, self.ctx_len = batch, seq, ctx_len
        self.n_lat = batch * seq
        self.n_ctx = batch * ctx_len
        self.n_all = self.n_lat + self.n_ctx
        assert seq % TM == 0 and self.n_ctx % TM == 0 and TM % ctx_len == 0
        self.lat_tiles = self.n_lat // TM
        self.all_tiles = self.n_all // TM
        self.tiles_per_seq = seq // TM

    def mod_spec(self):
        tps, batch = self.tiles_per_seq, self.batch
        return pl.BlockSpec((1, 8, D), lambda i, *_: (jnp.minimum(i // tps, batch), 0, 0))

    def row_spec(self, width):
        return pl.BlockSpec((TM, width), lambda i, *_: (i, 0))


def _conv_in_kernel(h_ref, mod_ref, w_ref, bg_ref, y_ref):
    a = _modnorm(h_ref[...], *_mix_mod(mod_ref[0])).astype(BF16)
    for c in range(D // CN):
        lo = c * CN
        bg_ref[:, lo:lo + CN] = _dot(a, w_ref[:, lo:lo + CN]).astype(BF16)
        cg = _dot(a, w_ref[:, D + lo:D + lo + CN])
        xv = _dot(a, w_ref[:, 2 * D + lo:2 * D + lo + CN])
        y_ref[:, lo:lo + CN] = (cg * xv).astype(BF16)


def _conv_in(rows, h, modg, w_in):
    out = jax.ShapeDtypeStruct((rows.n_all, D), BF16)
    return pl.pallas_call(
        _conv_in_kernel,
        grid=(rows.all_tiles,),
        in_specs=[rows.row_spec(D), rows.mod_spec(), _resident((D, 3 * D))],
        out_specs=[rows.row_spec(D), rows.row_spec(D)],
        out_shape=[out, out],
        compiler_params=_params(1),
        name="conv_in",
    )(h, modg, w_in)


HALO = 16


def _conv_out_kernel(n_lat, seq, ctx_len, h_ref, mod_ref, bg_ref, y_ref, yp_ref, yn_ref, cw_ref, w_ref, o_ref):
    i = pl.program_id(0)
    m = mod_ref[0]
    y = y_ref[...].astype(F32)
    row = lax.broadcasted_iota(jnp.int32, (TM, 1), 0)
    grow = row + i * TM
    seq_len = jnp.where(grow < n_lat, seq, ctx_len)
    pos = grow & (seq_len - 1)
    prev_row = yp_ref[HALO - 1:HALO, :].astype(F32)
    next_row = yn_ref[0:1, :].astype(F32)
    y_m1 = jnp.where(row == 0, prev_row, pltpu.roll(y, 1, 0))
    y_m1 = jnp.where(pos == 0, 0.0, y_m1)
    y_p1 = jnp.where(row == TM - 1, next_row, pltpu.roll(y, TM - 1, 0))
    y_p1 = jnp.where(pos == seq_len - 1, 0.0, y_p1)
    conv = cw_ref[0:1, :] * y_m1 + cw_ref[1:2, :] * y + cw_ref[2:3, :] * y_p1
    z = (bg_ref[...].astype(F32) * conv).astype(BF16)
    o_ref[...] = h_ref[...] + m[2:3] * _dot(z, w_ref[...])


def _conv_out(rows, h, modg, bg, y, conv_w, w_out):
    hb = TM // HALO
    last = rows.n_all // HALO - 1
    return pl.pallas_call(
        functools.partial(_conv_out_kernel, rows.n_lat, rows.seq, rows.ctx_len),
        grid=(rows.all_tiles,),
        in_specs=[rows.row_spec(D), rows.mod_spec(), rows.row_spec(D), rows.row_spec(D),
                  pl.BlockSpec((HALO, D), lambda i: (jnp.maximum(i * hb - 1, 0), 0)),
                  pl.BlockSpec((HALO, D), lambda i: (jnp.minimum((i + 1) * hb, last), 0)),
                  _resident((8, D)), _resident((D, D))],
        out_specs=rows.row_spec(D),
        out_shape=jax.ShapeDtypeStruct((rows.n_all, D), F32),
        compiler_params=_params(1),
        name="conv_out",
    )(h, modg, bg, y, y, y, conv_w, w_out)


def _swiglu_into(a, w1_ref, w3_ref, t_ref):
    for c in range(FFN_DIM // CN):
        lo = c * CN
        h1 = _dot(a, w1_ref[:, lo:lo + CN])
        h3 = _dot(a, w3_ref[:, lo:lo + CN])
        t_ref[:, lo:lo + CN] = (h1 * _sigmoid(h1) * h3).astype(BF16)


def _ffn_kernel(h_ref, mod_ref, w1_ref, w3_ref, w2_ref, o_ref, t_ref):
    m = mod_ref[0]
    x = h_ref[...]
    a = _modnorm(x, *_ffn_mod(m)).astype(BF16)
    _swiglu_into(a, w1_ref, w3_ref, t_ref)
    o_ref[...] = x + m[5:6] * _dot(t_ref[...], w2_ref[...])


def _ffn(rows, n_tiles, h, modg, w1, w3, w2):
    return pl.pallas_call(
        _ffn_kernel,
        grid=(n_tiles,),
        in_specs=[rows.row_spec(D), rows.mod_spec(), _resident((D, FFN_DIM)), _resident((D, FFN_DIM)),
                  _resident((FFN_DIM, D))],
        out_specs=rows.row_spec(D),
        out_shape=jax.ShapeDtypeStruct((n_tiles * TM, D), F32),
        scratch_shapes=[pltpu.VMEM((TM, FFN_DIM), BF16)],
        compiler_params=_params(1),
        name="ffn_dense",
    )(h, modg, w1, w3, w2)


def _head_group_sum(x2, gmat_ref):
    hi = x2.astype(BF16)
    lo = (x2 - hi.astype(F32)).astype(BF16)
    return _dot(hi, gmat_ref[...]) + _dot(lo, gmat_ref[...])


def _norm_rope(x, gain, gmat_ref, cos, sin):
    ms = _head_group_sum(x * x, gmat_ref) * (1.0 / HEAD_DIM)
    y = x * lax.rsqrt(ms + EPS) * gain
    lane = lax.broadcasted_iota(jnp.int32, (1, CN), 1)
    first_half = (lane & (ROPE_HALF - 1)) < (ROPE_HALF // 2)
    partner = jnp.where(first_half, pltpu.roll(y, CN - ROPE_HALF // 2, 1), pltpu.roll(y, ROPE_HALF // 2, 1))
    return y * cos + partner * sin


def _qkv_kernel(n_qk_chunks, n_q_chunks, q_scale, h_ref, mod_ref, w_ref, gmat_ref, qg_ref, kg_ref, cos_ref, sin_ref,
                o_ref):
    a = _modnorm(h_ref[...], *_mix_mod(mod_ref[0])).astype(BF16)
    n_chunks = w_ref.shape[1] // CN
    cos = cos_ref[...]
    sin = sin_ref[...]
    for c in range(n_chunks):
        lo = c * CN
        acc = _dot(a, w_ref[:, lo:lo + CN])
        if c < n_q_chunks:
            acc = _norm_rope(acc, qg_ref[...], gmat_ref, cos, sin) * q_scale
        elif c < n_qk_chunks:
            acc = _norm_rope(acc, kg_ref[...], gmat_ref, cos, sin)
        o_ref[:, lo:lo + CN] = acc.astype(BF16)


def _rope_tables(rows):
    seq = rows.seq
    pos = jnp.arange(seq)
    n_freq = ROPE_HALF // 2
    inv = ROPE_BASE ** (-jnp.arange(n_freq, dtype=F32) / n_freq)
    ang_r = (pos // GRID_W).astype(F32)[:, None] * inv
    ang_c = (pos % GRID_W).astype(F32)[:, None] * inv
    cos = jnp.concatenate([jnp.cos(ang_r)] * 2 + [jnp.cos(ang_c)] * 2, axis=-1)
    sin = jnp.concatenate([-jnp.sin(ang_r), jnp.sin(ang_r), -jnp.sin(ang_c), jnp.sin(ang_c)], axis=-1)
    cos = jnp.concatenate([cos, jnp.ones((TM, HEAD_DIM), F32)], axis=0)
    sin = jnp.concatenate([sin, jnp.zeros((TM, HEAD_DIM), F32)], axis=0)
    reps = CN // HEAD_DIM
    return jnp.tile(cos, (1, reps)), jnp.tile(sin, (1, reps))


def _qkv(rows, h, modg, w, q_gain, k_gain, tables, n_q_chunks, n_k_chunks):
    width = w.shape[1]
    reps = CN // HEAD_DIM
    head = jnp.arange(CN) // HEAD_DIM
    gmat = (head[:, None] == head[None, :]).astype(BF16)
    qg = jnp.tile(q_gain.reshape(1, HEAD_DIM), (1, reps))
    kg = jnp.tile(k_gain.reshape(1, HEAD_DIM), (1, reps))
    tps, lat_tiles = rows.tiles_per_seq, rows.lat_tiles
    tab_spec = pl.BlockSpec((TM, CN), lambda i: (jnp.where(i < lat_tiles, i % tps, tps), 0))
    return pl.pallas_call(
        functools.partial(_qkv_kernel, n_q_chunks + n_k_chunks, n_q_chunks, HEAD_DIM ** -0.5),
        grid=(rows.all_tiles,),
        in_specs=[rows.row_spec(D), rows.mod_spec(), _resident((D, width)), _resident((CN, CN)),
                  _resident((1, CN)), _resident((1, CN)), tab_spec, tab_spec],
        out_specs=rows.row_spec(width),
        out_shape=jax.ShapeDtypeStruct((rows.n_all, width), BF16),
        compiler_params=_params(1),
        name="qkv_proj",
    )(h, modg, w, gmat, qg, kg, tables[0], tables[1])


DH2 = 2 * HEAD_DIM


def _diff_attn_kernel(lam_init, n_lat_chunks, q_ref, kc_ref, vc_ref, kl_ref, vl_ref, lam_ref, subg_ref, o_ref,
                      m_ref, l_ref, acc_ref):
    q = q_ref[...]
    lane = lax.broadcasted_iota(jnp.int32, (1, DH2), 1)
    zero = jnp.zeros((), BF16)
    qs = (jnp.where(lane < HEAD_DIM, q, zero), jnp.where(lane >= HEAD_DIM, q, zero))
    m_ref[...] = jnp.full(m_ref.shape, -jnp.inf, F32)
    l_ref[...] = jnp.zeros(l_ref.shape, F32)
    acc_ref[...] = jnp.zeros(acc_ref.shape, F32)

    def update(k, v):
        for mi in range(2):
            s = lax.dot_general(qs[mi], k, (((1,), (1,)), ((), ())), preferred_element_type=F32)
            m_old = m_ref[mi]
            m_new = jnp.maximum(m_old, jnp.max(s, axis=-1, keepdims=True))
            alpha = jnp.exp(m_old - m_new)
            p = jnp.exp(s - m_new)
            l_ref[mi] = alpha * l_ref[mi] + jnp.sum(p, axis=-1, keepdims=True)
            acc_ref[mi] = alpha * acc_ref[mi] + _dot(p.astype(BF16), v)
            m_ref[mi] = m_new

    update(kc_ref[...], vc_ref[...])
    if n_lat_chunks:
        def body(c, carry):
            start = pl.multiple_of(c * ATT_TK, ATT_TK)
            update(kl_ref[pl.ds(start, ATT_TK), :], vl_ref[pl.ds(start, ATT_TK), :])
            return carry
        lax.fori_loop(0, n_lat_chunks, body, 0)

    lp = lam_ref[...]
    lam = (jnp.exp(jnp.sum(lp[0:1] * lp[1:2], axis=-1, keepdims=True))
           - jnp.exp(jnp.sum(lp[2:3] * lp[3:4], axis=-1, keepdims=True)) + lam_init)
    o = acc_ref[0] / l_ref[0] - lam * (acc_ref[1] / l_ref[1])
    o = o * lax.rsqrt(jnp.mean(o * o, axis=-1, keepdims=True) + EPS) * subg_ref[...]
    o_ref[...] = (o * (1.0 - lam_init)).astype(BF16)


def _diff_attn(rows, qkv, lam_p, sub_g, lam_init):
    batch, seq, ctx_len = rows.batch, rows.seq, rows.ctx_len
    nq = seq // ATT_TQ
    nh = DIFF_HEADS
    ctx0 = rows.n_lat // ctx_len
    sub_g = sub_g.reshape(1, DH2)
    small = [_resident((4, HEAD_DIM)), _resident((1, DH2))]
    kc_spec = pl.BlockSpec((ctx_len, DH2), lambda b, h, *_: (ctx0 + b, nh + h))
    vc_spec = pl.BlockSpec((ctx_len, DH2), lambda b, h, *_: (ctx0 + b, 2 * nh + h))

    def scratch(tq):
        return [pltpu.VMEM((2, tq, 1), F32), pltpu.VMEM((2, tq, 1), F32), pltpu.VMEM((2, tq, DH2), F32)]

    o_lat = pl.pallas_call(
        functools.partial(_diff_attn_kernel, lam_init, seq // ATT_TK),
        grid=(batch, nh, nq),
        in_specs=[pl.BlockSpec((ATT_TQ, DH2), lambda b, h, i: (b * nq + i, h)),
                  kc_spec, vc_spec,
                  pl.BlockSpec((seq, DH2), lambda b, h, i: (b, nh + h)),
                  pl.BlockSpec((seq, DH2), lambda b, h, i: (b, 2 * nh + h))] + small,
        out_specs=pl.BlockSpec((ATT_TQ, DH2), lambda b, h, i: (b * nq + i, h)),
        out_shape=jax.ShapeDtypeStruct((rows.n_lat, D), BF16),
        scratch_shapes=scratch(ATT_TQ),
        compiler_params=_params(3),
        name="diff_attn_latent",
    )(qkv, qkv, qkv, qkv, qkv, lam_p, sub_g)

    def ctx_kernel(q_ref, kc_ref, vc_ref, lam_ref, subg_ref, o_ref, m_ref, l_ref, acc_ref):
        _diff_attn_kernel(lam_init, 0, q_ref, kc_ref, vc_ref, None, None, lam_ref, subg_ref, o_ref, m_ref, l_ref,
                          acc_ref)

    o_ctx = pl.pallas_call(
        ctx_kernel,
        grid=(batch, nh),
        in_specs=[pl.BlockSpec((ctx_len, DH2), lambda b, h: (ctx0 + b, h)), kc_spec, vc_spec] + small,
        out_specs=pl.BlockSpec((ctx_len, DH2), lambda b, h: (b, h)),
        out_shape=jax.ShapeDtypeStruct((rows.n_ctx, D), BF16),
        scratch_shapes=scratch(ctx_len),
        compiler_params=_params(2),
        name="diff_attn_context",
    )(qkv, qkv, qkv, lam_p, sub_g)
    return o_lat, o_ctx


def _out_proj_kernel(lat_tiles, h_ref, mod_ref, zl_ref, zc_ref, w_ref, o_ref):
    i = pl.program_id(0)

    def project(z_ref):
        o_ref[...] = h_ref[...] + mod_ref[0][2:3] * _dot(z_ref[...], w_ref[...])

    pl.when(i < lat_tiles)(lambda: project(zl_ref))
    pl.when(i >= lat_tiles)(lambda: project(zc_ref))


def _out_proj(rows, n_tiles, h, modg, z_lat, z_ctx, w):
    lat_tiles = rows.lat_tiles
    return pl.pallas_call(
        functools.partial(_out_proj_kernel, lat_tiles),
        grid=(n_tiles,),
        in_specs=[rows.row_spec(D), rows.mod_spec(),
                  pl.BlockSpec((TM, D), lambda i: (jnp.minimum(i, lat_tiles - 1), 0)),
                  pl.BlockSpec((TM, D), lambda i: (jnp.maximum(i - lat_tiles, 0), 0)),
                  _resident((D, D))],
        out_specs=rows.row_spec(D),
        out_shape=jax.ShapeDtypeStruct((n_tiles * TM, D), F32),
        compiler_params=_params(1),
        name="out_proj",
    )(h, modg, z_lat, z_ctx, w)


def _gmlp_kernel(h_ref, mod_ref, win_ref, bin_ref, vg_ref, ws_ref, bs_ref, wout_ref, o_ref, u_ref, v_ref, t_ref):
    m = mod_ref[0]
    x = h_ref[...]
    a = _modnorm(x, *_mix_mod(m)).astype(BF16)
    ssq = jnp.zeros((TM, 1), F32)
    for c in range(2 * CM_WIDTH // CN):
        lo = c * CN
        z = _gelu_tanh(_dot(a, win_ref[:, lo:lo + CN]) + bin_ref[:, lo:lo + CN])
        if lo < CM_WIDTH:
            u_ref[:, lo:lo + CN] = z
        else:
            v_ref[:, lo - CM_WIDTH:lo - CM_WIDTH + CN] = z
            ssq = ssq + jnp.sum(z * z, axis=-1, keepdims=True)
    inv = lax.rsqrt(ssq * (1.0 / CM_WIDTH) + EPS)
    for g in range(CM_GROUPS):
        lo = g * CM_GW
        vn = (v_ref[:, lo:lo + CM_GW] * inv * vg_ref[:, lo:lo + CM_GW]).astype(BF16)
        for r in range(TM // CHUNK):
            r0 = r * CHUNK
            sv = _dot(ws_ref[g], vn[r0:r0 + CHUNK]) + bs_ref[:, g:g + 1]
            t_ref[r0:r0 + CHUNK, lo:lo + CM_GW] = (u_ref[r0:r0 + CHUNK, lo:lo + CM_GW] * sv).astype(BF16)
    o_ref[...] = x + m[2:3] * _dot(t_ref[...], wout_ref[...])


def _gmlp(rows, h, modg, w_in, b_in, v_g, w_s, b_s, w_out):
    return pl.pallas_call(
        _gmlp_kernel,
        grid=(rows.all_tiles,),
        in_specs=[rows.row_spec(D), rows.mod_spec(), _resident((D, 2 * CM_WIDTH)), _resident((1, 2 * CM_WIDTH)),
                  _resident((1, CM_WIDTH)), _resident((CM_GROUPS, CHUNK, CHUNK)), _resident((CHUNK, CM_GROUPS)),
                  _resident((CM_WIDTH, D))],
        out_specs=rows.row_spec(D),
        out_shape=jax.ShapeDtypeStruct((rows.n_all, D), F32),
        scratch_shapes=[pltpu.VMEM((TM, CM_WIDTH), F32), pltpu.VMEM((TM, CM_WIDTH), F32),
                        pltpu.VMEM((TM, CM_WIDTH), BF16)],
        compiler_params=_params(1),
        name="gmlp",
    )(h, modg, w_in, b_in, v_g, w_s, b_s, w_out)


SWA_BAND = SWA_TQ + 2 * SWA_WINDOW


def _swa_kernel(seq, sink_ref, q_ref, kc_ref, vc_ref, kl_ref, vl_ref, o_ref):
    qi = pl.program_id(1)
    q0 = qi * SWA_TQ
    start = jnp.clip(q0 - SWA_WINDOW, 0, seq - SWA_BAND)
    start = pl.multiple_of(start, SWA_WINDOW)
    kb = kl_ref[pl.ds(start, SWA_BAND), :]
    vb = vl_ref[pl.ds(start, SWA_BAND), :]
    kc = kc_ref[...]
    vc = vc_ref[...]
    qpos = q0 + lax.broadcasted_iota(jnp.int32, (SWA_TQ, SWA_BAND), 0)
    kpos = start + lax.broadcasted_iota(jnp.int32, (SWA_TQ, SWA_BAND), 1)
    in_band = jnp.abs(qpos - kpos) <= SWA_WINDOW
    nt = (((1,), (1,)), ((), ()))
    for j in range(SWA_KV_HEADS):
        kj = slice(j * HEAD_DIM, (j + 1) * HEAD_DIM)
        kbj, vbj, kcj, vcj = kb[:, kj], vb[:, kj], kc[:, kj], vc[:, kj]
        outs = []
        for g in range(SWA_GROUP):
            hq = j * SWA_GROUP + g
            qh = q_ref[:, hq * HEAD_DIM:(hq + 1) * HEAD_DIM]
            s_c = lax.dot_general(qh, kcj, nt, preferred_element_type=F32)
            s_b = lax.dot_general(qh, kbj, nt, preferred_element_type=F32)
            s_b = jnp.where(in_band, s_b, -jnp.inf)
            sink = sink_ref[hq]
            mx = jnp.maximum(jnp.maximum(jnp.max(s_c, axis=-1, keepdims=True),
                                         jnp.max(s_b, axis=-1, keepdims=True)), sink)
            p_c = jnp.exp(s_c - mx)
            p_b = jnp.exp(s_b - mx)
            denom = (jnp.sum(p_c, axis=-1, keepdims=True) + jnp.sum(p_b, axis=-1, keepdims=True)
                     + jnp.exp(sink - mx))
            o = _dot(p_c.astype(BF16), vcj) + _dot(p_b.astype(BF16), vbj)
            outs.append(o / denom)
        lo = j * SWA_GROUP * HEAD_DIM
        o_ref[:, lo:lo + SWA_GROUP * HEAD_DIM] = jnp.concatenate(outs, axis=-1).astype(BF16)


def _swa(rows, qkv, sink):
    batch, seq, ctx_len = rows.batch, rows.seq, rows.ctx_len
    nq = seq // SWA_TQ
    kvw = SWA_KV_HEADS * HEAD_DIM
    k_col = D // kvw
    ctx0 = rows.n_lat // ctx_len
    return pl.pallas_call(
        functools.partial(_swa_kernel, seq),
        grid=(batch, nq),
        in_specs=[pl.BlockSpec(memory_space=pltpu.SMEM),
                  pl.BlockSpec((SWA_TQ, D), lambda b, i: (b * nq + i, 0)),
                  pl.BlockSpec((ctx_len, kvw), lambda b, i: (ctx0 + b, k_col)),
                  pl.BlockSpec((ctx_len, kvw), lambda b, i: (ctx0 + b, k_col + 1)),
                  pl.BlockSpec((seq, kvw), lambda b, i: (b, k_col)),
                  pl.BlockSpec((seq, kvw), lambda b, i: (b, k_col + 1))],
        out_specs=pl.BlockSpec((SWA_TQ, D), lambda b, i: (b * nq + i, 0)),
        out_shape=jax.ShapeDtypeStruct((rows.n_lat, D), BF16),
        compiler_params=_params(2),
        name="swa_attn",
    )(sink, qkv, qkv, qkv, qkv, qkv)


def _router_kernel(h_ref, mod_ref, wr_ref, br_ref, a_ref, route_ref):
    a = _modnorm(h_ref[...], *_ffn_mod(mod_ref[0]))
    a_ref[...] = a
    logits = jnp.dot(a, wr_ref[...], precision=HIGHEST, preferred_element_type=F32) + br_ref[...]
    lane = lax.broadcasted_iota(jnp.int32, (TM, ROUTE_W), 1)
    m1 = jnp.max(logits, axis=-1, keepdims=True)
    i1 = jnp.min(jnp.where(logits == m1, lane, ROUTE_W), axis=-1, keepdims=True)
    rest = jnp.where(lane == i1, -jnp.inf, logits)
    m2 = jnp.max(rest, axis=-1, keepdims=True)
    i2 = jnp.min(jnp.where(rest == m2, lane, ROUTE_W), axis=-1, keepdims=True)
    e2 = jnp.exp(m2 - m1)
    gate1 = 1.0 / (1.0 + e2)
    gate2 = e2 / (1.0 + e2)
    rec = jnp.where(lane == 0, i1.astype(F32), 0.0)
    rec = jnp.where(lane == 1, i2.astype(F32), rec)
    rec = jnp.where(lane == 2, gate1, rec)
    route_ref[...] = jnp.where(lane == 3, gate2, rec)


def _router(rows, n_tiles, h, modg, w_r, b_r):
    pad = ROUTE_W - N_EXPERTS
    w_pad = jnp.pad(w_r, ((0, 0), (0, pad)))
    b_pad = jnp.concatenate([b_r.astype(F32), jnp.full((pad,), -1e30, F32)]).reshape(1, ROUTE_W)
    return pl.pallas_call(
        _router_kernel,
        grid=(n_tiles,),
        in_specs=[rows.row_spec(D), rows.mod_spec(), _resident((D, ROUTE_W)), _resident((1, ROUTE_W))],
        out_specs=[rows.row_spec(D), rows.row_spec(ROUTE_W)],
        out_shape=[jax.ShapeDtypeStruct((n_tiles * TM, D), F32), jax.ShapeDtypeStruct((n_tiles * TM, ROUTE_W), F32)],
        compiler_params=_params(1),
        name="moe_router",
    )(h, modg, w_pad, b_pad)


def _row_copy(src_ref, src_row, dst_ref, dst_row, sem):
    return pltpu.make_async_copy(src_ref.at[pl.ds(src_row, 1)], dst_ref.at[pl.ds(dst_row, 1)], sem)


def _dispatch_kernel(tok_ref, a_hbm, xs_ref, sem):
    base = pl.program_id(0) * MOE_RB

    def issue(r, carry):
        _row_copy(a_hbm, tok_ref[base + r], xs_ref, r, sem).start()
        return carry

    lax.fori_loop(0, MOE_RB, issue, 0)
    pltpu.make_async_copy(a_hbm.at[pl.ds(0, MOE_RB)], xs_ref, sem).wait()


def _dispatch(a, slot_tok, n_blocks):
    return pl.pallas_call(
        _dispatch_kernel,
        grid_spec=pltpu.PrefetchScalarGridSpec(
            num_scalar_prefetch=1,
            grid=(n_blocks,),
            in_specs=[pl.BlockSpec(memory_space=pl.ANY)],
            out_specs=pl.BlockSpec((MOE_RB, D), lambda i, tok: (i, 0)),
            scratch_shapes=[pltpu.SemaphoreType.DMA(())],
        ),
        out_shape=jax.ShapeDtypeStruct((n_blocks * MOE_RB, D), F32),
        compiler_params=_params(1),
        name="moe_dispatch",
    )(slot_tok, a)


def _expert_kernel(be_ref, xs_ref, w1_ref, w3_ref, w2_ref, ys_ref, t_ref):
    del be_ref
    a = xs_ref[...].astype(BF16)
    _swiglu_into(a, w1_ref, w3_ref, t_ref)
    ys_ref[...] = _dot(t_ref[...], w2_ref[...])


def _experts(xs, blk_exp, w1, w3, w2):
    n_blocks = blk_exp.shape[0]
    return pl.pallas_call(
        _expert_kernel,
        grid_spec=pltpu.PrefetchScalarGridSpec(
            num_scalar_prefetch=1,
            grid=(n_blocks,),
            in_specs=[pl.BlockSpec((MOE_RB, D), lambda i, be: (i, 0)),
                      pl.BlockSpec((None, D, FFN_DIM), lambda i, be: (be[i], 0, 0)),
                      pl.BlockSpec((None, D, FFN_DIM), lambda i, be: (be[i], 0, 0)),
                      pl.BlockSpec((None, FFN_DIM, D), lambda i, be: (be[i], 0, 0))],
            out_specs=pl.BlockSpec((MOE_RB, D), lambda i, be: (i, 0)),
            scratch_shapes=[pltpu.VMEM((MOE_RB, FFN_DIM), BF16)],
        ),
        out_shape=jax.ShapeDtypeStruct((n_blocks * MOE_RB, D), F32),
        compiler_params=_params(1),
        name="moe_experts",
    )(blk_exp, xs, w1, w3, w2)


def _combine_kernel(dest_ref, h_ref, mod_ref, route_ref, ys_hbm, o_ref, y1_ref, y2_ref, sem):
    base = pl.program_id(0) * (2 * TM)

    def issue(r, carry):
        _row_copy(ys_hbm, dest_ref[base + 2 * r], y1_ref, r, sem).start()
        _row_copy(ys_hbm, dest_ref[base + 2 * r + 1], y2_ref, r, sem).start()
        return carry

    lax.fori_loop(0, TM, issue, 0)
    pltpu.make_async_copy(ys_hbm.at[pl.ds(0, TM)], y1_ref, sem).wait()
    pltpu.make_async_copy(ys_hbm.at[pl.ds(0, TM)], y2_ref, sem).wait()
    route = route_ref[...]
    mix = route[:, 2:3] * y1_ref[...] + route[:, 3:4] * y2_ref[...]
    o_ref[...] = h_ref[...] + mod_ref[0][5:6] * mix


def _combine(rows, n_tiles, h, modg, route, ys, dest):
    tps, batch = rows.tiles_per_seq, rows.batch
    return pl.pallas_call(
        _combine_kernel,
        grid_spec=pltpu.PrefetchScalarGridSpec(
            num_scalar_prefetch=1,
            grid=(n_tiles,),
            in_specs=[pl.BlockSpec((TM, D), lambda i, d: (i, 0)),
                      pl.BlockSpec((1, 8, D), lambda i, d: (jnp.minimum(i // tps, batch), 0, 0)),
                      pl.BlockSpec((TM, ROUTE_W), lambda i, d: (i, 0)),
                      pl.BlockSpec(memory_space=pl.ANY)],
            out_specs=pl.BlockSpec((TM, D), lambda i, d: (i, 0)),
            scratch_shapes=[pltpu.VMEM((TM, D), F32), pltpu.VMEM((TM, D), F32), pltpu.SemaphoreType.DMA(())],
        ),
        out_shape=jax.ShapeDtypeStruct((n_tiles * TM, D), F32),
        compiler_params=_params(1),
        name="moe_combine",
    )(dest, h, modg, route, ys)


def _moe(rows, n_tiles, h, modg, w_r, b_r, w1, w3, w2):
    a, route = _router(rows, n_tiles, h, modg, w_r, b_r)
    n_tok = n_tiles * TM
    n_assign = 2 * n_tok
    expert = route[:, 0:2].astype(jnp.int32).reshape(-1)
    onehot = (expert[:, None] == jnp.arange(N_EXPERTS, dtype=jnp.int32)[None, :]).astype(jnp.int32)
    csum = jnp.cumsum(onehot, axis=0)
    counts = csum[-1]
    rank = jnp.sum(onehot * (csum - 1), axis=1)
    padded = (counts + MOE_RB - 1) // MOE_RB * MOE_RB
    pad_end = jnp.cumsum(padded)
    dest = (pad_end - padded)[expert] + rank
    n_blocks = n_assign // MOE_RB + N_EXPERTS
    token = jnp.arange(n_assign, dtype=jnp.int32) // 2
    slot_tok = jnp.zeros((n_blocks * MOE_RB,), jnp.int32).at[dest].set(token)
    blk_exp = jnp.minimum(
        jnp.searchsorted(pad_end, jnp.arange(n_blocks, dtype=jnp.int32) * MOE_RB, side="right"),
        N_EXPERTS - 1).astype(jnp.int32)
    xs = _dispatch(a, slot_tok, n_blocks)
    ys = _experts(xs, blk_exp, w1, w3, w2)
    return _combine(rows, n_tiles, h, modg, route, ys, dest.astype(jnp.int32))


def kernel(x, c, ctx, c_ctx, ada_w, ada_b, norm_mix_g, norm_ffn_g, sc_in_w, sc_conv_w, sc_out_w, da_qkv_w, da_out_w,
           da_q_norm_g, da_k_norm_g, da_lambda, da_sub_norm_g, cm_in_w, cm_in_b, cm_v_norm_g, cm_ws, cm_bs, cm_out_w,
           sw_qkv_w, sw_out_w, sw_q_norm_g, sw_k_norm_g, sw_sink, ffn_w1, ffn_w3, ffn_w2, moe_router_w, moe_router_b,
           moe_w1, moe_w3, moe_w2):
    batch, seq, _ = x.shape
    ctx_len = ctx.shape[1]
    depth = ada_w.shape[0]
    assert depth == 4 and batch + 1 <= 16
    rows = _Rows(batch, seq, ctx_len)

    cvec = jnp.concatenate([c, c_ctx[None, :], jnp.zeros((16 - batch - 1, D), F32)], axis=0)
    mod = _ada_all(cvec, ada_w, ada_b)[:, :batch + 1].reshape(depth, batch + 1, 6, D)
    gains = jnp.stack([norm_mix_g, norm_ffn_g], axis=1)[:, None]
    modg = jnp.concatenate([mod, jnp.broadcast_to(gains, (depth, batch + 1, 2, D))], axis=2)

    bf = lambda w: w.astype(BF16)
    h = jnp.concatenate([x.reshape(-1, D), ctx.reshape(-1, D)], axis=0)
    tables = _rope_tables(rows)

    bg, y = _conv_in(rows, h, modg[0], bf(sc_in_w[0]))
    conv_w = jnp.pad(sc_conv_w[0], ((0, 5), (0, 0)))
    h = _conv_out(rows, h, modg[0], bg, y, conv_w, bf(sc_out_w[0]))
    h = _ffn(rows, rows.all_tiles, h, modg[0], bf(ffn_w1[0]), bf(ffn_w3[0]), bf(ffn_w2[0]))

    nq_chunks = D // CN
    qkv = _qkv(rows, h, modg[1], bf(da_qkv_w[0]), da_q_norm_g[0], da_k_norm_g[0], tables, nq_chunks, nq_chunks)
    lam_init = 0.8 - 0.6 * math.exp(-0.3 * 1)
    o_lat, o_ctx = _diff_attn(rows, qkv, da_lambda[0], da_sub_norm_g[0], lam_init)
    h = _out_proj(rows, rows.all_tiles, h, modg[1], o_lat, o_ctx, bf(da_out_w[0]))
    h = _moe(rows, rows.all_tiles, h, modg[1], moe_router_w[0], moe_router_b[0], bf(moe_w1[0]), bf(moe_w3[0]),
             bf(moe_w2[0]))

    h = _gmlp(rows, h, modg[2], bf(cm_in_w[0]), cm_in_b[0].reshape(1, -1), cm_v_norm_g[0].reshape(1, -1),
              bf(cm_ws[0]), cm_bs[0].T, bf(cm_out_w[0]))
    h = _ffn(rows, rows.all_tiles, h, modg[2], bf(ffn_w1[1]), bf(ffn_w3[1]), bf(ffn_w2[1]))

    kv_chunks = SWA_KV_HEADS * HEAD_DIM // CN
    qkv = _qkv(rows, h, modg[3], bf(sw_qkv_w[0]), sw_q_norm_g[0], sw_k_norm_g[0], tables, nq_chunks, kv_chunks)
    o = _swa(rows, qkv, sw_sink[0])
    h = _out_proj(rows, rows.lat_tiles, h, modg[3], o, o, bf(sw_out_w[0]))
    h = _moe(rows, rows.lat_tiles, h, modg[3], moe_router_w[1], moe_router_b[1], bf(moe_w1[1]), bf(moe_w3[1]),
             bf(moe_w2[1]))
    return h.reshape(batch, seq, D)
```

```python
import functools
import math

import jax
import jax.numpy as jnp
from jax import lax
from jax.experimental import pallas as pl
from jax.experimental.pallas import tpu as pltpu

D = 1024
HEAD_DIM = 64
GRID_W = 64
ROPE_HALF = HEAD_DIM // 2
ROPE_BASE = 10000.0
DIFF_HEADS = D // (2 * HEAD_DIM)
SWA_Q_HEADS = D // HEAD_DIM
SWA_KV_HEADS = 4
SWA_GROUP = SWA_Q_HEADS // SWA_KV_HEADS
SWA_WINDOW = 128
CHUNK = 128
CM_WIDTH = 2 * D
CM_GROUPS = 8
CM_GW = CM_WIDTH // CM_GROUPS
FFN_DIM = 2816
N_EXPERTS = 8
EPS = 1e-6

F32 = jnp.float32
BF16 = jnp.bfloat16
HIGHEST = lax.Precision.HIGHEST
LOG2E = math.log2(math.e)

TM = 512
CN = 256
MOE_RB = 512
ATT_TQ = 512
ATT_TK = 1024
SWA_TQ = 256
ROUTE_W = 128
VMEM_LIMIT = 56 << 20


def _params(n_grid):
    return pltpu.CompilerParams(dimension_semantics=("arbitrary",) * n_grid, vmem_limit_bytes=VMEM_LIMIT)


def _resident(shape):
    zeros = (0,) * len(shape)
    return pl.BlockSpec(shape, lambda *_: zeros, pipeline_mode=pl.Buffered(1))


def _sigmoid(x):
    return 1.0 / (1.0 + jnp.exp(-x))


def _gelu_tanh(x):
    return 0.5 * x * (1.0 + jnp.tanh(math.sqrt(2.0 / math.pi) * (x + 0.044715 * (x * x * x))))


def _modnorm(x, g, shift, scale):
    y = x * lax.rsqrt(jnp.mean(x * x, axis=-1, keepdims=True) + EPS)
    return (y * g) * (1.0 + scale) + shift


def _dot(a, b):
    return jnp.dot(a, b, preferred_element_type=F32)


def _mix_mod(m):
    return m[6:7], m[0:1], m[1:2]


def _ffn_mod(m):
    return m[7:8], m[3:4], m[4:5]


def _ada_kernel(c_ref, w_ref, b_ref, o_ref):
    c = c_ref[...]
    s = c * _sigmoid(c)
    o_ref[0] = jnp.dot(s, w_ref[0], precision=HIGHEST, preferred_element_type=F32) + b_ref[0]


def _ada_all(cvec, ada_w, ada_b):
    depth, _, n = ada_w.shape
    tn = 1536
    rows = cvec.shape[0]
    return pl.pallas_call(
        _ada_kernel,
        grid=(depth, n // tn),
        in_specs=[pl.BlockSpec((rows, D), lambda l, j: (0, 0)),
                  pl.BlockSpec((1, D, tn), lambda l, j: (l, 0, j)),
                  pl.BlockSpec((1, 1, tn), lambda l, j: (l, 0, j))],
        out_specs=pl.BlockSpec((1, rows, tn), lambda l, j: (l, 0, j)),
        out_shape=jax.ShapeDtypeStruct((depth, rows, n), F32),
        compiler_params=_params(2),
        name="adaln",
    )(cvec, ada_w, ada_b.reshape(depth, 1, n))


class _Rows:
    def __init__(self, batch, seq, ctx_len):
        self.batch, self.seq, self.ctx_len = batch, seq, ctx_len
        self.n_lat = batch * seq
        self.n_ctx = batch * ctx_len
        self.n_all = self.n_lat + self.n_ctx
        assert seq % TM == 0 and self.n_ctx % TM == 0 and TM % ctx_len == 0
        self.lat_tiles = self.n_lat // TM
        self.all_tiles = self.n_all // TM
        self.tiles_per_seq = seq // TM

    def mod_spec(self):
        tps, batch = self.tiles_per_seq, self.batch
        return pl.BlockSpec((1, 8, D), lambda i, *_: (jnp.minimum(i // tps, batch), 0, 0))

    def row_spec(self, width):
        return pl.BlockSpec((TM, width), lambda i, *_: (i, 0))


def _conv_in_kernel(h_ref, mod_ref, w_ref, bg_ref, y_ref):
    a = _modnorm(h_ref[...], *_mix_mod(mod_ref[0])).astype(BF16)
    for c in range(D // CN):
        lo = c * CN
        bg_ref[:, lo:lo + CN] = _dot(a, w_ref[:, lo:lo + CN]).astype(BF16)
        cg = _dot(a, w_ref[:, D + lo:D + lo + CN])
        xv = _dot(a, w_ref[:, 2 * D + lo:2 * D + lo + CN])
        y_ref[:, lo:lo + CN] = (cg * xv).astype(BF16)


def _conv_in(rows, h, modg, w_in):
    out = jax.ShapeDtypeStruct((rows.n_all, D), BF16)
    return pl.pallas_call(
        _conv_in_kernel,
        grid=(rows.all_tiles,),
        in_specs=[rows.row_spec(D), rows.mod_spec(), _resident((D, 3 * D))],
        out_specs=[rows.row_spec(D), rows.row_spec(D)],
        out_shape=[out, out],
        compiler_params=_params(1),
        name="conv_in",
    )(h, modg, w_in)


HALO = 16


def _conv_out_kernel(n_lat, seq, ctx_len, h_ref, mod_ref, bg_ref, y_ref, yp_ref, yn_ref, cw_ref, w_ref, o_ref):
    i = pl.program_id(0)
    m = mod_ref[0]
    y = y_ref[...].astype(F32)
    row = lax.broadcasted_iota(jnp.int32, (TM, 1), 0)
    grow = row + i * TM
    seq_len = jnp.where(grow < n_lat, seq, ctx_len)
    pos = grow & (seq_len - 1)
    prev_row = yp_ref[HALO - 1:HALO, :].astype(F32)
    next_row = yn_ref[0:1, :].astype(F32)
    y_m1 = jnp.where(row == 0, prev_row, pltpu.roll(y, 1, 0))
    y_m1 = jnp.where(pos == 0, 0.0, y_m1)
    y_p1 = jnp.where(row == TM - 1, next_row, pltpu.roll(y, TM - 1, 0))
    y_p1 = jnp.where(pos == seq_len - 1, 0.0, y_p1)
    conv = cw_ref[0:1, :] * y_m1 + cw_ref[1:2, :] * y + cw_ref[2:3, :] * y_p1
    z = (bg_ref[...].astype(F32) * conv).astype(BF16)
    o_ref[...] = h_ref[...] + m[2:3] * _dot(z, w_ref[...])


def _conv_out(rows, h, modg, bg, y, conv_w, w_out):
    hb = TM // HALO
    last = rows.n_all // HALO - 1
    return pl.pallas_call(
        functools.partial(_conv_out_kernel, rows.n_lat, rows.seq, rows.ctx_len),
        grid=(rows.all_tiles,),
        in_specs=[rows.row_spec(D), rows.mod_spec(), rows.row_spec(D), rows.row_spec(D),
                  pl.BlockSpec((HALO, D), lambda i: (jnp.maximum(i * hb - 1, 0), 0)),
                  pl.BlockSpec((HALO, D), lambda i: (jnp.minimum((i + 1) * hb, last), 0)),
                  _resident((8, D)), _resident((D, D))],
        out_specs=rows.row_spec(D),
        out_shape=jax.ShapeDtypeStruct((rows.n_all, D), F32),
        compiler_params=_params(1),
        name="conv_out",
    )(h, modg, bg, y, y, y, conv_w, w_out)


def _swiglu_into(a, w1_ref, w3_ref, t_ref):
    for c in range(FFN_DIM // CN):
        lo = c * CN
        h1 = _dot(a, w1_ref[:, lo:lo + CN])
        h3 = _dot(a, w3_ref[:, lo:lo + CN])
        t_ref[:, lo:lo + CN] = (h1 * _sigmoid(h1) * h3).astype(BF16)


def _ffn_kernel(h_ref, mod_ref, w1_ref, w3_ref, w2_ref, o_ref, t_ref):
    m = mod_ref[0]
    x = h_ref[...]
    a = _modnorm(x, *_ffn_mod(m)).astype(BF16)
    _swiglu_into(a, w1_ref, w3_ref, t_ref)
    o_ref[...] = x + m[5:6] * _dot(t_ref[...], w2_ref[...])


def _ffn(rows, n_tiles, h, modg, w1, w3, w2):
    return pl.pallas_call(
        _ffn_kernel,
        grid=(n_tiles,),
        in_specs=[rows.row_spec(D), rows.mod_spec(), _resident((D, FFN_DIM)), _resident((D, FFN_DIM)),
                  _resident((FFN_DIM, D))],
        out_specs=rows.row_spec(D),
        out_shape=jax.ShapeDtypeStruct((n_tiles * TM, D), F32),
        scratch_shapes=[pltpu.VMEM((TM, FFN_DIM), BF16)],
        compiler_params=_params(1),
        name="ffn_dense",
    )(h, modg, w1, w3, w2)


def _head_group_sum(x2, gmat_ref):
    hi = x2.astype(BF16)
    lo = (x2 - hi.astype(F32)).astype(BF16)
    return _dot(hi, gmat_ref[...]) + _dot(lo, gmat_ref[...])


def _norm_rope(x, gain, gmat_ref, cos, sin):
    ms = _head_group_sum(x * x, gmat_ref) * (1.0 / HEAD_DIM)
    y = x * lax.rsqrt(ms + EPS) * gain
    lane = lax.broadcasted_iota(jnp.int32, (1, CN), 1)
    first_half = (lane & (ROPE_HALF - 1)) < (ROPE_HALF // 2)
    partner = jnp.where(first_half, pltpu.roll(y, CN - ROPE_HALF // 2, 1), pltpu.roll(y, ROPE_HALF // 2, 1))
    return y * cos + partner * sin


def _qkv_kernel(n_qk_chunks, n_q_chunks, q_scale, h_ref, mod_ref, w_ref, gmat_ref, qg_ref, kg_ref, cos_ref, sin_ref,
                o_ref):
    a = _modnorm(h_ref[...], *_mix_mod(mod_ref[0])).astype(BF16)
    n_chunks = w_ref.shape[1] // CN
    cos = cos_ref[...]
    sin = sin_ref[...]
    for c in range(n_chunks):
        lo = c * CN
        acc = _dot(a, w_ref[:, lo:lo + CN])
        if c < n_q_chunks:
            acc = _norm_rope(acc, qg_ref[...], gmat_ref, cos, sin) * q_scale
        elif c < n_qk_chunks:
            acc = _norm_rope(acc, kg_ref[...], gmat_ref, cos, sin)
        o_ref[:, lo:lo + CN] = acc.astype(BF16)


def _rope_tables(rows):
    seq = rows.seq
    pos = jnp.arange(seq)
    n_freq = ROPE_HALF // 2
    inv = ROPE_BASE ** (-jnp.arange(n_freq, dtype=F32) / n_freq)
    ang_r = (pos // GRID_W).astype(F32)[:, None] * inv
    ang_c = (pos % GRID_W).astype(F32)[:, None] * inv
    cos = jnp.concatenate([jnp.cos(ang_r)] * 2 + [jnp.cos(ang_c)] * 2, axis=-1)
    sin = jnp.concatenate([-jnp.sin(ang_r), jnp.sin(ang_r), -jnp.sin(ang_c), jnp.sin(ang_c)], axis=-1)
    cos = jnp.concatenate([cos, jnp.ones((TM, HEAD_DIM), F32)], axis=0)
    sin = jnp.concatenate([sin, jnp.zeros((TM, HEAD_DIM), F32)], axis=0)
    reps = CN // HEAD_DIM
    return jnp.tile(cos, (1, reps)), jnp.tile(sin, (1, reps))


def _qkv(rows, h, modg, w, q_gain, k_gain, tables, n_q_chunks, n_k_chunks):
    width = w.shape[1]
    reps = CN // HEAD_DIM
    head = jnp.arange(CN) // HEAD_DIM
    gmat = (head[:, None] == head[None, :]).astype(BF16)
    qg = jnp.tile(q_gain.reshape(1, HEAD_DIM), (1, reps))
    kg = jnp.tile(k_gain.reshape(1, HEAD_DIM), (1, reps))
    tps, lat_tiles = rows.tiles_per_seq, rows.lat_tiles
    tab_spec = pl.BlockSpec((TM, CN), lambda i: (jnp.where(i < lat_tiles, i % tps, tps), 0))
    return pl.pallas_call(
        functools.partial(_qkv_kernel, n_q_chunks + n_k_chunks, n_q_chunks, LOG2E * HEAD_DIM ** -0.5),
        grid=(rows.all_tiles,),
        in_specs=[rows.row_spec(D), rows.mod_spec(), _resident((D, width)), _resident((CN, CN)),
                  _resident((1, CN)), _resident((1, CN)), tab_spec, tab_spec],
        out_specs=rows.row_spec(width),
        out_shape=jax.ShapeDtypeStruct((rows.n_all, width), BF16),
        compiler_params=_params(1),
        name="qkv_proj",
    )(h, modg, w, gmat, qg, kg, tables[0], tables[1])


DH2 = 2 * HEAD_DIM


def _diff_attn_kernel(lam_init, n_lat_chunks, q_ref, kc_ref, vc_ref, kl_ref, vl_ref, lam_ref, subg_ref, o_ref,
                      m_ref, acc_ref, vca_ref, vla_ref):
    def augment():
        vca_ref[:, :DH2] = vc_ref[...]
        vca_ref[:, DH2:] = jnp.ones((vc_ref.shape[0], DH2), BF16)
        if n_lat_chunks:
            vla_ref[:, :DH2] = vl_ref[...]
            vla_ref[:, DH2:] = jnp.ones((vl_ref.shape[0], DH2), BF16)

    if n_lat_chunks:
        pl.when(pl.program_id(2) == 0)(augment)
    else:
        augment()

    q = q_ref[...]
    lane = lax.broadcasted_iota(jnp.int32, (1, DH2), 1)
    zero = jnp.zeros((), BF16)
    qs = (jnp.where(lane < HEAD_DIM, q, zero), jnp.where(lane >= HEAD_DIM, q, zero))
    m_ref[...] = jnp.full(m_ref.shape, -jnp.inf, F32)
    acc_ref[...] = jnp.zeros(acc_ref.shape, F32)

    def update(k, va):
        reps = k.shape[0] // DH2
        for mi in range(2):
            s = lax.dot_general(qs[mi], k, (((1,), (1,)), ((), ())), preferred_element_type=F32)
            m_old = m_ref[mi]
            m_new = jnp.maximum(m_old, jnp.max(s, axis=-1, keepdims=True))
            alpha = jnp.exp2(m_old - m_new)
            p = jnp.exp2(s - jnp.concatenate([m_new] * reps, axis=-1))
            acc_ref[mi] = jnp.concatenate([alpha, alpha], axis=-1) * acc_ref[mi] + _dot(p.astype(BF16), va)
            m_ref[mi] = m_new

    update(kc_ref[...], vca_ref[...])
    if n_lat_chunks:
        tk = kl_ref.shape[0] // n_lat_chunks

        def body(c, carry):
            start = pl.multiple_of(c * tk, tk)
            update(kl_ref[pl.ds(start, tk), :], vla_ref[pl.ds(start, tk), :])
            return carry
        lax.fori_loop(0, n_lat_chunks, body, 0)

    lp = lam_ref[...]
    lam = (jnp.exp(jnp.sum(lp[0:1] * lp[1:2], axis=-1, keepdims=True))
           - jnp.exp(jnp.sum(lp[2:3] * lp[3:4], axis=-1, keepdims=True)) + lam_init)
    acc0 = acc_ref[0]
    acc1 = acc_ref[1]
    o = acc0[:, :DH2] / acc0[:, DH2:] - lam * (acc1[:, :DH2] / acc1[:, DH2:])
    o = o * lax.rsqrt(jnp.mean(o * o, axis=-1, keepdims=True) + EPS) * subg_ref[...]
    o_ref[...] = (o * (1.0 - lam_init)).astype(BF16)


def _diff_attn(rows, qkv, lam_p, sub_g, lam_init):
    batch, seq, ctx_len = rows.batch, rows.seq, rows.ctx_len
    nq = seq // ATT_TQ
    nh = DIFF_HEADS
    ctx0 = rows.n_lat // ctx_len
    sub_g = sub_g.reshape(1, DH2)
    small = [_resident((4, HEAD_DIM)), _resident((1, DH2))]
    kc_spec = pl.BlockSpec((ctx_len, DH2), lambda b, h, *_: (ctx0 + b, nh + h))
    vc_spec = pl.BlockSpec((ctx_len, DH2), lambda b, h, *_: (ctx0 + b, 2 * nh + h))

    def scratch(tq, n_lat_keys):
        return [pltpu.VMEM((2, tq, DH2), F32), pltpu.VMEM((2, tq, 2 * DH2), F32),
                pltpu.VMEM((ctx_len, 2 * DH2), BF16), pltpu.VMEM((n_lat_keys, 2 * DH2), BF16)]

    o_lat = pl.pallas_call(
        functools.partial(_diff_attn_kernel, lam_init, pl.cdiv(seq, ATT_TK)),
        grid=(batch, nh, nq),
        in_specs=[pl.BlockSpec((ATT_TQ, DH2), lambda b, h, i: (b * nq + i, h)),
                  kc_spec, vc_spec,
                  pl.BlockSpec((seq, DH2), lambda b, h, i: (b, nh + h)),
                  pl.BlockSpec((seq, DH2), lambda b, h, i: (b, 2 * nh + h))] + small,
        out_specs=pl.BlockSpec((ATT_TQ, DH2), lambda b, h, i: (b * nq + i, h)),
        out_shape=jax.ShapeDtypeStruct((rows.n_lat, D), BF16),
        scratch_shapes=scratch(ATT_TQ, seq),
        compiler_params=_params(3),
        name="diff_attn_latent",
    )(qkv, qkv, qkv, qkv, qkv, lam_p, sub_g)

    def ctx_kernel(q_ref, kc_ref, vc_ref, lam_ref, subg_ref, o_ref, m_ref, acc_ref, vca_ref, vla_ref):
        _diff_attn_kernel(lam_init, 0, q_ref, kc_ref, vc_ref, None, None, lam_ref, subg_ref, o_ref, m_ref, acc_ref,
                          vca_ref, vla_ref)

    o_ctx = pl.pallas_call(
        ctx_kernel,
        grid=(batch, nh),
        in_specs=[pl.BlockSpec((ctx_len, DH2), lambda b, h: (ctx0 + b, h)), kc_spec, vc_spec] + small,
        out_specs=pl.BlockSpec((ctx_len, DH2), lambda b, h: (b, h)),
        out_shape=jax.ShapeDtypeStruct((rows.n_ctx, D), BF16),
        scratch_shapes=scratch(ctx_len, 16),
        compiler_params=_params(2),
        name="diff_attn_context",
    )(qkv, qkv, qkv, lam_p, sub_g)
    return o_lat, o_ctx


def _out_proj_kernel(lat_tiles, h_ref, mod_ref, zl_ref, zc_ref, w_ref, o_ref):
    i = pl.program_id(0)

    def project(z_ref):
        o_ref[...] = h_ref[...] + mod_ref[0][2:3] * _dot(z_ref[...], w_ref[...])

    pl.when(i < lat_tiles)(lambda: project(zl_ref))
    pl.when(i >= lat_tiles)(lambda: project(zc_ref))


def _out_proj(rows, n_tiles, h, modg, z_lat, z_ctx, w):
    lat_tiles = rows.lat_tiles
    return pl.pallas_call(
        functools.partial(_out_proj_kernel, lat_tiles),
        grid=(n_tiles,),
        in_specs=[rows.row_spec(D), rows.mod_spec(),
                  pl.BlockSpec((TM, D), lambda i: (jnp.minimum(i, lat_tiles - 1), 0)),
                  pl.BlockSpec((TM, D), lambda i: (jnp.maximum(i - lat_tiles, 0), 0)),
                  _resident((D, D))],
        out_specs=rows.row_spec(D),
        out_shape=jax.ShapeDtypeStruct((n_tiles * TM, D), F32),
        compiler_params=_params(1),
        name="out_proj",
    )(h, modg, z_lat, z_ctx, w)


def _gmlp_kernel(h_ref, mod_ref, win_ref, bin_ref, vg_ref, ws_ref, bs_ref, wout_ref, o_ref, u_ref, v_ref, t_ref):
    m = mod_ref[0]
    x = h_ref[...]
    a = _modnorm(x, *_mix_mod(m)).astype(BF16)
    ssq = jnp.zeros((TM, 1), F32)
    for c in range(2 * CM_WIDTH // CN):
        lo = c * CN
        z = _gelu_tanh(_dot(a, win_ref[:, lo:lo + CN]) + bin_ref[:, lo:lo + CN])
        if lo < CM_WIDTH:
            u_ref[:, lo:lo + CN] = z
        else:
            v_ref[:, lo - CM_WIDTH:lo - CM_WIDTH + CN] = z
            ssq = ssq + jnp.sum(z * z, axis=-1, keepdims=True)
    inv = lax.rsqrt(ssq * (1.0 / CM_WIDTH) + EPS)
    for g in range(CM_GROUPS):
        lo = g * CM_GW
        vn = (v_ref[:, lo:lo + CM_GW] * inv * vg_ref[:, lo:lo + CM_GW]).astype(BF16)
        for r in range(TM // CHUNK):
            r0 = r * CHUNK
            sv = _dot(ws_ref[g], vn[r0:r0 + CHUNK]) + bs_ref[:, g:g + 1]
            t_ref[r0:r0 + CHUNK, lo:lo + CM_GW] = (u_ref[r0:r0 + CHUNK, lo:lo + CM_GW] * sv).astype(BF16)
    o_ref[...] = x + m[2:3] * _dot(t_ref[...], wout_ref[...])


def _gmlp(rows, h, modg, w_in, b_in, v_g, w_s, b_s, w_out):
    return pl.pallas_call(
        _gmlp_kernel,
        grid=(rows.all_tiles,),
        in_specs=[rows.row_spec(D), rows.mod_spec(), _resident((D, 2 * CM_WIDTH)), _resident((1, 2 * CM_WIDTH)),
                  _resident((1, CM_WIDTH)), _resident((CM_GROUPS, CHUNK, CHUNK)), _resident((CHUNK, CM_GROUPS)),
                  _resident((CM_WIDTH, D))],
        out_specs=rows.row_spec(D),
        out_shape=jax.ShapeDtypeStruct((rows.n_all, D), F32),
        scratch_shapes=[pltpu.VMEM((TM, CM_WIDTH), F32), pltpu.VMEM((TM, CM_WIDTH), F32),
                        pltpu.VMEM((TM, CM_WIDTH), BF16)],
        compiler_params=_params(1),
        name="gmlp",
    )(h, modg, w_in, b_in, v_g, w_s, b_s, w_out)


SWA_BAND = SWA_TQ + 2 * SWA_WINDOW


def _swa_kernel(seq, sink_ref, q_ref, kc_ref, vc_ref, kl_ref, vl_ref, o_ref):
    qi = pl.program_id(1)
    q0 = qi * SWA_TQ
    start = jnp.clip(q0 - SWA_WINDOW, 0, seq - SWA_BAND)
    start = pl.multiple_of(start, SWA_WINDOW)
    kb = kl_ref[pl.ds(start, SWA_BAND), :]
    vb = vl_ref[pl.ds(start, SWA_BAND), :]
    kc = kc_ref[...]
    vc = vc_ref[...]
    qpos = q0 + lax.broadcasted_iota(jnp.int32, (SWA_TQ, SWA_BAND), 0)
    kpos = start + lax.broadcasted_iota(jnp.int32, (SWA_TQ, SWA_BAND), 1)
    in_band = jnp.abs(qpos - kpos) <= SWA_WINDOW
    nt = (((1,), (1,)), ((), ()))
    for j in range(SWA_KV_HEADS):
        kj = slice(j * HEAD_DIM, (j + 1) * HEAD_DIM)
        kbj, vbj, kcj, vcj = kb[:, kj], vb[:, kj], kc[:, kj], vc[:, kj]
        outs = []
        for g in range(SWA_GROUP):
            hq = j * SWA_GROUP + g
            qh = q_ref[:, hq * HEAD_DIM:(hq + 1) * HEAD_DIM]
            s_c = lax.dot_general(qh, kcj, nt, preferred_element_type=F32)
            s_b = lax.dot_general(qh, kbj, nt, preferred_element_type=F32)
            s_b = jnp.where(in_band, s_b, -jnp.inf)
            sink = sink_ref[hq] * LOG2E
            mx = jnp.maximum(jnp.maximum(jnp.max(s_c, axis=-1, keepdims=True),
                                         jnp.max(s_b, axis=-1, keepdims=True)), sink)
            p_c = jnp.exp2(s_c - mx)
            p_b = jnp.exp2(s_b - mx)
            denom = (jnp.sum(p_c, axis=-1, keepdims=True) + jnp.sum(p_b, axis=-1, keepdims=True)
                     + jnp.exp2(sink - mx))
            o = _dot(p_c.astype(BF16), vcj) + _dot(p_b.astype(BF16), vbj)
            outs.append(o / denom)
        lo = j * SWA_GROUP * HEAD_DIM
        o_ref[:, lo:lo + SWA_GROUP * HEAD_DIM] = jnp.concatenate(outs, axis=-1).astype(BF16)


def _swa(rows, qkv, sink):
    batch, seq, ctx_len = rows.batch, rows.seq, rows.ctx_len
    nq = seq // SWA_TQ
    kvw = SWA_KV_HEADS * HEAD_DIM
    k_col = D // kvw
    ctx0 = rows.n_lat // ctx_len
    return pl.pallas_call(
        functools.partial(_swa_kernel, seq),
        grid=(batch, nq),
        in_specs=[pl.BlockSpec(memory_space=pltpu.SMEM),
                  pl.BlockSpec((SWA_TQ, D), lambda b, i: (b * nq + i, 0)),
                  pl.BlockSpec((ctx_len, kvw), lambda b, i: (ctx0 + b, k_col)),
                  pl.BlockSpec((ctx_len, kvw), lambda b, i: (ctx0 + b, k_col + 1)),
                  pl.BlockSpec((seq, kvw), lambda b, i: (b, k_col)),
                  pl.BlockSpec((seq, kvw), lambda b, i: (b, k_col + 1))],
        out_specs=pl.BlockSpec((SWA_TQ, D), lambda b, i: (b * nq + i, 0)),
        out_shape=jax.ShapeDtypeStruct((rows.n_lat, D), BF16),
        compiler_params=_params(2),
        name="swa_attn",
    )(sink, qkv, qkv, qkv, qkv, qkv)


def _router_kernel(h_ref, mod_ref, wr_ref, br_ref, a_ref, route_ref):
    a = _modnorm(h_ref[...], *_ffn_mod(mod_ref[0]))
    a_ref[...] = a
    logits = jnp.dot(a, wr_ref[...], precision=HIGHEST, preferred_element_type=F32) + br_ref[...]
    lane = lax.broadcasted_iota(jnp.int32, (TM, ROUTE_W), 1)
    m1 = jnp.max(logits, axis=-1, keepdims=True)
    i1 = jnp.min(jnp.where(logits == m1, lane, ROUTE_W), axis=-1, keepdims=True)
    rest = jnp.where(lane == i1, -jnp.inf, logits)
    m2 = jnp.max(rest, axis=-1, keepdims=True)
    i2 = jnp.min(jnp.where(rest == m2, lane, ROUTE_W), axis=-1, keepdims=True)
    e2 = jnp.exp(m2 - m1)
    gate1 = 1.0 / (1.0 + e2)
    gate2 = e2 / (1.0 + e2)
    rec = jnp.where(lane == 0, i1.astype(F32), 0.0)
    rec = jnp.where(lane == 1, i2.astype(F32), rec)
    rec = jnp.where(lane == 2, gate1, rec)
    route_ref[...] = jnp.where(lane == 3, gate2, rec)


def _router(rows, n_tiles, h, modg, w_r, b_r):
    pad = ROUTE_W - N_EXPERTS
    w_pad = jnp.pad(w_r, ((0, 0), (0, pad)))
    b_pad = jnp.concatenate([b_r.astype(F32), jnp.full((pad,), -1e30, F32)]).reshape(1, ROUTE_W)
    return pl.pallas_call(
        _router_kernel,
        grid=(n_tiles,),
        in_specs=[rows.row_spec(D), rows.mod_spec(), _resident((D, ROUTE_W)), _resident((1, ROUTE_W))],
        out_specs=[rows.row_spec(D), rows.row_spec(ROUTE_W)],
        out_shape=[jax.ShapeDtypeStruct((n_tiles * TM, D), F32), jax.ShapeDtypeStruct((n_tiles * TM, ROUTE_W), F32)],
        compiler_params=_params(1),
        name="moe_router",
    )(h, modg, w_pad, b_pad)


def _row_copy(src_ref, src_row, dst_ref, dst_row, sem):
    return pltpu.make_async_copy(src_ref.at[pl.ds(src_row, 1)], dst_ref.at[pl.ds(dst_row, 1)], sem)


def _dispatch_kernel(tok_ref, a_hbm, xs_ref, sem):
    base = pl.program_id(0) * MOE_RB

    def issue(r, carry):
        _row_copy(a_hbm, tok_ref[base + r], xs_ref, r, sem).start()
        return carry

    lax.fori_loop(0, MOE_RB, issue, 0)
    pltpu.make_async_copy(a_hbm.at[pl.ds(0, MOE_RB)], xs_ref, sem).wait()


def _dispatch(a, slot_tok, n_blocks):
    return pl.pallas_call(
        _dispatch_kernel,
        grid_spec=pltpu.PrefetchScalarGridSpec(
            num_scalar_prefetch=1,
            grid=(n_blocks,),
            in_specs=[pl.BlockSpec(memory_space=pl.ANY)],
            out_specs=pl.BlockSpec((MOE_RB, D), lambda i, tok: (i, 0)),
            scratch_shapes=[pltpu.SemaphoreType.DMA(())],
        ),
        out_shape=jax.ShapeDtypeStruct((n_blocks * MOE_RB, D), F32),
        compiler_params=_params(1),
        name="moe_dispatch",
    )(slot_tok, a)


def _expert_kernel(be_ref, xs_ref, w1_ref, w3_ref, w2_ref, ys_ref, t_ref):
    del be_ref
    a = xs_ref[...].astype(BF16)
    _swiglu_into(a, w1_ref, w3_ref, t_ref)
    ys_ref[...] = _dot(t_ref[...], w2_ref[...])


def _experts(xs, blk_exp, w1, w3, w2):
    n_blocks = blk_exp.shape[0]
    return pl.pallas_call(
        _expert_kernel,
        grid_spec=pltpu.PrefetchScalarGridSpec(
            num_scalar_prefetch=1,
            grid=(n_blocks,),
            in_specs=[pl.BlockSpec((MOE_RB, D), lambda i, be: (i, 0)),
                      pl.BlockSpec((None, D, FFN_DIM), lambda i, be: (be[i], 0, 0)),
                      pl.BlockSpec((None, D, FFN_DIM), lambda i, be: (be[i], 0, 0)),
                      pl.BlockSpec((None, FFN_DIM, D), lambda i, be: (be[i], 0, 0))],
            out_specs=pl.BlockSpec((MOE_RB, D), lambda i, be: (i, 0)),
            scratch_shapes=[pltpu.VMEM((MOE_RB, FFN_DIM), BF16)],
        ),
        out_shape=jax.ShapeDtypeStruct((n_blocks * MOE_RB, D), F32),
        compiler_params=_params(1),
        name="moe_experts",
    )(blk_exp, xs, w1, w3, w2)


def _combine_kernel(dest_ref, h_ref, mod_ref, route_ref, ys_hbm, o_ref, y1_ref, y2_ref, sem):
    base = pl.program_id(0) * (2 * TM)

    def issue(r, carry):
        _row_copy(ys_hbm, dest_ref[base + 2 * r], y1_ref, r, sem).start()
        _row_copy(ys_hbm, dest_ref[base + 2 * r + 1], y2_ref, r, sem).start()
        return carry

    lax.fori_loop(0, TM, issue, 0)
    pltpu.make_async_copy(ys_hbm.at[pl.ds(0, TM)], y1_ref, sem).wait()
    pltpu.make_async_copy(ys_hbm.at[pl.ds(0, TM)], y2_ref, sem).wait()
    route = route_ref[...]
    mix = route[:, 2:3] * y1_ref[...] + route[:, 3:4] * y2_ref[...]
    o_ref[...] = h_ref[...] + mod_ref[0][5:6] * mix


def _combine(rows, n_tiles, h, modg, route, ys, dest):
    tps, batch = rows.tiles_per_seq, rows.batch
    return pl.pallas_call(
        _combine_kernel,
        grid_spec=pltpu.PrefetchScalarGridSpec(
            num_scalar_prefetch=1,
            grid=(n_tiles,),
            in_specs=[pl.BlockSpec((TM, D), lambda i, d: (i, 0)),
                      pl.BlockSpec((1, 8, D), lambda i, d: (jnp.minimum(i // tps, batch), 0, 0)),
                      pl.BlockSpec((TM, ROUTE_W), lambda i, d: (i, 0)),
                      pl.BlockSpec(memory_space=pl.ANY)],
            out_specs=pl.BlockSpec((TM, D), lambda i, d: (i, 0)),
            scratch_shapes=[pltpu.VMEM((TM, D), F32), pltpu.VMEM((TM, D), F32), pltpu.SemaphoreType.DMA(())],
        ),
        out_shape=jax.ShapeDtypeStruct((n_tiles * TM, D), F32),
        compiler_params=_params(1),
        name="moe_combine",
    )(dest, h, modg, route, ys)


def _moe(rows, n_tiles, h, modg, w_r, b_r, w1, w3, w2):
    a, route = _router(rows, n_tiles, h, modg, w_r, b_r)
    n_tok = n_tiles * TM
    n_assign = 2 * n_tok
    expert = route[:, 0:2].astype(jnp.int32).reshape(-1)
    onehot = (expert[:, None] == jnp.arange(N_EXPERTS, dtype=jnp.int32)[None, :]).astype(jnp.int32)
    csum = jnp.cumsum(onehot, axis=0)
    counts = csum[-1]
    rank = jnp.sum(onehot * (csum - 1), axis=1)
    padded = (counts + MOE_RB - 1) // MOE_RB * MOE_RB
    pad_end = jnp.cumsum(padded)
    dest = (pad_end - padded)[expert] + rank
    n_blocks = n_assign // MOE_RB + N_EXPERTS
    token = jnp.arange(n_assign, dtype=jnp.int32) // 2
    slot_tok = jnp.zeros((n_blocks * MOE_RB,), jnp.int32).at[dest].set(token)
    blk_exp = jnp.minimum(
        jnp.searchsorted(pad_end, jnp.arange(n_blocks, dtype=jnp.int32) * MOE_RB, side="right"),
        N_EXPERTS - 1).astype(jnp.int32)
    xs = _dispatch(a, slot_tok, n_blocks)
    ys = _experts(xs, blk_exp, w1, w3, w2)
    return _combine(rows, n_tiles, h, modg, route, ys, dest.astype(jnp.int32))


def kernel(x, c, ctx, c_ctx, ada_w, ada_b, norm_mix_g, norm_ffn_g, sc_in_w, sc_conv_w, sc_out_w, da_qkv_w, da_out_w,
           da_q_norm_g, da_k_norm_g, da_lambda, da_sub_norm_g, cm_in_w, cm_in_b, cm_v_norm_g, cm_ws, cm_bs, cm_out_w,
           sw_qkv_w, sw_out_w, sw_q_norm_g, sw_k_norm_g, sw_sink, ffn_w1, ffn_w3, ffn_w2, moe_router_w, moe_router_b,
           moe_w1, moe_w3, moe_w2):
    batch, seq, _ = x.shape
    ctx_len = ctx.shape[1]
    depth = ada_w.shape[0]
    assert depth == 4 and batch + 1 <= 16
    rows = _Rows(batch, seq, ctx_len)

    cvec = jnp.concatenate([c, c_ctx[None, :], jnp.zeros((16 - batch - 1, D), F32)], axis=0)
    mod = _ada_all(cvec, ada_w, ada_b)[:, :batch + 1].reshape(depth, batch + 1, 6, D)
    gains = jnp.stack([norm_mix_g, norm_ffn_g], axis=1)[:, None]
    modg = jnp.concatenate([mod, jnp.broadcast_to(gains, (depth, batch + 1, 2, D))], axis=2)

    bf = lambda w: w.astype(BF16)
    h = jnp.concatenate([x.reshape(-1, D), ctx.reshape(-1, D)], axis=0)
    tables = _rope_tables(rows)

    bg, y = _conv_in(rows, h, modg[0], bf(sc_in_w[0]))
    conv_w = jnp.pad(sc_conv_w[0], ((0, 5), (0, 0)))
    h = _conv_out(rows, h, modg[0], bg, y, conv_w, bf(sc_out_w[0]))
    h = _ffn(rows, rows.all_tiles, h, modg[0], bf(ffn_w1[0]), bf(ffn_w3[0]), bf(ffn_w2[0]))

    nq_chunks = D // CN
    qkv = _qkv(rows, h, modg[1], bf(da_qkv_w[0]), da_q_norm_g[0], da_k_norm_g[0], tables, nq_chunks, nq_chunks)
    lam_init = 0.8 - 0.6 * math.exp(-0.3 * 1)
    o_lat, o_ctx = _diff_attn(rows, qkv, da_lambda[0], da_sub_norm_g[0], lam_init)
    h = _out_proj(rows, rows.all_tiles, h, modg[1], o_lat, o_ctx, bf(da_out_w[0]))
    h = _moe(rows, rows.all_tiles, h, modg[1], moe_router_w[0], moe_router_b[0], bf(moe_w1[0]), bf(moe_w3[0]),
             bf(moe_w2[0]))

    h = _gmlp(rows, h, modg[2], bf(cm_in_w[0]), cm_in_b[0].reshape(1, -1), cm_v_norm_g[0].reshape(1, -1),
              bf(cm_ws[0]), cm_bs[0].T, bf(cm_out_w[0]))
    h = _ffn(rows, rows.all_tiles, h, modg[2], bf(ffn_w1[1]), bf(ffn_w3[1]), bf(ffn_w2[1]))

    kv_chunks = SWA_KV_HEADS * HEAD_DIM // CN
    qkv = _qkv(rows, h, modg[3], bf(sw_qkv_w[0]), sw_q_norm_g[0], sw_k_norm_g[0], tables, nq_chunks, kv_chunks)
    o = _swa(rows, qkv, sw_sink[0])
    h = _out_proj(rows, rows.lat_tiles, h, modg[3], o, o, bf(sw_out_w[0]))
    h = _moe(rows, rows.lat_tiles, h, modg[3], moe_router_w[1], moe_router_b[1], bf(moe_w1[1]), bf(moe_w3[1]),
             bf(moe_w2[1]))
    return h.reshape(batch, seq, D)
```

```python
import functools
import math

import jax
import jax.numpy as jnp
from jax import lax
from jax.experimental import pallas as pl
from jax.experimental.pallas import tpu as pltpu

D = 1024
HEAD_DIM = 64
GRID_W = 64
ROPE_HALF = HEAD_DIM // 2
ROPE_BASE = 10000.0
DIFF_HEADS = D // (2 * HEAD_DIM)
SWA_Q_HEADS = D // HEAD_DIM
SWA_KV_HEADS = 4
SWA_GROUP = SWA_Q_HEADS // SWA_KV_HEADS
SWA_WINDOW = 128
CHUNK = 128
CM_WIDTH = 2 * D
CM_GROUPS = 8
CM_GW = CM_WIDTH // CM_GROUPS
FFN_DIM = 2816
N_EXPERTS = 8
EPS = 1e-6

F32 = jnp.float32
BF16 = jnp.bfloat16
HIGHEST = lax.Precision.HIGHEST
LOG2E = math.log2(math.e)

TM = 512
CN = 256
MOE_RB = 512
ATT_TQ = 512
ATT_TK = 1024
SWA_TQ = 256
ROUTE_W = 128
MAX_UNSHIFTED_LOGIT = 60.0
VMEM_LIMIT = 56 << 20


def _params(n_grid):
    return pltpu.CompilerParams(dimension_semantics=("arbitrary",) * n_grid, vmem_limit_bytes=VMEM_LIMIT)


def _resident(shape):
    zeros = (0,) * len(shape)
    return pl.BlockSpec(shape, lambda *_: zeros, pipeline_mode=pl.Buffered(1))


def _sigmoid(x):
    return 1.0 / (1.0 + jnp.exp(-x))


def _gelu_tanh(x):
    return 0.5 * x * (1.0 + jnp.tanh(math.sqrt(2.0 / math.pi) * (x + 0.044715 * (x * x * x))))


def _modnorm(x, g, shift, scale):
    y = x * lax.rsqrt(jnp.mean(x * x, axis=-1, keepdims=True) + EPS)
    return (y * g) * (1.0 + scale) + shift


def _dot(a, b):
    return jnp.dot(a, b, preferred_element_type=F32)


def _mix_mod(m):
    return m[6:7], m[0:1], m[1:2]


def _ffn_mod(m):
    return m[7:8], m[3:4], m[4:5]


def _ada_kernel(c_ref, w_ref, b_ref, o_ref):
    c = c_ref[...]
    s = c * _sigmoid(c)
    o_ref[0] = jnp.dot(s, w_ref[0], precision=HIGHEST, preferred_element_type=F32) + b_ref[0]


def _ada_all(cvec, ada_w, ada_b):
    depth, _, n = ada_w.shape
    tn = 1536
    rows = cvec.shape[0]
    return pl.pallas_call(
        _ada_kernel,
        grid=(depth, n // tn),
        in_specs=[pl.BlockSpec((rows, D), lambda l, j: (0, 0)),
                  pl.BlockSpec((1, D, tn), lambda l, j: (l, 0, j)),
                  pl.BlockSpec((1, 1, tn), lambda l, j: (l, 0, j))],
        out_specs=pl.BlockSpec((1, rows, tn), lambda l, j: (l, 0, j)),
        out_shape=jax.ShapeDtypeStruct((depth, rows, n), F32),
        compiler_params=_params(2),
        name="adaln",
    )(cvec, ada_w, ada_b.reshape(depth, 1, n))


class _Rows:
    def __init__(self, batch, seq, ctx_len):
        self.batch, self.seq, self.ctx_len = batch, seq, ctx_len
        self.n_lat = batch * seq
        self.n_ctx = batch * ctx_len
        self.n_all = self.n_lat + self.n_ctx
        assert seq % TM == 0 and self.n_ctx % TM == 0 and TM % ctx_len == 0
        self.lat_tiles = self.n_lat // TM
        self.all_tiles = self.n_all // TM
        self.tiles_per_seq = seq // TM

    def mod_spec(self):
        tps, batch = self.tiles_per_seq, self.batch
        return pl.BlockSpec((1, 8, D), lambda i, *_: (jnp.minimum(i // tps, batch), 0, 0))

    def row_spec(self, width):
        return pl.BlockSpec((TM, width), lambda i, *_: (i, 0))


def _conv_in_kernel(h_ref, mod_ref, w_ref, bg_ref, y_ref):
    a = _modnorm(h_ref[...], *_mix_mod(mod_ref[0])).astype(BF16)
    for c in range(D // CN):
        lo = c * CN
        bg_ref[:, lo:lo + CN] = _dot(a, w_ref[:, lo:lo + CN]).astype(BF16)
        cg = _dot(a, w_ref[:, D + lo:D + lo + CN])
        xv = _dot(a, w_ref[:, 2 * D + lo:2 * D + lo + CN])
        y_ref[:, lo:lo + CN] = (cg * xv).astype(BF16)


def _conv_in(rows, h, modg, w_in):
    out = jax.ShapeDtypeStruct((rows.n_all, D), BF16)
    return pl.pallas_call(
        _conv_in_kernel,
        grid=(rows.all_tiles,),
        in_specs=[rows.row_spec(D), rows.mod_spec(), _resident((D, 3 * D))],
        out_specs=[rows.row_spec(D), rows.row_spec(D)],
        out_shape=[out, out],
        compiler_params=_params(1),
        name="conv_in",
    )(h, modg, w_in)


HALO = 16


def _conv_out_kernel(n_lat, seq, ctx_len, h_ref, mod_ref, bg_ref, y_ref, yp_ref, yn_ref, cw_ref, w_ref, o_ref):
    i = pl.program_id(0)
    m = mod_ref[0]
    y = y_ref[...].astype(F32)
    row = lax.broadcasted_iota(jnp.int32, (TM, 1), 0)
    grow = row + i * TM
    seq_len = jnp.where(grow < n_lat, seq, ctx_len)
    pos = grow & (seq_len - 1)
    prev_row = yp_ref[HALO - 1:HALO, :].astype(F32)
    next_row = yn_ref[0:1, :].astype(F32)
    y_m1 = jnp.where(row == 0, prev_row, pltpu.roll(y, 1, 0))
    y_m1 = jnp.where(pos == 0, 0.0, y_m1)
    y_p1 = jnp.where(row == TM - 1, next_row, pltpu.roll(y, TM - 1, 0))
    y_p1 = jnp.where(pos == seq_len - 1, 0.0, y_p1)
    conv = cw_ref[0:1, :] * y_m1 + cw_ref[1:2, :] * y + cw_ref[2:3, :] * y_p1
    z = (bg_ref[...].astype(F32) * conv).astype(BF16)
    o_ref[...] = h_ref[...] + m[2:3] * _dot(z, w_ref[...])


def _conv_out(rows, h, modg, bg, y, conv_w, w_out):
    hb = TM // HALO
    last = rows.n_all // HALO - 1
    return pl.pallas_call(
        functools.partial(_conv_out_kernel, rows.n_lat, rows.seq, rows.ctx_len),
        grid=(rows.all_tiles,),
        in_specs=[rows.row_spec(D), rows.mod_spec(), rows.row_spec(D), rows.row_spec(D),
                  pl.BlockSpec((HALO, D), lambda i: (jnp.maximum(i * hb - 1, 0), 0)),
                  pl.BlockSpec((HALO, D), lambda i: (jnp.minimum((i + 1) * hb, last), 0)),
                  _resident((8, D)), _resident((D, D))],
        out_specs=rows.row_spec(D),
        out_shape=jax.ShapeDtypeStruct((rows.n_all, D), F32),
        compiler_params=_params(1),
        name="conv_out",
    )(h, modg, bg, y, y, y, conv_w, w_out)


def _swiglu_into(a, w1_ref, w3_ref, t_ref):
    for c in range(FFN_DIM // CN):
        lo = c * CN
        h1 = _dot(a, w1_ref[:, lo:lo + CN])
        h3 = _dot(a, w3_ref[:, lo:lo + CN])
        t_ref[:, lo:lo + CN] = (h1 * _sigmoid(h1) * h3).astype(BF16)


def _ffn_kernel(h_ref, mod_ref, w1_ref, w3_ref, w2_ref, o_ref, t_ref):
    m = mod_ref[0]
    x = h_ref[...]
    a = _modnorm(x, *_ffn_mod(m)).astype(BF16)
    _swiglu_into(a, w1_ref, w3_ref, t_ref)
    o_ref[...] = x + m[5:6] * _dot(t_ref[...], w2_ref[...])


def _ffn(rows, n_tiles, h, modg, w1, w3, w2):
    return pl.pallas_call(
        _ffn_kernel,
        grid=(n_tiles,),
        in_specs=[rows.row_spec(D), rows.mod_spec(), _resident((D, FFN_DIM)), _resident((D, FFN_DIM)),
                  _resident((FFN_DIM, D))],
        out_specs=rows.row_spec(D),
        out_shape=jax.ShapeDtypeStruct((n_tiles * TM, D), F32),
        scratch_shapes=[pltpu.VMEM((TM, FFN_DIM), BF16)],
        compiler_params=_params(1),
        name="ffn_dense",
    )(h, modg, w1, w3, w2)


def _head_group_sum(x2, gmat_ref):
    hi = x2.astype(BF16)
    lo = (x2 - hi.astype(F32)).astype(BF16)
    return _dot(hi, gmat_ref[...]) + _dot(lo, gmat_ref[...])


def _norm_rope(x, gain, gmat_ref, cos, sin):
    ms = _head_group_sum(x * x, gmat_ref) * (1.0 / HEAD_DIM)
    y = x * lax.rsqrt(ms + EPS) * gain
    lane = lax.broadcasted_iota(jnp.int32, (1, CN), 1)
    first_half = (lane & (ROPE_HALF - 1)) < (ROPE_HALF // 2)
    partner = jnp.where(first_half, pltpu.roll(y, CN - ROPE_HALF // 2, 1), pltpu.roll(y, ROPE_HALF // 2, 1))
    return y * cos + partner * sin


def _qkv_kernel(n_qk_chunks, n_q_chunks, q_scale, h_ref, mod_ref, w_ref, gmat_ref, qg_ref, kg_ref, cos_ref, sin_ref,
                o_ref):
    a = _modnorm(h_ref[...], *_mix_mod(mod_ref[0])).astype(BF16)
    n_chunks = w_ref.shape[1] // CN
    cos = cos_ref[...]
    sin = sin_ref[...]
    for c in range(n_chunks):
        lo = c * CN
        acc = _dot(a, w_ref[:, lo:lo + CN])
        if c < n_q_chunks:
            acc = _norm_rope(acc, qg_ref[...], gmat_ref, cos, sin) * q_scale
        elif c < n_qk_chunks:
            acc = _norm_rope(acc, kg_ref[...], gmat_ref, cos, sin)
        o_ref[:, lo:lo + CN] = acc.astype(BF16)


def _rope_tables(rows):
    seq = rows.seq
    pos = jnp.arange(seq)
    n_freq = ROPE_HALF // 2
    inv = ROPE_BASE ** (-jnp.arange(n_freq, dtype=F32) / n_freq)
    ang_r = (pos // GRID_W).astype(F32)[:, None] * inv
    ang_c = (pos % GRID_W).astype(F32)[:, None] * inv
    cos = jnp.concatenate([jnp.cos(ang_r)] * 2 + [jnp.cos(ang_c)] * 2, axis=-1)
    sin = jnp.concatenate([-jnp.sin(ang_r), jnp.sin(ang_r), -jnp.sin(ang_c), jnp.sin(ang_c)], axis=-1)
    cos = jnp.concatenate([cos, jnp.ones((TM, HEAD_DIM), F32)], axis=0)
    sin = jnp.concatenate([sin, jnp.zeros((TM, HEAD_DIM), F32)], axis=0)
    reps = CN // HEAD_DIM
    return jnp.tile(cos, (1, reps)), jnp.tile(sin, (1, reps))


def _qkv(rows, h, modg, w, q_gain, k_gain, tables, n_q_chunks, n_k_chunks):
    width = w.shape[1]
    reps = CN // HEAD_DIM
    head = jnp.arange(CN) // HEAD_DIM
    gmat = (head[:, None] == head[None, :]).astype(BF16)
    qg = jnp.tile(q_gain.reshape(1, HEAD_DIM), (1, reps))
    kg = jnp.tile(k_gain.reshape(1, HEAD_DIM), (1, reps))
    tps, lat_tiles = rows.tiles_per_seq, rows.lat_tiles
    tab_spec = pl.BlockSpec((TM, CN), lambda i: (jnp.where(i < lat_tiles, i % tps, tps), 0))
    return pl.pallas_call(
        functools.partial(_qkv_kernel, n_q_chunks + n_k_chunks, n_q_chunks, LOG2E * HEAD_DIM ** -0.5),
        grid=(rows.all_tiles,),
        in_specs=[rows.row_spec(D), rows.mod_spec(), _resident((D, width)), _resident((CN, CN)),
                  _resident((1, CN)), _resident((1, CN)), tab_spec, tab_spec],
        out_specs=rows.row_spec(width),
        out_shape=jax.ShapeDtypeStruct((rows.n_all, width), BF16),
        compiler_params=_params(1),
        name="qkv_proj",
    )(h, modg, w, gmat, qg, kg, tables[0], tables[1])


DH2 = 2 * HEAD_DIM


def _diff_attn_kernel(lam_init, n_lat_chunks, q_ref, kc_ref, vc_ref, kl_ref, vl_ref, lam_ref, subg_ref, o_ref,
                      m_ref, acc_ref, vca_ref, vla_ref, kn_ref):
    def max_norm_sq(x):
        xf = x.astype(F32)
        return jnp.max(jnp.sum(xf * xf, axis=-1, keepdims=True), axis=0, keepdims=True)

    def per_head_setup():
        vca_ref[:, :DH2] = vc_ref[...]
        vca_ref[:, DH2:] = jnp.ones((vc_ref.shape[0], DH2), BF16)
        kn = max_norm_sq(kc_ref[...])
        if n_lat_chunks:
            vla_ref[:, :DH2] = vl_ref[...]
            vla_ref[:, DH2:] = jnp.ones((vl_ref.shape[0], DH2), BF16)
            kn = jnp.maximum(kn, max_norm_sq(kl_ref[...]))
        kn_ref[...] = jnp.broadcast_to(kn, kn_ref.shape)

    if n_lat_chunks:
        pl.when(pl.program_id(2) == 0)(per_head_setup)
    else:
        per_head_setup()

    q = q_ref[...]
    lane = lax.broadcasted_iota(jnp.int32, (1, DH2), 1)
    zero = jnp.zeros((), BF16)
    qs = (jnp.where(lane < HEAD_DIM, q, zero), jnp.where(lane >= HEAD_DIM, q, zero))
    acc_ref[...] = jnp.zeros(acc_ref.shape, F32)
    nt = (((1,), (1,)), ((), ()))

    def over_keys(update):
        update(kc_ref[...], vca_ref[...])
        if n_lat_chunks:
            tk = kl_ref.shape[0] // n_lat_chunks

            def body(c, carry):
                start = pl.multiple_of(c * tk, tk)
                update(kl_ref[pl.ds(start, tk), :], vla_ref[pl.ds(start, tk), :])
                return carry
            lax.fori_loop(0, n_lat_chunks, body, 0)

    def update_unshifted(k, va):
        for mi in range(2):
            s = lax.dot_general(qs[mi], k, nt, preferred_element_type=F32)
            acc_ref[mi] += _dot(jnp.exp2(s).astype(BF16), va)

    def update_online(k, va):
        reps = k.shape[0] // DH2
        for mi in range(2):
            s = lax.dot_general(qs[mi], k, nt, preferred_element_type=F32)
            m_old = m_ref[mi]
            m_new = jnp.maximum(m_old, jnp.max(s, axis=-1, keepdims=True))
            alpha = jnp.exp2(m_old - m_new)
            p = jnp.exp2(s - jnp.concatenate([m_new] * reps, axis=-1))
            acc_ref[mi] = jnp.concatenate([alpha, alpha], axis=-1) * acc_ref[mi] + _dot(p.astype(BF16), va)
            m_ref[mi] = m_new

    def online():
        m_ref[...] = jnp.full(m_ref.shape, -jnp.inf, F32)
        over_keys(update_online)

    bound_sq = max_norm_sq(q) * kn_ref[0:1, 0:1]
    small_logits = bound_sq[0, 0] <= MAX_UNSHIFTED_LOGIT ** 2
    pl.when(small_logits)(lambda: over_keys(update_unshifted))
    pl.when(jnp.logical_not(small_logits))(online)

    lp = lam_ref[...]
    lam = (jnp.exp(jnp.sum(lp[0:1] * lp[1:2], axis=-1, keepdims=True))
           - jnp.exp(jnp.sum(lp[2:3] * lp[3:4], axis=-1, keepdims=True)) + lam_init)
    acc0 = acc_ref[0]
    acc1 = acc_ref[1]
    o = acc0[:, :DH2] / acc0[:, DH2:] - lam * (acc1[:, :DH2] / acc1[:, DH2:])
    o = o * lax.rsqrt(jnp.mean(o * o, axis=-1, keepdims=True) + EPS) * subg_ref[...]
    o_ref[...] = (o * (1.0 - lam_init)).astype(BF16)


def _diff_attn(rows, qkv, lam_p, sub_g, lam_init):
    batch, seq, ctx_len = rows.batch, rows.seq, rows.ctx_len
    nq = seq // ATT_TQ
    nh = DIFF_HEADS
    ctx0 = rows.n_lat // ctx_len
    sub_g = sub_g.reshape(1, DH2)
    small = [_resident((4, HEAD_DIM)), _resident((1, DH2))]
    kc_spec = pl.BlockSpec((ctx_len, DH2), lambda b, h, *_: (ctx0 + b, nh + h))
    vc_spec = pl.BlockSpec((ctx_len, DH2), lambda b, h, *_: (ctx0 + b, 2 * nh + h))

    def scratch(tq, n_lat_keys):
        return [pltpu.VMEM((2, tq, DH2), F32), pltpu.VMEM((2, tq, 2 * DH2), F32),
                pltpu.VMEM((ctx_len, 2 * DH2), BF16), pltpu.VMEM((n_lat_keys, 2 * DH2), BF16),
                pltpu.VMEM((8, DH2), F32)]

    o_lat = pl.pallas_call(
        functools.partial(_diff_attn_kernel, lam_init, pl.cdiv(seq, ATT_TK)),
        grid=(batch, nh, nq),
        in_specs=[pl.BlockSpec((ATT_TQ, DH2), lambda b, h, i: (b * nq + i, h)),
                  kc_spec, vc_spec,
                  pl.BlockSpec((seq, DH2), lambda b, h, i: (b, nh + h)),
                  pl.BlockSpec((seq, DH2), lambda b, h, i: (b, 2 * nh + h))] + small,
        out_specs=pl.BlockSpec((ATT_TQ, DH2), lambda b, h, i: (b * nq + i, h)),
        out_shape=jax.ShapeDtypeStruct((rows.n_lat, D), BF16),
        scratch_shapes=scratch(ATT_TQ, seq),
        compiler_params=_params(3),
        name="diff_attn_latent",
    )(qkv, qkv, qkv, qkv, qkv, lam_p, sub_g)

    def ctx_kernel(q_ref, kc_ref, vc_ref, lam_ref, subg_ref, o_ref, *scratch_refs):
        _diff_attn_kernel(lam_init, 0, q_ref, kc_ref, vc_ref, None, None, lam_ref, subg_ref, o_ref, *scratch_refs)

    o_ctx = pl.pallas_call(
        ctx_kernel,
        grid=(batch, nh),
        in_specs=[pl.BlockSpec((ctx_len, DH2), lambda b, h: (ctx0 + b, h)), kc_spec, vc_spec] + small,
        out_specs=pl.BlockSpec((ctx_len, DH2), lambda b, h: (b, h)),
        out_shape=jax.ShapeDtypeStruct((rows.n_ctx, D), BF16),
        scratch_shapes=scratch(ctx_len, 16),
        compiler_params=_params(2),
        name="diff_attn_context",
    )(qkv, qkv, qkv, lam_p, sub_g)
    return o_lat, o_ctx


def _out_proj_kernel(lat_tiles, h_ref, mod_ref, zl_ref, zc_ref, w_ref, o_ref):
    i = pl.program_id(0)

    def project(z_ref):
        o_ref[...] = h_ref[...] + mod_ref[0][2:3] * _dot(z_ref[...], w_ref[...])

    pl.when(i < lat_tiles)(lambda: project(zl_ref))
    pl.when(i >= lat_tiles)(lambda: project(zc_ref))


def _out_proj(rows, n_tiles, h, modg, z_lat, z_ctx, w):
    lat_tiles = rows.lat_tiles
    return pl.pallas_call(
        functools.partial(_out_proj_kernel, lat_tiles),
        grid=(n_tiles,),
        in_specs=[rows.row_spec(D), rows.mod_spec(),
                  pl.BlockSpec((TM, D), lambda i: (jnp.minimum(i, lat_tiles - 1), 0)),
                  pl.BlockSpec((TM, D), lambda i: (jnp.maximum(i - lat_tiles, 0), 0)),
                  _resident((D, D))],
        out_specs=rows.row_spec(D),
        out_shape=jax.ShapeDtypeStruct((n_tiles * TM, D), F32),
        compiler_params=_params(1),
        name="out_proj",
    )(h, modg, z_lat, z_ctx, w)


def _gmlp_kernel(h_ref, mod_ref, win_ref, bin_ref, vg_ref, ws_ref, bs_ref, wout_ref, o_ref, u_ref, v_ref, t_ref):
    m = mod_ref[0]
    x = h_ref[...]
    a = _modnorm(x, *_mix_mod(m)).astype(BF16)
    ssq = jnp.zeros((TM, 1), F32)
    for c in range(2 * CM_WIDTH // CN):
        lo = c * CN
        z = _gelu_tanh(_dot(a, win_ref[:, lo:lo + CN]) + bin_ref[:, lo:lo + CN])
        if lo < CM_WIDTH:
            u_ref[:, lo:lo + CN] = z
        else:
            v_ref[:, lo - CM_WIDTH:lo - CM_WIDTH + CN] = z
            ssq = ssq + jnp.sum(z * z, axis=-1, keepdims=True)
    inv = lax.rsqrt(ssq * (1.0 / CM_WIDTH) + EPS)
    for g in range(CM_GROUPS):
        lo = g * CM_GW
        vn = (v_ref[:, lo:lo + CM_GW] * inv * vg_ref[:, lo:lo + CM_GW]).astype(BF16)
        for r in range(TM // CHUNK):
            r0 = r * CHUNK
            sv = _dot(ws_ref[g], vn[r0:r0 + CHUNK]) + bs_ref[:, g:g + 1]
            t_ref[r0:r0 + CHUNK, lo:lo + CM_GW] = (u_ref[r0:r0 + CHUNK, lo:lo + CM_GW] * sv).astype(BF16)
    o_ref[...] = x + m[2:3] * _dot(t_ref[...], wout_ref[...])


def _gmlp(rows, h, modg, w_in, b_in, v_g, w_s, b_s, w_out):
    return pl.pallas_call(
        _gmlp_kernel,
        grid=(rows.all_tiles,),
        in_specs=[rows.row_spec(D), rows.mod_spec(), _resident((D, 2 * CM_WIDTH)), _resident((1, 2 * CM_WIDTH)),
                  _resident((1, CM_WIDTH)), _resident((CM_GROUPS, CHUNK, CHUNK)), _resident((CHUNK, CM_GROUPS)),
                  _resident((CM_WIDTH, D))],
        out_specs=rows.row_spec(D),
        out_shape=jax.ShapeDtypeStruct((rows.n_all, D), F32),
        scratch_shapes=[pltpu.VMEM((TM, CM_WIDTH), F32), pltpu.VMEM((TM, CM_WIDTH), F32),
                        pltpu.VMEM((TM, CM_WIDTH), BF16)],
        compiler_params=_params(1),
        name="gmlp",
    )(h, modg, w_in, b_in, v_g, w_s, b_s, w_out)


SWA_BAND = SWA_TQ + 2 * SWA_WINDOW


def _swa_kernel(seq, sink_ref, q_ref, kc_ref, vc_ref, kl_ref, vl_ref, o_ref):
    qi = pl.program_id(1)
    q0 = qi * SWA_TQ
    start = jnp.clip(q0 - SWA_WINDOW, 0, seq - SWA_BAND)
    start = pl.multiple_of(start, SWA_WINDOW)
    kb = kl_ref[pl.ds(start, SWA_BAND), :]
    vb = vl_ref[pl.ds(start, SWA_BAND), :]
    kc = kc_ref[...]
    vc = vc_ref[...]
    qpos = q0 + lax.broadcasted_iota(jnp.int32, (SWA_TQ, SWA_BAND), 0)
    kpos = start + lax.broadcasted_iota(jnp.int32, (SWA_TQ, SWA_BAND), 1)
    in_band = jnp.abs(qpos - kpos) <= SWA_WINDOW
    nt = (((1,), (1,)), ((), ()))
    for j in range(SWA_KV_HEADS):
        kj = slice(j * HEAD_DIM, (j + 1) * HEAD_DIM)
        kbj, vbj, kcj, vcj = kb[:, kj], vb[:, kj], kc[:, kj], vc[:, kj]
        outs = []
        for g in range(SWA_GROUP):
            hq = j * SWA_GROUP + g
            qh = q_ref[:, hq * HEAD_DIM:(hq + 1) * HEAD_DIM]
            s_c = lax.dot_general(qh, kcj, nt, preferred_element_type=F32)
            s_b = lax.dot_general(qh, kbj, nt, preferred_element_type=F32)
            s_b = jnp.where(in_band, s_b, -jnp.inf)
            sink = sink_ref[hq] * LOG2E
            mx = jnp.maximum(jnp.maximum(jnp.max(s_c, axis=-1, keepdims=True),
                                         jnp.max(s_b, axis=-1, keepdims=True)), sink)
            p_c = jnp.exp2(s_c - mx)
            p_b = jnp.exp2(s_b - mx)
            denom = (jnp.sum(p_c, axis=-1, keepdims=True) + jnp.sum(p_b, axis=-1, keepdims=True)
                     + jnp.exp2(sink - mx))
            o = _dot(p_c.astype(BF16), vcj) + _dot(p_b.astype(BF16), vbj)
            outs.append(o / denom)
        lo = j * SWA_GROUP * HEAD_DIM
        o_ref[:, lo:lo + SWA_GROUP * HEAD_DIM] = jnp.concatenate(outs, axis=-1).astype(BF16)


def _swa(rows, qkv, sink):
    batch, seq, ctx_len = rows.batch, rows.seq, rows.ctx_len
    nq = seq // SWA_TQ
    kvw = SWA_KV_HEADS * HEAD_DIM
    k_col = D // kvw
    ctx0 = rows.n_lat // ctx_len
    return pl.pallas_call(
        functools.partial(_swa_kernel, seq),
        grid=(batch, nq),
        in_specs=[pl.BlockSpec(memory_space=pltpu.SMEM),
                  pl.BlockSpec((SWA_TQ, D), lambda b, i: (b * nq + i, 0)),
                  pl.BlockSpec((ctx_len, kvw), lambda b, i: (ctx0 + b, k_col)),
                  pl.BlockSpec((ctx_len, kvw), lambda b, i: (ctx0 + b, k_col + 1)),
                  pl.BlockSpec((seq, kvw), lambda b, i: (b, k_col)),
                  pl.BlockSpec((seq, kvw), lambda b, i: (b, k_col + 1))],
        out_specs=pl.BlockSpec((SWA_TQ, D), lambda b, i: (b * nq + i, 0)),
        out_shape=jax.ShapeDtypeStruct((rows.n_lat, D), BF16),
        compiler_params=_params(2),
        name="swa_attn",
    )(sink, qkv, qkv, qkv, qkv, qkv)


def _router_kernel(h_ref, mod_ref, wr_ref, br_ref, a_ref, route_ref):
    a = _modnorm(h_ref[...], *_ffn_mod(mod_ref[0]))
    a_ref[...] = a
    logits = jnp.dot(a, wr_ref[...], precision=HIGHEST, preferred_element_type=F32) + br_ref[...]
    lane = lax.broadcasted_iota(jnp.int32, (TM, ROUTE_W), 1)
    m1 = jnp.max(logits, axis=-1, keepdims=True)
    i1 = jnp.min(jnp.where(logits == m1, lane, ROUTE_W), axis=-1, keepdims=True)
    rest = jnp.where(lane == i1, -jnp.inf, logits)
    m2 = jnp.max(rest, axis=-1, keepdims=True)
    i2 = jnp.min(jnp.where(rest == m2, lane, ROUTE_W), axis=-1, keepdims=True)
    e2 = jnp.exp(m2 - m1)
    gate1 = 1.0 / (1.0 + e2)
    gate2 = e2 / (1.0 + e2)
    rec = jnp.where(lane == 0, i1.astype(F32), 0.0)
    rec = jnp.where(lane == 1, i2.astype(F32), rec)
    rec = jnp.where(lane == 2, gate1, rec)
    route_ref[...] = jnp.where(lane == 3, gate2, rec)


def _router(rows, n_tiles, h, modg, w_r, b_r):
    pad = ROUTE_W - N_EXPERTS
    w_pad = jnp.pad(w_r, ((0, 0), (0, pad)))
    b_pad = jnp.concatenate([b_r.astype(F32), jnp.full((pad,), -1e30, F32)]).reshape(1, ROUTE_W)
    return pl.pallas_call(
        _router_kernel,
        grid=(n_tiles,),
        in_specs=[rows.row_spec(D), rows.mod_spec(), _resident((D, ROUTE_W)), _resident((1, ROUTE_W))],
        out_specs=[rows.row_spec(D), rows.row_spec(ROUTE_W)],
        out_shape=[jax.ShapeDtypeStruct((n_tiles * TM, D), F32), jax.ShapeDtypeStruct((n_tiles * TM, ROUTE_W), F32)],
        compiler_params=_params(1),
        name="moe_router",
    )(h, modg, w_pad, b_pad)


def _row_copy(src_ref, src_row, dst_ref, dst_row, sem):
    return pltpu.make_async_copy(src_ref.at[pl.ds(src_row, 1)], dst_ref.at[pl.ds(dst_row, 1)], sem)


def _dispatch_kernel(tok_ref, a_hbm, xs_ref, sem):
    base = pl.program_id(0) * MOE_RB

    def issue(r, carry):
        _row_copy(a_hbm, tok_ref[base + r], xs_ref, r, sem).start()
        return carry

    lax.fori_loop(0, MOE_RB, issue, 0)
    pltpu.make_async_copy(a_hbm.at[pl.ds(0, MOE_RB)], xs_ref, sem).wait()


def _dispatch(a, slot_tok, n_blocks):
    return pl.pallas_call(
        _dispatch_kernel,
        grid_spec=pltpu.PrefetchScalarGridSpec(
            num_scalar_prefetch=1,
            grid=(n_blocks,),
            in_specs=[pl.BlockSpec(memory_space=pl.ANY)],
            out_specs=pl.BlockSpec((MOE_RB, D), lambda i, tok: (i, 0)),
            scratch_shapes=[pltpu.SemaphoreType.DMA(())],
        ),
        out_shape=jax.ShapeDtypeStruct((n_blocks * MOE_RB, D), F32),
        compiler_params=_params(1),
        name="moe_dispatch",
    )(slot_tok, a)


def _expert_kernel(be_ref, xs_ref, w1_ref, w3_ref, w2_ref, ys_ref, t_ref):
    del be_ref
    a = xs_ref[...].astype(BF16)
    _swiglu_into(a, w1_ref, w3_ref, t_ref)
    ys_ref[...] = _dot(t_ref[...], w2_ref[...])


def _experts(xs, blk_exp, w1, w3, w2):
    n_blocks = blk_exp.shape[0]
    return pl.pallas_call(
        _expert_kernel,
        grid_spec=pltpu.PrefetchScalarGridSpec(
            num_scalar_prefetch=1,
            grid=(n_blocks,),
            in_specs=[pl.BlockSpec((MOE_RB, D), lambda i, be: (i, 0)),
                      pl.BlockSpec((None, D, FFN_DIM), lambda i, be: (be[i], 0, 0)),
                      pl.BlockSpec((None, D, FFN_DIM), lambda i, be: (be[i], 0, 0)),
                      pl.BlockSpec((None, FFN_DIM, D), lambda i, be: (be[i], 0, 0))],
            out_specs=pl.BlockSpec((MOE_RB, D), lambda i, be: (i, 0)),
            scratch_shapes=[pltpu.VMEM((MOE_RB, FFN_DIM), BF16)],
        ),
        out_shape=jax.ShapeDtypeStruct((n_blocks * MOE_RB, D), F32),
        compiler_params=_params(1),
        name="moe_experts",
    )(blk_exp, xs, w1, w3, w2)


def _combine_kernel(dest_ref, h_ref, mod_ref, route_ref, ys_hbm, o_ref, y1_ref, y2_ref, sem):
    base = pl.program_id(0) * (2 * TM)

    def issue(r, carry):
        _row_copy(ys_hbm, dest_ref[base + 2 * r], y1_ref, r, sem).start()
        _row_copy(ys_hbm, dest_ref[base + 2 * r + 1], y2_ref, r, sem).start()
        return carry

    lax.fori_loop(0, TM, issue, 0)
    pltpu.make_async_copy(ys_hbm.at[pl.ds(0, TM)], y1_ref, sem).wait()
    pltpu.make_async_copy(ys_hbm.at[pl.ds(0, TM)], y2_ref, sem).wait()
    route = route_ref[...]
    mix = route[:, 2:3] * y1_ref[...] + route[:, 3:4] * y2_ref[...]
    o_ref[...] = h_ref[...] + mod_ref[0][5:6] * mix


def _combine(rows, n_tiles, h, modg, route, ys, dest):
    tps, batch = rows.tiles_per_seq, rows.batch
    return pl.pallas_call(
        _combine_kernel,
        grid_spec=pltpu.PrefetchScalarGridSpec(
            num_scalar_prefetch=1,
            grid=(n_tiles,),
            in_specs=[pl.BlockSpec((TM, D), lambda i, d: (i, 0)),
                      pl.BlockSpec((1, 8, D), lambda i, d: (jnp.minimum(i // tps, batch), 0, 0)),
                      pl.BlockSpec((TM, ROUTE_W), lambda i, d: (i, 0)),
                      pl.BlockSpec(memory_space=pl.ANY)],
            out_specs=pl.BlockSpec((TM, D), lambda i, d: (i, 0)),
            scratch_shapes=[pltpu.VMEM((TM, D), F32), pltpu.VMEM((TM, D), F32), pltpu.SemaphoreType.DMA(())],
        ),
        out_shape=jax.ShapeDtypeStruct((n_tiles * TM, D), F32),
        compiler_params=_params(1),
        name="moe_combine",
    )(dest, h, modg, route, ys)


def _moe(rows, n_tiles, h, modg, w_r, b_r, w1, w3, w2):
    a, route = _router(rows, n_tiles, h, modg, w_r, b_r)
    n_tok = n_tiles * TM
    n_assign = 2 * n_tok
    expert = route[:, 0:2].astype(jnp.int32).reshape(-1)
    onehot = (expert[:, None] == jnp.arange(N_EXPERTS, dtype=jnp.int32)[None, :]).astype(jnp.int32)
    csum = jnp.cumsum(onehot, axis=0)
    counts = csum[-1]
    rank = jnp.sum(onehot * (csum - 1), axis=1)
    padded = (counts + MOE_RB - 1) // MOE_RB * MOE_RB
    pad_end = jnp.cumsum(padded)
    dest = (pad_end - padded)[expert] + rank
    n_blocks = n_assign // MOE_RB + N_EXPERTS
    token = jnp.arange(n_assign, dtype=jnp.int32) // 2
    slot_tok = jnp.zeros((n_blocks * MOE_RB,), jnp.int32).at[dest].set(token)
    blk_exp = jnp.minimum(
        jnp.searchsorted(pad_end, jnp.arange(n_blocks, dtype=jnp.int32) * MOE_RB, side="right"),
        N_EXPERTS - 1).astype(jnp.int32)
    xs = _dispatch(a, slot_tok, n_blocks)
    ys = _experts(xs, blk_exp, w1, w3, w2)
    return _combine(rows, n_tiles, h, modg, route, ys, dest.astype(jnp.int32))


def kernel(x, c, ctx, c_ctx, ada_w, ada_b, norm_mix_g, norm_ffn_g, sc_in_w, sc_conv_w, sc_out_w, da_qkv_w, da_out_w,
           da_q_norm_g, da_k_norm_g, da_lambda, da_sub_norm_g, cm_in_w, cm_in_b, cm_v_norm_g, cm_ws, cm_bs, cm_out_w,
           sw_qkv_w, sw_out_w, sw_q_norm_g, sw_k_norm_g, sw_sink, ffn_w1, ffn_w3, ffn_w2, moe_router_w, moe_router_b,
           moe_w1, moe_w3, moe_w2):
    batch, seq, _ = x.shape
    ctx_len = ctx.shape[1]
    depth = ada_w.shape[0]
    assert depth == 4 and batch + 1 <= 16
    rows = _Rows(batch, seq, ctx_len)

    cvec = jnp.concatenate([c, c_ctx[None, :], jnp.zeros((16 - batch - 1, D), F32)], axis=0)
    mod = _ada_all(cvec, ada_w, ada_b)[:, :batch + 1].reshape(depth, batch + 1, 6, D)
    gains = jnp.stack([norm_mix_g, norm_ffn_g], axis=1)[:, None]
    modg = jnp.concatenate([mod, jnp.broadcast_to(gains, (depth, batch + 1, 2, D))], axis=2)

    bf = lambda w: w.astype(BF16)
    h = jnp.concatenate([x.reshape(-1, D), ctx.reshape(-1, D)], axis=0)
    tables = _rope_tables(rows)

    bg, y = _conv_in(rows, h, modg[0], bf(sc_in_w[0]))
    conv_w = jnp.pad(sc_conv_w[0], ((0, 5), (0, 0)))
    h = _conv_out(rows, h, modg[0], bg, y, conv_w, bf(sc_out_w[0]))
    h = _ffn(rows, rows.all_tiles, h, modg[0], bf(ffn_w1[0]), bf(ffn_w3[0]), bf(ffn_w2[0]))

    nq_chunks = D // CN
    qkv = _qkv(rows, h, modg[1], bf(da_qkv_w[0]), da_q_norm_g[0], da_k_norm_g[0], tables, nq_chunks, nq_chunks)
    lam_init = 0.8 - 0.6 * math.exp(-0.3 * 1)
    o_lat, o_ctx = _diff_attn(rows, qkv, da_lambda[0], da_sub_norm_g[0], lam_init)
    h = _out_proj(rows, rows.all_tiles, h, modg[1], o_lat, o_ctx, bf(da_out_w[0]))
    h = _moe(rows, rows.all_tiles, h, modg[1], moe_router_w[0], moe_router_b[0], bf(moe_w1[0]), bf(moe_w3[0]),
             bf(moe_w2[0]))

    h = _gmlp(rows, h, modg[2], bf(cm_in_w[0]), cm_in_b[0].reshape(1, -1), cm_v_norm_g[0].reshape(1, -1),
              bf(cm_ws[0]), cm_bs[0].T, bf(cm_out_w[0]))
    h = _ffn(rows, rows.all_tiles, h, modg[2], bf(ffn_w1[1]), bf(ffn_w3[1]), bf(ffn_w2[1]))

    kv_chunks = SWA_KV_HEADS * HEAD_DIM // CN
    qkv = _qkv(rows, h, modg[3], bf(sw_qkv_w[0]), sw_q_norm_g[0], sw_k_norm_g[0], tables, nq_chunks, kv_chunks)
    o = _swa(rows, qkv, sw_sink[0])
    h = _out_proj(rows, rows.lat_tiles, h, modg[3], o, o, bf(sw_out_w[0]))
    h = _moe(rows, rows.lat_tiles, h, modg[3], moe_router_w[1], moe_router_b[1], bf(moe_w1[1]), bf(moe_w3[1]),
             bf(moe_w2[1]))
    return h.reshape(batch, seq, D)
```

```python
import functools
import math

import jax
import jax.numpy as jnp
from jax import lax
from jax.experimental import pallas as pl
from jax.experimental.pallas import tpu as pltpu

D = 1024
HEAD_DIM = 64
GRID_W = 64
ROPE_HALF = HEAD_DIM // 2
ROPE_BASE = 10000.0
DIFF_HEADS = D // (2 * HEAD_DIM)
SWA_Q_HEADS = D // HEAD_DIM
SWA_KV_HEADS = 4
SWA_GROUP = SWA_Q_HEADS // SWA_KV_HEADS
SWA_WINDOW = 128
CHUNK = 128
CM_WIDTH = 2 * D
CM_GROUPS = 8
CM_GW = CM_WIDTH // CM_GROUPS
FFN_DIM = 2816
N_EXPERTS = 8
EPS = 1e-6

F32 = jnp.float32
BF16 = jnp.bfloat16
HIGHEST = lax.Precision.HIGHEST
LOG2E = math.log2(math.e)

TM = 512
CN = 256
MOE_RB = 512
ATT_TQ = 512
ATT_TK = 1024
SWA_TQ = 256
ROUTE_W = 128
MAX_UNSHIFTED_LOGIT = 60.0
VMEM_LIMIT = 56 << 20


def _params(n_grid):
    return pltpu.CompilerParams(dimension_semantics=("arbitrary",) * n_grid, vmem_limit_bytes=VMEM_LIMIT)


def _resident(shape):
    zeros = (0,) * len(shape)
    return pl.BlockSpec(shape, lambda *_: zeros, pipeline_mode=pl.Buffered(1))


def _sigmoid(x):
    return 1.0 / (1.0 + jnp.exp(-x))


def _gelu_tanh(x):
    return 0.5 * x * (1.0 + jnp.tanh(math.sqrt(2.0 / math.pi) * (x + 0.044715 * (x * x * x))))


def _modnorm(x, g, shift, scale):
    y = x * lax.rsqrt(jnp.mean(x * x, axis=-1, keepdims=True) + EPS)
    return (y * g) * (1.0 + scale) + shift


def _dot(a, b):
    return jnp.dot(a, b, preferred_element_type=F32)


def _mix_mod(m):
    return m[6:7], m[0:1], m[1:2]


def _ffn_mod(m):
    return m[7:8], m[3:4], m[4:5]


def _ada_kernel(c_ref, w_ref, b_ref, o_ref):
    c = c_ref[...]
    s = c * _sigmoid(c)
    o_ref[0] = jnp.dot(s, w_ref[0], precision=HIGHEST, preferred_element_type=F32) + b_ref[0]


def _ada_all(cvec, ada_w, ada_b):
    depth, _, n = ada_w.shape
    tn = 1536
    rows = cvec.shape[0]
    return pl.pallas_call(
        _ada_kernel,
        grid=(depth, n // tn),
        in_specs=[pl.BlockSpec((rows, D), lambda l, j: (0, 0)),
                  pl.BlockSpec((1, D, tn), lambda l, j: (l, 0, j)),
                  pl.BlockSpec((1, 1, tn), lambda l, j: (l, 0, j))],
        out_specs=pl.BlockSpec((1, rows, tn), lambda l, j: (l, 0, j)),
        out_shape=jax.ShapeDtypeStruct((depth, rows, n), F32),
        compiler_params=_params(2),
        name="adaln",
    )(cvec, ada_w, ada_b.reshape(depth, 1, n))


class _Rows:
    def __init__(self, batch, seq, ctx_len):
        self.batch, self.seq, self.ctx_len = batch, seq, ctx_len
        self.n_lat = batch * seq
        self.n_ctx = batch * ctx_len
        self.n_all = self.n_lat + self.n_ctx
        assert seq % TM == 0 and self.n_ctx % TM == 0 and TM % ctx_len == 0
        self.lat_tiles = self.n_lat // TM
        self.all_tiles = self.n_all // TM
        self.tiles_per_seq = seq // TM

    def mod_spec(self):
        tps, batch = self.tiles_per_seq, self.batch
        return pl.BlockSpec((1, 8, D), lambda i, *_: (jnp.minimum(i // tps, batch), 0, 0))

    def row_spec(self, width):
        return pl.BlockSpec((TM, width), lambda i, *_: (i, 0))


def _conv_in_kernel(h_ref, mod_ref, w_ref, bg_ref, y_ref):
    a = _modnorm(h_ref[...], *_mix_mod(mod_ref[0])).astype(BF16)
    for c in range(D // CN):
        lo = c * CN
        bg_ref[:, lo:lo + CN] = _dot(a, w_ref[:, lo:lo + CN]).astype(BF16)
        cg = _dot(a, w_ref[:, D + lo:D + lo + CN])
        xv = _dot(a, w_ref[:, 2 * D + lo:2 * D + lo + CN])
        y_ref[:, lo:lo + CN] = (cg * xv).astype(BF16)


def _conv_in(rows, h, modg, w_in):
    out = jax.ShapeDtypeStruct((rows.n_all, D), BF16)
    return pl.pallas_call(
        _conv_in_kernel,
        grid=(rows.all_tiles,),
        in_specs=[rows.row_spec(D), rows.mod_spec(), _resident((D, 3 * D))],
        out_specs=[rows.row_spec(D), rows.row_spec(D)],
        out_shape=[out, out],
        compiler_params=_params(1),
        name="conv_in",
    )(h, modg, w_in)


HALO = 16


def _conv_out_kernel(n_lat, seq, ctx_len, h_ref, mod_ref, bg_ref, y_ref, yp_ref, yn_ref, cw_ref, w_ref, o_ref):
    i = pl.program_id(0)
    m = mod_ref[0]
    y = y_ref[...].astype(F32)
    row = lax.broadcasted_iota(jnp.int32, (TM, 1), 0)
    grow = row + i * TM
    seq_len = jnp.where(grow < n_lat, seq, ctx_len)
    pos = grow & (seq_len - 1)
    prev_row = yp_ref[HALO - 1:HALO, :].astype(F32)
    next_row = yn_ref[0:1, :].astype(F32)
    y_m1 = jnp.where(row == 0, prev_row, pltpu.roll(y, 1, 0))
    y_m1 = jnp.where(pos == 0, 0.0, y_m1)
    y_p1 = jnp.where(row == TM - 1, next_row, pltpu.roll(y, TM - 1, 0))
    y_p1 = jnp.where(pos == seq_len - 1, 0.0, y_p1)
    conv = cw_ref[0:1, :] * y_m1 + cw_ref[1:2, :] * y + cw_ref[2:3, :] * y_p1
    z = (bg_ref[...].astype(F32) * conv).astype(BF16)
    o_ref[...] = h_ref[...] + m[2:3] * _dot(z, w_ref[...])


def _conv_out(rows, h, modg, bg, y, conv_w, w_out):
    hb = TM // HALO
    last = rows.n_all // HALO - 1
    return pl.pallas_call(
        functools.partial(_conv_out_kernel, rows.n_lat, rows.seq, rows.ctx_len),
        grid=(rows.all_tiles,),
        in_specs=[rows.row_spec(D), rows.mod_spec(), rows.row_spec(D), rows.row_spec(D),
                  pl.BlockSpec((HALO, D), lambda i: (jnp.maximum(i * hb - 1, 0), 0)),
                  pl.BlockSpec((HALO, D), lambda i: (jnp.minimum((i + 1) * hb, last), 0)),
                  _resident((8, D)), _resident((D, D))],
        out_specs=rows.row_spec(D),
        out_shape=jax.ShapeDtypeStruct((rows.n_all, D), F32),
        compiler_params=_params(1),
        name="conv_out",
    )(h, modg, bg, y, y, y, conv_w, w_out)


def _swiglu_into(a, w1_ref, w3_ref, t_ref):
    for c in range(FFN_DIM // CN):
        lo = c * CN
        h1 = _dot(a, w1_ref[:, lo:lo + CN])
        h3 = _dot(a, w3_ref[:, lo:lo + CN])
        t_ref[:, lo:lo + CN] = (h1 * _sigmoid(h1) * h3).astype(BF16)


def _ffn_kernel(h_ref, mod_ref, w1_ref, w3_ref, w2_ref, o_ref, t_ref):
    m = mod_ref[0]
    x = h_ref[...]
    a = _modnorm(x, *_ffn_mod(m)).astype(BF16)
    _swiglu_into(a, w1_ref, w3_ref, t_ref)
    o_ref[...] = x + m[5:6] * _dot(t_ref[...], w2_ref[...])


def _ffn(rows, n_tiles, h, modg, w1, w3, w2):
    return pl.pallas_call(
        _ffn_kernel,
        grid=(n_tiles,),
        in_specs=[rows.row_spec(D), rows.mod_spec(), _resident((D, FFN_DIM)), _resident((D, FFN_DIM)),
                  _resident((FFN_DIM, D))],
        out_specs=rows.row_spec(D),
        out_shape=jax.ShapeDtypeStruct((n_tiles * TM, D), F32),
        scratch_shapes=[pltpu.VMEM((TM, FFN_DIM), BF16)],
        compiler_params=_params(1),
        name="ffn_dense",
    )(h, modg, w1, w3, w2)


def _head_group_sum(x2, gmat_ref):
    hi = x2.astype(BF16)
    lo = (x2 - hi.astype(F32)).astype(BF16)
    return _dot(hi, gmat_ref[...]) + _dot(lo, gmat_ref[...])


def _norm_rope(x, gain, gmat_ref, cos, sin):
    ms = _head_group_sum(x * x, gmat_ref) * (1.0 / HEAD_DIM)
    y = x * lax.rsqrt(ms + EPS) * gain
    lane = lax.broadcasted_iota(jnp.int32, (1, CN), 1)
    first_half = (lane & (ROPE_HALF - 1)) < (ROPE_HALF // 2)
    partner = jnp.where(first_half, pltpu.roll(y, CN - ROPE_HALF // 2, 1), pltpu.roll(y, ROPE_HALF // 2, 1))
    return y * cos + partner * sin


def _qkv_kernel(n_qk_chunks, n_q_chunks, q_scale, h_ref, mod_ref, w_ref, gmat_ref, qg_ref, kg_ref, cos_ref, sin_ref,
                o_ref):
    a = _modnorm(h_ref[...], *_mix_mod(mod_ref[0])).astype(BF16)
    n_chunks = w_ref.shape[1] // CN
    cos = cos_ref[...]
    sin = sin_ref[...]
    for c in range(n_chunks):
        lo = c * CN
        acc = _dot(a, w_ref[:, lo:lo + CN])
        if c < n_q_chunks:
            acc = _norm_rope(acc, qg_ref[...], gmat_ref, cos, sin) * q_scale
        elif c < n_qk_chunks:
            acc = _norm_rope(acc, kg_ref[...], gmat_ref, cos, sin)
        o_ref[:, lo:lo + CN] = acc.astype(BF16)


def _rope_tables(rows):
    seq = rows.seq
    pos = jnp.arange(seq)
    n_freq = ROPE_HALF // 2
    inv = ROPE_BASE ** (-jnp.arange(n_freq, dtype=F32) / n_freq)
    ang_r = (pos // GRID_W).astype(F32)[:, None] * inv
    ang_c = (pos % GRID_W).astype(F32)[:, None] * inv
    cos = jnp.concatenate([jnp.cos(ang_r)] * 2 + [jnp.cos(ang_c)] * 2, axis=-1)
    sin = jnp.concatenate([-jnp.sin(ang_r), jnp.sin(ang_r), -jnp.sin(ang_c), jnp.sin(ang_c)], axis=-1)
    cos = jnp.concatenate([cos, jnp.ones((TM, HEAD_DIM), F32)], axis=0)
    sin = jnp.concatenate([sin, jnp.zeros((TM, HEAD_DIM), F32)], axis=0)
    reps = CN // HEAD_DIM
    return jnp.tile(cos, (1, reps)), jnp.tile(sin, (1, reps))


def _qkv(rows, h, modg, w, q_gain, k_gain, tables, n_q_chunks, n_k_chunks):
    width = w.shape[1]
    reps = CN // HEAD_DIM
    head = jnp.arange(CN) // HEAD_DIM
    gmat = (head[:, None] == head[None, :]).astype(BF16)
    qg = jnp.tile(q_gain.reshape(1, HEAD_DIM), (1, reps))
    kg = jnp.tile(k_gain.reshape(1, HEAD_DIM), (1, reps))
    tps, lat_tiles = rows.tiles_per_seq, rows.lat_tiles
    tab_spec = pl.BlockSpec((TM, CN), lambda i: (jnp.where(i < lat_tiles, i % tps, tps), 0))
    return pl.pallas_call(
        functools.partial(_qkv_kernel, n_q_chunks + n_k_chunks, n_q_chunks, LOG2E * HEAD_DIM ** -0.5),
        grid=(rows.all_tiles,),
        in_specs=[rows.row_spec(D), rows.mod_spec(), _resident((D, width)), _resident((CN, CN)),
                  _resident((1, CN)), _resident((1, CN)), tab_spec, tab_spec],
        out_specs=rows.row_spec(width),
        out_shape=jax.ShapeDtypeStruct((rows.n_all, width), BF16),
        compiler_params=_params(1),
        name="qkv_proj",
    )(h, modg, w, gmat, qg, kg, tables[0], tables[1])


DH2 = 2 * HEAD_DIM


def _diff_attn_kernel(lam_init, n_lat_chunks, q_ref, kc_ref, vc_ref, kl_ref, vl_ref, lam_ref, subg_ref, o_ref,
                      m_ref, acc_ref, vca_ref, vla_ref, kn_ref):
    def max_norm_sq(x):
        xf = x.astype(F32)
        return jnp.max(jnp.sum(xf * xf, axis=-1, keepdims=True), axis=0, keepdims=True)

    def per_head_setup():
        vca_ref[:, :DH2] = vc_ref[...]
        vca_ref[:, DH2:] = jnp.ones((vc_ref.shape[0], DH2), BF16)
        kn = max_norm_sq(kc_ref[...])
        if n_lat_chunks:
            vla_ref[:, :DH2] = vl_ref[...]
            vla_ref[:, DH2:] = jnp.ones((vl_ref.shape[0], DH2), BF16)
            kn = jnp.maximum(kn, max_norm_sq(kl_ref[...]))
        kn_ref[...] = jnp.broadcast_to(kn, kn_ref.shape)

    if n_lat_chunks:
        pl.when(pl.program_id(2) == 0)(per_head_setup)
    else:
        per_head_setup()

    q = q_ref[...]
    lane = lax.broadcasted_iota(jnp.int32, (1, DH2), 1)
    zero = jnp.zeros((), BF16)
    qs = (jnp.where(lane < HEAD_DIM, q, zero), jnp.where(lane >= HEAD_DIM, q, zero))
    acc_ref[...] = jnp.zeros(acc_ref.shape, F32)
    nt = (((1,), (1,)), ((), ()))

    def over_keys(update):
        update(kc_ref[...], vca_ref[...])
        if n_lat_chunks:
            tk = kl_ref.shape[0] // n_lat_chunks

            def body(c, carry):
                start = pl.multiple_of(c * tk, tk)
                update(kl_ref[pl.ds(start, tk), :], vla_ref[pl.ds(start, tk), :])
                return carry
            lax.fori_loop(0, n_lat_chunks, body, 0)

    def update_unshifted(k, va):
        for mi in range(2):
            s = lax.dot_general(qs[mi], k, nt, preferred_element_type=F32)
            acc_ref[mi] += _dot(jnp.exp2(s).astype(BF16), va)

    def update_online(k, va):
        reps = k.shape[0] // DH2
        for mi in range(2):
            s = lax.dot_general(qs[mi], k, nt, preferred_element_type=F32)
            m_old = m_ref[mi]
            m_new = jnp.maximum(m_old, jnp.max(s, axis=-1, keepdims=True))
            alpha = jnp.exp2(m_old - m_new)
            p = jnp.exp2(s - jnp.concatenate([m_new] * reps, axis=-1))
            acc_ref[mi] = jnp.concatenate([alpha, alpha], axis=-1) * acc_ref[mi] + _dot(p.astype(BF16), va)
            m_ref[mi] = m_new

    def online():
        m_ref[...] = jnp.full(m_ref.shape, -jnp.inf, F32)
        over_keys(update_online)

    bound_sq = max_norm_sq(q) * kn_ref[0:1, 0:1]
    small_logits = bound_sq[0, 0] <= MAX_UNSHIFTED_LOGIT ** 2
    pl.when(small_logits)(lambda: over_keys(update_unshifted))
    pl.when(jnp.logical_not(small_logits))(online)

    lp = lam_ref[...]
    lam = (jnp.exp(jnp.sum(lp[0:1] * lp[1:2], axis=-1, keepdims=True))
           - jnp.exp(jnp.sum(lp[2:3] * lp[3:4], axis=-1, keepdims=True)) + lam_init)
    acc0 = acc_ref[0]
    acc1 = acc_ref[1]
    o = acc0[:, :DH2] / acc0[:, DH2:] - lam * (acc1[:, :DH2] / acc1[:, DH2:])
    o = o * lax.rsqrt(jnp.mean(o * o, axis=-1, keepdims=True) + EPS) * subg_ref[...]
    o_ref[...] = (o * (1.0 - lam_init)).astype(BF16)


def _diff_attn(rows, qkv, lam_p, sub_g, lam_init):
    batch, seq, ctx_len = rows.batch, rows.seq, rows.ctx_len
    nq = seq // ATT_TQ
    nh = DIFF_HEADS
    ctx0 = rows.n_lat // ctx_len
    sub_g = sub_g.reshape(1, DH2)
    small = [_resident((4, HEAD_DIM)), _resident((1, DH2))]
    kc_spec = pl.BlockSpec((ctx_len, DH2), lambda b, h, *_: (ctx0 + b, nh + h))
    vc_spec = pl.BlockSpec((ctx_len, DH2), lambda b, h, *_: (ctx0 + b, 2 * nh + h))

    def scratch(tq, n_lat_keys):
        return [pltpu.VMEM((2, tq, DH2), F32), pltpu.VMEM((2, tq, 2 * DH2), F32),
                pltpu.VMEM((ctx_len, 2 * DH2), BF16), pltpu.VMEM((n_lat_keys, 2 * DH2), BF16),
                pltpu.VMEM((8, DH2), F32)]

    o_lat = pl.pallas_call(
        functools.partial(_diff_attn_kernel, lam_init, pl.cdiv(seq, ATT_TK)),
        grid=(batch, nh, nq),
        in_specs=[pl.BlockSpec((ATT_TQ, DH2), lambda b, h, i: (b * nq + i, h)),
                  kc_spec, vc_spec,
                  pl.BlockSpec((seq, DH2), lambda b, h, i: (b, nh + h)),
                  pl.BlockSpec((seq, DH2), lambda b, h, i: (b, 2 * nh + h))] + small,
        out_specs=pl.BlockSpec((ATT_TQ, DH2), lambda b, h, i: (b * nq + i, h)),
        out_shape=jax.ShapeDtypeStruct((rows.n_lat, D), BF16),
        scratch_shapes=scratch(ATT_TQ, seq),
        compiler_params=_params(3),
        name="diff_attn_latent",
    )(qkv, qkv, qkv, qkv, qkv, lam_p, sub_g)

    def ctx_kernel(q_ref, kc_ref, vc_ref, lam_ref, subg_ref, o_ref, *scratch_refs):
        _diff_attn_kernel(lam_init, 0, q_ref, kc_ref, vc_ref, None, None, lam_ref, subg_ref, o_ref, *scratch_refs)

    o_ctx = pl.pallas_call(
        ctx_kernel,
        grid=(batch, nh),
        in_specs=[pl.BlockSpec((ctx_len, DH2), lambda b, h: (ctx0 + b, h)), kc_spec, vc_spec] + small,
        out_specs=pl.BlockSpec((ctx_len, DH2), lambda b, h: (b, h)),
        out_shape=jax.ShapeDtypeStruct((rows.n_ctx, D), BF16),
        scratch_shapes=scratch(ctx_len, 16),
        compiler_params=_params(2),
        name="diff_attn_context",
    )(qkv, qkv, qkv, lam_p, sub_g)
    return o_lat, o_ctx


def _out_proj_kernel(lat_tiles, h_ref, mod_ref, zl_ref, zc_ref, w_ref, o_ref):
    i = pl.program_id(0)

    def project(z_ref):
        o_ref[...] = h_ref[...] + mod_ref[0][2:3] * _dot(z_ref[...], w_ref[...])

    pl.when(i < lat_tiles)(lambda: project(zl_ref))
    pl.when(i >= lat_tiles)(lambda: project(zc_ref))


def _out_proj(rows, n_tiles, h, modg, z_lat, z_ctx, w):
    lat_tiles = rows.lat_tiles
    return pl.pallas_call(
        functools.partial(_out_proj_kernel, lat_tiles),
        grid=(n_tiles,),
        in_specs=[rows.row_spec(D), rows.mod_spec(),
                  pl.BlockSpec((TM, D), lambda i: (jnp.minimum(i, lat_tiles - 1), 0)),
                  pl.BlockSpec((TM, D), lambda i: (jnp.maximum(i - lat_tiles, 0), 0)),
                  _resident((D, D))],
        out_specs=rows.row_spec(D),
        out_shape=jax.ShapeDtypeStruct((n_tiles * TM, D), F32),
        compiler_params=_params(1),
        name="out_proj",
    )(h, modg, z_lat, z_ctx, w)


def _gmlp_kernel(h_ref, mod_ref, win_ref, bin_ref, vg_ref, ws_ref, bs_ref, wout_ref, o_ref, u_ref, v_ref, t_ref):
    m = mod_ref[0]
    x = h_ref[...]
    a = _modnorm(x, *_mix_mod(m)).astype(BF16)
    ssq = jnp.zeros((TM, 1), F32)
    for c in range(2 * CM_WIDTH // CN):
        lo = c * CN
        z = _gelu_tanh(_dot(a, win_ref[:, lo:lo + CN]) + bin_ref[:, lo:lo + CN])
        if lo < CM_WIDTH:
            u_ref[:, lo:lo + CN] = z
        else:
            v_ref[:, lo - CM_WIDTH:lo - CM_WIDTH + CN] = z
            ssq = ssq + jnp.sum(z * z, axis=-1, keepdims=True)
    inv = lax.rsqrt(ssq * (1.0 / CM_WIDTH) + EPS)
    for g in range(CM_GROUPS):
        lo = g * CM_GW
        vn = (v_ref[:, lo:lo + CM_GW] * inv * vg_ref[:, lo:lo + CM_GW]).astype(BF16)
        for r in range(TM // CHUNK):
            r0 = r * CHUNK
            sv = _dot(ws_ref[g], vn[r0:r0 + CHUNK]) + bs_ref[:, g:g + 1]
            t_ref[r0:r0 + CHUNK, lo:lo + CM_GW] = (u_ref[r0:r0 + CHUNK, lo:lo + CM_GW] * sv).astype(BF16)
    o_ref[...] = x + m[2:3] * _dot(t_ref[...], wout_ref[...])


def _gmlp(rows, h, modg, w_in, b_in, v_g, w_s, b_s, w_out):
    return pl.pallas_call(
        _gmlp_kernel,
        grid=(rows.all_tiles,),
        in_specs=[rows.row_spec(D), rows.mod_spec(), _resident((D, 2 * CM_WIDTH)), _resident((1, 2 * CM_WIDTH)),
                  _resident((1, CM_WIDTH)), _resident((CM_GROUPS, CHUNK, CHUNK)), _resident((CHUNK, CM_GROUPS)),
                  _resident((CM_WIDTH, D))],
        out_specs=rows.row_spec(D),
        out_shape=jax.ShapeDtypeStruct((rows.n_all, D), F32),
        scratch_shapes=[pltpu.VMEM((TM, CM_WIDTH), F32), pltpu.VMEM((TM, CM_WIDTH), F32),
                        pltpu.VMEM((TM, CM_WIDTH), BF16)],
        compiler_params=_params(1),
        name="gmlp",
    )(h, modg, w_in, b_in, v_g, w_s, b_s, w_out)


SWA_BAND = SWA_TQ + 2 * SWA_WINDOW


def _swa_kernel(seq, sink_ref, q_ref, kc_ref, vc_ref, kl_ref, vl_ref, o_ref):
    qi = pl.program_id(1)
    q0 = qi * SWA_TQ
    start = jnp.clip(q0 - SWA_WINDOW, 0, seq - SWA_BAND)
    start = pl.multiple_of(start, SWA_WINDOW)
    kb = kl_ref[pl.ds(start, SWA_BAND), :]
    vb = vl_ref[pl.ds(start, SWA_BAND), :]
    kc = kc_ref[...]
    vc = vc_ref[...]
    qpos = q0 + lax.broadcasted_iota(jnp.int32, (SWA_TQ, SWA_BAND), 0)
    kpos = start + lax.broadcasted_iota(jnp.int32, (SWA_TQ, SWA_BAND), 1)
    in_band = jnp.abs(qpos - kpos) <= SWA_WINDOW
    nt = (((1,), (1,)), ((), ()))
    for j in range(SWA_KV_HEADS):
        kj = slice(j * HEAD_DIM, (j + 1) * HEAD_DIM)
        kbj, vbj, kcj, vcj = kb[:, kj], vb[:, kj], kc[:, kj], vc[:, kj]
        outs = []
        for g in range(SWA_GROUP):
            hq = j * SWA_GROUP + g
            qh = q_ref[:, hq * HEAD_DIM:(hq + 1) * HEAD_DIM]
            s_c = lax.dot_general(qh, kcj, nt, preferred_element_type=F32)
            s_b = lax.dot_general(qh, kbj, nt, preferred_element_type=F32)
            s_b = jnp.where(in_band, s_b, -jnp.inf)
            sink = sink_ref[hq] * LOG2E
            mx = jnp.maximum(jnp.maximum(jnp.max(s_c, axis=-1, keepdims=True),
                                         jnp.max(s_b, axis=-1, keepdims=True)), sink)
            p_c = jnp.exp2(s_c - mx)
            p_b = jnp.exp2(s_b - mx)
            denom = (jnp.sum(p_c, axis=-1, keepdims=True) + jnp.sum(p_b, axis=-1, keepdims=True)
                     + jnp.exp2(sink - mx))
            o = _dot(p_c.astype(BF16), vcj) + _dot(p_b.astype(BF16), vbj)
            outs.append(o / denom)
        lo = j * SWA_GROUP * HEAD_DIM
        o_ref[:, lo:lo + SWA_GROUP * HEAD_DIM] = jnp.concatenate(outs, axis=-1).astype(BF16)


def _swa(rows, qkv, sink):
    batch, seq, ctx_len = rows.batch, rows.seq, rows.ctx_len
    nq = seq // SWA_TQ
    kvw = SWA_KV_HEADS * HEAD_DIM
    k_col = D // kvw
    ctx0 = rows.n_lat // ctx_len
    return pl.pallas_call(
        functools.partial(_swa_kernel, seq),
        grid=(batch, nq),
        in_specs=[pl.BlockSpec(memory_space=pltpu.SMEM),
                  pl.BlockSpec((SWA_TQ, D), lambda b, i: (b * nq + i, 0)),
                  pl.BlockSpec((ctx_len, kvw), lambda b, i: (ctx0 + b, k_col)),
                  pl.BlockSpec((ctx_len, kvw), lambda b, i: (ctx0 + b, k_col + 1)),
                  pl.BlockSpec((seq, kvw), lambda b, i: (b, k_col)),
                  pl.BlockSpec((seq, kvw), lambda b, i: (b, k_col + 1))],
        out_specs=pl.BlockSpec((SWA_TQ, D), lambda b, i: (b * nq + i, 0)),
        out_shape=jax.ShapeDtypeStruct((rows.n_lat, D), BF16),
        compiler_params=_params(2),
        name="swa_attn",
    )(sink, qkv, qkv, qkv, qkv, qkv)


def _router_kernel(h_ref, mod_ref, wr_ref, br_ref, a_ref, route_ref, cnt_ref):
    a = _modnorm(h_ref[...], *_ffn_mod(mod_ref[0]))
    a_ref[...] = a.astype(BF16)
    logits = jnp.dot(a, wr_ref[...], precision=HIGHEST, preferred_element_type=F32) + br_ref[...]
    lane = lax.broadcasted_iota(jnp.int32, (TM, ROUTE_W), 1)
    m1 = jnp.max(logits, axis=-1, keepdims=True)
    i1 = jnp.min(jnp.where(logits == m1, lane, ROUTE_W), axis=-1, keepdims=True)
    rest = jnp.where(lane == i1, -jnp.inf, logits)
    m2 = jnp.max(rest, axis=-1, keepdims=True)
    i2 = jnp.min(jnp.where(rest == m2, lane, ROUTE_W), axis=-1, keepdims=True)
    e2 = jnp.exp(m2 - m1)
    gate1 = 1.0 / (1.0 + e2)
    gate2 = e2 / (1.0 + e2)
    rec = jnp.where(lane == 0, i1.astype(F32), 0.0)
    rec = jnp.where(lane == 1, i2.astype(F32), rec)
    rec = jnp.where(lane == 2, gate1, rec)
    route_ref[...] = jnp.where(lane == 3, gate2, rec)
    chosen = jnp.where((lane == i1) | (lane == i2), 1.0, 0.0)
    cnt_ref[0] = jnp.broadcast_to(jnp.sum(chosen, axis=0, keepdims=True), (8, ROUTE_W))


def _router(rows, n_tiles, h, modg, w_r, b_r):
    pad = ROUTE_W - N_EXPERTS
    w_pad = jnp.pad(w_r, ((0, 0), (0, pad)))
    b_pad = jnp.concatenate([b_r.astype(F32), jnp.full((pad,), -1e30, F32)]).reshape(1, ROUTE_W)
    return pl.pallas_call(
        _router_kernel,
        grid=(n_tiles,),
        in_specs=[rows.row_spec(D), rows.mod_spec(), _resident((D, ROUTE_W)), _resident((1, ROUTE_W))],
        out_specs=[rows.row_spec(D), rows.row_spec(ROUTE_W), pl.BlockSpec((1, 8, ROUTE_W), lambda i: (i, 0, 0))],
        out_shape=[jax.ShapeDtypeStruct((n_tiles * TM, D), BF16), jax.ShapeDtypeStruct((n_tiles * TM, ROUTE_W), F32),
                   jax.ShapeDtypeStruct((n_tiles, 8, ROUTE_W), F32)],
        compiler_params=_params(1),
        name="moe_router",
    )(h, modg, w_pad, b_pad)


SEG_ALIGN = 16
SEG_BITS = tuple(1 << b for b in range(TM.bit_length() - 1, SEG_ALIGN.bit_length() - 2, -1))
LROWS = -(-(2 * TM + N_EXPERTS * (SEG_ALIGN - 1)) // 128) * 128
XS_W = D + 128
META_W = 3 * N_EXPERTS


def _segment_copies(meta_ref, tile, local_ref, slots_hbm, sem, to_slots):
    for e in range(N_EXPERTS):
        loff = meta_ref[tile * META_W + e]
        goff = meta_ref[tile * META_W + N_EXPERTS + e]
        seg = meta_ref[tile * META_W + 2 * N_EXPERTS + e]
        for bit in SEG_BITS:
            done = seg & ~(2 * bit - 1)
            lo = local_ref.at[pl.ds(pl.multiple_of(loff + done, SEG_ALIGN), bit)]
            gl = slots_hbm.at[pl.ds(pl.multiple_of(goff + done, SEG_ALIGN), bit)]
            copy = pltpu.make_async_copy(lo, gl, sem) if to_slots else pltpu.make_async_copy(gl, lo, sem)
            yield (seg & bit) != 0, copy


def _run_segment_copies(meta_ref, tile, local_ref, slots_hbm, sem, to_slots):
    for cond, copy in _segment_copies(meta_ref, tile, local_ref, slots_hbm, sem, to_slots):
        pl.when(cond)(copy.start)
    for cond, copy in _segment_copies(meta_ref, tile, local_ref, slots_hbm, sem, to_slots):
        pl.when(cond)(copy.wait)


def _split3(x):
    hi = x.astype(BF16).astype(F32)
    mid = (x - hi).astype(BF16).astype(F32)
    lo = ((x - hi) - mid).astype(BF16).astype(F32)
    return hi, mid, lo


def _dispatch_kernel(n_blocks, meta_ref, fill_ref, a_ref, route_ref, tri_ref, xs_hbm, slot_ref, local_ref, zero_ref,
                     sem, zsem):
    i = pl.program_id(0)

    @pl.when(i == 0)
    def _zero_unwritten_slots():
        zero_ref[...] = jnp.zeros(zero_ref.shape, BF16)
        tails = []
        for e in range(N_EXPERTS):
            off, length = fill_ref[e], fill_ref[N_EXPERTS + e]
            for bit in (b for b in SEG_BITS if b < MOE_RB):
                done = length & ~(2 * bit - 1)
                dst = xs_hbm.at[pl.ds(pl.multiple_of(off + done, SEG_ALIGN), bit)]
                tails.append(((length & bit) != 0, pltpu.make_async_copy(zero_ref.at[pl.ds(0, bit)], dst, zsem)))
        for cond, copy in tails:
            pl.when(cond)(copy.start)

        def block_copy(blk):
            dst = xs_hbm.at[pl.ds(pl.multiple_of(blk * MOE_RB, MOE_RB), MOE_RB)]
            return pltpu.make_async_copy(zero_ref, dst, zsem)

        n_used = fill_ref[2 * N_EXPERTS]
        lax.fori_loop(n_used, n_blocks, lambda blk, c: (block_copy(blk).start(), c)[1], 0)
        for cond, copy in tails:
            pl.when(cond)(copy.wait)
        lax.fori_loop(n_used, n_blocks, lambda blk, c: (block_copy(blk).wait(), c)[1], 0)

    route = route_ref[...]
    lane = lax.broadcasted_iota(jnp.int32, (TM, ROUTE_W), 1)
    pick1 = lane == route[:, 0:1].astype(jnp.int32)
    pick2 = lane == route[:, 1:2].astype(jnp.int32)
    chosen = jnp.where(pick1 | pick2, 1.0, 0.0).astype(BF16)
    rank = _dot(tri_ref[...], chosen)
    lane1 = lax.broadcasted_iota(jnp.int32, (1, ROUTE_W), 1)
    loff = jnp.zeros((1, ROUTE_W), F32)
    for e in range(N_EXPERTS):
        loff = jnp.where(lane1 == e, meta_ref[i * META_W + e].astype(F32), loff)
    place = rank + loff
    slot1 = jnp.sum(jnp.where(pick1, place, 0.0), axis=-1, keepdims=True)
    slot2 = jnp.sum(jnp.where(pick2, place, 0.0), axis=-1, keepdims=True)
    slot_ref[...] = jnp.where(lane == 0, slot1, jnp.where(lane == 1, slot2, 0.0))

    row1 = jnp.transpose(jnp.broadcast_to(slot1, (TM, 128)))[0:1, :]
    row2 = jnp.transpose(jnp.broadcast_to(slot2, (TM, 128)))[0:1, :]
    lrow = lax.broadcasted_iota(jnp.int32, (LROWS, TM), 0).astype(F32)
    p1 = jnp.where(lrow == row1, 1.0, 0.0).astype(BF16)
    p2 = jnp.where(lrow == row2, 1.0, 0.0).astype(BF16)
    local_ref[:, :D] = _dot(p1 + p2, a_ref[...]).astype(BF16)
    gates = []
    for col in (2, 3):
        pieces = _split3(route[:, col:col + 1])
        g = jnp.zeros((TM, ROUTE_W), F32)
        for k, piece in enumerate(pieces):
            g = jnp.where(lane == k, piece, g)
        gates.append(g.astype(BF16))
    local_ref[:, D:] = (_dot(p1, gates[0]) + _dot(p2, gates[1])).astype(BF16)
    _run_segment_copies(meta_ref, i, local_ref, xs_hbm, sem, True)


def _dispatch(n_tiles, n_blocks, a, route, meta, fill):
    tri = (jnp.arange(TM)[:, None] > jnp.arange(TM)[None, :]).astype(BF16)
    return pl.pallas_call(
        functools.partial(_dispatch_kernel, n_blocks),
        grid_spec=pltpu.PrefetchScalarGridSpec(
            num_scalar_prefetch=2,
            grid=(n_tiles,),
            in_specs=[pl.BlockSpec((TM, D), lambda i, *_: (i, 0)),
                      pl.BlockSpec((TM, ROUTE_W), lambda i, *_: (i, 0)),
                      _resident((TM, TM))],
            out_specs=[pl.BlockSpec(memory_space=pl.ANY), pl.BlockSpec((TM, ROUTE_W), lambda i, *_: (i, 0))],
            scratch_shapes=[pltpu.VMEM((LROWS, XS_W), BF16), pltpu.VMEM((MOE_RB, XS_W), BF16),
                            pltpu.SemaphoreType.DMA(()), pltpu.SemaphoreType.DMA(())],
        ),
        out_shape=[jax.ShapeDtypeStruct((n_blocks * MOE_RB, XS_W), BF16),
                   jax.ShapeDtypeStruct((n_tiles * TM, ROUTE_W), F32)],
        compiler_params=_params(1),
        name="moe_dispatch",
    )(meta, fill, a, route, tri)


def _expert_kernel(be_ref, fill_ref, xs_ref, w1_ref, w3_ref, w2_ref, ys_ref, t_ref):
    del be_ref
    used = pl.program_id(0) < fill_ref[2 * N_EXPERTS]

    @pl.when(used)
    def _ffn():
        _swiglu_into(xs_ref[:, :D], w1_ref, w3_ref, t_ref)
        gate = jnp.sum(xs_ref[:, D:].astype(F32), axis=-1, keepdims=True)
        ys_ref[...] = (gate * _dot(t_ref[...], w2_ref[...])).astype(BF16)

    @pl.when(jnp.logical_not(used))
    def _idle():
        ys_ref[...] = jnp.zeros(ys_ref.shape, BF16)


def _experts(xs, blk_exp, fill, w1, w3, w2):
    n_blocks = blk_exp.shape[0]
    last_used = lambda i, fill: jnp.minimum(i, fill[2 * N_EXPERTS] - 1)
    return pl.pallas_call(
        _expert_kernel,
        grid_spec=pltpu.PrefetchScalarGridSpec(
            num_scalar_prefetch=2,
            grid=(n_blocks,),
            in_specs=[pl.BlockSpec((MOE_RB, XS_W), lambda i, be, fill: (last_used(i, fill), 0)),
                      pl.BlockSpec((None, D, FFN_DIM), lambda i, be, fill: (be[i], 0, 0)),
                      pl.BlockSpec((None, D, FFN_DIM), lambda i, be, fill: (be[i], 0, 0)),
                      pl.BlockSpec((None, FFN_DIM, D), lambda i, be, fill: (be[i], 0, 0))],
            out_specs=pl.BlockSpec((MOE_RB, D), lambda i, be, fill: (i, 0)),
            scratch_shapes=[pltpu.VMEM((MOE_RB, FFN_DIM), BF16)],
        ),
        out_shape=jax.ShapeDtypeStruct((n_blocks * MOE_RB, D), BF16),
        compiler_params=_params(1),
        name="moe_experts",
    )(blk_exp, fill, xs, w1, w3, w2)


def _combine_kernel(meta_ref, h_ref, mod_ref, slot_ref, ys_hbm, o_ref, local_ref, sem):
    i = pl.program_id(0)

    @pl.when(i == 0)
    def _init():
        local_ref[...] = jnp.zeros(local_ref.shape, BF16)

    _run_segment_copies(meta_ref, i, local_ref, ys_hbm, sem, False)
    slots = slot_ref[...]
    lcol = lax.broadcasted_iota(jnp.int32, (TM, LROWS), 1).astype(F32)
    gather = jnp.where((lcol == slots[:, 0:1]) | (lcol == slots[:, 1:2]), 1.0, 0.0).astype(BF16)
    o_ref[...] = h_ref[...] + mod_ref[0][5:6] * _dot(gather, local_ref[...])


def _combine(rows, n_tiles, h, modg, slots, ys, meta):
    tps, batch = rows.tiles_per_seq, rows.batch
    return pl.pallas_call(
        _combine_kernel,
        grid_spec=pltpu.PrefetchScalarGridSpec(
            num_scalar_prefetch=1,
            grid=(n_tiles,),
            in_specs=[pl.BlockSpec((TM, D), lambda i, m: (i, 0)),
                      pl.BlockSpec((1, 8, D), lambda i, m: (jnp.minimum(i // tps, batch), 0, 0)),
                      pl.BlockSpec((TM, ROUTE_W), lambda i, m: (i, 0)),
                      pl.BlockSpec(memory_space=pl.ANY)],
            out_specs=pl.BlockSpec((TM, D), lambda i, m: (i, 0)),
            scratch_shapes=[pltpu.VMEM((LROWS, D), BF16), pltpu.SemaphoreType.DMA(())],
        ),
        out_shape=jax.ShapeDtypeStruct((n_tiles * TM, D), F32),
        compiler_params=_params(1),
        name="moe_combine",
    )(meta, h, modg, slots, ys)


def _moe(rows, n_tiles, h, modg, w_r, b_r, w1, w3, w2):
    a, route, cnt = _router(rows, n_tiles, h, modg, w_r, b_r)
    cnt = cnt[:, 0, :N_EXPERTS].astype(jnp.int32)
    seg = (cnt + SEG_ALIGN - 1) // SEG_ALIGN * SEG_ALIGN
    loff = jnp.cumsum(seg, axis=1) - seg
    total = jnp.sum(seg, axis=0)
    region = (total + MOE_RB - 1) // MOE_RB * MOE_RB
    region_end = jnp.cumsum(region)
    region_start = region_end - region
    goff = region_start[None, :] + jnp.cumsum(seg, axis=0) - seg
    meta = jnp.concatenate([loff, goff, seg], axis=1).reshape(-1)
    n_blocks = pl.cdiv(2 * n_tiles * TM + n_tiles * N_EXPERTS * (SEG_ALIGN - 1), MOE_RB) + N_EXPERTS
    n_used = region_end[-1] // MOE_RB
    fill = jnp.concatenate([region_start + total, region - total, n_used[None]]).astype(jnp.int32)
    blk = jnp.minimum(jnp.arange(n_blocks, dtype=jnp.int32), n_used - 1)
    blk_exp = jnp.minimum(jnp.searchsorted(region_end, blk * MOE_RB, side="right"), N_EXPERTS - 1).astype(jnp.int32)
    xs, slots = _dispatch(n_tiles, n_blocks, a, route, meta, fill)
    ys = _experts(xs, blk_exp, fill, w1, w3, w2)
    return _combine(rows, n_tiles, h, modg, slots, ys, meta)


def kernel(x, c, ctx, c_ctx, ada_w, ada_b, norm_mix_g, norm_ffn_g, sc_in_w, sc_conv_w, sc_out_w, da_qkv_w, da_out_w,
           da_q_norm_g, da_k_norm_g, da_lambda, da_sub_norm_g, cm_in_w, cm_in_b, cm_v_norm_g, cm_ws, cm_bs, cm_out_w,
           sw_qkv_w, sw_out_w, sw_q_norm_g, sw_k_norm_g, sw_sink, ffn_w1, ffn_w3, ffn_w2, moe_router_w, moe_router_b,
           moe_w1, moe_w3, moe_w2):
    batch, seq, _ = x.shape
    ctx_len = ctx.shape[1]
    depth = ada_w.shape[0]
    assert depth == 4 and batch + 1 <= 16
    rows = _Rows(batch, seq, ctx_len)

    cvec = jnp.concatenate([c, c_ctx[None, :], jnp.zeros((16 - batch - 1, D), F32)], axis=0)
    mod = _ada_all(cvec, ada_w, ada_b)[:, :batch + 1].reshape(depth, batch + 1, 6, D)
    gains = jnp.stack([norm_mix_g, norm_ffn_g], axis=1)[:, None]
    modg = jnp.concatenate([mod, jnp.broadcast_to(gains, (depth, batch + 1, 2, D))], axis=2)

    bf = lambda w: w.astype(BF16)
    h = jnp.concatenate([x.reshape(-1, D), ctx.reshape(-1, D)], axis=0)
    tables = _rope_tables(rows)

    bg, y = _conv_in(rows, h, modg[0], bf(sc_in_w[0]))
    conv_w = jnp.pad(sc_conv_w[0], ((0, 5), (0, 0)))
    h = _conv_out(rows, h, modg[0], bg, y, conv_w, bf(sc_out_w[0]))
    h = _ffn(rows, rows.all_tiles, h, modg[0], bf(ffn_w1[0]), bf(ffn_w3[0]), bf(ffn_w2[0]))

    nq_chunks = D // CN
    qkv = _qkv(rows, h, modg[1], bf(da_qkv_w[0]), da_q_norm_g[0], da_k_norm_g[0], tables, nq_chunks, nq_chunks)
    lam_init = 0.8 - 0.6 * math.exp(-0.3 * 1)
    o_lat, o_ctx = _diff_attn(rows, qkv, da_lambda[0], da_sub_norm_g[0], lam_init)
    h = _out_proj(rows, rows.all_tiles, h, modg[1], o_lat, o_ctx, bf(da_out_w[0]))
    h = _moe(rows, rows.all_tiles, h, modg[1], moe_router_w[0], moe_router_b[0], bf(moe_w1[0]), bf(moe_w3[0]),
             bf(moe_w2[0]))

    h = _gmlp(rows, h, modg[2], bf(cm_in_w[0]), cm_in_b[0].reshape(1, -1), cm_v_norm_g[0].reshape(1, -1),
              bf(cm_ws[0]), cm_bs[0].T, bf(cm_out_w[0]))
    h = _ffn(rows, rows.all_tiles, h, modg[2], bf(ffn_w1[1]), bf(ffn_w3[1]), bf(ffn_w2[1]))

    kv_chunks = SWA_KV_HEADS * HEAD_DIM // CN
    qkv = _qkv(rows, h, modg[3], bf(sw_qkv_w[0]), sw_q_norm_g[0], sw_k_norm_g[0], tables, nq_chunks, kv_chunks)
    o = _swa(rows, qkv, sw_sink[0])
    h = _out_proj(rows, rows.lat_tiles, h, modg[3], o, o, bf(sw_out_w[0]))
    h = _moe(rows, rows.lat_tiles, h, modg[3], moe_router_w[1], moe_router_b[1], bf(moe_w1[1]), bf(moe_w3[1]),
             bf(moe_w2[1]))
    return h.reshape(batch, seq, D)
```

```python
import functools
import math

import jax
import jax.numpy as jnp
from jax import lax
from jax.experimental import pallas as pl
from jax.experimental.pallas import tpu as pltpu

D = 1024
HEAD_DIM = 64
GRID_W = 64
ROPE_HALF = HEAD_DIM // 2
ROPE_BASE = 10000.0
DIFF_HEADS = D // (2 * HEAD_DIM)
SWA_Q_HEADS = D // HEAD_DIM
SWA_KV_HEADS = 4
SWA_GROUP = SWA_Q_HEADS // SWA_KV_HEADS
SWA_WINDOW = 128
CHUNK = 128
CM_WIDTH = 2 * D
CM_GROUPS = 8
CM_GW = CM_WIDTH // CM_GROUPS
FFN_DIM = 2816
N_EXPERTS = 8
EPS = 1e-6

F32 = jnp.float32
BF16 = jnp.bfloat16
HIGHEST = lax.Precision.HIGHEST
LOG2E = math.log2(math.e)

TM = 512
CN = 256
MOE_RB = 512
ATT_TQ = 512
ATT_TK = 1024
SWA_TQ = 256
ROUTE_W = 128
MAX_UNSHIFTED_LOGIT = 60.0
VMEM_LIMIT = 56 << 20


def _params(n_grid):
    return pltpu.CompilerParams(dimension_semantics=("arbitrary",) * n_grid, vmem_limit_bytes=VMEM_LIMIT)


def _resident(shape):
    zeros = (0,) * len(shape)
    return pl.BlockSpec(shape, lambda *_: zeros, pipeline_mode=pl.Buffered(1))


def _sigmoid(x):
    return 1.0 / (1.0 + jnp.exp(-x))


def _gelu_tanh(x):
    return 0.5 * x * (1.0 + jnp.tanh(math.sqrt(2.0 / math.pi) * (x + 0.044715 * (x * x * x))))


def _modnorm(x, g, shift, scale):
    y = x * lax.rsqrt(jnp.mean(x * x, axis=-1, keepdims=True) + EPS)
    return (y * g) * (1.0 + scale) + shift


def _dot(a, b):
    return jnp.dot(a, b, preferred_element_type=F32)


def _mix_mod(m):
    return m[6:7], m[0:1], m[1:2]


def _ffn_mod(m):
    return m[7:8], m[3:4], m[4:5]


def _ada_kernel(c_ref, w_ref, b_ref, o_ref):
    c = c_ref[...]
    s = c * _sigmoid(c)
    o_ref[0] = jnp.dot(s, w_ref[0], precision=HIGHEST, preferred_element_type=F32) + b_ref[0]


def _ada_all(cvec, ada_w, ada_b):
    depth, _, n = ada_w.shape
    tn = 1536
    rows = cvec.shape[0]
    return pl.pallas_call(
        _ada_kernel,
        grid=(depth, n // tn),
        in_specs=[pl.BlockSpec((rows, D), lambda l, j: (0, 0)),
                  pl.BlockSpec((1, D, tn), lambda l, j: (l, 0, j)),
                  pl.BlockSpec((1, 1, tn), lambda l, j: (l, 0, j))],
        out_specs=pl.BlockSpec((1, rows, tn), lambda l, j: (l, 0, j)),
        out_shape=jax.ShapeDtypeStruct((depth, rows, n), F32),
        compiler_params=_params(2),
        name="adaln",
    )(cvec, ada_w, ada_b.reshape(depth, 1, n))


class _Rows:
    def __init__(self, batch, seq, ctx_len):
        self.batch, self.seq, self.ctx_len = batch, seq, ctx_len
        self.n_lat = batch * seq
        self.n_ctx = batch * ctx_len
        self.n_all = self.n_lat + self.n_ctx
        assert seq % TM == 0 and self.n_ctx % TM == 0 and TM % ctx_len == 0
        self.lat_tiles = self.n_lat // TM
        self.all_tiles = self.n_all // TM
        self.tiles_per_seq = seq // TM

    def mod_spec(self):
        tps, batch = self.tiles_per_seq, self.batch
        return pl.BlockSpec((1, 8, D), lambda i, *_: (jnp.minimum(i // tps, batch), 0, 0))

    def row_spec(self, width):
        return pl.BlockSpec((TM, width), lambda i, *_: (i, 0))

    def split_specs(self, width):
        lat_tiles = self.lat_tiles
        return [pl.BlockSpec((TM, width), lambda i, *_: (jnp.minimum(i, lat_tiles - 1), 0)),
                pl.BlockSpec((TM, width), lambda i, *_: (jnp.maximum(i - lat_tiles, 0), 0))]


def _conv_in_kernel(lat_tiles, x_ref, ctx_ref, mod_ref, w_ref, bg_ref, y_ref):
    h = jnp.where(pl.program_id(0) < lat_tiles, x_ref[...], ctx_ref[...])
    a = _modnorm(h, *_mix_mod(mod_ref[0])).astype(BF16)
    for c in range(D // CN):
        lo = c * CN
        bg_ref[:, lo:lo + CN] = _dot(a, w_ref[:, lo:lo + CN]).astype(BF16)
        cg = _dot(a, w_ref[:, D + lo:D + lo + CN])
        xv = _dot(a, w_ref[:, 2 * D + lo:2 * D + lo + CN])
        y_ref[:, lo:lo + CN] = (cg * xv).astype(BF16)


def _conv_in(rows, x, ctx, modg, w_in):
    out = jax.ShapeDtypeStruct((rows.n_all, D), BF16)
    return pl.pallas_call(
        functools.partial(_conv_in_kernel, rows.lat_tiles),
        grid=(rows.all_tiles,),
        in_specs=rows.split_specs(D) + [rows.mod_spec(), _resident((D, 3 * D))],
        out_specs=[rows.row_spec(D), rows.row_spec(D)],
        out_shape=[out, out],
        compiler_params=_params(1),
        name="conv_in",
    )(x, ctx, modg, w_in)


HALO = 16


def _conv_out_kernel(n_lat, seq, ctx_len, x_ref, ctx_ref, mod_ref, bg_ref, y_ref, yp_ref, yn_ref, cw_ref, w_ref,
                     o_ref):
    i = pl.program_id(0)
    h = jnp.where(i * TM < n_lat, x_ref[...], ctx_ref[...])
    m = mod_ref[0]
    y = y_ref[...].astype(F32)
    row = lax.broadcasted_iota(jnp.int32, (TM, 1), 0)
    grow = row + i * TM
    seq_len = jnp.where(grow < n_lat, seq, ctx_len)
    pos = grow & (seq_len - 1)
    prev_row = yp_ref[HALO - 1:HALO, :].astype(F32)
    next_row = yn_ref[0:1, :].astype(F32)
    y_m1 = jnp.where(row == 0, prev_row, pltpu.roll(y, 1, 0))
    y_m1 = jnp.where(pos == 0, 0.0, y_m1)
    y_p1 = jnp.where(row == TM - 1, next_row, pltpu.roll(y, TM - 1, 0))
    y_p1 = jnp.where(pos == seq_len - 1, 0.0, y_p1)
    conv = cw_ref[0:1, :] * y_m1 + cw_ref[1:2, :] * y + cw_ref[2:3, :] * y_p1
    z = (bg_ref[...].astype(F32) * conv).astype(BF16)
    o_ref[...] = h + m[2:3] * _dot(z, w_ref[...])


def _conv_out(rows, x, ctx, modg, bg, y, conv_w, w_out):
    hb = TM // HALO
    last = rows.n_all // HALO - 1
    return pl.pallas_call(
        functools.partial(_conv_out_kernel, rows.n_lat, rows.seq, rows.ctx_len),
        grid=(rows.all_tiles,),
        in_specs=rows.split_specs(D) + [rows.mod_spec(), rows.row_spec(D), rows.row_spec(D),
                  pl.BlockSpec((HALO, D), lambda i: (jnp.maximum(i * hb - 1, 0), 0)),
                  pl.BlockSpec((HALO, D), lambda i: (jnp.minimum((i + 1) * hb, last), 0)),
                  _resident((8, D)), _resident((D, D))],
        out_specs=rows.row_spec(D),
        out_shape=jax.ShapeDtypeStruct((rows.n_all, D), F32),
        compiler_params=_params(1),
        name="conv_out",
    )(x, ctx, modg, bg, y, y, y, conv_w, w_out)


def _swiglu_into(a, w1_ref, w3_ref, t_ref):
    for c in range(FFN_DIM // CN):
        lo = c * CN
        h1 = _dot(a, w1_ref[:, lo:lo + CN])
        h3 = _dot(a, w3_ref[:, lo:lo + CN])
        t_ref[:, lo:lo + CN] = (h1 * _sigmoid(h1) * h3).astype(BF16)


def _ffn_kernel(h_ref, mod_ref, w1_ref, w3_ref, w2_ref, o_ref, t_ref):
    m = mod_ref[0]
    x = h_ref[...]
    a = _modnorm(x, *_ffn_mod(m)).astype(BF16)
    _swiglu_into(a, w1_ref, w3_ref, t_ref)
    o_ref[...] = x + m[5:6] * _dot(t_ref[...], w2_ref[...])


def _ffn(rows, n_tiles, h, modg, w1, w3, w2):
    return pl.pallas_call(
        _ffn_kernel,
        grid=(n_tiles,),
        in_specs=[rows.row_spec(D), rows.mod_spec(), _resident((D, FFN_DIM)), _resident((D, FFN_DIM)),
                  _resident((FFN_DIM, D))],
        out_specs=rows.row_spec(D),
        out_shape=jax.ShapeDtypeStruct((n_tiles * TM, D), F32),
        scratch_shapes=[pltpu.VMEM((TM, FFN_DIM), BF16)],
        compiler_params=_params(1),
        name="ffn_dense",
    )(h, modg, w1, w3, w2)


def _norm_rope(x, gmat_ref, cos, sin):
    ms = _dot((x * x).astype(BF16), gmat_ref[...]) * (1.0 / HEAD_DIM)
    xn = x * lax.rsqrt(ms + EPS)
    lane = lax.broadcasted_iota(jnp.int32, (1, CN), 1)
    first_half = (lane & (ROPE_HALF - 1)) < (ROPE_HALF // 2)
    partner = jnp.where(first_half, pltpu.roll(xn, CN - ROPE_HALF // 2, 1), pltpu.roll(xn, ROPE_HALF // 2, 1))
    return xn * cos + partner * sin


def _qkv_kernel(n_qk_chunks, n_q_chunks, h_ref, mod_ref, w_ref, gmat_ref, qcos_ref, qsin_ref, kcos_ref, ksin_ref,
                o_ref):
    a = _modnorm(h_ref[...], *_mix_mod(mod_ref[0])).astype(BF16)
    n_chunks = w_ref.shape[1] // CN
    for c in range(n_chunks):
        lo = c * CN
        acc = _dot(a, w_ref[:, lo:lo + CN])
        if c < n_q_chunks:
            acc = _norm_rope(acc, gmat_ref, qcos_ref[...], qsin_ref[...])
        elif c < n_qk_chunks:
            acc = _norm_rope(acc, gmat_ref, kcos_ref[...], ksin_ref[...])
        o_ref[:, lo:lo + CN] = acc.astype(BF16)


def _rope_tables(rows, gain, scale):
    seq = rows.seq
    pos = jnp.arange(seq)
    n_freq = ROPE_HALF // 2
    inv = ROPE_BASE ** (-jnp.arange(n_freq, dtype=F32) / n_freq)
    ang_r = (pos // GRID_W).astype(F32)[:, None] * inv
    ang_c = (pos % GRID_W).astype(F32)[:, None] * inv
    cos = jnp.concatenate([jnp.cos(ang_r)] * 2 + [jnp.cos(ang_c)] * 2, axis=-1)
    sin = jnp.concatenate([-jnp.sin(ang_r), jnp.sin(ang_r), -jnp.sin(ang_c), jnp.sin(ang_c)], axis=-1)
    cos = jnp.concatenate([cos, jnp.ones((TM, HEAD_DIM), F32)], axis=0)
    sin = jnp.concatenate([sin, jnp.zeros((TM, HEAD_DIM), F32)], axis=0)
    dim = jnp.arange(HEAD_DIM)
    partner = jnp.where((dim % ROPE_HALF) < n_freq, dim + n_freq, dim - n_freq)
    g = gain.astype(F32) * scale
    reps = CN // HEAD_DIM
    return jnp.tile(cos * g[None, :], (1, reps)), jnp.tile(sin * g[partner][None, :], (1, reps))


def _qkv(rows, h, modg, w, q_gain, k_gain, n_q_chunks, n_k_chunks):
    width = w.shape[1]
    head = jnp.arange(CN) // HEAD_DIM
    gmat = (head[:, None] == head[None, :]).astype(BF16)
    tables = _rope_tables(rows, q_gain, LOG2E * HEAD_DIM ** -0.5) + _rope_tables(rows, k_gain, 1.0)
    tps, lat_tiles = rows.tiles_per_seq, rows.lat_tiles
    tab_spec = pl.BlockSpec((TM, CN), lambda i: (jnp.where(i < lat_tiles, i % tps, tps), 0))
    return pl.pallas_call(
        functools.partial(_qkv_kernel, n_q_chunks + n_k_chunks, n_q_chunks),
        grid=(rows.all_tiles,),
        in_specs=[rows.row_spec(D), rows.mod_spec(), _resident((D, width)), _resident((CN, CN))] + [tab_spec] * 4,
        out_specs=rows.row_spec(width),
        out_shape=jax.ShapeDtypeStruct((rows.n_all, width), BF16),
        compiler_params=_params(1),
        name="qkv_proj",
    )(h, modg, w, gmat, *tables)


DH2 = 2 * HEAD_DIM


def _diff_attn_kernel(lam_init, n_lat_chunks, q_ref, kc_ref, vc_ref, kl_ref, vl_ref, lam_ref, subg_ref, o_ref,
                      m_ref, acc_ref, vca_ref, vla_ref, kn_ref):
    def max_norm_sq(x):
        xf = x.astype(F32)
        return jnp.max(jnp.sum(xf * xf, axis=-1, keepdims=True), axis=0, keepdims=True)

    def per_head_setup():
        vca_ref[:, :DH2] = vc_ref[...]
        vca_ref[:, DH2:] = jnp.ones((vc_ref.shape[0], DH2), BF16)
        kn = max_norm_sq(kc_ref[...])
        if n_lat_chunks:
            vla_ref[:, :DH2] = vl_ref[...]
            vla_ref[:, DH2:] = jnp.ones((vl_ref.shape[0], DH2), BF16)
            kn = jnp.maximum(kn, max_norm_sq(kl_ref[...]))
        kn_ref[...] = jnp.broadcast_to(kn, kn_ref.shape)

    if n_lat_chunks:
        pl.when(pl.program_id(2) == 0)(per_head_setup)
    else:
        per_head_setup()

    q = q_ref[...]
    lane = lax.broadcasted_iota(jnp.int32, (1, DH2), 1)
    zero = jnp.zeros((), BF16)
    qs = (jnp.where(lane < HEAD_DIM, q, zero), jnp.where(lane >= HEAD_DIM, q, zero))
    acc_ref[...] = jnp.zeros(acc_ref.shape, F32)
    nt = (((1,), (1,)), ((), ()))

    def over_keys(update):
        update(kc_ref[...], vca_ref[...])
        if n_lat_chunks:
            tk = kl_ref.shape[0] // n_lat_chunks

            def body(c, carry):
                start = pl.multiple_of(c * tk, tk)
                update(kl_ref[pl.ds(start, tk), :], vla_ref[pl.ds(start, tk), :])
                return carry
            lax.fori_loop(0, n_lat_chunks, body, 0)

    def update_unshifted(k, va):
        for mi in range(2):
            s = lax.dot_general(qs[mi], k, nt, preferred_element_type=F32)
            acc_ref[mi] += _dot(jnp.exp2(s).astype(BF16), va)

    def update_online(k, va):
        reps = k.shape[0] // DH2
        for mi in range(2):
            s = lax.dot_general(qs[mi], k, nt, preferred_element_type=F32)
            m_old = m_ref[mi]
            m_new = jnp.maximum(m_old, jnp.max(s, axis=-1, keepdims=True))
            alpha = jnp.exp2(m_old - m_new)
            p = jnp.exp2(s - jnp.concatenate([m_new] * reps, axis=-1))
            acc_ref[mi] = jnp.concatenate([alpha, alpha], axis=-1) * acc_ref[mi] + _dot(p.astype(BF16), va)
            m_ref[mi] = m_new

    def online():
        m_ref[...] = jnp.full(m_ref.shape, -jnp.inf, F32)
        over_keys(update_online)

    bound_sq = max_norm_sq(q) * kn_ref[0:1, 0:1]
    small_logits = bound_sq[0, 0] <= MAX_UNSHIFTED_LOGIT ** 2
    pl.when(small_logits)(lambda: over_keys(update_unshifted))
    pl.when(jnp.logical_not(small_logits))(online)

    lp = lam_ref[...]
    lam = (jnp.exp(jnp.sum(lp[0:1] * lp[1:2], axis=-1, keepdims=True))
           - jnp.exp(jnp.sum(lp[2:3] * lp[3:4], axis=-1, keepdims=True)) + lam_init)
    acc0 = acc_ref[0]
    acc1 = acc_ref[1]
    o = acc0[:, :DH2] / acc0[:, DH2:] - lam * (acc1[:, :DH2] / acc1[:, DH2:])
    o = o * lax.rsqrt(jnp.mean(o * o, axis=-1, keepdims=True) + EPS) * subg_ref[...]
    o_ref[...] = (o * (1.0 - lam_init)).astype(BF16)


def _diff_attn(rows, qkv, lam_p, sub_g, lam_init):
    batch, seq, ctx_len = rows.batch, rows.seq, rows.ctx_len
    nq = seq // ATT_TQ
    nh = DIFF_HEADS
    ctx0 = rows.n_lat // ctx_len
    sub_g = sub_g.reshape(1, DH2)
    small = [_resident((4, HEAD_DIM)), _resident((1, DH2))]
    kc_spec = pl.BlockSpec((ctx_len, DH2), lambda b, h, *_: (ctx0 + b, nh + h))
    vc_spec = pl.BlockSpec((ctx_len, DH2), lambda b, h, *_: (ctx0 + b, 2 * nh + h))

    def scratch(tq, n_lat_keys):
        return [pltpu.VMEM((2, tq, DH2), F32), pltpu.VMEM((2, tq, 2 * DH2), F32),
                pltpu.VMEM((ctx_len, 2 * DH2), BF16), pltpu.VMEM((n_lat_keys, 2 * DH2), BF16),
                pltpu.VMEM((8, DH2), F32)]

    o_lat = pl.pallas_call(
        functools.partial(_diff_attn_kernel, lam_init, pl.cdiv(seq, ATT_TK)),
        grid=(batch, nh, nq),
        in_specs=[pl.BlockSpec((ATT_TQ, DH2), lambda b, h, i: (b * nq + i, h)),
                  kc_spec, vc_spec,
                  pl.BlockSpec((seq, DH2), lambda b, h, i: (b, nh + h)),
                  pl.BlockSpec((seq, DH2), lambda b, h, i: (b, 2 * nh + h))] + small,
        out_specs=pl.BlockSpec((ATT_TQ, DH2), lambda b, h, i: (b * nq + i, h)),
        out_shape=jax.ShapeDtypeStruct((rows.n_lat, D), BF16),
        scratch_shapes=scratch(ATT_TQ, seq),
        compiler_params=_params(3),
        name="diff_attn_latent",
    )(qkv, qkv, qkv, qkv, qkv, lam_p, sub_g)

    def ctx_kernel(q_ref, kc_ref, vc_ref, lam_ref, subg_ref, o_ref, *scratch_refs):
        _diff_attn_kernel(lam_init, 0, q_ref, kc_ref, vc_ref, None, None, lam_ref, subg_ref, o_ref, *scratch_refs)

    o_ctx = pl.pallas_call(
        ctx_kernel,
        grid=(batch, nh),
        in_specs=[pl.BlockSpec((ctx_len, DH2), lambda b, h: (ctx0 + b, h)), kc_spec, vc_spec] + small,
        out_specs=pl.BlockSpec((ctx_len, DH2), lambda b, h: (b, h)),
        out_shape=jax.ShapeDtypeStruct((rows.n_ctx, D), BF16),
        scratch_shapes=scratch(ctx_len, 16),
        compiler_params=_params(2),
        name="diff_attn_context",
    )(qkv, qkv, qkv, lam_p, sub_g)
    return o_lat, o_ctx


def _out_proj_kernel(lat_tiles, h_ref, mod_ref, zl_ref, zc_ref, w_ref, o_ref):
    i = pl.program_id(0)

    def project(z_ref):
        o_ref[...] = h_ref[...] + mod_ref[0][2:3] * _dot(z_ref[...], w_ref[...])

    pl.when(i < lat_tiles)(lambda: project(zl_ref))
    pl.when(i >= lat_tiles)(lambda: project(zc_ref))


def _out_proj(rows, n_tiles, h, modg, z_lat, z_ctx, w):
    lat_tiles = rows.lat_tiles
    return pl.pallas_call(
        functools.partial(_out_proj_kernel, lat_tiles),
        grid=(n_tiles,),
        in_specs=[rows.row_spec(D), rows.mod_spec(),
                  pl.BlockSpec((TM, D), lambda i: (jnp.minimum(i, lat_tiles - 1), 0)),
                  pl.BlockSpec((TM, D), lambda i: (jnp.maximum(i - lat_tiles, 0), 0)),
                  _resident((D, D))],
        out_specs=rows.row_spec(D),
        out_shape=jax.ShapeDtypeStruct((n_tiles * TM, D), F32),
        compiler_params=_params(1),
        name="out_proj",
    )(h, modg, z_lat, z_ctx, w)


def _gmlp_kernel(h_ref, mod_ref, win_ref, bin_ref, vg_ref, ws_ref, bs_ref, wout_ref, o_ref, u_ref, v_ref, t_ref):
    m = mod_ref[0]
    x = h_ref[...]
    a = _modnorm(x, *_mix_mod(m)).astype(BF16)
    ssq = jnp.zeros((TM, 1), F32)
    for c in range(2 * CM_WIDTH // CN):
        lo = c * CN
        z = _gelu_tanh(_dot(a, win_ref[:, lo:lo + CN]) + bin_ref[:, lo:lo + CN])
        if lo < CM_WIDTH:
            u_ref[:, lo:lo + CN] = z
        else:
            v_ref[:, lo - CM_WIDTH:lo - CM_WIDTH + CN] = z
            ssq = ssq + jnp.sum(z * z, axis=-1, keepdims=True)
    inv = lax.rsqrt(ssq * (1.0 / CM_WIDTH) + EPS)
    for g in range(CM_GROUPS):
        lo = g * CM_GW
        vn = (v_ref[:, lo:lo + CM_GW] * inv * vg_ref[:, lo:lo + CM_GW]).astype(BF16)
        for r in range(TM // CHUNK):
            r0 = r * CHUNK
            sv = _dot(ws_ref[g], vn[r0:r0 + CHUNK]) + bs_ref[:, g:g + 1]
            t_ref[r0:r0 + CHUNK, lo:lo + CM_GW] = (u_ref[r0:r0 + CHUNK, lo:lo + CM_GW] * sv).astype(BF16)
    o_ref[...] = x + m[2:3] * _dot(t_ref[...], wout_ref[...])


def _gmlp(rows, h, modg, w_in, b_in, v_g, w_s, b_s, w_out):
    return pl.pallas_call(
        _gmlp_kernel,
        grid=(rows.all_tiles,),
        in_specs=[rows.row_spec(D), rows.mod_spec(), _resident((D, 2 * CM_WIDTH)), _resident((1, 2 * CM_WIDTH)),
                  _resident((1, CM_WIDTH)), _resident((CM_GROUPS, CHUNK, CHUNK)), _resident((CHUNK, CM_GROUPS)),
                  _resident((CM_WIDTH, D))],
        out_specs=rows.row_spec(D),
        out_shape=jax.ShapeDtypeStruct((rows.n_all, D), F32),
        scratch_shapes=[pltpu.VMEM((TM, CM_WIDTH), F32), pltpu.VMEM((TM, CM_WIDTH), F32),
                        pltpu.VMEM((TM, CM_WIDTH), BF16)],
        compiler_params=_params(1),
        name="gmlp",
    )(h, modg, w_in, b_in, v_g, w_s, b_s, w_out)


SWA_BAND = SWA_TQ + 2 * SWA_WINDOW


SWA_PAIR = 2 * HEAD_DIM
N_KV_VARIANTS = 2 * SWA_KV_HEADS


def _swa_kernel(seq, sink_ref, q_ref, kc_ref, vc_ref, kl_ref, vl_ref, o_ref, kcv_ref, vcv_ref, klv_ref, vlv_ref,
                kn_ref):
    qi = pl.program_id(1)
    lane = lax.broadcasted_iota(jnp.int32, (1, SWA_PAIR), 1)
    lo_half = lane < HEAD_DIM

    def max_norm_sq(x):
        xf = x.astype(F32)
        return jnp.max(jnp.sum(xf * xf, axis=-1, keepdims=True), axis=0, keepdims=True)

    @pl.when(qi == 0)
    def _per_batch_setup():
        kn = jnp.zeros((1, 1), F32)
        for src_ref, dst_ref in ((kc_ref, kcv_ref), (vc_ref, vcv_ref), (kl_ref, klv_ref), (vl_ref, vlv_ref)):
            for half in range(SWA_KV_HEADS // 2):
                x = src_ref[:, half * SWA_PAIR:(half + 1) * SWA_PAIR].astype(F32)
                xr = pltpu.roll(x, HEAD_DIM, 1)
                j0, j1 = 2 * half, 2 * half + 1
                dst_ref[2 * j0] = jnp.where(lo_half, x, 0.0).astype(BF16)
                dst_ref[2 * j0 + 1] = jnp.where(lo_half, 0.0, xr).astype(BF16)
                dst_ref[2 * j1] = jnp.where(lo_half, xr, 0.0).astype(BF16)
                dst_ref[2 * j1 + 1] = jnp.where(lo_half, 0.0, x).astype(BF16)
                if src_ref is kc_ref or src_ref is kl_ref:
                    kn = jnp.maximum(kn, max_norm_sq(x))
        kn_ref[...] = jnp.broadcast_to(kn, kn_ref.shape)

    q0 = qi * SWA_TQ
    start = jnp.clip(q0 - SWA_WINDOW, 0, seq - SWA_BAND)
    start = pl.multiple_of(start, SWA_WINDOW)
    qpos = q0 + lax.broadcasted_iota(jnp.int32, (SWA_TQ, SWA_BAND), 0)
    kpos = start + lax.broadcasted_iota(jnp.int32, (SWA_TQ, SWA_BAND), 1)
    in_band = jnp.abs(qpos - kpos) <= SWA_WINDOW
    nt = (((1,), (1,)), ((), ()))

    def unshifted():
        in_band2 = jnp.concatenate([in_band, in_band], axis=0)
        top = lax.broadcasted_iota(jnp.int32, (2 * SWA_TQ, 1), 0) < SWA_TQ
        for j in range(SWA_KV_HEADS):
            lo = j * SWA_GROUP * HEAD_DIM
            q2 = jnp.concatenate([q_ref[:, lo:lo + SWA_PAIR], q_ref[:, lo + SWA_PAIR:lo + 2 * SWA_PAIR]], axis=0)
            out = jnp.zeros((2 * SWA_TQ, SWA_PAIR), F32)
            for var in range(2):
                v = 2 * j + var
                s_c = lax.dot_general(q2, kcv_ref[v], nt, preferred_element_type=F32)
                s_b = lax.dot_general(q2, klv_ref[v, pl.ds(start, SWA_BAND), :], nt, preferred_element_type=F32)
                p_c = jnp.exp2(s_c)
                p_b = jnp.where(in_band2, jnp.exp2(s_b), 0.0)
                hq = j * SWA_GROUP + var
                sink = jnp.where(top, sink_ref[hq] * LOG2E, sink_ref[hq + 2] * LOG2E)
                denom = (jnp.sum(p_c, axis=-1, keepdims=True) + jnp.sum(p_b, axis=-1, keepdims=True)
                         + jnp.exp2(sink))
                o = (_dot(p_c.astype(BF16), vcv_ref[v])
                     + _dot(p_b.astype(BF16), vlv_ref[v, pl.ds(start, SWA_BAND), :]))
                out = out + o / denom
            o_ref[:, lo:lo + SWA_PAIR] = out[:SWA_TQ].astype(BF16)
            o_ref[:, lo + SWA_PAIR:lo + 2 * SWA_PAIR] = out[SWA_TQ:].astype(BF16)

    def shifted():
        kb = kl_ref[pl.ds(start, SWA_BAND), :]
        vb = vl_ref[pl.ds(start, SWA_BAND), :]
        kc = kc_ref[...]
        vc = vc_ref[...]
        for j in range(SWA_KV_HEADS):
            kj = slice(j * HEAD_DIM, (j + 1) * HEAD_DIM)
            kbj, vbj, kcj, vcj = kb[:, kj], vb[:, kj], kc[:, kj], vc[:, kj]
            outs = []
            for g in range(SWA_GROUP):
                hq = j * SWA_GROUP + g
                qh = q_ref[:, hq * HEAD_DIM:(hq + 1) * HEAD_DIM]
                s_c = lax.dot_general(qh, kcj, nt, preferred_element_type=F32)
                s_b = lax.dot_general(qh, kbj, nt, preferred_element_type=F32)
                s_b = jnp.where(in_band, s_b, -jnp.inf)
                sink = sink_ref[hq] * LOG2E
                mx = jnp.maximum(jnp.maximum(jnp.max(s_c, axis=-1, keepdims=True),
                                             jnp.max(s_b, axis=-1, keepdims=True)), sink)
                p_c = jnp.exp2(s_c - mx)
                p_b = jnp.exp2(s_b - mx)
                denom = (jnp.sum(p_c, axis=-1, keepdims=True) + jnp.sum(p_b, axis=-1, keepdims=True)
                         + jnp.exp2(sink - mx))
                o = _dot(p_c.astype(BF16), vcj) + _dot(p_b.astype(BF16), vbj)
                outs.append(o / denom)
            lo = j * SWA_GROUP * HEAD_DIM
            o_ref[:, lo:lo + SWA_GROUP * HEAD_DIM] = jnp.concatenate(outs, axis=-1).astype(BF16)

    qn = jnp.zeros((1, 1), F32)
    for pair in range(SWA_Q_HEADS // 2):
        qn = jnp.maximum(qn, max_norm_sq(q_ref[:, pair * SWA_PAIR:(pair + 1) * SWA_PAIR]))
    max_sink = sink_ref[0]
    for hq in range(1, SWA_Q_HEADS):
        max_sink = jnp.maximum(max_sink, sink_ref[hq])
    small_logits = jnp.logical_and((qn * kn_ref[0:1, 0:1])[0, 0] <= MAX_UNSHIFTED_LOGIT ** 2,
                                   max_sink * LOG2E <= MAX_UNSHIFTED_LOGIT)
    pl.when(small_logits)(unshifted)
    pl.when(jnp.logical_not(small_logits))(shifted)


def _swa(rows, qkv, sink):
    batch, seq, ctx_len = rows.batch, rows.seq, rows.ctx_len
    nq = seq // SWA_TQ
    kvw = SWA_KV_HEADS * HEAD_DIM
    k_col = D // kvw
    ctx0 = rows.n_lat // ctx_len
    return pl.pallas_call(
        functools.partial(_swa_kernel, seq),
        grid=(batch, nq),
        in_specs=[pl.BlockSpec(memory_space=pltpu.SMEM),
                  pl.BlockSpec((SWA_TQ, D), lambda b, i: (b * nq + i, 0)),
                  pl.BlockSpec((ctx_len, kvw), lambda b, i: (ctx0 + b, k_col)),
                  pl.BlockSpec((ctx_len, kvw), lambda b, i: (ctx0 + b, k_col + 1)),
                  pl.BlockSpec((seq, kvw), lambda b, i: (b, k_col)),
                  pl.BlockSpec((seq, kvw), lambda b, i: (b, k_col + 1))],
        out_specs=pl.BlockSpec((SWA_TQ, D), lambda b, i: (b * nq + i, 0)),
        out_shape=jax.ShapeDtypeStruct((rows.n_lat, D), BF16),
        scratch_shapes=[pltpu.VMEM((N_KV_VARIANTS, ctx_len, SWA_PAIR), BF16),
                        pltpu.VMEM((N_KV_VARIANTS, ctx_len, SWA_PAIR), BF16),
                        pltpu.VMEM((N_KV_VARIANTS, seq, SWA_PAIR), BF16),
                        pltpu.VMEM((N_KV_VARIANTS, seq, SWA_PAIR), BF16),
                        pltpu.VMEM((8, SWA_PAIR), F32)],
        compiler_params=_params(2),
        name="swa_attn",
    )(sink, qkv, qkv, qkv, qkv, qkv)


def _router_kernel(h_ref, mod_ref, wr_ref, br_ref, a_ref, route_ref, cnt_ref):
    a = _modnorm(h_ref[...], *_ffn_mod(mod_ref[0]))
    a_hi = a.astype(BF16)
    a_ref[...] = a_hi
    a_lo = (a - a_hi.astype(F32)).astype(BF16)
    hi_part = _dot(a_hi, wr_ref[...])
    logits = (hi_part[:, :ROUTE_W] + hi_part[:, ROUTE_W:]) + _dot(a_lo, wr_ref[:, :ROUTE_W]) + br_ref[...]
    lane = lax.broadcasted_iota(jnp.int32, (TM, ROUTE_W), 1)
    m1 = jnp.max(logits, axis=-1, keepdims=True)
    i1 = jnp.min(jnp.where(logits == m1, lane, ROUTE_W), axis=-1, keepdims=True)
    rest = jnp.where(lane == i1, -jnp.inf, logits)
    m2 = jnp.max(rest, axis=-1, keepdims=True)
    i2 = jnp.min(jnp.where(rest == m2, lane, ROUTE_W), axis=-1, keepdims=True)
    e2 = jnp.exp(m2 - m1)
    gate1 = 1.0 / (1.0 + e2)
    gate2 = e2 / (1.0 + e2)
    rec = jnp.where(lane == 0, i1.astype(F32), 0.0)
    rec = jnp.where(lane == 1, i2.astype(F32), rec)
    rec = jnp.where(lane == 2, gate1, rec)
    route_ref[...] = jnp.where(lane == 3, gate2, rec)
    chosen = jnp.where((lane == i1) | (lane == i2), 1.0, 0.0)
    cnt_ref[0] = jnp.broadcast_to(jnp.sum(chosen, axis=0, keepdims=True), (8, ROUTE_W))


def _router(rows, n_tiles, h, modg, w_r, b_r):
    pad = ROUTE_W - N_EXPERTS
    w_pad = jnp.pad(w_r, ((0, 0), (0, pad)))
    w_hi = w_pad.astype(BF16)
    w_lo = (w_pad - w_hi.astype(F32)).astype(BF16)
    w_split = jnp.concatenate([w_hi, w_lo], axis=1)
    b_pad = jnp.concatenate([b_r.astype(F32), jnp.full((pad,), -1e30, F32)]).reshape(1, ROUTE_W)
    return pl.pallas_call(
        _router_kernel,
        grid=(n_tiles,),
        in_specs=[rows.row_spec(D), rows.mod_spec(), _resident((D, 2 * ROUTE_W)), _resident((1, ROUTE_W))],
        out_specs=[rows.row_spec(D), rows.row_spec(ROUTE_W), pl.BlockSpec((1, 8, ROUTE_W), lambda i: (i, 0, 0))],
        out_shape=[jax.ShapeDtypeStruct((n_tiles * TM, D), BF16), jax.ShapeDtypeStruct((n_tiles * TM, ROUTE_W), F32),
                   jax.ShapeDtypeStruct((n_tiles, 8, ROUTE_W), F32)],
        compiler_params=_params(1),
        name="moe_router",
    )(h, modg, w_split, b_pad)


SEG_ALIGN = 16
SEG_BITS = tuple(1 << b for b in range(TM.bit_length() - 1, SEG_ALIGN.bit_length() - 2, -1))
LROWS = -(-(2 * TM + N_EXPERTS * (SEG_ALIGN - 1)) // 128) * 128
XS_W = D + 128
META_W = 3 * N_EXPERTS


def _segment_copies(meta_ref, tile, local_ref, slots_hbm, sem, to_slots):
    for e in range(N_EXPERTS):
        loff = meta_ref[tile * META_W + e]
        goff = meta_ref[tile * META_W + N_EXPERTS + e]
        seg = meta_ref[tile * META_W + 2 * N_EXPERTS + e]
        for bit in SEG_BITS:
            done = seg & ~(2 * bit - 1)
            lo = local_ref.at[pl.ds(pl.multiple_of(loff + done, SEG_ALIGN), bit)]
            gl = slots_hbm.at[pl.ds(pl.multiple_of(goff + done, SEG_ALIGN), bit)]
            copy = pltpu.make_async_copy(lo, gl, sem) if to_slots else pltpu.make_async_copy(gl, lo, sem)
            yield (seg & bit) != 0, copy


def _run_segment_copies(meta_ref, tile, local_ref, slots_hbm, sem, to_slots):
    for cond, copy in _segment_copies(meta_ref, tile, local_ref, slots_hbm, sem, to_slots):
        pl.when(cond)(copy.start)
    for cond, copy in _segment_copies(meta_ref, tile, local_ref, slots_hbm, sem, to_slots):
        pl.when(cond)(copy.wait)


def _split3(x):
    hi = x.astype(BF16).astype(F32)
    mid = (x - hi).astype(BF16).astype(F32)
    lo = ((x - hi) - mid).astype(BF16).astype(F32)
    return hi, mid, lo


def _dispatch_kernel(n_blocks, meta_ref, fill_ref, a_ref, route_ref, tri_ref, xs_hbm, slot_ref, local_ref, zero_ref,
                     sem, zsem):
    i = pl.program_id(0)

    @pl.when(i == 0)
    def _zero_unwritten_slots():
        zero_ref[...] = jnp.zeros(zero_ref.shape, BF16)
        tails = []
        for e in range(N_EXPERTS):
            off, length = fill_ref[e], fill_ref[N_EXPERTS + e]
            for bit in (b for b in SEG_BITS if b < MOE_RB):
                done = length & ~(2 * bit - 1)
                dst = xs_hbm.at[pl.ds(pl.multiple_of(off + done, SEG_ALIGN), bit)]
                tails.append(((length & bit) != 0, pltpu.make_async_copy(zero_ref.at[pl.ds(0, bit)], dst, zsem)))
        for cond, copy in tails:
            pl.when(cond)(copy.start)

        def block_copy(blk):
            dst = xs_hbm.at[pl.ds(pl.multiple_of(blk * MOE_RB, MOE_RB), MOE_RB)]
            return pltpu.make_async_copy(zero_ref, dst, zsem)

        n_used = fill_ref[2 * N_EXPERTS]
        lax.fori_loop(n_used, n_blocks, lambda blk, c: (block_copy(blk).start(), c)[1], 0)
        for cond, copy in tails:
            pl.when(cond)(copy.wait)
        lax.fori_loop(n_used, n_blocks, lambda blk, c: (block_copy(blk).wait(), c)[1], 0)

    route = route_ref[...]
    lane = lax.broadcasted_iota(jnp.int32, (TM, ROUTE_W), 1)
    pick1 = lane == route[:, 0:1].astype(jnp.int32)
    pick2 = lane == route[:, 1:2].astype(jnp.int32)
    chosen = jnp.where(pick1 | pick2, 1.0, 0.0).astype(BF16)
    rank = _dot(tri_ref[...], chosen)
    lane1 = lax.broadcasted_iota(jnp.int32, (1, ROUTE_W), 1)
    loff = jnp.zeros((1, ROUTE_W), F32)
    for e in range(N_EXPERTS):
        loff = jnp.where(lane1 == e, meta_ref[i * META_W + e].astype(F32), loff)
    place = rank + loff
    slot1 = jnp.sum(jnp.where(pick1, place, 0.0), axis=-1, keepdims=True)
    slot2 = jnp.sum(jnp.where(pick2, place, 0.0), axis=-1, keepdims=True)
    slot_ref[...] = jnp.where(lane == 0, slot1, jnp.where(lane == 1, slot2, 0.0))

    row1 = jnp.transpose(jnp.broadcast_to(slot1, (TM, 128)))[0:1, :]
    row2 = jnp.transpose(jnp.broadcast_to(slot2, (TM, 128)))[0:1, :]
    lrow = lax.broadcasted_iota(jnp.int32, (LROWS, TM), 0).astype(F32)
    p1 = jnp.where(lrow == row1, 1.0, 0.0).astype(BF16)
    p2 = jnp.where(lrow == row2, 1.0, 0.0).astype(BF16)
    local_ref[:, :D] = _dot(p1 + p2, a_ref[...]).astype(BF16)
    gates = []
    for col in (2, 3):
        pieces = _split3(route[:, col:col + 1])
        g = jnp.zeros((TM, ROUTE_W), F32)
        for k, piece in enumerate(pieces):
            g = jnp.where(lane == k, piece, g)
        gates.append(g.astype(BF16))
    local_ref[:, D:] = (_dot(p1, gates[0]) + _dot(p2, gates[1])).astype(BF16)
    _run_segment_copies(meta_ref, i, local_ref, xs_hbm, sem, True)


def _dispatch(n_tiles, n_blocks, a, route, meta, fill):
    tri = (jnp.arange(TM)[:, None] > jnp.arange(TM)[None, :]).astype(BF16)
    return pl.pallas_call(
        functools.partial(_dispatch_kernel, n_blocks),
        grid_spec=pltpu.PrefetchScalarGridSpec(
            num_scalar_prefetch=2,
            grid=(n_tiles,),
            in_specs=[pl.BlockSpec((TM, D), lambda i, *_: (i, 0)),
                      pl.BlockSpec((TM, ROUTE_W), lambda i, *_: (i, 0)),
                      _resident((TM, TM))],
            out_specs=[pl.BlockSpec(memory_space=pl.ANY), pl.BlockSpec((TM, ROUTE_W), lambda i, *_: (i, 0))],
            scratch_shapes=[pltpu.VMEM((LROWS, XS_W), BF16), pltpu.VMEM((MOE_RB, XS_W), BF16),
                            pltpu.SemaphoreType.DMA(()), pltpu.SemaphoreType.DMA(())],
        ),
        out_shape=[jax.ShapeDtypeStruct((n_blocks * MOE_RB, XS_W), BF16),
                   jax.ShapeDtypeStruct((n_tiles * TM, ROUTE_W), F32)],
        compiler_params=_params(1),
        name="moe_dispatch",
    )(meta, fill, a, route, tri)


def _expert_kernel(be_ref, fill_ref, xs_ref, w1_ref, w3_ref, w2_ref, ys_ref, t_ref):
    del be_ref
    used = pl.program_id(0) < fill_ref[2 * N_EXPERTS]

    @pl.when(used)
    def _ffn():
        _swiglu_into(xs_ref[:, :D], w1_ref, w3_ref, t_ref)
        gate = jnp.sum(xs_ref[:, D:].astype(F32), axis=-1, keepdims=True)
        ys_ref[...] = (gate * _dot(t_ref[...], w2_ref[...])).astype(BF16)

    @pl.when(jnp.logical_not(used))
    def _idle():
        ys_ref[...] = jnp.zeros(ys_ref.shape, BF16)


def _experts(xs, blk_exp, fill, w1, w3, w2):
    n_blocks = blk_exp.shape[0]
    last_used = lambda i, fill: jnp.minimum(i, fill[2 * N_EXPERTS] - 1)
    return pl.pallas_call(
        _expert_kernel,
        grid_spec=pltpu.PrefetchScalarGridSpec(
            num_scalar_prefetch=2,
            grid=(n_blocks,),
            in_specs=[pl.BlockSpec((MOE_RB, XS_W), lambda i, be, fill: (last_used(i, fill), 0)),
                      pl.BlockSpec((None, D, FFN_DIM), lambda i, be, fill: (be[i], 0, 0)),
                      pl.BlockSpec((None, D, FFN_DIM), lambda i, be, fill: (be[i], 0, 0)),
                      pl.BlockSpec((None, FFN_DIM, D), lambda i, be, fill: (be[i], 0, 0))],
            out_specs=pl.BlockSpec((MOE_RB, D), lambda i, be, fill: (i, 0)),
            scratch_shapes=[pltpu.VMEM((MOE_RB, FFN_DIM), BF16)],
        ),
        out_shape=jax.ShapeDtypeStruct((n_blocks * MOE_RB, D), BF16),
        compiler_params=_params(1),
        name="moe_experts",
    )(blk_exp, fill, xs, w1, w3, w2)


def _combine_kernel(meta_ref, h_ref, mod_ref, slot_ref, ys_hbm, o_ref, local_ref, sem):
    i = pl.program_id(0)

    @pl.when(i == 0)
    def _init():
        local_ref[...] = jnp.zeros(local_ref.shape, BF16)

    _run_segment_copies(meta_ref, i, local_ref, ys_hbm, sem, False)
    slots = slot_ref[...]
    lcol = lax.broadcasted_iota(jnp.int32, (TM, LROWS), 1).astype(F32)
    gather = jnp.where((lcol == slots[:, 0:1]) | (lcol == slots[:, 1:2]), 1.0, 0.0).astype(BF16)
    o_ref[...] = h_ref[...] + mod_ref[0][5:6] * _dot(gather, local_ref[...])


def _combine(rows, n_tiles, h, modg, slots, ys, meta):
    tps, batch = rows.tiles_per_seq, rows.batch
    return pl.pallas_call(
        _combine_kernel,
        grid_spec=pltpu.PrefetchScalarGridSpec(
            num_scalar_prefetch=1,
            grid=(n_tiles,),
            in_specs=[pl.BlockSpec((TM, D), lambda i, m: (i, 0)),
                      pl.BlockSpec((1, 8, D), lambda i, m: (jnp.minimum(i // tps, batch), 0, 0)),
                      pl.BlockSpec((TM, ROUTE_W), lambda i, m: (i, 0)),
                      pl.BlockSpec(memory_space=pl.ANY)],
            out_specs=pl.BlockSpec((TM, D), lambda i, m: (i, 0)),
            scratch_shapes=[pltpu.VMEM((LROWS, D), BF16), pltpu.SemaphoreType.DMA(())],
        ),
        out_shape=jax.ShapeDtypeStruct((n_tiles * TM, D), F32),
        compiler_params=_params(1),
        name="moe_combine",
    )(meta, h, modg, slots, ys)


def _moe(rows, n_tiles, h, modg, w_r, b_r, w1, w3, w2):
    a, route, cnt = _router(rows, n_tiles, h, modg, w_r, b_r)
    cnt = cnt[:, 0, :N_EXPERTS].astype(jnp.int32)
    seg = (cnt + SEG_ALIGN - 1) // SEG_ALIGN * SEG_ALIGN
    loff = jnp.cumsum(seg, axis=1) - seg
    total = jnp.sum(seg, axis=0)
    region = (total + MOE_RB - 1) // MOE_RB * MOE_RB
    region_end = jnp.cumsum(region)
    region_start = region_end - region
    goff = region_start[None, :] + jnp.cumsum(seg, axis=0) - seg
    meta = jnp.concatenate([loff, goff, seg], axis=1).reshape(-1)
    n_blocks = pl.cdiv(2 * n_tiles * TM + n_tiles * N_EXPERTS * (SEG_ALIGN - 1), MOE_RB) + N_EXPERTS
    n_used = region_end[-1] // MOE_RB
    fill = jnp.concatenate([region_start + total, region - total, n_used[None]]).astype(jnp.int32)
    blk = jnp.minimum(jnp.arange(n_blocks, dtype=jnp.int32), n_used - 1)
    blk_exp = jnp.minimum(jnp.searchsorted(region_end, blk * MOE_RB, side="right"), N_EXPERTS - 1).astype(jnp.int32)
    xs, slots = _dispatch(n_tiles, n_blocks, a, route, meta, fill)
    ys = _experts(xs, blk_exp, fill, w1, w3, w2)
    return _combine(rows, n_tiles, h, modg, slots, ys, meta)


def kernel(x, c, ctx, c_ctx, ada_w, ada_b, norm_mix_g, norm_ffn_g, sc_in_w, sc_conv_w, sc_out_w, da_qkv_w, da_out_w,
           da_q_norm_g, da_k_norm_g, da_lambda, da_sub_norm_g, cm_in_w, cm_in_b, cm_v_norm_g, cm_ws, cm_bs, cm_out_w,
           sw_qkv_w, sw_out_w, sw_q_norm_g, sw_k_norm_g, sw_sink, ffn_w1, ffn_w3, ffn_w2, moe_router_w, moe_router_b,
           moe_w1, moe_w3, moe_w2):
    batch, seq, _ = x.shape
    ctx_len = ctx.shape[1]
    depth = ada_w.shape[0]
    assert depth == 4 and batch + 1 <= 16
    rows = _Rows(batch, seq, ctx_len)

    cvec = jnp.concatenate([c, c_ctx[None, :], jnp.zeros((16 - batch - 1, D), F32)], axis=0)
    mod = _ada_all(cvec, ada_w, ada_b)[:, :batch + 1].reshape(depth, batch + 1, 6, D)
    gains = jnp.stack([norm_mix_g, norm_ffn_g], axis=1)[:, None]
    modg = jnp.concatenate([mod, jnp.broadcast_to(gains, (depth, batch + 1, 2, D))], axis=2)

    bf = lambda w: w.astype(BF16)
    x2, ctx2 = x.reshape(-1, D), ctx.reshape(-1, D)

    bg, y = _conv_in(rows, x2, ctx2, modg[0], bf(sc_in_w[0]))
    conv_w = jnp.pad(sc_conv_w[0], ((0, 5), (0, 0)))
    h = _conv_out(rows, x2, ctx2, modg[0], bg, y, conv_w, bf(sc_out_w[0]))
    h = _ffn(rows, rows.all_tiles, h, modg[0], bf(ffn_w1[0]), bf(ffn_w3[0]), bf(ffn_w2[0]))

    nq_chunks = D // CN
    qkv = _qkv(rows, h, modg[1], bf(da_qkv_w[0]), da_q_norm_g[0], da_k_norm_g[0], nq_chunks, nq_chunks)
    lam_init = 0.8 - 0.6 * math.exp(-0.3 * 1)
    o_lat, o_ctx = _diff_attn(rows, qkv, da_lambda[0], da_sub_norm_g[0], lam_init)
    h = _out_proj(rows, rows.all_tiles, h, modg[1], o_lat, o_ctx, bf(da_out_w[0]))
    h = _moe(rows, rows.all_tiles, h, modg[1], moe_router_w[0], moe_router_b[0], bf(moe_w1[0]), bf(moe_w3[0]),
             bf(moe_w2[0]))

    h = _gmlp(rows, h, modg[2], bf(cm_in_w[0]), cm_in_b[0].reshape(1, -1), cm_v_norm_g[0].reshape(1, -1),
              bf(cm_ws[0]), cm_bs[0].T, bf(cm_out_w[0]))
    h = _ffn(rows, rows.all_tiles, h, modg[2], bf(ffn_w1[1]), bf(ffn_w3[1]), bf(ffn_w2[1]))

    kv_chunks = SWA_KV_HEADS * HEAD_DIM // CN
    qkv = _qkv(rows, h, modg[3], bf(sw_qkv_w[0]), sw_q_norm_g[0], sw_k_norm_g[0], nq_chunks, kv_chunks)
    o = _swa(rows, qkv, sw_sink[0])
    h = _out_proj(rows, rows.lat_tiles, h, modg[3], o, o, bf(sw_out_w[0]))
    h = _moe(rows, rows.lat_tiles, h, modg[3], moe_router_w[1], moe_router_b[1], bf(moe_w1[1]), bf(moe_w3[1]),
             bf(moe_w2[1]))
    return h.reshape(batch, seq, D)
```

```python
import functools
import math

import jax
import jax.numpy as jnp
from jax import lax
from jax.experimental import pallas as pl
from jax.experimental.pallas import tpu as pltpu

D = 1024
HEAD_DIM = 64
GRID_W = 64
ROPE_HALF = HEAD_DIM // 2
ROPE_BASE = 10000.0
DIFF_HEADS = D // (2 * HEAD_DIM)
SWA_Q_HEADS = D // HEAD_DIM
SWA_KV_HEADS = 4
SWA_GROUP = SWA_Q_HEADS // SWA_KV_HEADS
SWA_WINDOW = 128
CHUNK = 128
CM_WIDTH = 2 * D
CM_GROUPS = 8
CM_GW = CM_WIDTH // CM_GROUPS
FFN_DIM = 2816
N_EXPERTS = 8
EPS = 1e-6

F32 = jnp.float32
BF16 = jnp.bfloat16
HIGHEST = lax.Precision.HIGHEST
LOG2E = math.log2(math.e)

TM = 512
CN = 256
MOE_RB = 512
ATT_TQ = 512
ATT_TK = 1024
SWA_TQ = 256
ROUTE_W = 128
MAX_UNSHIFTED_LOGIT = 60.0
VMEM_LIMIT = 56 << 20


def _params(n_grid):
    return pltpu.CompilerParams(dimension_semantics=("arbitrary",) * n_grid, vmem_limit_bytes=VMEM_LIMIT)


def _resident(shape):
    zeros = (0,) * len(shape)
    return pl.BlockSpec(shape, lambda *_: zeros, pipeline_mode=pl.Buffered(1))


def _sigmoid(x):
    return 1.0 / (1.0 + jnp.exp(-x))


def _gelu_tanh(x):
    return 0.5 * x * (1.0 + jnp.tanh(math.sqrt(2.0 / math.pi) * (x + 0.044715 * (x * x * x))))


def _modnorm(x, g, shift, scale):
    y = x * lax.rsqrt(jnp.mean(x * x, axis=-1, keepdims=True) + EPS)
    return (y * g) * (1.0 + scale) + shift


def _dot(a, b):
    return jnp.dot(a, b, preferred_element_type=F32)


def _mix_mod(m):
    return m[6:7], m[0:1], m[1:2]


def _ffn_mod(m):
    return m[7:8], m[3:4], m[4:5]


def _ada_kernel(c_ref, w_ref, b_ref, o_ref):
    c = c_ref[...]
    s = c * _sigmoid(c)
    o_ref[0] = jnp.dot(s, w_ref[0], precision=HIGHEST, preferred_element_type=F32) + b_ref[0]


def _ada_all(cvec, ada_w, ada_b):
    depth, _, n = ada_w.shape
    tn = 1536
    rows = cvec.shape[0]
    return pl.pallas_call(
        _ada_kernel,
        grid=(depth, n // tn),
        in_specs=[pl.BlockSpec((rows, D), lambda l, j: (0, 0)),
                  pl.BlockSpec((1, D, tn), lambda l, j: (l, 0, j)),
                  pl.BlockSpec((1, 1, tn), lambda l, j: (l, 0, j))],
        out_specs=pl.BlockSpec((1, rows, tn), lambda l, j: (l, 0, j)),
        out_shape=jax.ShapeDtypeStruct((depth, rows, n), F32),
        compiler_params=_params(2),
        name="adaln",
    )(cvec, ada_w, ada_b.reshape(depth, 1, n))


class _Rows:
    def __init__(self, batch, seq, ctx_len):
        self.batch, self.seq, self.ctx_len = batch, seq, ctx_len
        self.n_lat = batch * seq
        self.n_ctx = batch * ctx_len
        self.n_all = self.n_lat + self.n_ctx
        assert seq % TM == 0 and self.n_ctx % TM == 0 and TM % ctx_len == 0
        self.lat_tiles = self.n_lat // TM
        self.all_tiles = self.n_all // TM
        self.tiles_per_seq = seq // TM

    def mod_spec(self):
        tps, batch = self.tiles_per_seq, self.batch
        return pl.BlockSpec((1, 8, D), lambda i, *_: (jnp.minimum(i // tps, batch), 0, 0))

    def row_spec(self, width):
        return pl.BlockSpec((TM, width), lambda i, *_: (i, 0))

    def split_specs(self, width):
        lat_tiles = self.lat_tiles
        return [pl.BlockSpec((TM, width), lambda i, *_: (jnp.minimum(i, lat_tiles - 1), 0)),
                pl.BlockSpec((TM, width), lambda i, *_: (jnp.maximum(i - lat_tiles, 0), 0))]


def _conv_in_kernel(lat_tiles, x_ref, ctx_ref, mod_ref, w_ref, bg_ref, y_ref):
    h = jnp.where(pl.program_id(0) < lat_tiles, x_ref[...], ctx_ref[...])
    a = _modnorm(h, *_mix_mod(mod_ref[0])).astype(BF16)
    for c in range(D // CN):
        lo = c * CN
        bg_ref[:, lo:lo + CN] = _dot(a, w_ref[:, lo:lo + CN]).astype(BF16)
        cg = _dot(a, w_ref[:, D + lo:D + lo + CN])
        xv = _dot(a, w_ref[:, 2 * D + lo:2 * D + lo + CN])
        y_ref[:, lo:lo + CN] = (cg * xv).astype(BF16)


def _conv_in(rows, x, ctx, modg, w_in):
    out = jax.ShapeDtypeStruct((rows.n_all, D), BF16)
    return pl.pallas_call(
        functools.partial(_conv_in_kernel, rows.lat_tiles),
        grid=(rows.all_tiles,),
        in_specs=rows.split_specs(D) + [rows.mod_spec(), _resident((D, 3 * D))],
        out_specs=[rows.row_spec(D), rows.row_spec(D)],
        out_shape=[out, out],
        compiler_params=_params(1),
        name="conv_in",
    )(x, ctx, modg, w_in)


HALO = 16


def _conv_out_kernel(n_lat, seq, ctx_len, x_ref, ctx_ref, mod_ref, bg_ref, y_ref, yp_ref, yn_ref, cw_ref, w_ref,
                     o_ref):
    i = pl.program_id(0)
    h = jnp.where(i * TM < n_lat, x_ref[...], ctx_ref[...])
    m = mod_ref[0]
    y = y_ref[...].astype(F32)
    row = lax.broadcasted_iota(jnp.int32, (TM, 1), 0)
    grow = row + i * TM
    seq_len = jnp.where(grow < n_lat, seq, ctx_len)
    pos = grow & (seq_len - 1)
    prev_row = yp_ref[HALO - 1:HALO, :].astype(F32)
    next_row = yn_ref[0:1, :].astype(F32)
    y_m1 = jnp.where(row == 0, prev_row, pltpu.roll(y, 1, 0))
    y_m1 = jnp.where(pos == 0, 0.0, y_m1)
    y_p1 = jnp.where(row == TM - 1, next_row, pltpu.roll(y, TM - 1, 0))
    y_p1 = jnp.where(pos == seq_len - 1, 0.0, y_p1)
    conv = cw_ref[0:1, :] * y_m1 + cw_ref[1:2, :] * y + cw_ref[2:3, :] * y_p1
    z = (bg_ref[...].astype(F32) * conv).astype(BF16)
    o_ref[...] = h + m[2:3] * _dot(z, w_ref[...])


def _conv_out(rows, x, ctx, modg, bg, y, conv_w, w_out):
    hb = TM // HALO
    last = rows.n_all // HALO - 1
    return pl.pallas_call(
        functools.partial(_conv_out_kernel, rows.n_lat, rows.seq, rows.ctx_len),
        grid=(rows.all_tiles,),
        in_specs=rows.split_specs(D) + [rows.mod_spec(), rows.row_spec(D), rows.row_spec(D),
                  pl.BlockSpec((HALO, D), lambda i: (jnp.maximum(i * hb - 1, 0), 0)),
                  pl.BlockSpec((HALO, D), lambda i: (jnp.minimum((i + 1) * hb, last), 0)),
                  _resident((8, D)), _resident((D, D))],
        out_specs=rows.row_spec(D),
        out_shape=jax.ShapeDtypeStruct((rows.n_all, D), F32),
        compiler_params=_params(1),
        name="conv_out",
    )(x, ctx, modg, bg, y, y, y, conv_w, w_out)


W2_KC = 768


def _swiglu(a, w1_ref, w3_ref, w2_ref, t_ref):
    for c in range(FFN_DIM // CN):
        lo = c * CN
        h1 = _dot(a, w1_ref[:, lo:lo + CN].astype(BF16))
        h3 = _dot(a, w3_ref[:, lo:lo + CN].astype(BF16))
        t_ref[:, lo:lo + CN] = (h1 * _sigmoid(h1) * h3).astype(BF16)
    out = None
    for lo in range(0, FFN_DIM, W2_KC):
        hi = min(lo + W2_KC, FFN_DIM)
        part = _dot(t_ref[:, lo:hi], w2_ref[lo:hi, :].astype(BF16))
        out = part if out is None else out + part
    return out


def _ffn_kernel(h_ref, mod_ref, w1_ref, w3_ref, w2_ref, o_ref, t_ref):
    m = mod_ref[0]
    x = h_ref[...]
    a = _modnorm(x, *_ffn_mod(m)).astype(BF16)
    o_ref[...] = x + m[5:6] * _swiglu(a, w1_ref, w3_ref, w2_ref, t_ref)


def _ffn(rows, n_tiles, h, modg, layer, w1, w3, w2):
    def slab(d0, d1):
        return pl.BlockSpec((None, d0, d1), lambda i: (layer, 0, 0), pipeline_mode=pl.Buffered(1))

    return pl.pallas_call(
        _ffn_kernel,
        grid=(n_tiles,),
        in_specs=[rows.row_spec(D), rows.mod_spec(), slab(D, FFN_DIM), slab(D, FFN_DIM), slab(FFN_DIM, D)],
        out_specs=rows.row_spec(D),
        out_shape=jax.ShapeDtypeStruct((n_tiles * TM, D), F32),
        scratch_shapes=[pltpu.VMEM((TM, FFN_DIM), BF16)],
        compiler_params=_params(1),
        name="ffn_dense",
    )(h, modg, w1, w3, w2)


def _norm_rope(x, gmat_ref, cos, sin):
    ms = _dot((x * x).astype(BF16), gmat_ref[...]) * (1.0 / HEAD_DIM)
    xn = x * lax.rsqrt(ms + EPS)
    lane = lax.broadcasted_iota(jnp.int32, (1, CN), 1)
    first_half = (lane & (ROPE_HALF - 1)) < (ROPE_HALF // 2)
    partner = jnp.where(first_half, pltpu.roll(xn, CN - ROPE_HALF // 2, 1), pltpu.roll(xn, ROPE_HALF // 2, 1))
    return xn * cos + partner * sin


def _qkv_kernel(n_qk_chunks, n_q_chunks, h_ref, mod_ref, w_ref, gmat_ref, qcos_ref, qsin_ref, kcos_ref, ksin_ref,
                o_ref):
    a = _modnorm(h_ref[...], *_mix_mod(mod_ref[0])).astype(BF16)
    n_chunks = w_ref.shape[1] // CN
    for c in range(n_chunks):
        lo = c * CN
        acc = _dot(a, w_ref[:, lo:lo + CN])
        if c < n_q_chunks:
            acc = _norm_rope(acc, gmat_ref, qcos_ref[...], qsin_ref[...])
        elif c < n_qk_chunks:
            acc = _norm_rope(acc, gmat_ref, kcos_ref[...], ksin_ref[...])
        o_ref[:, lo:lo + CN] = acc.astype(BF16)


def _rope_tables(rows, gain, scale):
    seq = rows.seq
    pos = jnp.arange(seq)
    n_freq = ROPE_HALF // 2
    inv = ROPE_BASE ** (-jnp.arange(n_freq, dtype=F32) / n_freq)
    ang_r = (pos // GRID_W).astype(F32)[:, None] * inv
    ang_c = (pos % GRID_W).astype(F32)[:, None] * inv
    cos = jnp.concatenate([jnp.cos(ang_r)] * 2 + [jnp.cos(ang_c)] * 2, axis=-1)
    sin = jnp.concatenate([-jnp.sin(ang_r), jnp.sin(ang_r), -jnp.sin(ang_c), jnp.sin(ang_c)], axis=-1)
    cos = jnp.concatenate([cos, jnp.ones((TM, HEAD_DIM), F32)], axis=0)
    sin = jnp.concatenate([sin, jnp.zeros((TM, HEAD_DIM), F32)], axis=0)
    dim = jnp.arange(HEAD_DIM)
    partner = jnp.where((dim % ROPE_HALF) < n_freq, dim + n_freq, dim - n_freq)
    g = gain.astype(F32) * scale
    reps = CN // HEAD_DIM
    return jnp.tile(cos * g[None, :], (1, reps)), jnp.tile(sin * g[partner][None, :], (1, reps))


def _qkv(rows, h, modg, w, q_gain, k_gain, n_q_chunks, n_k_chunks):
    width = w.shape[1]
    head = jnp.arange(CN) // HEAD_DIM
    gmat = (head[:, None] == head[None, :]).astype(BF16)
    tables = _rope_tables(rows, q_gain, LOG2E * HEAD_DIM ** -0.5) + _rope_tables(rows, k_gain, 1.0)
    tps, lat_tiles = rows.tiles_per_seq, rows.lat_tiles
    tab_spec = pl.BlockSpec((TM, CN), lambda i: (jnp.where(i < lat_tiles, i % tps, tps), 0))
    return pl.pallas_call(
        functools.partial(_qkv_kernel, n_q_chunks + n_k_chunks, n_q_chunks),
        grid=(rows.all_tiles,),
        in_specs=[rows.row_spec(D), rows.mod_spec(), _resident((D, width)), _resident((CN, CN))] + [tab_spec] * 4,
        out_specs=rows.row_spec(width),
        out_shape=jax.ShapeDtypeStruct((rows.n_all, width), BF16),
        compiler_params=_params(1),
        name="qkv_proj",
    )(h, modg, w, gmat, *tables)


DH2 = 2 * HEAD_DIM


def _diff_attn_kernel(lam_init, n_lat_chunks, q_ref, kc_ref, vc_ref, kl_ref, vl_ref, lam_ref, subg_ref, o_ref,
                      m_ref, acc_ref, vca_ref, vla_ref, kn_ref):
    def max_norm_sq(x):
        xf = x.astype(F32)
        return jnp.max(jnp.sum(xf * xf, axis=-1, keepdims=True), axis=0, keepdims=True)

    def per_head_setup():
        vca_ref[:, :DH2] = vc_ref[...]
        vca_ref[:, DH2:] = jnp.ones((vc_ref.shape[0], DH2), BF16)
        kn = max_norm_sq(kc_ref[...])
        if n_lat_chunks:
            vla_ref[:, :DH2] = vl_ref[...]
            vla_ref[:, DH2:] = jnp.ones((vl_ref.shape[0], DH2), BF16)
            kn = jnp.maximum(kn, max_norm_sq(kl_ref[...]))
        kn_ref[...] = jnp.broadcast_to(kn, kn_ref.shape)

    if n_lat_chunks:
        pl.when(pl.program_id(2) == 0)(per_head_setup)
    else:
        per_head_setup()

    q = q_ref[...]
    lane = lax.broadcasted_iota(jnp.int32, (1, DH2), 1)
    zero = jnp.zeros((), BF16)
    qs = (jnp.where(lane < HEAD_DIM, q, zero), jnp.where(lane >= HEAD_DIM, q, zero))
    nt = (((1,), (1,)), ((), ()))

    def over_keys(update):
        update(kc_ref[...], vca_ref[...])
        if n_lat_chunks:
            tk = kl_ref.shape[0] // n_lat_chunks

            def body(c, carry):
                start = pl.multiple_of(c * tk, tk)
                update(kl_ref[pl.ds(start, tk), :], vla_ref[pl.ds(start, tk), :])
                return carry
            lax.fori_loop(0, n_lat_chunks, body, 0)

    def unshifted():
        for mi in range(2):
            s = lax.dot_general(qs[mi], kc_ref[...], nt, preferred_element_type=F32)
            acc = _dot(jnp.exp2(s).astype(BF16), vca_ref[...])
            if n_lat_chunks:
                s = lax.dot_general(qs[mi], kl_ref[...], nt, preferred_element_type=F32)
                acc = acc + _dot(jnp.exp2(s).astype(BF16), vla_ref[...])
            acc_ref[mi] = acc

    def update_online(k, va):
        reps = k.shape[0] // DH2
        for mi in range(2):
            s = lax.dot_general(qs[mi], k, nt, preferred_element_type=F32)
            m_old = m_ref[mi]
            m_new = jnp.maximum(m_old, jnp.max(s, axis=-1, keepdims=True))
            alpha = jnp.exp2(m_old - m_new)
            p = jnp.exp2(s - jnp.concatenate([m_new] * reps, axis=-1))
            acc_ref[mi] = jnp.concatenate([alpha, alpha], axis=-1) * acc_ref[mi] + _dot(p.astype(BF16), va)
            m_ref[mi] = m_new

    def online():
        m_ref[...] = jnp.full(m_ref.shape, -jnp.inf, F32)
        acc_ref[...] = jnp.zeros(acc_ref.shape, F32)
        over_keys(update_online)

    bound_sq = max_norm_sq(q) * kn_ref[0:1, 0:1]
    small_logits = bound_sq[0, 0] <= MAX_UNSHIFTED_LOGIT ** 2
    pl.when(small_logits)(unshifted)
    pl.when(jnp.logical_not(small_logits))(online)

    lp = lam_ref[...]
    lam = (jnp.exp(jnp.sum(lp[0:1] * lp[1:2], axis=-1, keepdims=True))
           - jnp.exp(jnp.sum(lp[2:3] * lp[3:4], axis=-1, keepdims=True)) + lam_init)
    acc0 = acc_ref[0]
    acc1 = acc_ref[1]
    o = acc0[:, :DH2] / acc0[:, DH2:] - lam * (acc1[:, :DH2] / acc1[:, DH2:])
    o = o * lax.rsqrt(jnp.mean(o * o, axis=-1, keepdims=True) + EPS) * subg_ref[...]
    o_ref[...] = (o * (1.0 - lam_init)).astype(BF16)


def _diff_attn(rows, qkv, lam_p, sub_g, lam_init):
    batch, seq, ctx_len = rows.batch, rows.seq, rows.ctx_len
    nq = seq // ATT_TQ
    nh = DIFF_HEADS
    ctx0 = rows.n_lat // ctx_len
    sub_g = sub_g.reshape(1, DH2)
    small = [_resident((4, HEAD_DIM)), _resident((1, DH2))]
    kc_spec = pl.BlockSpec((ctx_len, DH2), lambda b, h, *_: (ctx0 + b, nh + h))
    vc_spec = pl.BlockSpec((ctx_len, DH2), lambda b, h, *_: (ctx0 + b, 2 * nh + h))

    def scratch(tq, n_lat_keys):
        return [pltpu.VMEM((2, tq, DH2), F32), pltpu.VMEM((2, tq, 2 * DH2), F32),
                pltpu.VMEM((ctx_len, 2 * DH2), BF16), pltpu.VMEM((n_lat_keys, 2 * DH2), BF16),
                pltpu.VMEM((8, DH2), F32)]

    o_lat = pl.pallas_call(
        functools.partial(_diff_attn_kernel, lam_init, pl.cdiv(seq, ATT_TK)),
        grid=(batch, nh, nq),
        in_specs=[pl.BlockSpec((ATT_TQ, DH2), lambda b, h, i: (b * nq + i, h)),
                  kc_spec, vc_spec,
                  pl.BlockSpec((seq, DH2), lambda b, h, i: (b, nh + h)),
                  pl.BlockSpec((seq, DH2), lambda b, h, i: (b, 2 * nh + h))] + small,
        out_specs=pl.BlockSpec((ATT_TQ, DH2), lambda b, h, i: (b * nq + i, h)),
        out_shape=jax.ShapeDtypeStruct((rows.n_lat, D), BF16),
        scratch_shapes=scratch(ATT_TQ, seq),
        compiler_params=_params(3),
        name="diff_attn_latent",
    )(qkv, qkv, qkv, qkv, qkv, lam_p, sub_g)

    def ctx_kernel(q_ref, kc_ref, vc_ref, lam_ref, subg_ref, o_ref, *scratch_refs):
        _diff_attn_kernel(lam_init, 0, q_ref, kc_ref, vc_ref, None, None, lam_ref, subg_ref, o_ref, *scratch_refs)

    o_ctx = pl.pallas_call(
        ctx_kernel,
        grid=(batch, nh),
        in_specs=[pl.BlockSpec((ctx_len, DH2), lambda b, h: (ctx0 + b, h)), kc_spec, vc_spec] + small,
        out_specs=pl.BlockSpec((ctx_len, DH2), lambda b, h: (b, h)),
        out_shape=jax.ShapeDtypeStruct((rows.n_ctx, D), BF16),
        scratch_shapes=scratch(ctx_len, 16),
        compiler_params=_params(2),
        name="diff_attn_context",
    )(qkv, qkv, qkv, lam_p, sub_g)
    return o_lat, o_ctx


def _out_proj_kernel(lat_tiles, h_ref, mod_ref, zl_ref, zc_ref, w_ref, o_ref):
    i = pl.program_id(0)

    def project(z_ref):
        o_ref[...] = h_ref[...] + mod_ref[0][2:3] * _dot(z_ref[...], w_ref[...])

    pl.when(i < lat_tiles)(lambda: project(zl_ref))
    pl.when(i >= lat_tiles)(lambda: project(zc_ref))


def _out_proj(rows, n_tiles, h, modg, z_lat, z_ctx, w):
    lat_tiles = rows.lat_tiles
    return pl.pallas_call(
        functools.partial(_out_proj_kernel, lat_tiles),
        grid=(n_tiles,),
        in_specs=[rows.row_spec(D), rows.mod_spec(),
                  pl.BlockSpec((TM, D), lambda i: (jnp.minimum(i, lat_tiles - 1), 0)),
                  pl.BlockSpec((TM, D), lambda i: (jnp.maximum(i - lat_tiles, 0), 0)),
                  _resident((D, D))],
        out_specs=rows.row_spec(D),
        out_shape=jax.ShapeDtypeStruct((n_tiles * TM, D), F32),
        compiler_params=_params(1),
        name="out_proj",
    )(h, modg, z_lat, z_ctx, w)


def _gmlp_kernel(h_ref, mod_ref, win_ref, bin_ref, vg_ref, ws_ref, bs_ref, wout_ref, o_ref, u_ref, v_ref, t_ref):
    m = mod_ref[0]
    x = h_ref[...]
    a = _modnorm(x, *_mix_mod(m)).astype(BF16)
    ssq = jnp.zeros((TM, 1), F32)
    for c in range(2 * CM_WIDTH // CN):
        lo = c * CN
        z = _gelu_tanh(_dot(a, win_ref[:, lo:lo + CN]) + bin_ref[:, lo:lo + CN])
        if lo < CM_WIDTH:
            u_ref[:, lo:lo + CN] = z
        else:
            v_ref[:, lo - CM_WIDTH:lo - CM_WIDTH + CN] = z
            ssq = ssq + jnp.sum(z * z, axis=-1, keepdims=True)
    inv = lax.rsqrt(ssq * (1.0 / CM_WIDTH) + EPS)
    for g in range(CM_GROUPS):
        lo = g * CM_GW
        vn = (v_ref[:, lo:lo + CM_GW] * inv * vg_ref[:, lo:lo + CM_GW]).astype(BF16)
        for r in range(TM // CHUNK):
            r0 = r * CHUNK
            sv = _dot(ws_ref[g], vn[r0:r0 + CHUNK]) + bs_ref[:, g:g + 1]
            t_ref[r0:r0 + CHUNK, lo:lo + CM_GW] = (u_ref[r0:r0 + CHUNK, lo:lo + CM_GW] * sv).astype(BF16)
    o_ref[...] = x + m[2:3] * _dot(t_ref[...], wout_ref[...])


def _gmlp(rows, h, modg, w_in, b_in, v_g, w_s, b_s, w_out):
    return pl.pallas_call(
        _gmlp_kernel,
        grid=(rows.all_tiles,),
        in_specs=[rows.row_spec(D), rows.mod_spec(), _resident((D, 2 * CM_WIDTH)), _resident((1, 2 * CM_WIDTH)),
                  _resident((1, CM_WIDTH)), _resident((CM_GROUPS, CHUNK, CHUNK)), _resident((CHUNK, CM_GROUPS)),
                  _resident((CM_WIDTH, D))],
        out_specs=rows.row_spec(D),
        out_shape=jax.ShapeDtypeStruct((rows.n_all, D), F32),
        scratch_shapes=[pltpu.VMEM((TM, CM_WIDTH), F32), pltpu.VMEM((TM, CM_WIDTH), F32),
                        pltpu.VMEM((TM, CM_WIDTH), BF16)],
        compiler_params=_params(1),
        name="gmlp",
    )(h, modg, w_in, b_in, v_g, w_s, b_s, w_out)


SWA_BAND = SWA_TQ + 2 * SWA_WINDOW


SWA_PAIR = 2 * HEAD_DIM
N_KV_VARIANTS = 2 * SWA_KV_HEADS


def _swa_kernel(seq, sink_ref, q_ref, kc_ref, vc_ref, kl_ref, vl_ref, o_ref, kcv_ref, vcv_ref, klv_ref, vlv_ref,
                kn_ref):
    qi = pl.program_id(1)
    lane = lax.broadcasted_iota(jnp.int32, (1, SWA_PAIR), 1)
    lo_half = lane < HEAD_DIM

    def max_norm_sq(x):
        xf = x.astype(F32)
        return jnp.max(jnp.sum(xf * xf, axis=-1, keepdims=True), axis=0, keepdims=True)

    @pl.when(qi == 0)
    def _per_batch_setup():
        kn = jnp.zeros((1, 1), F32)
        for src_ref, dst_ref in ((kc_ref, kcv_ref), (vc_ref, vcv_ref), (kl_ref, klv_ref), (vl_ref, vlv_ref)):
            for half in range(SWA_KV_HEADS // 2):
                x = src_ref[:, half * SWA_PAIR:(half + 1) * SWA_PAIR].astype(F32)
                xr = pltpu.roll(x, HEAD_DIM, 1)
                j0, j1 = 2 * half, 2 * half + 1
                dst_ref[2 * j0] = jnp.where(lo_half, x, 0.0).astype(BF16)
                dst_ref[2 * j0 + 1] = jnp.where(lo_half, 0.0, xr).astype(BF16)
                dst_ref[2 * j1] = jnp.where(lo_half, xr, 0.0).astype(BF16)
                dst_ref[2 * j1 + 1] = jnp.where(lo_half, 0.0, x).astype(BF16)
                if src_ref is kc_ref or src_ref is kl_ref:
                    kn = jnp.maximum(kn, max_norm_sq(x))
        kn_ref[...] = jnp.broadcast_to(kn, kn_ref.shape)

    q0 = qi * SWA_TQ
    start = jnp.clip(q0 - SWA_WINDOW, 0, seq - SWA_BAND)
    start = pl.multiple_of(start, SWA_WINDOW)
    qpos = q0 + lax.broadcasted_iota(jnp.int32, (SWA_TQ, SWA_BAND), 0)
    kpos = start + lax.broadcasted_iota(jnp.int32, (SWA_TQ, SWA_BAND), 1)
    in_band = jnp.abs(qpos - kpos) <= SWA_WINDOW
    nt = (((1,), (1,)), ((), ()))

    def unshifted():
        in_band2 = jnp.concatenate([in_band, in_band], axis=0)
        top = lax.broadcasted_iota(jnp.int32, (2 * SWA_TQ, 1), 0) < SWA_TQ
        for j in range(SWA_KV_HEADS):
            lo = j * SWA_GROUP * HEAD_DIM
            q2 = jnp.concatenate([q_ref[:, lo:lo + SWA_PAIR], q_ref[:, lo + SWA_PAIR:lo + 2 * SWA_PAIR]], axis=0)
            out = jnp.zeros((2 * SWA_TQ, SWA_PAIR), F32)
            for var in range(2):
                v = 2 * j + var
                s_c = lax.dot_general(q2, kcv_ref[v], nt, preferred_element_type=F32)
                s_b = lax.dot_general(q2, klv_ref[v, pl.ds(start, SWA_BAND), :], nt, preferred_element_type=F32)
                p_c = jnp.exp2(s_c)
                p_b = jnp.where(in_band2, jnp.exp2(s_b), 0.0)
                hq = j * SWA_GROUP + var
                sink = jnp.where(top, sink_ref[hq] * LOG2E, sink_ref[hq + 2] * LOG2E)
                denom = (jnp.sum(p_c, axis=-1, keepdims=True) + jnp.sum(p_b, axis=-1, keepdims=True)
                         + jnp.exp2(sink))
                o = (_dot(p_c.astype(BF16), vcv_ref[v])
                     + _dot(p_b.astype(BF16), vlv_ref[v, pl.ds(start, SWA_BAND), :]))
                out = out + o / denom
            o_ref[:, lo:lo + SWA_PAIR] = out[:SWA_TQ].astype(BF16)
            o_ref[:, lo + SWA_PAIR:lo + 2 * SWA_PAIR] = out[SWA_TQ:].astype(BF16)

    def shifted():
        kb = kl_ref[pl.ds(start, SWA_BAND), :]
        vb = vl_ref[pl.ds(start, SWA_BAND), :]
        kc = kc_ref[...]
        vc = vc_ref[...]
        for j in range(SWA_KV_HEADS):
            kj = slice(j * HEAD_DIM, (j + 1) * HEAD_DIM)
            kbj, vbj, kcj, vcj = kb[:, kj], vb[:, kj], kc[:, kj], vc[:, kj]
            outs = []
            for g in range(SWA_GROUP):
                hq = j * SWA_GROUP + g
                qh = q_ref[:, hq * HEAD_DIM:(hq + 1) * HEAD_DIM]
                s_c = lax.dot_general(qh, kcj, nt, preferred_element_type=F32)
                s_b = lax.dot_general(qh, kbj, nt, preferred_element_type=F32)
                s_b = jnp.where(in_band, s_b, -jnp.inf)
                sink = sink_ref[hq] * LOG2E
                mx = jnp.maximum(jnp.maximum(jnp.max(s_c, axis=-1, keepdims=True),
                                             jnp.max(s_b, axis=-1, keepdims=True)), sink)
                p_c = jnp.exp2(s_c - mx)
                p_b = jnp.exp2(s_b - mx)
                denom = (jnp.sum(p_c, axis=-1, keepdims=True) + jnp.sum(p_b, axis=-1, keepdims=True)
                         + jnp.exp2(sink - mx))
                o = _dot(p_c.astype(BF16), vcj) + _dot(p_b.astype(BF16), vbj)
                outs.append(o / denom)
            lo = j * SWA_GROUP * HEAD_DIM
            o_ref[:, lo:lo + SWA_GROUP * HEAD_DIM] = jnp.concatenate(outs, axis=-1).astype(BF16)

    qn = jnp.zeros((1, 1), F32)
    for pair in range(SWA_Q_HEADS // 2):
        qn = jnp.maximum(qn, max_norm_sq(q_ref[:, pair * SWA_PAIR:(pair + 1) * SWA_PAIR]))
    max_sink = sink_ref[0]
    for hq in range(1, SWA_Q_HEADS):
        max_sink = jnp.maximum(max_sink, sink_ref[hq])
    small_logits = jnp.logical_and((qn * kn_ref[0:1, 0:1])[0, 0] <= MAX_UNSHIFTED_LOGIT ** 2,
                                   max_sink * LOG2E <= MAX_UNSHIFTED_LOGIT)
    pl.when(small_logits)(unshifted)
    pl.when(jnp.logical_not(small_logits))(shifted)


def _swa(rows, qkv, sink):
    batch, seq, ctx_len = rows.batch, rows.seq, rows.ctx_len
    nq = seq // SWA_TQ
    kvw = SWA_KV_HEADS * HEAD_DIM
    k_col = D // kvw
    ctx0 = rows.n_lat // ctx_len
    return pl.pallas_call(
        functools.partial(_swa_kernel, seq),
        grid=(batch, nq),
        in_specs=[pl.BlockSpec(memory_space=pltpu.SMEM),
                  pl.BlockSpec((SWA_TQ, D), lambda b, i: (b * nq + i, 0)),
                  pl.BlockSpec((ctx_len, kvw), lambda b, i: (ctx0 + b, k_col)),
                  pl.BlockSpec((ctx_len, kvw), lambda b, i: (ctx0 + b, k_col + 1)),
                  pl.BlockSpec((seq, kvw), lambda b, i: (b, k_col)),
                  pl.BlockSpec((seq, kvw), lambda b, i: (b, k_col + 1))],
        out_specs=pl.BlockSpec((SWA_TQ, D), lambda b, i: (b * nq + i, 0)),
        out_shape=jax.ShapeDtypeStruct((rows.n_lat, D), BF16),
        scratch_shapes=[pltpu.VMEM((N_KV_VARIANTS, ctx_len, SWA_PAIR), BF16),
                        pltpu.VMEM((N_KV_VARIANTS, ctx_len, SWA_PAIR), BF16),
                        pltpu.VMEM((N_KV_VARIANTS, seq, SWA_PAIR), BF16),
                        pltpu.VMEM((N_KV_VARIANTS, seq, SWA_PAIR), BF16),
                        pltpu.VMEM((8, SWA_PAIR), F32)],
        compiler_params=_params(2),
        name="swa_attn",
    )(sink, qkv, qkv, qkv, qkv, qkv)


def _router_kernel(h_ref, mod_ref, wr_ref, br_ref, a_ref, route_ref, cnt_ref):
    a = _modnorm(h_ref[...], *_ffn_mod(mod_ref[0]))
    a_hi = a.astype(BF16)
    a_ref[...] = a_hi
    a_lo = (a - a_hi.astype(F32)).astype(BF16)
    hi_part = _dot(a_hi, wr_ref[...])
    logits = (hi_part[:, :ROUTE_W] + hi_part[:, ROUTE_W:]) + _dot(a_lo, wr_ref[:, :ROUTE_W]) + br_ref[...]
    lane = lax.broadcasted_iota(jnp.int32, (TM, ROUTE_W), 1)
    m1 = jnp.max(logits, axis=-1, keepdims=True)
    i1 = jnp.min(jnp.where(logits == m1, lane, ROUTE_W), axis=-1, keepdims=True)
    rest = jnp.where(lane == i1, -jnp.inf, logits)
    m2 = jnp.max(rest, axis=-1, keepdims=True)
    i2 = jnp.min(jnp.where(rest == m2, lane, ROUTE_W), axis=-1, keepdims=True)
    e2 = jnp.exp(m2 - m1)
    gate1 = 1.0 / (1.0 + e2)
    gate2 = e2 / (1.0 + e2)
    rec = jnp.where(lane == 0, i1.astype(F32), 0.0)
    rec = jnp.where(lane == 1, i2.astype(F32), rec)
    rec = jnp.where(lane == 2, gate1, rec)
    route_ref[...] = jnp.where(lane == 3, gate2, rec)
    chosen = jnp.where((lane == i1) | (lane == i2), 1.0, 0.0)
    cnt_ref[0] = jnp.broadcast_to(jnp.sum(chosen, axis=0, keepdims=True), (8, ROUTE_W))


def _router(rows, n_tiles, h, modg, w_r, b_r):
    pad = ROUTE_W - N_EXPERTS
    w_pad = jnp.pad(w_r, ((0, 0), (0, pad)))
    w_hi = w_pad.astype(BF16)
    w_lo = (w_pad - w_hi.astype(F32)).astype(BF16)
    w_split = jnp.concatenate([w_hi, w_lo], axis=1)
    b_pad = jnp.concatenate([b_r.astype(F32), jnp.full((pad,), -1e30, F32)]).reshape(1, ROUTE_W)
    return pl.pallas_call(
        _router_kernel,
        grid=(n_tiles,),
        in_specs=[rows.row_spec(D), rows.mod_spec(), _resident((D, 2 * ROUTE_W)), _resident((1, ROUTE_W))],
        out_specs=[rows.row_spec(D), rows.row_spec(ROUTE_W), pl.BlockSpec((1, 8, ROUTE_W), lambda i: (i, 0, 0))],
        out_shape=[jax.ShapeDtypeStruct((n_tiles * TM, D), BF16), jax.ShapeDtypeStruct((n_tiles * TM, ROUTE_W), F32),
                   jax.ShapeDtypeStruct((n_tiles, 8, ROUTE_W), F32)],
        compiler_params=_params(1),
        name="moe_router",
    )(h, modg, w_split, b_pad)


SEG_ALIGN = 16
SEG_BITS = tuple(1 << b for b in range(TM.bit_length() - 1, SEG_ALIGN.bit_length() - 2, -1))
LROWS = -(-(2 * TM + N_EXPERTS * (SEG_ALIGN - 1)) // 128) * 128
XS_W = D + 128
META_W = 3 * N_EXPERTS


def _segment_copies(meta_ref, tile, local_ref, slots_hbm, sem, to_slots):
    for e in range(N_EXPERTS):
        loff = meta_ref[tile * META_W + e]
        goff = meta_ref[tile * META_W + N_EXPERTS + e]
        seg = meta_ref[tile * META_W + 2 * N_EXPERTS + e]
        for bit in SEG_BITS:
            done = seg & ~(2 * bit - 1)
            lo = local_ref.at[pl.ds(pl.multiple_of(loff + done, SEG_ALIGN), bit)]
            gl = slots_hbm.at[pl.ds(pl.multiple_of(goff + done, SEG_ALIGN), bit)]
            copy = pltpu.make_async_copy(lo, gl, sem) if to_slots else pltpu.make_async_copy(gl, lo, sem)
            yield (seg & bit) != 0, copy


def _run_segment_copies(meta_ref, tile, local_ref, slots_hbm, sem, to_slots):
    for cond, copy in _segment_copies(meta_ref, tile, local_ref, slots_hbm, sem, to_slots):
        pl.when(cond)(copy.start)
    for cond, copy in _segment_copies(meta_ref, tile, local_ref, slots_hbm, sem, to_slots):
        pl.when(cond)(copy.wait)


def _split3(x):
    hi = x.astype(BF16).astype(F32)
    mid = (x - hi).astype(BF16).astype(F32)
    lo = ((x - hi) - mid).astype(BF16).astype(F32)
    return hi, mid, lo


def _dispatch_kernel(n_blocks, meta_ref, fill_ref, a_ref, route_ref, tri_ref, xs_hbm, slot_ref, local_ref, zero_ref,
                     sem, zsem):
    i = pl.program_id(0)

    @pl.when(i == 0)
    def _zero_unwritten_slots():
        zero_ref[...] = jnp.zeros(zero_ref.shape, BF16)
        tails = []
        for e in range(N_EXPERTS):
            off, length = fill_ref[e], fill_ref[N_EXPERTS + e]
            for bit in (b for b in SEG_BITS if b < MOE_RB):
                done = length & ~(2 * bit - 1)
                dst = xs_hbm.at[pl.ds(pl.multiple_of(off + done, SEG_ALIGN), bit)]
                tails.append(((length & bit) != 0, pltpu.make_async_copy(zero_ref.at[pl.ds(0, bit)], dst, zsem)))
        for cond, copy in tails:
            pl.when(cond)(copy.start)

        def block_copy(blk):
            dst = xs_hbm.at[pl.ds(pl.multiple_of(blk * MOE_RB, MOE_RB), MOE_RB)]
            return pltpu.make_async_copy(zero_ref, dst, zsem)

        n_used = fill_ref[2 * N_EXPERTS]
        lax.fori_loop(n_used, n_blocks, lambda blk, c: (block_copy(blk).start(), c)[1], 0)
        for cond, copy in tails:
            pl.when(cond)(copy.wait)
        lax.fori_loop(n_used, n_blocks, lambda blk, c: (block_copy(blk).wait(), c)[1], 0)

    route = route_ref[...]
    lane = lax.broadcasted_iota(jnp.int32, (TM, ROUTE_W), 1)
    pick1 = lane == route[:, 0:1].astype(jnp.int32)
    pick2 = lane == route[:, 1:2].astype(jnp.int32)
    chosen = jnp.where(pick1 | pick2, 1.0, 0.0).astype(BF16)
    rank = _dot(tri_ref[...], chosen)
    lane1 = lax.broadcasted_iota(jnp.int32, (1, ROUTE_W), 1)
    loff = jnp.zeros((1, ROUTE_W), F32)
    for e in range(N_EXPERTS):
        loff = jnp.where(lane1 == e, meta_ref[i * META_W + e].astype(F32), loff)
    place = rank + loff
    slot1 = jnp.sum(jnp.where(pick1, place, 0.0), axis=-1, keepdims=True)
    slot2 = jnp.sum(jnp.where(pick2, place, 0.0), axis=-1, keepdims=True)
    slot_ref[...] = jnp.where(lane == 0, slot1, jnp.where(lane == 1, slot2, 0.0))

    row1 = jnp.transpose(jnp.broadcast_to(slot1, (TM, 128)))[0:1, :]
    row2 = jnp.transpose(jnp.broadcast_to(slot2, (TM, 128)))[0:1, :]
    lrow = lax.broadcasted_iota(jnp.int32, (LROWS, TM), 0).astype(F32)
    p1 = jnp.where(lrow == row1, 1.0, 0.0).astype(BF16)
    p2 = jnp.where(lrow == row2, 1.0, 0.0).astype(BF16)
    local_ref[:, :D] = _dot(p1 + p2, a_ref[...]).astype(BF16)
    gates = []
    for col in (2, 3):
        pieces = _split3(route[:, col:col + 1])
        g = jnp.zeros((TM, ROUTE_W), F32)
        for k, piece in enumerate(pieces):
            g = jnp.where(lane == k, piece, g)
        gates.append(g.astype(BF16))
    local_ref[:, D:] = (_dot(p1, gates[0]) + _dot(p2, gates[1])).astype(BF16)
    _run_segment_copies(meta_ref, i, local_ref, xs_hbm, sem, True)


def _dispatch(n_tiles, n_blocks, a, route, meta, fill):
    tri = (jnp.arange(TM)[:, None] > jnp.arange(TM)[None, :]).astype(BF16)
    return pl.pallas_call(
        functools.partial(_dispatch_kernel, n_blocks),
        grid_spec=pltpu.PrefetchScalarGridSpec(
            num_scalar_prefetch=2,
            grid=(n_tiles,),
            in_specs=[pl.BlockSpec((TM, D), lambda i, *_: (i, 0)),
                      pl.BlockSpec((TM, ROUTE_W), lambda i, *_: (i, 0)),
                      _resident((TM, TM))],
            out_specs=[pl.BlockSpec(memory_space=pl.ANY), pl.BlockSpec((TM, ROUTE_W), lambda i, *_: (i, 0))],
            scratch_shapes=[pltpu.VMEM((LROWS, XS_W), BF16), pltpu.VMEM((MOE_RB, XS_W), BF16),
                            pltpu.SemaphoreType.DMA(()), pltpu.SemaphoreType.DMA(())],
        ),
        out_shape=[jax.ShapeDtypeStruct((n_blocks * MOE_RB, XS_W), BF16),
                   jax.ShapeDtypeStruct((n_tiles * TM, ROUTE_W), F32)],
        compiler_params=_params(1),
        name="moe_dispatch",
    )(meta, fill, a, route, tri)


def _expert_kernel(be_ref, fill_ref, xs_ref, w1_ref, w3_ref, w2_ref, ys_ref, t_ref):
    del be_ref
    used = pl.program_id(0) < fill_ref[2 * N_EXPERTS]

    @pl.when(used)
    def _ffn():
        y = _swiglu(xs_ref[:, :D], w1_ref, w3_ref, w2_ref, t_ref)
        gate = jnp.sum(xs_ref[:, D:].astype(F32), axis=-1, keepdims=True)
        ys_ref[...] = (gate * y).astype(BF16)

    @pl.when(jnp.logical_not(used))
    def _idle():
        ys_ref[...] = jnp.zeros(ys_ref.shape, BF16)


def _experts(xs, blk_exp, fill, layer, w1, w3, w2):
    def slab(d0, d1):
        return pl.BlockSpec((None, None, d0, d1), lambda i, be, fill: (layer, be[i], 0, 0),
                            pipeline_mode=pl.Buffered(1))

    n_blocks = blk_exp.shape[0]
    last_used = lambda i, fill: jnp.minimum(i, fill[2 * N_EXPERTS] - 1)
    return pl.pallas_call(
        _expert_kernel,
        grid_spec=pltpu.PrefetchScalarGridSpec(
            num_scalar_prefetch=2,
            grid=(n_blocks,),
            in_specs=[pl.BlockSpec((MOE_RB, XS_W), lambda i, be, fill: (last_used(i, fill), 0)),
                      slab(D, FFN_DIM), slab(D, FFN_DIM), slab(FFN_DIM, D)],
            out_specs=pl.BlockSpec((MOE_RB, D), lambda i, be, fill: (i, 0)),
            scratch_shapes=[pltpu.VMEM((MOE_RB, FFN_DIM), BF16)],
        ),
        out_shape=jax.ShapeDtypeStruct((n_blocks * MOE_RB, D), BF16),
        compiler_params=_params(1),
        name="moe_experts",
    )(blk_exp, fill, xs, w1, w3, w2)


def _combine_kernel(meta_ref, h_ref, mod_ref, slot_ref, ys_hbm, o_ref, local_ref, sem):
    i = pl.program_id(0)

    @pl.when(i == 0)
    def _init():
        local_ref[...] = jnp.zeros(local_ref.shape, BF16)

    _run_segment_copies(meta_ref, i, local_ref, ys_hbm, sem, False)
    slots = slot_ref[...]
    lcol = lax.broadcasted_iota(jnp.int32, (TM, LROWS), 1).astype(F32)
    gather = jnp.where((lcol == slots[:, 0:1]) | (lcol == slots[:, 1:2]), 1.0, 0.0).astype(BF16)
    o_ref[...] = h_ref[...] + mod_ref[0][5:6] * _dot(gather, local_ref[...])


def _combine(rows, n_tiles, h, modg, slots, ys, meta):
    tps, batch = rows.tiles_per_seq, rows.batch
    return pl.pallas_call(
        _combine_kernel,
        grid_spec=pltpu.PrefetchScalarGridSpec(
            num_scalar_prefetch=1,
            grid=(n_tiles,),
            in_specs=[pl.BlockSpec((TM, D), lambda i, m: (i, 0)),
                      pl.BlockSpec((1, 8, D), lambda i, m: (jnp.minimum(i // tps, batch), 0, 0)),
                      pl.BlockSpec((TM, ROUTE_W), lambda i, m: (i, 0)),
                      pl.BlockSpec(memory_space=pl.ANY)],
            out_specs=pl.BlockSpec((TM, D), lambda i, m: (i, 0)),
            scratch_shapes=[pltpu.VMEM((LROWS, D), BF16), pltpu.SemaphoreType.DMA(())],
        ),
        out_shape=jax.ShapeDtypeStruct((n_tiles * TM, D), F32),
        compiler_params=_params(1),
        name="moe_combine",
    )(meta, h, modg, slots, ys)


def _moe(rows, n_tiles, h, modg, w_r, b_r, layer, w1, w3, w2):
    a, route, cnt = _router(rows, n_tiles, h, modg, w_r, b_r)
    cnt = cnt[:, 0, :N_EXPERTS].astype(jnp.int32)
    seg = (cnt + SEG_ALIGN - 1) // SEG_ALIGN * SEG_ALIGN
    loff = jnp.cumsum(seg, axis=1) - seg
    total = jnp.sum(seg, axis=0)
    region = (total + MOE_RB - 1) // MOE_RB * MOE_RB
    region_end = jnp.cumsum(region)
    region_start = region_end - region
    goff = region_start[None, :] + jnp.cumsum(seg, axis=0) - seg
    meta = jnp.concatenate([loff, goff, seg], axis=1).reshape(-1)
    n_blocks = pl.cdiv(2 * n_tiles * TM + n_tiles * N_EXPERTS * (SEG_ALIGN - 1), MOE_RB) + N_EXPERTS
    n_used = region_end[-1] // MOE_RB
    fill = jnp.concatenate([region_start + total, region - total, n_used[None]]).astype(jnp.int32)
    blk = jnp.minimum(jnp.arange(n_blocks, dtype=jnp.int32), n_used - 1)
    blk_exp = jnp.minimum(jnp.searchsorted(region_end, blk * MOE_RB, side="right"), N_EXPERTS - 1).astype(jnp.int32)
    xs, slots = _dispatch(n_tiles, n_blocks, a, route, meta, fill)
    ys = _experts(xs, blk_exp, fill, layer, w1, w3, w2)
    return _combine(rows, n_tiles, h, modg, slots, ys, meta)


def kernel(x, c, ctx, c_ctx, ada_w, ada_b, norm_mix_g, norm_ffn_g, sc_in_w, sc_conv_w, sc_out_w, da_qkv_w, da_out_w,
           da_q_norm_g, da_k_norm_g, da_lambda, da_sub_norm_g, cm_in_w, cm_in_b, cm_v_norm_g, cm_ws, cm_bs, cm_out_w,
           sw_qkv_w, sw_out_w, sw_q_norm_g, sw_k_norm_g, sw_sink, ffn_w1, ffn_w3, ffn_w2, moe_router_w, moe_router_b,
           moe_w1, moe_w3, moe_w2):
    batch, seq, _ = x.shape
    ctx_len = ctx.shape[1]
    depth = ada_w.shape[0]
    assert depth == 4 and batch + 1 <= 16
    rows = _Rows(batch, seq, ctx_len)

    cvec = jnp.concatenate([c, c_ctx[None, :], jnp.zeros((16 - batch - 1, D), F32)], axis=0)
    mod = _ada_all(cvec, ada_w, ada_b)[:, :batch + 1].reshape(depth, batch + 1, 6, D)
    gains = jnp.stack([norm_mix_g, norm_ffn_g], axis=1)[:, None]
    modg = jnp.concatenate([mod, jnp.broadcast_to(gains, (depth, batch + 1, 2, D))], axis=2)

    bf = lambda w: w.astype(BF16)
    x2, ctx2 = x.reshape(-1, D), ctx.reshape(-1, D)

    bg, y = _conv_in(rows, x2, ctx2, modg[0], bf(sc_in_w[0]))
    conv_w = jnp.pad(sc_conv_w[0], ((0, 5), (0, 0)))
    h = _conv_out(rows, x2, ctx2, modg[0], bg, y, conv_w, bf(sc_out_w[0]))
    h = _ffn(rows, rows.all_tiles, h, modg[0], 0, ffn_w1, ffn_w3, ffn_w2)

    nq_chunks = D // CN
    qkv = _qkv(rows, h, modg[1], bf(da_qkv_w[0]), da_q_norm_g[0], da_k_norm_g[0], nq_chunks, nq_chunks)
    lam_init = 0.8 - 0.6 * math.exp(-0.3 * 1)
    o_lat, o_ctx = _diff_attn(rows, qkv, da_lambda[0], da_sub_norm_g[0], lam_init)
    h = _out_proj(rows, rows.all_tiles, h, modg[1], o_lat, o_ctx, bf(da_out_w[0]))
    h = _moe(rows, rows.all_tiles, h, modg[1], moe_router_w[0], moe_router_b[0], 0, moe_w1, moe_w3, moe_w2)

    h = _gmlp(rows, h, modg[2], bf(cm_in_w[0]), cm_in_b[0].reshape(1, -1), cm_v_norm_g[0].reshape(1, -1),
              bf(cm_ws[0]), cm_bs[0].T, bf(cm_out_w[0]))
    h = _ffn(rows, rows.all_tiles, h, modg[2], 1, ffn_w1, ffn_w3, ffn_w2)

    kv_chunks = SWA_KV_HEADS * HEAD_DIM // CN
    qkv = _qkv(rows, h, modg[3], bf(sw_qkv_w[0]), sw_q_norm_g[0], sw_k_norm_g[0], nq_chunks, kv_chunks)
    o = _swa(rows, qkv, sw_sink[0])
    h = _out_proj(rows, rows.lat_tiles, h, modg[3], o, o, bf(sw_out_w[0]))
    h = _moe(rows, rows.lat_tiles, h, modg[3], moe_router_w[1], moe_router_b[1], 1, moe_w1, moe_w3, moe_w2)
    return h.reshape(batch, seq, D)
```

```python
import functools
import math

import jax
import jax.numpy as jnp
from jax import lax
from jax.experimental import pallas as pl
from jax.experimental.pallas import tpu as pltpu

D = 1024
HEAD_DIM = 64
GRID_W = 64
ROPE_HALF = HEAD_DIM // 2
ROPE_BASE = 10000.0
DIFF_HEADS = D // (2 * HEAD_DIM)
SWA_Q_HEADS = D // HEAD_DIM
SWA_KV_HEADS = 4
SWA_GROUP = SWA_Q_HEADS // SWA_KV_HEADS
SWA_WINDOW = 128
CHUNK = 128
CM_WIDTH = 2 * D
CM_GROUPS = 8
CM_GW = CM_WIDTH // CM_GROUPS
FFN_DIM = 2816
N_EXPERTS = 8
EPS = 1e-6

F32 = jnp.float32
BF16 = jnp.bfloat16
HIGHEST = lax.Precision.HIGHEST
LOG2E = math.log2(math.e)

TM = 512
CN = 256
MOE_RB = 512
ATT_TQ = 512
ATT_TK = 1024
SWA_TQ = 256
ROUTE_W = 128
MAX_UNSHIFTED_LOGIT = 60.0
VMEM_LIMIT = 56 << 20


def _params(n_grid):
    return pltpu.CompilerParams(dimension_semantics=("arbitrary",) * n_grid, vmem_limit_bytes=VMEM_LIMIT)


def _resident(shape):
    zeros = (0,) * len(shape)
    return pl.BlockSpec(shape, lambda *_: zeros, pipeline_mode=pl.Buffered(1))


def _sigmoid(x):
    return 1.0 / (1.0 + jnp.exp(-x))


def _gelu_tanh(x):
    return 0.5 * x * (1.0 + jnp.tanh(math.sqrt(2.0 / math.pi) * (x + 0.044715 * (x * x * x))))


def _modnorm(x, g, shift, scale):
    y = x * lax.rsqrt(jnp.mean(x * x, axis=-1, keepdims=True) + EPS)
    return (y * g) * (1.0 + scale) + shift


def _dot(a, b):
    return jnp.dot(a, b, preferred_element_type=F32)


def _mix_mod(m):
    return m[6:7], m[0:1], m[1:2]


def _ffn_mod(m):
    return m[7:8], m[3:4], m[4:5]


def _ada_kernel(c_ref, w_ref, b_ref, o_ref):
    c = c_ref[...]
    s = c * _sigmoid(c)
    o_ref[0] = jnp.dot(s, w_ref[0], precision=HIGHEST, preferred_element_type=F32) + b_ref[0]


def _ada_all(cvec, ada_w, ada_b):
    depth, _, n = ada_w.shape
    tn = 1536
    rows = cvec.shape[0]
    return pl.pallas_call(
        _ada_kernel,
        grid=(depth, n // tn),
        in_specs=[pl.BlockSpec((rows, D), lambda l, j: (0, 0)),
                  pl.BlockSpec((1, D, tn), lambda l, j: (l, 0, j)),
                  pl.BlockSpec((1, 1, tn), lambda l, j: (l, 0, j))],
        out_specs=pl.BlockSpec((1, rows, tn), lambda l, j: (l, 0, j)),
        out_shape=jax.ShapeDtypeStruct((depth, rows, n), F32),
        compiler_params=_params(2),
        name="adaln",
    )(cvec, ada_w, ada_b.reshape(depth, 1, n))


class _Rows:
    def __init__(self, batch, seq, ctx_len):
        self.batch, self.seq, self.ctx_len = batch, seq, ctx_len
        self.n_lat = batch * seq
        self.n_ctx = batch * ctx_len
        self.n_all = self.n_lat + self.n_ctx
        assert seq % TM == 0 and self.n_ctx % TM == 0 and TM % ctx_len == 0
        self.lat_tiles = self.n_lat // TM
        self.all_tiles = self.n_all // TM
        self.tiles_per_seq = seq // TM

    def mod_spec(self):
        tps, batch = self.tiles_per_seq, self.batch
        return pl.BlockSpec((1, 8, D), lambda i, *_: (jnp.minimum(i // tps, batch), 0, 0))

    def row_spec(self, width):
        return pl.BlockSpec((TM, width), lambda i, *_: (i, 0))

    def split_specs(self, width):
        lat_tiles = self.lat_tiles
        return [pl.BlockSpec((TM, width), lambda i, *_: (jnp.minimum(i, lat_tiles - 1), 0)),
                pl.BlockSpec((TM, width), lambda i, *_: (jnp.maximum(i - lat_tiles, 0), 0))]


def _conv_in_kernel(lat_tiles, x_ref, ctx_ref, mod_ref, w_ref, bg_ref, y_ref):
    h = jnp.where(pl.program_id(0) < lat_tiles, x_ref[...], ctx_ref[...])
    a = _modnorm(h, *_mix_mod(mod_ref[0])).astype(BF16)
    for c in range(D // CN):
        lo = c * CN
        bg_ref[:, lo:lo + CN] = _dot(a, w_ref[:, lo:lo + CN]).astype(BF16)
        cg = _dot(a, w_ref[:, D + lo:D + lo + CN])
        xv = _dot(a, w_ref[:, 2 * D + lo:2 * D + lo + CN])
        y_ref[:, lo:lo + CN] = (cg * xv).astype(BF16)


def _conv_in(rows, x, ctx, modg, w_in):
    out = jax.ShapeDtypeStruct((rows.n_all, D), BF16)
    return pl.pallas_call(
        functools.partial(_conv_in_kernel, rows.lat_tiles),
        grid=(rows.all_tiles,),
        in_specs=rows.split_specs(D) + [rows.mod_spec(), _resident((D, 3 * D))],
        out_specs=[rows.row_spec(D), rows.row_spec(D)],
        out_shape=[out, out],
        compiler_params=_params(1),
        name="conv_in",
    )(x, ctx, modg, w_in)


HALO = 16


def _conv_out_kernel(n_lat, seq, ctx_len, x_ref, ctx_ref, mod_ref, bg_ref, y_ref, yp_ref, yn_ref, cw_ref, w_ref,
                     o_ref):
    i = pl.program_id(0)
    h = jnp.where(i * TM < n_lat, x_ref[...], ctx_ref[...])
    m = mod_ref[0]
    y = y_ref[...].astype(F32)
    row = lax.broadcasted_iota(jnp.int32, (TM, 1), 0)
    grow = row + i * TM
    seq_len = jnp.where(grow < n_lat, seq, ctx_len)
    pos = grow & (seq_len - 1)
    prev_row = yp_ref[HALO - 1:HALO, :].astype(F32)
    next_row = yn_ref[0:1, :].astype(F32)
    y_m1 = jnp.where(row == 0, prev_row, pltpu.roll(y, 1, 0))
    y_m1 = jnp.where(pos == 0, 0.0, y_m1)
    y_p1 = jnp.where(row == TM - 1, next_row, pltpu.roll(y, TM - 1, 0))
    y_p1 = jnp.where(pos == seq_len - 1, 0.0, y_p1)
    conv = cw_ref[0:1, :] * y_m1 + cw_ref[1:2, :] * y + cw_ref[2:3, :] * y_p1
    z = (bg_ref[...].astype(F32) * conv).astype(BF16)
    o_ref[...] = h + m[2:3] * _dot(z, w_ref[...])


def _conv_out(rows, x, ctx, modg, bg, y, conv_w, w_out):
    hb = TM // HALO
    last = rows.n_all // HALO - 1
    return pl.pallas_call(
        functools.partial(_conv_out_kernel, rows.n_lat, rows.seq, rows.ctx_len),
        grid=(rows.all_tiles,),
        in_specs=rows.split_specs(D) + [rows.mod_spec(), rows.row_spec(D), rows.row_spec(D),
                  pl.BlockSpec((HALO, D), lambda i: (jnp.maximum(i * hb - 1, 0), 0)),
                  pl.BlockSpec((HALO, D), lambda i: (jnp.minimum((i + 1) * hb, last), 0)),
                  _resident((8, D)), _resident((D, D))],
        out_specs=rows.row_spec(D),
        out_shape=jax.ShapeDtypeStruct((rows.n_all, D), F32),
        compiler_params=_params(1),
        name="conv_out",
    )(x, ctx, modg, bg, y, y, y, conv_w, w_out)


W2_KC = 768


def _swiglu(a, w1_ref, w3_ref, w2_ref, t_ref):
    for c in range(FFN_DIM // CN):
        lo = c * CN
        h1 = _dot(a, w1_ref[:, lo:lo + CN].astype(BF16))
        h3 = _dot(a, w3_ref[:, lo:lo + CN].astype(BF16))
        t_ref[:, lo:lo + CN] = (h1 * _sigmoid(h1) * h3).astype(BF16)
    out = None
    for lo in range(0, FFN_DIM, W2_KC):
        hi = min(lo + W2_KC, FFN_DIM)
        part = _dot(t_ref[:, lo:hi], w2_ref[lo:hi, :].astype(BF16))
        out = part if out is None else out + part
    return out


def _ffn_kernel(h_ref, mod_ref, w1_ref, w3_ref, w2_ref, o_ref, t_ref):
    m = mod_ref[0]
    x = h_ref[...]
    a = _modnorm(x, *_ffn_mod(m)).astype(BF16)
    o_ref[...] = x + m[5:6] * _swiglu(a, w1_ref, w3_ref, w2_ref, t_ref)


def _ffn(rows, n_tiles, h, modg, layer, w1, w3, w2):
    def slab(d0, d1):
        return pl.BlockSpec((None, d0, d1), lambda i: (layer, 0, 0), pipeline_mode=pl.Buffered(1))

    return pl.pallas_call(
        _ffn_kernel,
        grid=(n_tiles,),
        in_specs=[rows.row_spec(D), rows.mod_spec(), slab(D, FFN_DIM), slab(D, FFN_DIM), slab(FFN_DIM, D)],
        out_specs=rows.row_spec(D),
        out_shape=jax.ShapeDtypeStruct((n_tiles * TM, D), F32),
        scratch_shapes=[pltpu.VMEM((TM, FFN_DIM), BF16)],
        compiler_params=_params(1),
        name="ffn_dense",
    )(h, modg, w1, w3, w2)


def _norm_rope(x, gmat_ref, cos, sin):
    ms = _dot((x * x).astype(BF16), gmat_ref[...]) * (1.0 / HEAD_DIM)
    xn = x * lax.rsqrt(ms + EPS)
    lane = lax.broadcasted_iota(jnp.int32, (1, CN), 1)
    first_half = (lane & (ROPE_HALF - 1)) < (ROPE_HALF // 2)
    partner = jnp.where(first_half, pltpu.roll(xn, CN - ROPE_HALF // 2, 1), pltpu.roll(xn, ROPE_HALF // 2, 1))
    return xn * cos + partner * sin


def _qkv_kernel(n_qk_chunks, n_q_chunks, h_ref, mod_ref, w_ref, gmat_ref, qcos_ref, qsin_ref, kcos_ref, ksin_ref,
                o_ref):
    a = _modnorm(h_ref[...], *_mix_mod(mod_ref[0])).astype(BF16)
    n_chunks = w_ref.shape[1] // CN

    def project(c):
        return _dot(a, w_ref[:, c * CN:(c + 1) * CN])

    nxt = project(0)
    for c in range(n_chunks):
        acc = nxt
        if c + 1 < n_chunks:
            nxt = project(c + 1)
        if c < n_q_chunks:
            acc = _norm_rope(acc, gmat_ref, qcos_ref[...], qsin_ref[...])
        elif c < n_qk_chunks:
            acc = _norm_rope(acc, gmat_ref, kcos_ref[...], ksin_ref[...])
        o_ref[:, c * CN:(c + 1) * CN] = acc.astype(BF16)


def _rope_tables(rows, gain, scale):
    seq = rows.seq
    pos = jnp.arange(seq)
    n_freq = ROPE_HALF // 2
    inv = ROPE_BASE ** (-jnp.arange(n_freq, dtype=F32) / n_freq)
    ang_r = (pos // GRID_W).astype(F32)[:, None] * inv
    ang_c = (pos % GRID_W).astype(F32)[:, None] * inv
    cos = jnp.concatenate([jnp.cos(ang_r)] * 2 + [jnp.cos(ang_c)] * 2, axis=-1)
    sin = jnp.concatenate([-jnp.sin(ang_r), jnp.sin(ang_r), -jnp.sin(ang_c), jnp.sin(ang_c)], axis=-1)
    cos = jnp.concatenate([cos, jnp.ones((TM, HEAD_DIM), F32)], axis=0)
    sin = jnp.concatenate([sin, jnp.zeros((TM, HEAD_DIM), F32)], axis=0)
    dim = jnp.arange(HEAD_DIM)
    partner = jnp.where((dim % ROPE_HALF) < n_freq, dim + n_freq, dim - n_freq)
    g = gain.astype(F32) * scale
    reps = CN // HEAD_DIM
    return jnp.tile(cos * g[None, :], (1, reps)), jnp.tile(sin * g[partner][None, :], (1, reps))


def _qkv(rows, h, modg, w, q_gain, k_gain, n_q_chunks, n_k_chunks):
    width = w.shape[1]
    head = jnp.arange(CN) // HEAD_DIM
    gmat = (head[:, None] == head[None, :]).astype(BF16)
    tables = _rope_tables(rows, q_gain, LOG2E * HEAD_DIM ** -0.5) + _rope_tables(rows, k_gain, 1.0)
    tps, lat_tiles = rows.tiles_per_seq, rows.lat_tiles
    tab_spec = pl.BlockSpec((TM, CN), lambda i: (jnp.where(i < lat_tiles, i % tps, tps), 0))
    return pl.pallas_call(
        functools.partial(_qkv_kernel, n_q_chunks + n_k_chunks, n_q_chunks),
        grid=(rows.all_tiles,),
        in_specs=[rows.row_spec(D), rows.mod_spec(), _resident((D, width)), _resident((CN, CN))] + [tab_spec] * 4,
        out_specs=rows.row_spec(width),
        out_shape=jax.ShapeDtypeStruct((rows.n_all, width), BF16),
        compiler_params=_params(1),
        name="qkv_proj",
    )(h, modg, w, gmat, *tables)


DH2 = 2 * HEAD_DIM


def _diff_attn_kernel(lam_init, n_lat_chunks, q_ref, kc_ref, vc_ref, kl_ref, vl_ref, lam_ref, subg_ref, o_ref,
                      m_ref, acc_ref, vca_ref, vla_ref, kn_ref):
    def max_norm_sq(x):
        xf = x.astype(F32)
        return jnp.max(jnp.sum(xf * xf, axis=-1, keepdims=True), axis=0, keepdims=True)

    def per_head_setup():
        vca_ref[:, :DH2] = vc_ref[...]
        vca_ref[:, DH2:] = jnp.ones((vc_ref.shape[0], DH2), BF16)
        kn = max_norm_sq(kc_ref[...])
        if n_lat_chunks:
            vla_ref[:, :DH2] = vl_ref[...]
            vla_ref[:, DH2:] = jnp.ones((vl_ref.shape[0], DH2), BF16)
            kn = jnp.maximum(kn, max_norm_sq(kl_ref[...]))
        kn_ref[...] = jnp.broadcast_to(kn, kn_ref.shape)

    if n_lat_chunks:
        pl.when(pl.program_id(2) == 0)(per_head_setup)
    else:
        per_head_setup()

    q = q_ref[...]
    lane = lax.broadcasted_iota(jnp.int32, (1, DH2), 1)
    zero = jnp.zeros((), BF16)
    qs = (jnp.where(lane < HEAD_DIM, q, zero), jnp.where(lane >= HEAD_DIM, q, zero))
    nt = (((1,), (1,)), ((), ()))

    def over_keys(update):
        update(kc_ref[...], vca_ref[...])
        if n_lat_chunks:
            tk = kl_ref.shape[0] // n_lat_chunks

            def body(c, carry):
                start = pl.multiple_of(c * tk, tk)
                update(kl_ref[pl.ds(start, tk), :], vla_ref[pl.ds(start, tk), :])
                return carry
            lax.fori_loop(0, n_lat_chunks, body, 0)

    def unshifted():
        for mi in range(2):
            s = lax.dot_general(qs[mi], kc_ref[...], nt, preferred_element_type=F32)
            acc = _dot(jnp.exp2(s).astype(BF16), vca_ref[...])
            if n_lat_chunks:
                s = lax.dot_general(qs[mi], kl_ref[...], nt, preferred_element_type=F32)
                acc = acc + _dot(jnp.exp2(s).astype(BF16), vla_ref[...])
            acc_ref[mi] = acc

    def update_online(k, va):
        reps = k.shape[0] // DH2
        for mi in range(2):
            s = lax.dot_general(qs[mi], k, nt, preferred_element_type=F32)
            m_old = m_ref[mi]
            m_new = jnp.maximum(m_old, jnp.max(s, axis=-1, keepdims=True))
            alpha = jnp.exp2(m_old - m_new)
            p = jnp.exp2(s - jnp.concatenate([m_new] * reps, axis=-1))
            acc_ref[mi] = jnp.concatenate([alpha, alpha], axis=-1) * acc_ref[mi] + _dot(p.astype(BF16), va)
            m_ref[mi] = m_new

    def online():
        m_ref[...] = jnp.full(m_ref.shape, -jnp.inf, F32)
        acc_ref[...] = jnp.zeros(acc_ref.shape, F32)
        over_keys(update_online)

    bound_sq = max_norm_sq(q) * kn_ref[0:1, 0:1]
    small_logits = bound_sq[0, 0] <= MAX_UNSHIFTED_LOGIT ** 2
    pl.when(small_logits)(unshifted)
    pl.when(jnp.logical_not(small_logits))(online)

    lp = lam_ref[...]
    lam = (jnp.exp(jnp.sum(lp[0:1] * lp[1:2], axis=-1, keepdims=True))
           - jnp.exp(jnp.sum(lp[2:3] * lp[3:4], axis=-1, keepdims=True)) + lam_init)
    acc0 = acc_ref[0]
    acc1 = acc_ref[1]
    o = acc0[:, :DH2] / acc0[:, DH2:] - lam * (acc1[:, :DH2] / acc1[:, DH2:])
    o = o * lax.rsqrt(jnp.mean(o * o, axis=-1, keepdims=True) + EPS) * subg_ref[...]
    o_ref[...] = (o * (1.0 - lam_init)).astype(BF16)


def _diff_attn(rows, qkv, lam_p, sub_g, lam_init):
    batch, seq, ctx_len = rows.batch, rows.seq, rows.ctx_len
    nq = seq // ATT_TQ
    nh = DIFF_HEADS
    ctx0 = rows.n_lat // ctx_len
    sub_g = sub_g.reshape(1, DH2)
    small = [_resident((4, HEAD_DIM)), _resident((1, DH2))]
    kc_spec = pl.BlockSpec((ctx_len, DH2), lambda b, h, *_: (ctx0 + b, nh + h))
    vc_spec = pl.BlockSpec((ctx_len, DH2), lambda b, h, *_: (ctx0 + b, 2 * nh + h))

    def scratch(tq, n_lat_keys):
        return [pltpu.VMEM((2, tq, DH2), F32), pltpu.VMEM((2, tq, 2 * DH2), F32),
                pltpu.VMEM((ctx_len, 2 * DH2), BF16), pltpu.VMEM((n_lat_keys, 2 * DH2), BF16),
                pltpu.VMEM((8, DH2), F32)]

    o_lat = pl.pallas_call(
        functools.partial(_diff_attn_kernel, lam_init, pl.cdiv(seq, ATT_TK)),
        grid=(batch, nh, nq),
        in_specs=[pl.BlockSpec((ATT_TQ, DH2), lambda b, h, i: (b * nq + i, h)),
                  kc_spec, vc_spec,
                  pl.BlockSpec((seq, DH2), lambda b, h, i: (b, nh + h)),
                  pl.BlockSpec((seq, DH2), lambda b, h, i: (b, 2 * nh + h))] + small,
        out_specs=pl.BlockSpec((ATT_TQ, DH2), lambda b, h, i: (b * nq + i, h)),
        out_shape=jax.ShapeDtypeStruct((rows.n_lat, D), BF16),
        scratch_shapes=scratch(ATT_TQ, seq),
        compiler_params=_params(3),
        name="diff_attn_latent",
    )(qkv, qkv, qkv, qkv, qkv, lam_p, sub_g)

    def ctx_kernel(q_ref, kc_ref, vc_ref, lam_ref, subg_ref, o_ref, *scratch_refs):
        _diff_attn_kernel(lam_init, 0, q_ref, kc_ref, vc_ref, None, None, lam_ref, subg_ref, o_ref, *scratch_refs)

    o_ctx = pl.pallas_call(
        ctx_kernel,
        grid=(batch, nh),
        in_specs=[pl.BlockSpec((ctx_len, DH2), lambda b, h: (ctx0 + b, h)), kc_spec, vc_spec] + small,
        out_specs=pl.BlockSpec((ctx_len, DH2), lambda b, h: (b, h)),
        out_shape=jax.ShapeDtypeStruct((rows.n_ctx, D), BF16),
        scratch_shapes=scratch(ctx_len, 16),
        compiler_params=_params(2),
        name="diff_attn_context",
    )(qkv, qkv, qkv, lam_p, sub_g)
    return o_lat, o_ctx


def _out_proj_kernel(lat_tiles, h_ref, mod_ref, zl_ref, zc_ref, w_ref, o_ref):
    i = pl.program_id(0)

    def project(z_ref):
        o_ref[...] = h_ref[...] + mod_ref[0][2:3] * _dot(z_ref[...], w_ref[...])

    pl.when(i < lat_tiles)(lambda: project(zl_ref))
    pl.when(i >= lat_tiles)(lambda: project(zc_ref))


def _out_proj(rows, n_tiles, h, modg, z_lat, z_ctx, w):
    lat_tiles = rows.lat_tiles
    return pl.pallas_call(
        functools.partial(_out_proj_kernel, lat_tiles),
        grid=(n_tiles,),
        in_specs=[rows.row_spec(D), rows.mod_spec(),
                  pl.BlockSpec((TM, D), lambda i: (jnp.minimum(i, lat_tiles - 1), 0)),
                  pl.BlockSpec((TM, D), lambda i: (jnp.maximum(i - lat_tiles, 0), 0)),
                  _resident((D, D))],
        out_specs=rows.row_spec(D),
        out_shape=jax.ShapeDtypeStruct((n_tiles * TM, D), F32),
        compiler_params=_params(1),
        name="out_proj",
    )(h, modg, z_lat, z_ctx, w)


def _gmlp_kernel(h_ref, mod_ref, win_ref, bin_ref, vg_ref, ws_ref, bs_ref, wout_ref, o_ref, u_ref, v_ref, t_ref):
    m = mod_ref[0]
    x = h_ref[...]
    a = _modnorm(x, *_mix_mod(m)).astype(BF16)
    ssq = jnp.zeros((TM, 1), F32)
    for c in range(2 * CM_WIDTH // CN):
        lo = c * CN
        z = _gelu_tanh(_dot(a, win_ref[:, lo:lo + CN]) + bin_ref[:, lo:lo + CN])
        if lo < CM_WIDTH:
            u_ref[:, lo:lo + CN] = z
        else:
            v_ref[:, lo - CM_WIDTH:lo - CM_WIDTH + CN] = z
            ssq = ssq + jnp.sum(z * z, axis=-1, keepdims=True)
    inv = lax.rsqrt(ssq * (1.0 / CM_WIDTH) + EPS)
    for g in range(CM_GROUPS):
        lo = g * CM_GW
        vn = (v_ref[:, lo:lo + CM_GW] * inv * vg_ref[:, lo:lo + CM_GW]).astype(BF16)
        for r in range(TM // CHUNK):
            r0 = r * CHUNK
            sv = _dot(ws_ref[g], vn[r0:r0 + CHUNK]) + bs_ref[:, g:g + 1]
            t_ref[r0:r0 + CHUNK, lo:lo + CM_GW] = (u_ref[r0:r0 + CHUNK, lo:lo + CM_GW] * sv).astype(BF16)
    o_ref[...] = x + m[2:3] * _dot(t_ref[...], wout_ref[...])


def _gmlp(rows, h, modg, w_in, b_in, v_g, w_s, b_s, w_out):
    return pl.pallas_call(
        _gmlp_kernel,
        grid=(rows.all_tiles,),
        in_specs=[rows.row_spec(D), rows.mod_spec(), _resident((D, 2 * CM_WIDTH)), _resident((1, 2 * CM_WIDTH)),
                  _resident((1, CM_WIDTH)), _resident((CM_GROUPS, CHUNK, CHUNK)), _resident((CHUNK, CM_GROUPS)),
                  _resident((CM_WIDTH, D))],
        out_specs=rows.row_spec(D),
        out_shape=jax.ShapeDtypeStruct((rows.n_all, D), F32),
        scratch_shapes=[pltpu.VMEM((TM, CM_WIDTH), F32), pltpu.VMEM((TM, CM_WIDTH), F32),
                        pltpu.VMEM((TM, CM_WIDTH), BF16)],
        compiler_params=_params(1),
        name="gmlp",
    )(h, modg, w_in, b_in, v_g, w_s, b_s, w_out)


SWA_BAND = SWA_TQ + 2 * SWA_WINDOW


SWA_PAIR = 2 * HEAD_DIM
N_KV_VARIANTS = 2 * SWA_KV_HEADS


def _swa_kernel(seq, sink_ref, q_ref, kc_ref, vc_ref, kl_ref, vl_ref, o_ref, kcv_ref, vcv_ref, klv_ref, vlv_ref,
                kn_ref):
    qi = pl.program_id(1)
    lane = lax.broadcasted_iota(jnp.int32, (1, SWA_PAIR), 1)
    lo_half = lane < HEAD_DIM

    def max_norm_sq(x):
        xf = x.astype(F32)
        return jnp.max(jnp.sum(xf * xf, axis=-1, keepdims=True), axis=0, keepdims=True)

    @pl.when(qi == 0)
    def _per_batch_setup():
        kn = jnp.zeros((1, 1), F32)
        for src_ref, dst_ref in ((kc_ref, kcv_ref), (vc_ref, vcv_ref), (kl_ref, klv_ref), (vl_ref, vlv_ref)):
            for half in range(SWA_KV_HEADS // 2):
                x = src_ref[:, half * SWA_PAIR:(half + 1) * SWA_PAIR].astype(F32)
                xr = pltpu.roll(x, HEAD_DIM, 1)
                j0, j1 = 2 * half, 2 * half + 1
                dst_ref[2 * j0] = jnp.where(lo_half, x, 0.0).astype(BF16)
                dst_ref[2 * j0 + 1] = jnp.where(lo_half, 0.0, xr).astype(BF16)
                dst_ref[2 * j1] = jnp.where(lo_half, xr, 0.0).astype(BF16)
                dst_ref[2 * j1 + 1] = jnp.where(lo_half, 0.0, x).astype(BF16)
                if src_ref is kc_ref or src_ref is kl_ref:
                    kn = jnp.maximum(kn, max_norm_sq(x))
        kn_ref[...] = jnp.broadcast_to(kn, kn_ref.shape)

    q0 = qi * SWA_TQ
    start = jnp.clip(q0 - SWA_WINDOW, 0, seq - SWA_BAND)
    start = pl.multiple_of(start, SWA_WINDOW)
    qpos = q0 + lax.broadcasted_iota(jnp.int32, (SWA_TQ, SWA_BAND), 0)
    kpos = start + lax.broadcasted_iota(jnp.int32, (SWA_TQ, SWA_BAND), 1)
    in_band = jnp.abs(qpos - kpos) <= SWA_WINDOW
    nt = (((1,), (1,)), ((), ()))

    def unshifted():
        in_band2 = jnp.concatenate([in_band, in_band], axis=0)
        top = lax.broadcasted_iota(jnp.int32, (2 * SWA_TQ, 1), 0) < SWA_TQ
        for j in range(SWA_KV_HEADS):
            lo = j * SWA_GROUP * HEAD_DIM
            q2 = jnp.concatenate([q_ref[:, lo:lo + SWA_PAIR], q_ref[:, lo + SWA_PAIR:lo + 2 * SWA_PAIR]], axis=0)
            out = jnp.zeros((2 * SWA_TQ, SWA_PAIR), F32)
            for var in range(2):
                v = 2 * j + var
                s_c = lax.dot_general(q2, kcv_ref[v], nt, preferred_element_type=F32)
                s_b = lax.dot_general(q2, klv_ref[v, pl.ds(start, SWA_BAND), :], nt, preferred_element_type=F32)
                p_c = jnp.exp2(s_c)
                p_b = jnp.where(in_band2, jnp.exp2(s_b), 0.0)
                hq = j * SWA_GROUP + var
                sink = jnp.where(top, sink_ref[hq] * LOG2E, sink_ref[hq + 2] * LOG2E)
                denom = (jnp.sum(p_c, axis=-1, keepdims=True) + jnp.sum(p_b, axis=-1, keepdims=True)
                         + jnp.exp2(sink))
                o = (_dot(p_c.astype(BF16), vcv_ref[v])
                     + _dot(p_b.astype(BF16), vlv_ref[v, pl.ds(start, SWA_BAND), :]))
                out = out + o / denom
            o_ref[:, lo:lo + SWA_PAIR] = out[:SWA_TQ].astype(BF16)
            o_ref[:, lo + SWA_PAIR:lo + 2 * SWA_PAIR] = out[SWA_TQ:].astype(BF16)

    def shifted():
        kb = kl_ref[pl.ds(start, SWA_BAND), :]
        vb = vl_ref[pl.ds(start, SWA_BAND), :]
        kc = kc_ref[...]
        vc = vc_ref[...]
        for j in range(SWA_KV_HEADS):
            kj = slice(j * HEAD_DIM, (j + 1) * HEAD_DIM)
            kbj, vbj, kcj, vcj = kb[:, kj], vb[:, kj], kc[:, kj], vc[:, kj]
            outs = []
            for g in range(SWA_GROUP):
                hq = j * SWA_GROUP + g
                qh = q_ref[:, hq * HEAD_DIM:(hq + 1) * HEAD_DIM]
                s_c = lax.dot_general(qh, kcj, nt, preferred_element_type=F32)
                s_b = lax.dot_general(qh, kbj, nt, preferred_element_type=F32)
                s_b = jnp.where(in_band, s_b, -jnp.inf)
                sink = sink_ref[hq] * LOG2E
                mx = jnp.maximum(jnp.maximum(jnp.max(s_c, axis=-1, keepdims=True),
                                             jnp.max(s_b, axis=-1, keepdims=True)), sink)
                p_c = jnp.exp2(s_c - mx)
                p_b = jnp.exp2(s_b - mx)
                denom = (jnp.sum(p_c, axis=-1, keepdims=True) + jnp.sum(p_b, axis=-1, keepdims=True)
                         + jnp.exp2(sink - mx))
                o = _dot(p_c.astype(BF16), vcj) + _dot(p_b.astype(BF16), vbj)
                outs.append(o / denom)
            lo = j * SWA_GROUP * HEAD_DIM
            o_ref[:, lo:lo + SWA_GROUP * HEAD_DIM] = jnp.concatenate(outs, axis=-1).astype(BF16)

    qn = jnp.zeros((1, 1), F32)
    for pair in range(SWA_Q_HEADS // 2):
        qn = jnp.maximum(qn, max_norm_sq(q_ref[:, pair * SWA_PAIR:(pair + 1) * SWA_PAIR]))
    max_sink = sink_ref[0]
    for hq in range(1, SWA_Q_HEADS):
        max_sink = jnp.maximum(max_sink, sink_ref[hq])
    small_logits = jnp.logical_and((qn * kn_ref[0:1, 0:1])[0, 0] <= MAX_UNSHIFTED_LOGIT ** 2,
                                   max_sink * LOG2E <= MAX_UNSHIFTED_LOGIT)
    pl.when(small_logits)(unshifted)
    pl.when(jnp.logical_not(small_logits))(shifted)


def _swa(rows, qkv, sink):
    batch, seq, ctx_len = rows.batch, rows.seq, rows.ctx_len
    nq = seq // SWA_TQ
    kvw = SWA_KV_HEADS * HEAD_DIM
    k_col = D // kvw
    ctx0 = rows.n_lat // ctx_len
    return pl.pallas_call(
        functools.partial(_swa_kernel, seq),
        grid=(batch, nq),
        in_specs=[pl.BlockSpec(memory_space=pltpu.SMEM),
                  pl.BlockSpec((SWA_TQ, D), lambda b, i: (b * nq + i, 0)),
                  pl.BlockSpec((ctx_len, kvw), lambda b, i: (ctx0 + b, k_col)),
                  pl.BlockSpec((ctx_len, kvw), lambda b, i: (ctx0 + b, k_col + 1)),
                  pl.BlockSpec((seq, kvw), lambda b, i: (b, k_col)),
                  pl.BlockSpec((seq, kvw), lambda b, i: (b, k_col + 1))],
        out_specs=pl.BlockSpec((SWA_TQ, D), lambda b, i: (b * nq + i, 0)),
        out_shape=jax.ShapeDtypeStruct((rows.n_lat, D), BF16),
        scratch_shapes=[pltpu.VMEM((N_KV_VARIANTS, ctx_len, SWA_PAIR), BF16),
                        pltpu.VMEM((N_KV_VARIANTS, ctx_len, SWA_PAIR), BF16),
                        pltpu.VMEM((N_KV_VARIANTS, seq, SWA_PAIR), BF16),
                        pltpu.VMEM((N_KV_VARIANTS, seq, SWA_PAIR), BF16),
                        pltpu.VMEM((8, SWA_PAIR), F32)],
        compiler_params=_params(2),
        name="swa_attn",
    )(sink, qkv, qkv, qkv, qkv, qkv)


def _router_kernel(h_ref, mod_ref, wr_ref, br_ref, a_ref, route_ref, cnt_ref):
    a = _modnorm(h_ref[...], *_ffn_mod(mod_ref[0]))
    a_hi = a.astype(BF16)
    a_ref[...] = a_hi
    a_lo = (a - a_hi.astype(F32)).astype(BF16)
    hi_part = _dot(a_hi, wr_ref[...])
    logits = (hi_part[:, :ROUTE_W] + hi_part[:, ROUTE_W:]) + _dot(a_lo, wr_ref[:, :ROUTE_W]) + br_ref[...]
    lane = lax.broadcasted_iota(jnp.int32, (TM, ROUTE_W), 1)
    m1 = jnp.max(logits, axis=-1, keepdims=True)
    i1 = jnp.min(jnp.where(logits == m1, lane, ROUTE_W), axis=-1, keepdims=True)
    rest = jnp.where(lane == i1, -jnp.inf, logits)
    m2 = jnp.max(rest, axis=-1, keepdims=True)
    i2 = jnp.min(jnp.where(rest == m2, lane, ROUTE_W), axis=-1, keepdims=True)
    e2 = jnp.exp(m2 - m1)
    gate1 = 1.0 / (1.0 + e2)
    gate2 = e2 / (1.0 + e2)
    rec = jnp.where(lane == 0, i1.astype(F32), 0.0)
    rec = jnp.where(lane == 1, i2.astype(F32), rec)
    rec = jnp.where(lane == 2, gate1, rec)
    route_ref[...] = jnp.where(lane == 3, gate2, rec)
    chosen = jnp.where((lane == i1) | (lane == i2), 1.0, 0.0)
    cnt_ref[0] = jnp.broadcast_to(jnp.sum(chosen, axis=0, keepdims=True), (8, ROUTE_W))


def _router(rows, n_tiles, h, modg, w_r, b_r):
    pad = ROUTE_W - N_EXPERTS
    w_pad = jnp.pad(w_r, ((0, 0), (0, pad)))
    w_hi = w_pad.astype(BF16)
    w_lo = (w_pad - w_hi.astype(F32)).astype(BF16)
    w_split = jnp.concatenate([w_hi, w_lo], axis=1)
    b_pad = jnp.concatenate([b_r.astype(F32), jnp.full((pad,), -1e30, F32)]).reshape(1, ROUTE_W)
    return pl.pallas_call(
        _router_kernel,
        grid=(n_tiles,),
        in_specs=[rows.row_spec(D), rows.mod_spec(), _resident((D, 2 * ROUTE_W)), _resident((1, ROUTE_W))],
        out_specs=[rows.row_spec(D), rows.row_spec(ROUTE_W), pl.BlockSpec((1, 8, ROUTE_W), lambda i: (i, 0, 0))],
        out_shape=[jax.ShapeDtypeStruct((n_tiles * TM, D), BF16), jax.ShapeDtypeStruct((n_tiles * TM, ROUTE_W), F32),
                   jax.ShapeDtypeStruct((n_tiles, 8, ROUTE_W), F32)],
        compiler_params=_params(1),
        name="moe_router",
    )(h, modg, w_split, b_pad)


SEG_ALIGN = 16
SEG_BITS = tuple(1 << b for b in range(TM.bit_length() - 1, SEG_ALIGN.bit_length() - 2, -1))
LROWS = -(-(2 * TM + N_EXPERTS * (SEG_ALIGN - 1)) // 128) * 128
XS_W = D + 128
META_W = 3 * N_EXPERTS


def _segment_copies(meta_ref, tile, local_ref, slots_hbm, sem, to_slots):
    for e in range(N_EXPERTS):
        loff = meta_ref[tile * META_W + e]
        goff = meta_ref[tile * META_W + N_EXPERTS + e]
        seg = meta_ref[tile * META_W + 2 * N_EXPERTS + e]
        for bit in SEG_BITS:
            done = seg & ~(2 * bit - 1)
            lo = local_ref.at[pl.ds(pl.multiple_of(loff + done, SEG_ALIGN), bit)]
            gl = slots_hbm.at[pl.ds(pl.multiple_of(goff + done, SEG_ALIGN), bit)]
            copy = pltpu.make_async_copy(lo, gl, sem) if to_slots else pltpu.make_async_copy(gl, lo, sem)
            yield (seg & bit) != 0, copy


def _start_segment_copies(*args):
    for cond, copy in _segment_copies(*args):
        pl.when(cond)(copy.start)


def _wait_segment_copies(*args):
    for cond, copy in _segment_copies(*args):
        pl.when(cond)(copy.wait)


def _split3(x):
    hi = x.astype(BF16).astype(F32)
    mid = (x - hi).astype(BF16).astype(F32)
    lo = ((x - hi) - mid).astype(BF16).astype(F32)
    return hi, mid, lo


def _dispatch_kernel(n_blocks, meta_ref, fill_ref, a_ref, route_ref, tri_ref, xs_hbm, slot_ref, local_ref, zero_ref,
                     sem, zsem):
    i = pl.program_id(0)

    @pl.when(i == 0)
    def _zero_unwritten_slots():
        zero_ref[...] = jnp.zeros(zero_ref.shape, BF16)
        tails = []
        for e in range(N_EXPERTS):
            off, length = fill_ref[e], fill_ref[N_EXPERTS + e]
            for bit in (b for b in SEG_BITS if b < MOE_RB):
                done = length & ~(2 * bit - 1)
                dst = xs_hbm.at[pl.ds(pl.multiple_of(off + done, SEG_ALIGN), bit)]
                tails.append(((length & bit) != 0, pltpu.make_async_copy(zero_ref.at[pl.ds(0, bit)], dst, zsem)))
        for cond, copy in tails:
            pl.when(cond)(copy.start)

        def block_copy(blk):
            dst = xs_hbm.at[pl.ds(pl.multiple_of(blk * MOE_RB, MOE_RB), MOE_RB)]
            return pltpu.make_async_copy(zero_ref, dst, zsem)

        n_used = fill_ref[2 * N_EXPERTS]
        lax.fori_loop(n_used, n_blocks, lambda blk, c: (block_copy(blk).start(), c)[1], 0)
        for cond, copy in tails:
            pl.when(cond)(copy.wait)
        lax.fori_loop(n_used, n_blocks, lambda blk, c: (block_copy(blk).wait(), c)[1], 0)

    route = route_ref[...]
    lane = lax.broadcasted_iota(jnp.int32, (TM, ROUTE_W), 1)
    pick1 = lane == route[:, 0:1].astype(jnp.int32)
    pick2 = lane == route[:, 1:2].astype(jnp.int32)
    chosen = jnp.where(pick1 | pick2, 1.0, 0.0).astype(BF16)
    rank = _dot(tri_ref[...], chosen)
    lane1 = lax.broadcasted_iota(jnp.int32, (1, ROUTE_W), 1)
    loff = jnp.zeros((1, ROUTE_W), F32)
    for e in range(N_EXPERTS):
        loff = jnp.where(lane1 == e, meta_ref[i * META_W + e].astype(F32), loff)
    place = rank + loff
    slot1 = jnp.sum(jnp.where(pick1, place, 0.0), axis=-1, keepdims=True)
    slot2 = jnp.sum(jnp.where(pick2, place, 0.0), axis=-1, keepdims=True)
    slot_ref[...] = jnp.where(lane == 0, slot1, jnp.where(lane == 1, slot2, 0.0))

    row1 = jnp.transpose(jnp.broadcast_to(slot1, (TM, 128)))[0:1, :]
    row2 = jnp.transpose(jnp.broadcast_to(slot2, (TM, 128)))[0:1, :]
    lrow = lax.broadcasted_iota(jnp.int32, (LROWS, TM), 0).astype(F32)
    p1 = jnp.where(lrow == row1, 1.0, 0.0).astype(BF16)
    p2 = jnp.where(lrow == row2, 1.0, 0.0).astype(BF16)
    local_ref[:, :D] = _dot(p1 + p2, a_ref[...]).astype(BF16)
    gates = []
    for col in (2, 3):
        pieces = _split3(route[:, col:col + 1])
        g = jnp.zeros((TM, ROUTE_W), F32)
        for k, piece in enumerate(pieces):
            g = jnp.where(lane == k, piece, g)
        gates.append(g.astype(BF16))
    local_ref[:, D:] = (_dot(p1, gates[0]) + _dot(p2, gates[1])).astype(BF16)
    _start_segment_copies(meta_ref, i, local_ref, xs_hbm, sem, True)
    _wait_segment_copies(meta_ref, i, local_ref, xs_hbm, sem, True)


def _dispatch(n_tiles, n_blocks, a, route, meta, fill):
    tri = (jnp.arange(TM)[:, None] > jnp.arange(TM)[None, :]).astype(BF16)
    return pl.pallas_call(
        functools.partial(_dispatch_kernel, n_blocks),
        grid_spec=pltpu.PrefetchScalarGridSpec(
            num_scalar_prefetch=2,
            grid=(n_tiles,),
            in_specs=[pl.BlockSpec((TM, D), lambda i, *_: (i, 0)),
                      pl.BlockSpec((TM, ROUTE_W), lambda i, *_: (i, 0)),
                      _resident((TM, TM))],
            out_specs=[pl.BlockSpec(memory_space=pl.ANY), pl.BlockSpec((TM, ROUTE_W), lambda i, *_: (i, 0))],
            scratch_shapes=[pltpu.VMEM((LROWS, XS_W), BF16), pltpu.VMEM((MOE_RB, XS_W), BF16),
                            pltpu.SemaphoreType.DMA(()), pltpu.SemaphoreType.DMA(())],
        ),
        out_shape=[jax.ShapeDtypeStruct((n_blocks * MOE_RB, XS_W), BF16),
                   jax.ShapeDtypeStruct((n_tiles * TM, ROUTE_W), F32)],
        compiler_params=_params(1),
        name="moe_dispatch",
    )(meta, fill, a, route, tri)


def _expert_kernel(be_ref, fill_ref, xs_ref, w1_ref, w3_ref, w2_ref, ys_ref, t_ref):
    del be_ref
    used = pl.program_id(0) < fill_ref[2 * N_EXPERTS]

    @pl.when(used)
    def _ffn():
        y = _swiglu(xs_ref[:, :D], w1_ref, w3_ref, w2_ref, t_ref)
        gate = jnp.sum(xs_ref[:, D:].astype(F32), axis=-1, keepdims=True)
        ys_ref[...] = (gate * y).astype(BF16)

    @pl.when(jnp.logical_not(used))
    def _idle():
        ys_ref[...] = jnp.zeros(ys_ref.shape, BF16)


def _experts(xs, blk_exp, fill, layer, w1, w3, w2):
    def slab(d0, d1):
        return pl.BlockSpec((None, None, d0, d1), lambda i, be, fill: (layer, be[i], 0, 0),
                            pipeline_mode=pl.Buffered(1))

    n_blocks = blk_exp.shape[0]
    last_used = lambda i, fill: jnp.minimum(i, fill[2 * N_EXPERTS] - 1)
    return pl.pallas_call(
        _expert_kernel,
        grid_spec=pltpu.PrefetchScalarGridSpec(
            num_scalar_prefetch=2,
            grid=(n_blocks,),
            in_specs=[pl.BlockSpec((MOE_RB, XS_W), lambda i, be, fill: (last_used(i, fill), 0)),
                      slab(D, FFN_DIM), slab(D, FFN_DIM), slab(FFN_DIM, D)],
            out_specs=pl.BlockSpec((MOE_RB, D), lambda i, be, fill: (i, 0)),
            scratch_shapes=[pltpu.VMEM((MOE_RB, FFN_DIM), BF16)],
        ),
        out_shape=jax.ShapeDtypeStruct((n_blocks * MOE_RB, D), BF16),
        compiler_params=_params(1),
        name="moe_experts",
    )(blk_exp, fill, xs, w1, w3, w2)


def _combine_kernel(meta_ref, h_ref, mod_ref, slot_ref, ys_hbm, o_ref, local_ref, sem):
    i = pl.program_id(0)

    buf = i % 2

    def fetch(tile, b):
        return (meta_ref, tile, local_ref.at[b], ys_hbm, sem.at[b], False)

    @pl.when(i == 0)
    def _init():
        local_ref[...] = jnp.zeros(local_ref.shape, BF16)
        _start_segment_copies(*fetch(0, 0))

    @pl.when(i + 1 < pl.num_programs(0))
    def _prefetch():
        _start_segment_copies(*fetch(i + 1, 1 - buf))

    _wait_segment_copies(*fetch(i, buf))
    slots = slot_ref[...]
    lcol = lax.broadcasted_iota(jnp.int32, (TM, LROWS), 1).astype(F32)
    gather = jnp.where((lcol == slots[:, 0:1]) | (lcol == slots[:, 1:2]), 1.0, 0.0).astype(BF16)
    o_ref[...] = h_ref[...] + mod_ref[0][5:6] * _dot(gather, local_ref[buf])


def _combine(rows, n_tiles, h, modg, slots, ys, meta):
    tps, batch = rows.tiles_per_seq, rows.batch
    return pl.pallas_call(
        _combine_kernel,
        grid_spec=pltpu.PrefetchScalarGridSpec(
            num_scalar_prefetch=1,
            grid=(n_tiles,),
            in_specs=[pl.BlockSpec((TM, D), lambda i, m: (i, 0)),
                      pl.BlockSpec((1, 8, D), lambda i, m: (jnp.minimum(i // tps, batch), 0, 0)),
                      pl.BlockSpec((TM, ROUTE_W), lambda i, m: (i, 0)),
                      pl.BlockSpec(memory_space=pl.ANY)],
            out_specs=pl.BlockSpec((TM, D), lambda i, m: (i, 0)),
            scratch_shapes=[pltpu.VMEM((2, LROWS, D), BF16), pltpu.SemaphoreType.DMA((2,))],
        ),
        out_shape=jax.ShapeDtypeStruct((n_tiles * TM, D), F32),
        compiler_params=_params(1),
        name="moe_combine",
    )(meta, h, modg, slots, ys)


def _moe(rows, n_tiles, h, modg, w_r, b_r, layer, w1, w3, w2):
    a, route, cnt = _router(rows, n_tiles, h, modg, w_r, b_r)
    cnt = cnt[:, 0, :N_EXPERTS].astype(jnp.int32)
    seg = (cnt + SEG_ALIGN - 1) // SEG_ALIGN * SEG_ALIGN
    loff = jnp.cumsum(seg, axis=1) - seg
    total = jnp.sum(seg, axis=0)
    region = (total + MOE_RB - 1) // MOE_RB * MOE_RB
    region_end = jnp.cumsum(region)
    region_start = region_end - region
    goff = region_start[None, :] + jnp.cumsum(seg, axis=0) - seg
    meta = jnp.concatenate([loff, goff, seg], axis=1).reshape(-1)
    n_blocks = pl.cdiv(2 * n_tiles * TM + n_tiles * N_EXPERTS * (SEG_ALIGN - 1), MOE_RB) + N_EXPERTS
    n_used = region_end[-1] // MOE_RB
    fill = jnp.concatenate([region_start + total, region - total, n_used[None]]).astype(jnp.int32)
    blk = jnp.minimum(jnp.arange(n_blocks, dtype=jnp.int32), n_used - 1)
    blk_exp = jnp.minimum(jnp.searchsorted(region_end, blk * MOE_RB, side="right"), N_EXPERTS - 1).astype(jnp.int32)
    xs, slots = _dispatch(n_tiles, n_blocks, a, route, meta, fill)
    ys = _experts(xs, blk_exp, fill, layer, w1, w3, w2)
    return _combine(rows, n_tiles, h, modg, slots, ys, meta)


def kernel(x, c, ctx, c_ctx, ada_w, ada_b, norm_mix_g, norm_ffn_g, sc_in_w, sc_conv_w, sc_out_w, da_qkv_w, da_out_w,
           da_q_norm_g, da_k_norm_g, da_lambda, da_sub_norm_g, cm_in_w, cm_in_b, cm_v_norm_g, cm_ws, cm_bs, cm_out_w,
           sw_qkv_w, sw_out_w, sw_q_norm_g, sw_k_norm_g, sw_sink, ffn_w1, ffn_w3, ffn_w2, moe_router_w, moe_router_b,
           moe_w1, moe_w3, moe_w2):
    batch, seq, _ = x.shape
    ctx_len = ctx.shape[1]
    depth = ada_w.shape[0]
    assert depth == 4 and batch + 1 <= 16
    rows = _Rows(batch, seq, ctx_len)

    cvec = jnp.concatenate([c, c_ctx[None, :], jnp.zeros((16 - batch - 1, D), F32)], axis=0)
    mod = _ada_all(cvec, ada_w, ada_b)[:, :batch + 1].reshape(depth, batch + 1, 6, D)
    gains = jnp.stack([norm_mix_g, norm_ffn_g], axis=1)[:, None]
    modg = jnp.concatenate([mod, jnp.broadcast_to(gains, (depth, batch + 1, 2, D))], axis=2)

    bf = lambda w: w.astype(BF16)
    x2, ctx2 = x.reshape(-1, D), ctx.reshape(-1, D)

    bg, y = _conv_in(rows, x2, ctx2, modg[0], bf(sc_in_w[0]))
    conv_w = jnp.pad(sc_conv_w[0], ((0, 5), (0, 0)))
    h = _conv_out(rows, x2, ctx2, modg[0], bg, y, conv_w, bf(sc_out_w[0]))
    h = _ffn(rows, rows.all_tiles, h, modg[0], 0, ffn_w1, ffn_w3, ffn_w2)

    nq_chunks = D // CN
    qkv = _qkv(rows, h, modg[1], bf(da_qkv_w[0]), da_q_norm_g[0], da_k_norm_g[0], nq_chunks, nq_chunks)
    lam_init = 0.8 - 0.6 * math.exp(-0.3 * 1)
    o_lat, o_ctx = _diff_attn(rows, qkv, da_lambda[0], da_sub_norm_g[0], lam_init)
    h = _out_proj(rows, rows.all_tiles, h, modg[1], o_lat, o_ctx, bf(da_out_w[0]))
    h = _moe(rows, rows.all_tiles, h, modg[1], moe_router_w[0], moe_router_b[0], 0, moe_w1, moe_w3, moe_w2)

    h = _gmlp(rows, h, modg[2], bf(cm_in_w[0]), cm_in_b[0].reshape(1, -1), cm_v_norm_g[0].reshape(1, -1),
              bf(cm_ws[0]), cm_bs[0].T, bf(cm_out_w[0]))
    h = _ffn(rows, rows.all_tiles, h, modg[2], 1, ffn_w1, ffn_w3, ffn_w2)

    kv_chunks = SWA_KV_HEADS * HEAD_DIM // CN
    qkv = _qkv(rows, h, modg[3], bf(sw_qkv_w[0]), sw_q_norm_g[0], sw_k_norm_g[0], nq_chunks, kv_chunks)
    o = _swa(rows, qkv, sw_sink[0])
    h = _out_proj(rows, rows.lat_tiles, h, modg[3], o, o, bf(sw_out_w[0]))
    h = _moe(rows, rows.lat_tiles, h, modg[3], moe_router_w[1], moe_router_b[1], 1, moe_w1, moe_w3, moe_w2)
    return h.reshape(batch, seq, D)
```

```python
import functools
import math

import jax
import jax.numpy as jnp
from jax import lax
from jax.experimental import pallas as pl
from jax.experimental.pallas import tpu as pltpu

D = 1024
HEAD_DIM = 64
GRID_W = 64
ROPE_HALF = HEAD_DIM // 2
ROPE_BASE = 10000.0
DIFF_HEADS = D // (2 * HEAD_DIM)
SWA_Q_HEADS = D // HEAD_DIM
SWA_KV_HEADS = 4
SWA_GROUP = SWA_Q_HEADS // SWA_KV_HEADS
SWA_WINDOW = 128
CHUNK = 128
CM_WIDTH = 2 * D
CM_GROUPS = 8
CM_GW = CM_WIDTH // CM_GROUPS
FFN_DIM = 2816
N_EXPERTS = 8
EPS = 1e-6

F32 = jnp.float32
BF16 = jnp.bfloat16
HIGHEST = lax.Precision.HIGHEST
LOG2E = math.log2(math.e)

TM = 512
CN = 256
MOE_RB = 512
ATT_TQ = 512
ATT_TK = 1024
SWA_TQ = 256
ROUTE_W = 128
MAX_UNSHIFTED_LOGIT = 60.0
VMEM_LIMIT = 56 << 20


def _params(n_grid):
    return pltpu.CompilerParams(dimension_semantics=("arbitrary",) * n_grid, vmem_limit_bytes=VMEM_LIMIT)


def _resident(shape):
    zeros = (0,) * len(shape)
    return pl.BlockSpec(shape, lambda *_: zeros, pipeline_mode=pl.Buffered(1))


def _sigmoid(x):
    return 1.0 / (1.0 + jnp.exp(-x))


def _gelu_tanh(x):
    return 0.5 * x * (1.0 + jnp.tanh(math.sqrt(2.0 / math.pi) * (x + 0.044715 * (x * x * x))))


def _modnorm(x, g, shift, scale):
    y = x * lax.rsqrt(jnp.mean(x * x, axis=-1, keepdims=True) + EPS)
    return (y * g) * (1.0 + scale) + shift


def _dot(a, b):
    return jnp.dot(a, b, preferred_element_type=F32)


def _mix_mod(m):
    return m[6:7], m[0:1], m[1:2]


def _ffn_mod(m):
    return m[7:8], m[3:4], m[4:5]


def _ada_kernel(c_ref, w_ref, b_ref, o_ref):
    c = c_ref[...]
    s = c * _sigmoid(c)
    o_ref[0] = jnp.dot(s, w_ref[0], precision=HIGHEST, preferred_element_type=F32) + b_ref[0]


def _ada_all(cvec, ada_w, ada_b):
    depth, _, n = ada_w.shape
    tn = 1536
    rows = cvec.shape[0]
    return pl.pallas_call(
        _ada_kernel,
        grid=(depth, n // tn),
        in_specs=[pl.BlockSpec((rows, D), lambda l, j: (0, 0)),
                  pl.BlockSpec((1, D, tn), lambda l, j: (l, 0, j)),
                  pl.BlockSpec((1, 1, tn), lambda l, j: (l, 0, j))],
        out_specs=pl.BlockSpec((1, rows, tn), lambda l, j: (l, 0, j)),
        out_shape=jax.ShapeDtypeStruct((depth, rows, n), F32),
        compiler_params=_params(2),
        name="adaln",
    )(cvec, ada_w, ada_b.reshape(depth, 1, n))


class _Rows:
    def __init__(self, batch, seq, ctx_len):
        self.batch, self.seq, self.ctx_len = batch, seq, ctx_len
        self.n_lat = batch * seq
        self.n_ctx = batch * ctx_len
        self.n_all = self.n_lat + self.n_ctx
        assert seq % TM == 0 and self.n_ctx % TM == 0 and TM % ctx_len == 0
        self.lat_tiles = self.n_lat // TM
        self.all_tiles = self.n_all // TM
        self.tiles_per_seq = seq // TM

    def mod_spec(self):
        tps, batch = self.tiles_per_seq, self.batch
        return pl.BlockSpec((1, 8, D), lambda i, *_: (jnp.minimum(i // tps, batch), 0, 0))

    def row_spec(self, width):
        return pl.BlockSpec((TM, width), lambda i, *_: (i, 0))

    def split_specs(self, width):
        lat_tiles = self.lat_tiles
        return [pl.BlockSpec((TM, width), lambda i, *_: (jnp.minimum(i, lat_tiles - 1), 0)),
                pl.BlockSpec((TM, width), lambda i, *_: (jnp.maximum(i - lat_tiles, 0), 0))]


def _conv_in_kernel(lat_tiles, x_ref, ctx_ref, mod_ref, w_ref, bg_ref, y_ref):
    h = jnp.where(pl.program_id(0) < lat_tiles, x_ref[...], ctx_ref[...])
    a = _modnorm(h, *_mix_mod(mod_ref[0])).astype(BF16)
    for c in range(D // CN):
        lo = c * CN
        bg_ref[:, lo:lo + CN] = _dot(a, w_ref[:, lo:lo + CN]).astype(BF16)
        cg = _dot(a, w_ref[:, D + lo:D + lo + CN])
        xv = _dot(a, w_ref[:, 2 * D + lo:2 * D + lo + CN])
        y_ref[:, lo:lo + CN] = (cg * xv).astype(BF16)


def _conv_in(rows, x, ctx, modg, w_in):
    out = jax.ShapeDtypeStruct((rows.n_all, D), BF16)
    return pl.pallas_call(
        functools.partial(_conv_in_kernel, rows.lat_tiles),
        grid=(rows.all_tiles,),
        in_specs=rows.split_specs(D) + [rows.mod_spec(), _resident((D, 3 * D))],
        out_specs=[rows.row_spec(D), rows.row_spec(D)],
        out_shape=[out, out],
        compiler_params=_params(1),
        name="conv_in",
    )(x, ctx, modg, w_in)


HALO = 16


def _conv_out_kernel(n_lat, seq, ctx_len, x_ref, ctx_ref, mod_ref, bg_ref, y_ref, yp_ref, yn_ref, cw_ref, w_ref,
                     o_ref):
    i = pl.program_id(0)
    h = jnp.where(i * TM < n_lat, x_ref[...], ctx_ref[...])
    m = mod_ref[0]
    row = lax.broadcasted_iota(jnp.int32, (TM, 1), 0)
    grow = row + i * TM
    seq_len = jnp.where(grow < n_lat, seq, ctx_len)
    pos = grow & (seq_len - 1)
    out = None
    for c in range(D // CN):
        cols = slice(c * CN, (c + 1) * CN)
        y = y_ref[:, cols].astype(F32)
        prev_row = yp_ref[HALO - 1:HALO, cols].astype(F32)
        next_row = yn_ref[0:1, cols].astype(F32)
        y_m1 = jnp.where(row == 0, prev_row, pltpu.roll(y, 1, 0))
        y_m1 = jnp.where(pos == 0, 0.0, y_m1)
        y_p1 = jnp.where(row == TM - 1, next_row, pltpu.roll(y, TM - 1, 0))
        y_p1 = jnp.where(pos == seq_len - 1, 0.0, y_p1)
        conv = cw_ref[0:1, cols] * y_m1 + cw_ref[1:2, cols] * y + cw_ref[2:3, cols] * y_p1
        z = (bg_ref[:, cols].astype(F32) * conv).astype(BF16)
        part = _dot(z, w_ref[cols, :])
        out = part if out is None else out + part
    o_ref[...] = h + m[2:3] * out


def _conv_out(rows, x, ctx, modg, bg, y, conv_w, w_out):
    hb = TM // HALO
    last = rows.n_all // HALO - 1
    return pl.pallas_call(
        functools.partial(_conv_out_kernel, rows.n_lat, rows.seq, rows.ctx_len),
        grid=(rows.all_tiles,),
        in_specs=rows.split_specs(D) + [rows.mod_spec(), rows.row_spec(D), rows.row_spec(D),
                  pl.BlockSpec((HALO, D), lambda i: (jnp.maximum(i * hb - 1, 0), 0)),
                  pl.BlockSpec((HALO, D), lambda i: (jnp.minimum((i + 1) * hb, last), 0)),
                  _resident((8, D)), _resident((D, D))],
        out_specs=rows.row_spec(D),
        out_shape=jax.ShapeDtypeStruct((rows.n_all, D), F32),
        compiler_params=_params(1),
        name="conv_out",
    )(x, ctx, modg, bg, y, y, y, conv_w, w_out)


W2_KC = 768


def _swiglu(a, w1_ref, w3_ref, w2_ref, t_ref):
    for c in range(FFN_DIM // CN):
        lo = c * CN
        h1 = _dot(a, w1_ref[:, lo:lo + CN].astype(BF16))
        h3 = _dot(a, w3_ref[:, lo:lo + CN].astype(BF16))
        t_ref[:, lo:lo + CN] = (h1 * _sigmoid(h1) * h3).astype(BF16)
    out = None
    for lo in range(0, FFN_DIM, W2_KC):
        hi = min(lo + W2_KC, FFN_DIM)
        part = _dot(t_ref[:, lo:hi], w2_ref[lo:hi, :].astype(BF16))
        out = part if out is None else out + part
    return out


def _ffn_kernel(h_ref, mod_ref, w1_ref, w3_ref, w2_ref, o_ref, t_ref):
    m = mod_ref[0]
    x = h_ref[...]
    a = _modnorm(x, *_ffn_mod(m)).astype(BF16)
    o_ref[...] = x + m[5:6] * _swiglu(a, w1_ref, w3_ref, w2_ref, t_ref)


def _ffn(rows, n_tiles, h, modg, layer, w1, w3, w2):
    def slab(d0, d1):
        return pl.BlockSpec((None, d0, d1), lambda i: (layer, 0, 0), pipeline_mode=pl.Buffered(1))

    return pl.pallas_call(
        _ffn_kernel,
        grid=(n_tiles,),
        in_specs=[rows.row_spec(D), rows.mod_spec(), slab(D, FFN_DIM), slab(D, FFN_DIM), slab(FFN_DIM, D)],
        out_specs=rows.row_spec(D),
        out_shape=jax.ShapeDtypeStruct((n_tiles * TM, D), F32),
        scratch_shapes=[pltpu.VMEM((TM, FFN_DIM), BF16)],
        compiler_params=_params(1),
        name="ffn_dense",
    )(h, modg, w1, w3, w2)


def _norm_rope(x, gmat_ref, cos, sin):
    ms = _dot((x * x).astype(BF16), gmat_ref[...]) * (1.0 / HEAD_DIM)
    xn = x * lax.rsqrt(ms + EPS)
    lane = lax.broadcasted_iota(jnp.int32, (1, CN), 1)
    first_half = (lane & (ROPE_HALF - 1)) < (ROPE_HALF // 2)
    partner = jnp.where(first_half, pltpu.roll(xn, CN - ROPE_HALF // 2, 1), pltpu.roll(xn, ROPE_HALF // 2, 1))
    return xn * cos + partner * sin


def _qkv_kernel(n_qk_chunks, n_q_chunks, h_ref, mod_ref, w_ref, gmat_ref, qcos_ref, qsin_ref, kcos_ref, ksin_ref,
                o_ref):
    a = _modnorm(h_ref[...], *_mix_mod(mod_ref[0])).astype(BF16)
    n_chunks = w_ref.shape[1] // CN

    def project(c):
        return _dot(a, w_ref[:, c * CN:(c + 1) * CN])

    nxt = project(0)
    for c in range(n_chunks):
        acc = nxt
        if c + 1 < n_chunks:
            nxt = project(c + 1)
        if c < n_q_chunks:
            acc = _norm_rope(acc, gmat_ref, qcos_ref[...], qsin_ref[...])
        elif c < n_qk_chunks:
            acc = _norm_rope(acc, gmat_ref, kcos_ref[...], ksin_ref[...])
        o_ref[:, c * CN:(c + 1) * CN] = acc.astype(BF16)


def _rope_tables(rows, gain, scale):
    seq = rows.seq
    pos = jnp.arange(seq)
    n_freq = ROPE_HALF // 2
    inv = ROPE_BASE ** (-jnp.arange(n_freq, dtype=F32) / n_freq)
    ang_r = (pos // GRID_W).astype(F32)[:, None] * inv
    ang_c = (pos % GRID_W).astype(F32)[:, None] * inv
    cos = jnp.concatenate([jnp.cos(ang_r)] * 2 + [jnp.cos(ang_c)] * 2, axis=-1)
    sin = jnp.concatenate([-jnp.sin(ang_r), jnp.sin(ang_r), -jnp.sin(ang_c), jnp.sin(ang_c)], axis=-1)
    cos = jnp.concatenate([cos, jnp.ones((TM, HEAD_DIM), F32)], axis=0)
    sin = jnp.concatenate([sin, jnp.zeros((TM, HEAD_DIM), F32)], axis=0)
    dim = jnp.arange(HEAD_DIM)
    partner = jnp.where((dim % ROPE_HALF) < n_freq, dim + n_freq, dim - n_freq)
    g = gain.astype(F32) * scale
    reps = CN // HEAD_DIM
    return jnp.tile(cos * g[None, :], (1, reps)), jnp.tile(sin * g[partner][None, :], (1, reps))


def _qkv(rows, h, modg, w, q_gain, k_gain, n_q_chunks, n_k_chunks):
    width = w.shape[1]
    head = jnp.arange(CN) // HEAD_DIM
    gmat = (head[:, None] == head[None, :]).astype(BF16)
    tables = _rope_tables(rows, q_gain, LOG2E * HEAD_DIM ** -0.5) + _rope_tables(rows, k_gain, 1.0)
    tps, lat_tiles = rows.tiles_per_seq, rows.lat_tiles
    tab_spec = pl.BlockSpec((TM, CN), lambda i: (jnp.where(i < lat_tiles, i % tps, tps), 0))
    return pl.pallas_call(
        functools.partial(_qkv_kernel, n_q_chunks + n_k_chunks, n_q_chunks),
        grid=(rows.all_tiles,),
        in_specs=[rows.row_spec(D), rows.mod_spec(), _resident((D, width)), _resident((CN, CN))] + [tab_spec] * 4,
        out_specs=rows.row_spec(width),
        out_shape=jax.ShapeDtypeStruct((rows.n_all, width), BF16),
        compiler_params=_params(1),
        name="qkv_proj",
    )(h, modg, w, gmat, *tables)


DH2 = 2 * HEAD_DIM


def _diff_attn_kernel(lam_init, n_lat_chunks, q_ref, kc_ref, vc_ref, kl_ref, vl_ref, lam_ref, subg_ref, o_ref,
                      m_ref, acc_ref, vca_ref, vla_ref, kn_ref):
    def max_norm_sq(x):
        xf = x.astype(F32)
        return jnp.max(jnp.sum(xf * xf, axis=-1, keepdims=True), axis=0, keepdims=True)

    def per_head_setup():
        vca_ref[:, :DH2] = vc_ref[...]
        vca_ref[:, DH2:] = jnp.ones((vc_ref.shape[0], DH2), BF16)
        kn = max_norm_sq(kc_ref[...])
        if n_lat_chunks:
            vla_ref[:, :DH2] = vl_ref[...]
            vla_ref[:, DH2:] = jnp.ones((vl_ref.shape[0], DH2), BF16)
            kn = jnp.maximum(kn, max_norm_sq(kl_ref[...]))
        kn_ref[...] = jnp.broadcast_to(kn, kn_ref.shape)

    if n_lat_chunks:
        pl.when(pl.program_id(2) == 0)(per_head_setup)
    else:
        per_head_setup()

    q = q_ref[...]
    lane = lax.broadcasted_iota(jnp.int32, (1, DH2), 1)
    zero = jnp.zeros((), BF16)
    qs = (jnp.where(lane < HEAD_DIM, q, zero), jnp.where(lane >= HEAD_DIM, q, zero))
    nt = (((1,), (1,)), ((), ()))

    def over_keys(update):
        update(kc_ref[...], vca_ref[...])
        if n_lat_chunks:
            tk = kl_ref.shape[0] // n_lat_chunks

            def body(c, carry):
                start = pl.multiple_of(c * tk, tk)
                update(kl_ref[pl.ds(start, tk), :], vla_ref[pl.ds(start, tk), :])
                return carry
            lax.fori_loop(0, n_lat_chunks, body, 0)

    lp = lam_ref[...]
    lam = (jnp.exp(jnp.sum(lp[0:1] * lp[1:2], axis=-1, keepdims=True))
           - jnp.exp(jnp.sum(lp[2:3] * lp[3:4], axis=-1, keepdims=True)) + lam_init)

    def unshifted():
        n_sub = 2
        sub = q.shape[0] // n_sub

        def logits(r):
            out = []
            for mi in range(2):
                qm = qs[mi][r * sub:(r + 1) * sub]
                s_c = lax.dot_general(qm, kc_ref[...], nt, preferred_element_type=F32)
                s_l = lax.dot_general(qm, kl_ref[...], nt, preferred_element_type=F32) if n_lat_chunks else None
                out.append((s_c, s_l))
            return out

        all_logits = [logits(r) for r in range(n_sub)]
        for r in range(n_sub):
            p_ctx, p_lat, sums = [], [], []
            for s_c, s_l in all_logits[r]:
                e = jnp.exp2(s_c)
                total = jnp.sum(e, axis=-1, keepdims=True)
                p_ctx.append(e)
                if s_l is not None:
                    e = jnp.exp2(s_l)
                    total = total + jnp.sum(e, axis=-1, keepdims=True)
                    p_lat.append(e)
                sums.append(total)
            r0 = 1.0 / sums[0]
            r1 = lam / sums[1]

            def weights(p):
                return (p[0] * r0 - p[1] * r1).astype(BF16)

            o = _dot(weights(p_ctx), vc_ref[...])
            if n_lat_chunks:
                o = o + _dot(weights(p_lat), vl_ref[...])
            acc_ref[0, r * sub:(r + 1) * sub, :DH2] = o

    def update_online(k, va):
        reps = k.shape[0] // DH2
        for mi in range(2):
            s = lax.dot_general(qs[mi], k, nt, preferred_element_type=F32)
            m_old = m_ref[mi]
            m_new = jnp.maximum(m_old, jnp.max(s, axis=-1, keepdims=True))
            alpha = jnp.exp2(m_old - m_new)
            p = jnp.exp2(s - jnp.concatenate([m_new] * reps, axis=-1))
            acc_ref[mi] = jnp.concatenate([alpha, alpha], axis=-1) * acc_ref[mi] + _dot(p.astype(BF16), va)
            m_ref[mi] = m_new

    def online():
        m_ref[...] = jnp.full(m_ref.shape, -jnp.inf, F32)
        acc_ref[...] = jnp.zeros(acc_ref.shape, F32)
        over_keys(update_online)
        acc0 = acc_ref[0]
        acc1 = acc_ref[1]
        acc_ref[0, :, :DH2] = acc0[:, :DH2] / acc0[:, DH2:] - lam * (acc1[:, :DH2] / acc1[:, DH2:])

    bound_sq = max_norm_sq(q) * kn_ref[0:1, 0:1]
    small_logits = bound_sq[0, 0] <= MAX_UNSHIFTED_LOGIT ** 2
    pl.when(small_logits)(unshifted)
    pl.when(jnp.logical_not(small_logits))(online)

    o = acc_ref[0, :, :DH2]
    o = o * lax.rsqrt(jnp.mean(o * o, axis=-1, keepdims=True) + EPS) * subg_ref[...]
    o_ref[...] = (o * (1.0 - lam_init)).astype(BF16)


def _diff_attn(rows, qkv, lam_p, sub_g, lam_init):
    batch, seq, ctx_len = rows.batch, rows.seq, rows.ctx_len
    nq = seq // ATT_TQ
    nh = DIFF_HEADS
    ctx0 = rows.n_lat // ctx_len
    sub_g = sub_g.reshape(1, DH2)
    small = [_resident((4, HEAD_DIM)), _resident((1, DH2))]
    kc_spec = pl.BlockSpec((ctx_len, DH2), lambda b, h, *_: (ctx0 + b, nh + h))
    vc_spec = pl.BlockSpec((ctx_len, DH2), lambda b, h, *_: (ctx0 + b, 2 * nh + h))

    def scratch(tq, n_lat_keys):
        return [pltpu.VMEM((2, tq, DH2), F32), pltpu.VMEM((2, tq, 2 * DH2), F32),
                pltpu.VMEM((ctx_len, 2 * DH2), BF16), pltpu.VMEM((n_lat_keys, 2 * DH2), BF16),
                pltpu.VMEM((8, DH2), F32)]

    o_lat = pl.pallas_call(
        functools.partial(_diff_attn_kernel, lam_init, pl.cdiv(seq, ATT_TK)),
        grid=(batch, nh, nq),
        in_specs=[pl.BlockSpec((ATT_TQ, DH2), lambda b, h, i: (b * nq + i, h)),
                  kc_spec, vc_spec,
                  pl.BlockSpec((seq, DH2), lambda b, h, i: (b, nh + h)),
                  pl.BlockSpec((seq, DH2), lambda b, h, i: (b, 2 * nh + h))] + small,
        out_specs=pl.BlockSpec((ATT_TQ, DH2), lambda b, h, i: (b * nq + i, h)),
        out_shape=jax.ShapeDtypeStruct((rows.n_lat, D), BF16),
        scratch_shapes=scratch(ATT_TQ, seq),
        compiler_params=_params(3),
        name="diff_attn_latent",
    )(qkv, qkv, qkv, qkv, qkv, lam_p, sub_g)

    def ctx_kernel(q_ref, kc_ref, vc_ref, lam_ref, subg_ref, o_ref, *scratch_refs):
        _diff_attn_kernel(lam_init, 0, q_ref, kc_ref, vc_ref, None, None, lam_ref, subg_ref, o_ref, *scratch_refs)

    o_ctx = pl.pallas_call(
        ctx_kernel,
        grid=(batch, nh),
        in_specs=[pl.BlockSpec((ctx_len, DH2), lambda b, h: (ctx0 + b, h)), kc_spec, vc_spec] + small,
        out_specs=pl.BlockSpec((ctx_len, DH2), lambda b, h: (b, h)),
        out_shape=jax.ShapeDtypeStruct((rows.n_ctx, D), BF16),
        scratch_shapes=scratch(ctx_len, 16),
        compiler_params=_params(2),
        name="diff_attn_context",
    )(qkv, qkv, qkv, lam_p, sub_g)
    return o_lat, o_ctx


def _out_proj_kernel(lat_tiles, h_ref, mod_ref, zl_ref, zc_ref, w_ref, o_ref):
    i = pl.program_id(0)

    def project(z_ref):
        o_ref[...] = h_ref[...] + mod_ref[0][2:3] * _dot(z_ref[...], w_ref[...])

    pl.when(i < lat_tiles)(lambda: project(zl_ref))
    pl.when(i >= lat_tiles)(lambda: project(zc_ref))


def _out_proj(rows, n_tiles, h, modg, z_lat, z_ctx, w):
    lat_tiles = rows.lat_tiles
    return pl.pallas_call(
        functools.partial(_out_proj_kernel, lat_tiles),
        grid=(n_tiles,),
        in_specs=[rows.row_spec(D), rows.mod_spec(),
                  pl.BlockSpec((TM, D), lambda i: (jnp.minimum(i, lat_tiles - 1), 0)),
                  pl.BlockSpec((TM, D), lambda i: (jnp.maximum(i - lat_tiles, 0), 0)),
                  _resident((D, D))],
        out_specs=rows.row_spec(D),
        out_shape=jax.ShapeDtypeStruct((n_tiles * TM, D), F32),
        compiler_params=_params(1),
        name="out_proj",
    )(h, modg, z_lat, z_ctx, w)


def _gmlp_kernel(h_ref, mod_ref, win_ref, bin_ref, vg_ref, ws_ref, bs_ref, wout_ref, o_ref, u_ref, v_ref, t_ref):
    m = mod_ref[0]
    x = h_ref[...]
    a = _modnorm(x, *_mix_mod(m)).astype(BF16)
    n_half = CM_WIDTH // CN

    def in_proj(c):
        lo = c * CN
        return _gelu_tanh(_dot(a, win_ref[:, lo:lo + CN]) + bin_ref[:, lo:lo + CN])

    ssq = jnp.zeros((TM, 1), F32)
    for c in range(n_half):
        z = in_proj(n_half + c)
        v_ref[:, c * CN:(c + 1) * CN] = z
        ssq = ssq + jnp.sum(z * z, axis=-1, keepdims=True)
    for c in range(n_half):
        u_ref[:, c * CN:(c + 1) * CN] = in_proj(c)
    inv = lax.rsqrt(ssq * (1.0 / CM_WIDTH) + EPS)

    def mix(g):
        lo = g * CM_GW
        vn = (v_ref[:, lo:lo + CM_GW] * inv * vg_ref[:, lo:lo + CM_GW]).astype(BF16)
        return [_dot(ws_ref[g], vn[r * CHUNK:(r + 1) * CHUNK]) + bs_ref[:, g:g + 1] for r in range(TM // CHUNK)]

    out = None
    nxt = mix(0)
    for g in range(CM_GROUPS):
        lo = g * CM_GW
        sv = nxt
        if g + 1 < CM_GROUPS:
            nxt = mix(g + 1)
        for r in range(TM // CHUNK):
            r0 = r * CHUNK
            t_ref[r0:r0 + CHUNK, lo:lo + CM_GW] = (u_ref[r0:r0 + CHUNK, lo:lo + CM_GW] * sv[r]).astype(BF16)
        part = _dot(t_ref[:, lo:lo + CM_GW], wout_ref[lo:lo + CM_GW, :])
        out = part if out is None else out + part
    o_ref[...] = x + m[2:3] * out


def _gmlp(rows, h, modg, w_in, b_in, v_g, w_s, b_s, w_out):
    return pl.pallas_call(
        _gmlp_kernel,
        grid=(rows.all_tiles,),
        in_specs=[rows.row_spec(D), rows.mod_spec(), _resident((D, 2 * CM_WIDTH)), _resident((1, 2 * CM_WIDTH)),
                  _resident((1, CM_WIDTH)), _resident((CM_GROUPS, CHUNK, CHUNK)), _resident((CHUNK, CM_GROUPS)),
                  _resident((CM_WIDTH, D))],
        out_specs=rows.row_spec(D),
        out_shape=jax.ShapeDtypeStruct((rows.n_all, D), F32),
        scratch_shapes=[pltpu.VMEM((TM, CM_WIDTH), F32), pltpu.VMEM((TM, CM_WIDTH), F32),
                        pltpu.VMEM((TM, CM_WIDTH), BF16)],
        compiler_params=_params(1),
        name="gmlp",
    )(h, modg, w_in, b_in, v_g, w_s, b_s, w_out)


SWA_BAND = SWA_TQ + 2 * SWA_WINDOW


SWA_PAIR = 2 * HEAD_DIM
N_KV_VARIANTS = 2 * SWA_KV_HEADS


def _swa_kernel(seq, sink_ref, q_ref, kc_ref, vc_ref, kl_ref, vl_ref, o_ref, kcv_ref, vcv_ref, klv_ref, vlv_ref,
                kn_ref):
    qi = pl.program_id(1)
    lane = lax.broadcasted_iota(jnp.int32, (1, SWA_PAIR), 1)
    lo_half = lane < HEAD_DIM

    def max_norm_sq(x):
        xf = x.astype(F32)
        return jnp.max(jnp.sum(xf * xf, axis=-1, keepdims=True), axis=0, keepdims=True)

    @pl.when(qi == 0)
    def _per_batch_setup():
        kn = jnp.zeros((1, 1), F32)
        for src_ref, dst_ref in ((kc_ref, kcv_ref), (vc_ref, vcv_ref), (kl_ref, klv_ref), (vl_ref, vlv_ref)):
            for half in range(SWA_KV_HEADS // 2):
                x = src_ref[:, half * SWA_PAIR:(half + 1) * SWA_PAIR].astype(F32)
                xr = pltpu.roll(x, HEAD_DIM, 1)
                j0, j1 = 2 * half, 2 * half + 1
                dst_ref[2 * j0] = jnp.where(lo_half, x, 0.0).astype(BF16)
                dst_ref[2 * j0 + 1] = jnp.where(lo_half, 0.0, xr).astype(BF16)
                dst_ref[2 * j1] = jnp.where(lo_half, xr, 0.0).astype(BF16)
                dst_ref[2 * j1 + 1] = jnp.where(lo_half, 0.0, x).astype(BF16)
                if src_ref is kc_ref or src_ref is kl_ref:
                    kn = jnp.maximum(kn, max_norm_sq(x))
        kn_ref[...] = jnp.broadcast_to(kn, kn_ref.shape)

    q0 = qi * SWA_TQ
    start = jnp.clip(q0 - SWA_WINDOW, 0, seq - SWA_BAND)
    start = pl.multiple_of(start, SWA_WINDOW)
    qpos = q0 + lax.broadcasted_iota(jnp.int32, (SWA_TQ, SWA_BAND), 0)
    kpos = start + lax.broadcasted_iota(jnp.int32, (SWA_TQ, SWA_BAND), 1)
    in_band = jnp.abs(qpos - kpos) <= SWA_WINDOW
    nt = (((1,), (1,)), ((), ()))

    def unshifted():
        in_band2 = jnp.concatenate([in_band, in_band], axis=0)
        top = lax.broadcasted_iota(jnp.int32, (2 * SWA_TQ, 1), 0) < SWA_TQ
        def logits(v):
            lo = (v // 2) * SWA_GROUP * HEAD_DIM
            q2 = jnp.concatenate([q_ref[:, lo:lo + SWA_PAIR], q_ref[:, lo + SWA_PAIR:lo + 2 * SWA_PAIR]], axis=0)
            return (lax.dot_general(q2, kcv_ref[v], nt, preferred_element_type=F32),
                    lax.dot_general(q2, klv_ref[v, pl.ds(start, SWA_BAND), :], nt, preferred_element_type=F32))

        nxt = logits(0)
        for j in range(SWA_KV_HEADS):
            lo = j * SWA_GROUP * HEAD_DIM
            out = jnp.zeros((2 * SWA_TQ, SWA_PAIR), F32)
            for var in range(2):
                v = 2 * j + var
                s_c, s_b = nxt
                if v + 1 < N_KV_VARIANTS:
                    nxt = logits(v + 1)
                p_c = jnp.exp2(s_c)
                p_b = jnp.where(in_band2, jnp.exp2(s_b), 0.0)
                hq = j * SWA_GROUP + var
                sink = jnp.where(top, sink_ref[hq] * LOG2E, sink_ref[hq + 2] * LOG2E)
                denom = (jnp.sum(p_c, axis=-1, keepdims=True) + jnp.sum(p_b, axis=-1, keepdims=True)
                         + jnp.exp2(sink))
                o = (_dot(p_c.astype(BF16), vcv_ref[v])
                     + _dot(p_b.astype(BF16), vlv_ref[v, pl.ds(start, SWA_BAND), :]))
                out = out + o / denom
            o_ref[:, lo:lo + SWA_PAIR] = out[:SWA_TQ].astype(BF16)
            o_ref[:, lo + SWA_PAIR:lo + 2 * SWA_PAIR] = out[SWA_TQ:].astype(BF16)

    def shifted():
        kb = kl_ref[pl.ds(start, SWA_BAND), :]
        vb = vl_ref[pl.ds(start, SWA_BAND), :]
        kc = kc_ref[...]
        vc = vc_ref[...]
        for j in range(SWA_KV_HEADS):
            kj = slice(j * HEAD_DIM, (j + 1) * HEAD_DIM)
            kbj, vbj, kcj, vcj = kb[:, kj], vb[:, kj], kc[:, kj], vc[:, kj]
            outs = []
            for g in range(SWA_GROUP):
                hq = j * SWA_GROUP + g
                qh = q_ref[:, hq * HEAD_DIM:(hq + 1) * HEAD_DIM]
                s_c = lax.dot_general(qh, kcj, nt, preferred_element_type=F32)
                s_b = lax.dot_general(qh, kbj, nt, preferred_element_type=F32)
                s_b = jnp.where(in_band, s_b, -jnp.inf)
                sink = sink_ref[hq] * LOG2E
                mx = jnp.maximum(jnp.maximum(jnp.max(s_c, axis=-1, keepdims=True),
                                             jnp.max(s_b, axis=-1, keepdims=True)), sink)
                p_c = jnp.exp2(s_c - mx)
                p_b = jnp.exp2(s_b - mx)
                denom = (jnp.sum(p_c, axis=-1, keepdims=True) + jnp.sum(p_b, axis=-1, keepdims=True)
                         + jnp.exp2(sink - mx))
                o = _dot(p_c.astype(BF16), vcj) + _dot(p_b.astype(BF16), vbj)
                outs.append(o / denom)
            lo = j * SWA_GROUP * HEAD_DIM
            o_ref[:, lo:lo + SWA_GROUP * HEAD_DIM] = jnp.concatenate(outs, axis=-1).astype(BF16)

    qn = jnp.zeros((1, 1), F32)
    for pair in range(SWA_Q_HEADS // 2):
        qn = jnp.maximum(qn, max_norm_sq(q_ref[:, pair * SWA_PAIR:(pair + 1) * SWA_PAIR]))
    max_sink = sink_ref[0]
    for hq in range(1, SWA_Q_HEADS):
        max_sink = jnp.maximum(max_sink, sink_ref[hq])
    small_logits = jnp.logical_and((qn * kn_ref[0:1, 0:1])[0, 0] <= MAX_UNSHIFTED_LOGIT ** 2,
                                   max_sink * LOG2E <= MAX_UNSHIFTED_LOGIT)
    pl.when(small_logits)(unshifted)
    pl.when(jnp.logical_not(small_logits))(shifted)


def _swa(rows, qkv, sink):
    batch, seq, ctx_len = rows.batch, rows.seq, rows.ctx_len
    nq = seq // SWA_TQ
    kvw = SWA_KV_HEADS * HEAD_DIM
    k_col = D // kvw
    ctx0 = rows.n_lat // ctx_len
    return pl.pallas_call(
        functools.partial(_swa_kernel, seq),
        grid=(batch, nq),
        in_specs=[pl.BlockSpec(memory_space=pltpu.SMEM),
                  pl.BlockSpec((SWA_TQ, D), lambda b, i: (b * nq + i, 0)),
                  pl.BlockSpec((ctx_len, kvw), lambda b, i: (ctx0 + b, k_col)),
                  pl.BlockSpec((ctx_len, kvw), lambda b, i: (ctx0 + b, k_col + 1)),
                  pl.BlockSpec((seq, kvw), lambda b, i: (b, k_col)),
                  pl.BlockSpec((seq, kvw), lambda b, i: (b, k_col + 1))],
        out_specs=pl.BlockSpec((SWA_TQ, D), lambda b, i: (b * nq + i, 0)),
        out_shape=jax.ShapeDtypeStruct((rows.n_lat, D), BF16),
        scratch_shapes=[pltpu.VMEM((N_KV_VARIANTS, ctx_len, SWA_PAIR), BF16),
                        pltpu.VMEM((N_KV_VARIANTS, ctx_len, SWA_PAIR), BF16),
                        pltpu.VMEM((N_KV_VARIANTS, seq, SWA_PAIR), BF16),
                        pltpu.VMEM((N_KV_VARIANTS, seq, SWA_PAIR), BF16),
                        pltpu.VMEM((8, SWA_PAIR), F32)],
        compiler_params=_params(2),
        name="swa_attn",
    )(sink, qkv, qkv, qkv, qkv, qkv)


def _router_kernel(h_ref, mod_ref, wr_ref, br_ref, a_ref, route_ref, cnt_ref):
    a = _modnorm(h_ref[...], *_ffn_mod(mod_ref[0]))
    a_hi = a.astype(BF16)
    a_ref[...] = a_hi
    a_lo = (a - a_hi.astype(F32)).astype(BF16)
    hi_part = _dot(a_hi, wr_ref[...])
    logits = (hi_part[:, :ROUTE_W] + hi_part[:, ROUTE_W:]) + _dot(a_lo, wr_ref[:, :ROUTE_W]) + br_ref[...]
    lane = lax.broadcasted_iota(jnp.int32, (TM, ROUTE_W), 1)
    m1 = jnp.max(logits, axis=-1, keepdims=True)
    i1 = jnp.min(jnp.where(logits == m1, lane, ROUTE_W), axis=-1, keepdims=True)
    rest = jnp.where(lane == i1, -jnp.inf, logits)
    m2 = jnp.max(rest, axis=-1, keepdims=True)
    i2 = jnp.min(jnp.where(rest == m2, lane, ROUTE_W), axis=-1, keepdims=True)
    e2 = jnp.exp(m2 - m1)
    gate1 = 1.0 / (1.0 + e2)
    gate2 = e2 / (1.0 + e2)
    rec = jnp.where(lane == 0, i1.astype(F32), 0.0)
    rec = jnp.where(lane == 1, i2.astype(F32), rec)
    rec = jnp.where(lane == 2, gate1, rec)
    route_ref[...] = jnp.where(lane == 3, gate2, rec)
    chosen = jnp.where((lane == i1) | (lane == i2), 1.0, 0.0)
    cnt_ref[0] = jnp.broadcast_to(jnp.sum(chosen, axis=0, keepdims=True), (8, ROUTE_W))


def _router(rows, n_tiles, h, modg, w_r, b_r):
    pad = ROUTE_W - N_EXPERTS
    w_pad = jnp.pad(w_r, ((0, 0), (0, pad)))
    w_hi = w_pad.astype(BF16)
    w_lo = (w_pad - w_hi.astype(F32)).astype(BF16)
    w_split = jnp.concatenate([w_hi, w_lo], axis=1)
    b_pad = jnp.concatenate([b_r.astype(F32), jnp.full((pad,), -1e30, F32)]).reshape(1, ROUTE_W)
    return pl.pallas_call(
        _router_kernel,
        grid=(n_tiles,),
        in_specs=[rows.row_spec(D), rows.mod_spec(), _resident((D, 2 * ROUTE_W)), _resident((1, ROUTE_W))],
        out_specs=[rows.row_spec(D), rows.row_spec(ROUTE_W), pl.BlockSpec((1, 8, ROUTE_W), lambda i: (i, 0, 0))],
        out_shape=[jax.ShapeDtypeStruct((n_tiles * TM, D), BF16), jax.ShapeDtypeStruct((n_tiles * TM, ROUTE_W), F32),
                   jax.ShapeDtypeStruct((n_tiles, 8, ROUTE_W), F32)],
        compiler_params=_params(1),
        name="moe_router",
    )(h, modg, w_split, b_pad)


SEG_ALIGN = 16
SEG_BITS = tuple(1 << b for b in range(TM.bit_length() - 1, SEG_ALIGN.bit_length() - 2, -1))
LROWS = -(-(2 * TM + N_EXPERTS * (SEG_ALIGN - 1)) // 128) * 128
XS_W = D + 128
META_W = 3 * N_EXPERTS


def _segment_copies(meta_ref, tile, local_ref, slots_hbm, sem, to_slots):
    for e in range(N_EXPERTS):
        loff = meta_ref[tile * META_W + e]
        goff = meta_ref[tile * META_W + N_EXPERTS + e]
        seg = meta_ref[tile * META_W + 2 * N_EXPERTS + e]
        for bit in SEG_BITS:
            done = seg & ~(2 * bit - 1)
            lo = local_ref.at[pl.ds(pl.multiple_of(loff + done, SEG_ALIGN), bit)]
            gl = slots_hbm.at[pl.ds(pl.multiple_of(goff + done, SEG_ALIGN), bit)]
            copy = pltpu.make_async_copy(lo, gl, sem) if to_slots else pltpu.make_async_copy(gl, lo, sem)
            yield (seg & bit) != 0, copy


def _start_segment_copies(*args):
    for cond, copy in _segment_copies(*args):
        pl.when(cond)(copy.start)


def _wait_segment_copies(*args):
    for cond, copy in _segment_copies(*args):
        pl.when(cond)(copy.wait)


def _split3(x):
    hi = x.astype(BF16).astype(F32)
    mid = (x - hi).astype(BF16).astype(F32)
    lo = ((x - hi) - mid).astype(BF16).astype(F32)
    return hi, mid, lo


def _dispatch_kernel(n_blocks, meta_ref, fill_ref, a_ref, route_ref, tri_ref, xs_hbm, slot_ref, local_ref, zero_ref,
                     sem, zsem):
    i = pl.program_id(0)

    @pl.when(i == 0)
    def _zero_unwritten_slots():
        zero_ref[...] = jnp.zeros(zero_ref.shape, BF16)
        tails = []
        for e in range(N_EXPERTS):
            off, length = fill_ref[e], fill_ref[N_EXPERTS + e]
            for bit in (b for b in SEG_BITS if b < MOE_RB):
                done = length & ~(2 * bit - 1)
                dst = xs_hbm.at[pl.ds(pl.multiple_of(off + done, SEG_ALIGN), bit)]
                tails.append(((length & bit) != 0, pltpu.make_async_copy(zero_ref.at[pl.ds(0, bit)], dst, zsem)))
        for cond, copy in tails:
            pl.when(cond)(copy.start)

        def block_copy(blk):
            dst = xs_hbm.at[pl.ds(pl.multiple_of(blk * MOE_RB, MOE_RB), MOE_RB)]
            return pltpu.make_async_copy(zero_ref, dst, zsem)

        n_used = fill_ref[2 * N_EXPERTS]
        lax.fori_loop(n_used, n_blocks, lambda blk, c: (block_copy(blk).start(), c)[1], 0)
        for cond, copy in tails:
            pl.when(cond)(copy.wait)
        lax.fori_loop(n_used, n_blocks, lambda blk, c: (block_copy(blk).wait(), c)[1], 0)

    route = route_ref[...]
    lane = lax.broadcasted_iota(jnp.int32, (TM, ROUTE_W), 1)
    pick1 = lane == route[:, 0:1].astype(jnp.int32)
    pick2 = lane == route[:, 1:2].astype(jnp.int32)
    chosen = jnp.where(pick1 | pick2, 1.0, 0.0).astype(BF16)
    rank = _dot(tri_ref[...], chosen)
    lane1 = lax.broadcasted_iota(jnp.int32, (1, ROUTE_W), 1)
    loff = jnp.zeros((1, ROUTE_W), F32)
    for e in range(N_EXPERTS):
        loff = jnp.where(lane1 == e, meta_ref[i * META_W + e].astype(F32), loff)
    place = rank + loff
    slot1 = jnp.sum(jnp.where(pick1, place, 0.0), axis=-1, keepdims=True)
    slot2 = jnp.sum(jnp.where(pick2, place, 0.0), axis=-1, keepdims=True)
    slot_ref[...] = jnp.where(lane == 0, slot1, jnp.where(lane == 1, slot2, 0.0))

    row1 = jnp.transpose(jnp.broadcast_to(slot1, (TM, 128)))[0:1, :]
    row2 = jnp.transpose(jnp.broadcast_to(slot2, (TM, 128)))[0:1, :]
    lrow = lax.broadcasted_iota(jnp.int32, (LROWS, TM), 0).astype(F32)
    p1 = jnp.where(lrow == row1, 1.0, 0.0).astype(BF16)
    p2 = jnp.where(lrow == row2, 1.0, 0.0).astype(BF16)
    local_ref[:, :D] = _dot(p1 + p2, a_ref[...]).astype(BF16)
    gates = []
    for col in (2, 3):
        pieces = _split3(route[:, col:col + 1])
        g = jnp.zeros((TM, ROUTE_W), F32)
        for k, piece in enumerate(pieces):
            g = jnp.where(lane == k, piece, g)
        gates.append(g.astype(BF16))
    local_ref[:, D:] = (_dot(p1, gates[0]) + _dot(p2, gates[1])).astype(BF16)
    _start_segment_copies(meta_ref, i, local_ref, xs_hbm, sem, True)
    _wait_segment_copies(meta_ref, i, local_ref, xs_hbm, sem, True)


def _dispatch(n_tiles, n_blocks, a, route, meta, fill):
    tri = (jnp.arange(TM)[:, None] > jnp.arange(TM)[None, :]).astype(BF16)
    return pl.pallas_call(
        functools.partial(_dispatch_kernel, n_blocks),
        grid_spec=pltpu.PrefetchScalarGridSpec(
            num_scalar_prefetch=2,
            grid=(n_tiles,),
            in_specs=[pl.BlockSpec((TM, D), lambda i, *_: (i, 0)),
                      pl.BlockSpec((TM, ROUTE_W), lambda i, *_: (i, 0)),
                      _resident((TM, TM))],
            out_specs=[pl.BlockSpec(memory_space=pl.ANY), pl.BlockSpec((TM, ROUTE_W), lambda i, *_: (i, 0))],
            scratch_shapes=[pltpu.VMEM((LROWS, XS_W), BF16), pltpu.VMEM((MOE_RB, XS_W), BF16),
                            pltpu.SemaphoreType.DMA(()), pltpu.SemaphoreType.DMA(())],
        ),
        out_shape=[jax.ShapeDtypeStruct((n_blocks * MOE_RB, XS_W), BF16),
                   jax.ShapeDtypeStruct((n_tiles * TM, ROUTE_W), F32)],
        compiler_params=_params(1),
        name="moe_dispatch",
    )(meta, fill, a, route, tri)


def _expert_kernel(be_ref, fill_ref, xs_ref, w1_ref, w3_ref, w2_ref, ys_ref, t_ref):
    del be_ref
    used = pl.program_id(0) < fill_ref[2 * N_EXPERTS]

    @pl.when(used)
    def _ffn():
        y = _swiglu(xs_ref[:, :D], w1_ref, w3_ref, w2_ref, t_ref)
        gate = jnp.sum(xs_ref[:, D:].astype(F32), axis=-1, keepdims=True)
        ys_ref[...] = (gate * y).astype(BF16)

    @pl.when(jnp.logical_not(used))
    def _idle():
        ys_ref[...] = jnp.zeros(ys_ref.shape, BF16)


def _experts(xs, blk_exp, fill, layer, w1, w3, w2):
    def slab(d0, d1):
        return pl.BlockSpec((None, None, d0, d1), lambda i, be, fill: (layer, be[i], 0, 0),
                            pipeline_mode=pl.Buffered(1))

    n_blocks = blk_exp.shape[0]
    last_used = lambda i, fill: jnp.minimum(i, fill[2 * N_EXPERTS] - 1)
    return pl.pallas_call(
        _expert_kernel,
        grid_spec=pltpu.PrefetchScalarGridSpec(
            num_scalar_prefetch=2,
            grid=(n_blocks,),
            in_specs=[pl.BlockSpec((MOE_RB, XS_W), lambda i, be, fill: (last_used(i, fill), 0)),
                      slab(D, FFN_DIM), slab(D, FFN_DIM), slab(FFN_DIM, D)],
            out_specs=pl.BlockSpec((MOE_RB, D), lambda i, be, fill: (i, 0)),
            scratch_shapes=[pltpu.VMEM((MOE_RB, FFN_DIM), BF16)],
        ),
        out_shape=jax.ShapeDtypeStruct((n_blocks * MOE_RB, D), BF16),
        compiler_params=_params(1),
        name="moe_experts",
    )(blk_exp, fill, xs, w1, w3, w2)


def _combine_kernel(meta_ref, h_ref, mod_ref, slot_ref, ys_hbm, o_ref, local_ref, sem):
    i = pl.program_id(0)

    buf = i % 2

    def fetch(tile, b):
        return (meta_ref, tile, local_ref.at[b], ys_hbm, sem.at[b], False)

    @pl.when(i == 0)
    def _init():
        local_ref[...] = jnp.zeros(local_ref.shape, BF16)
        _start_segment_copies(*fetch(0, 0))

    @pl.when(i + 1 < pl.num_programs(0))
    def _prefetch():
        _start_segment_copies(*fetch(i + 1, 1 - buf))

    _wait_segment_copies(*fetch(i, buf))
    slots = slot_ref[...]
    lcol = lax.broadcasted_iota(jnp.int32, (TM, LROWS), 1).astype(F32)
    gather = jnp.where((lcol == slots[:, 0:1]) | (lcol == slots[:, 1:2]), 1.0, 0.0).astype(BF16)
    o_ref[...] = h_ref[...] + mod_ref[0][5:6] * _dot(gather, local_ref[buf])


def _combine(rows, n_tiles, h, modg, slots, ys, meta):
    tps, batch = rows.tiles_per_seq, rows.batch
    return pl.pallas_call(
        _combine_kernel,
        grid_spec=pltpu.PrefetchScalarGridSpec(
            num_scalar_prefetch=1,
            grid=(n_tiles,),
            in_specs=[pl.BlockSpec((TM, D), lambda i, m: (i, 0)),
                      pl.BlockSpec((1, 8, D), lambda i, m: (jnp.minimum(i // tps, batch), 0, 0)),
                      pl.BlockSpec((TM, ROUTE_W), lambda i, m: (i, 0)),
                      pl.BlockSpec(memory_space=pl.ANY)],
            out_specs=pl.BlockSpec((TM, D), lambda i, m: (i, 0)),
            scratch_shapes=[pltpu.VMEM((2, LROWS, D), BF16), pltpu.SemaphoreType.DMA((2,))],
        ),
        out_shape=jax.ShapeDtypeStruct((n_tiles * TM, D), F32),
        compiler_params=_params(1),
        name="moe_combine",
    )(meta, h, modg, slots, ys)


def _moe(rows, n_tiles, h, modg, w_r, b_r, layer, w1, w3, w2):
    a, route, cnt = _router(rows, n_tiles, h, modg, w_r, b_r)
    cnt = cnt[:, 0, :N_EXPERTS].astype(jnp.int32)
    seg = (cnt + SEG_ALIGN - 1) // SEG_ALIGN * SEG_ALIGN
    loff = jnp.cumsum(seg, axis=1) - seg
    total = jnp.sum(seg, axis=0)
    region = (total + MOE_RB - 1) // MOE_RB * MOE_RB
    region_end = jnp.cumsum(region)
    region_start = region_end - region
    goff = region_start[None, :] + jnp.cumsum(seg, axis=0) - seg
    meta = jnp.concatenate([loff, goff, seg], axis=1).reshape(-1)
    n_blocks = pl.cdiv(2 * n_tiles * TM + n_tiles * N_EXPERTS * (SEG_ALIGN - 1), MOE_RB) + N_EXPERTS
    n_used = region_end[-1] // MOE_RB
    fill = jnp.concatenate([region_start + total, region - total, n_used[None]]).astype(jnp.int32)
    blk = jnp.minimum(jnp.arange(n_blocks, dtype=jnp.int32), n_used - 1)
    blk_exp = jnp.minimum(jnp.searchsorted(region_end, blk * MOE_RB, side="right"), N_EXPERTS - 1).astype(jnp.int32)
    xs, slots = _dispatch(n_tiles, n_blocks, a, route, meta, fill)
    ys = _experts(xs, blk_exp, fill, layer, w1, w3, w2)
    return _combine(rows, n_tiles, h, modg, slots, ys, meta)


def kernel(x, c, ctx, c_ctx, ada_w, ada_b, norm_mix_g, norm_ffn_g, sc_in_w, sc_conv_w, sc_out_w, da_qkv_w, da_out_w,
           da_q_norm_g, da_k_norm_g, da_lambda, da_sub_norm_g, cm_in_w, cm_in_b, cm_v_norm_g, cm_ws, cm_bs, cm_out_w,
           sw_qkv_w, sw_out_w, sw_q_norm_g, sw_k_norm_g, sw_sink, ffn_w1, ffn_w3, ffn_w2, moe_router_w, moe_router_b,
           moe_w1, moe_w3, moe_w2):
    batch, seq, _ = x.shape
    ctx_len = ctx.shape[1]
    depth = ada_w.shape[0]
    assert depth == 4 and batch + 1 <= 16
    rows = _Rows(batch, seq, ctx_len)

    cvec = jnp.concatenate([c, c_ctx[None, :], jnp.zeros((16 - batch - 1, D), F32)], axis=0)
    mod = _ada_all(cvec, ada_w, ada_b)[:, :batch + 1].reshape(depth, batch + 1, 6, D)
    gains = jnp.stack([norm_mix_g, norm_ffn_g], axis=1)[:, None]
    modg = jnp.concatenate([mod, jnp.broadcast_to(gains, (depth, batch + 1, 2, D))], axis=2)

    bf = lambda w: w.astype(BF16)
    x2, ctx2 = x.reshape(-1, D), ctx.reshape(-1, D)

    bg, y = _conv_in(rows, x2, ctx2, modg[0], bf(sc_in_w[0]))
    conv_w = jnp.pad(sc_conv_w[0], ((0, 5), (0, 0)))
    h = _conv_out(rows, x2, ctx2, modg[0], bg, y, conv_w, bf(sc_out_w[0]))
    h = _ffn(rows, rows.all_tiles, h, modg[0], 0, ffn_w1, ffn_w3, ffn_w2)

    nq_chunks = D // CN
    qkv = _qkv(rows, h, modg[1], bf(da_qkv_w[0]), da_q_norm_g[0], da_k_norm_g[0], nq_chunks, nq_chunks)
    lam_init = 0.8 - 0.6 * math.exp(-0.3 * 1)
    o_lat, o_ctx = _diff_attn(rows, qkv, da_lambda[0], da_sub_norm_g[0], lam_init)
    h = _out_proj(rows, rows.all_tiles, h, modg[1], o_lat, o_ctx, bf(da_out_w[0]))
    h = _moe(rows, rows.all_tiles, h, modg[1], moe_router_w[0], moe_router_b[0], 0, moe_w1, moe_w3, moe_w2)

    h = _gmlp(rows, h, modg[2], bf(cm_in_w[0]), cm_in_b[0].reshape(1, -1), cm_v_norm_g[0].reshape(1, -1),
              bf(cm_ws[0]), cm_bs[0].T, bf(cm_out_w[0]))
    h = _ffn(rows, rows.all_tiles, h, modg[2], 1, ffn_w1, ffn_w3, ffn_w2)

    kv_chunks = SWA_KV_HEADS * HEAD_DIM // CN
    qkv = _qkv(rows, h, modg[3], bf(sw_qkv_w[0]), sw_q_norm_g[0], sw_k_norm_g[0], nq_chunks, kv_chunks)
    o = _swa(rows, qkv, sw_sink[0])
    h = _out_proj(rows, rows.lat_tiles, h, modg[3], o, o, bf(sw_out_w[0]))
    h = _moe(rows, rows.lat_tiles, h, modg[3], moe_router_w[1], moe_router_b[1], 1, moe_w1, moe_w3, moe_w2)
    return h.reshape(batch, seq, D)
```

```python
import functools
import math

import jax
import jax.numpy as jnp
from jax import lax
from jax.experimental import pallas as pl
from jax.experimental.pallas import tpu as pltpu

D = 1024
HEAD_DIM = 64
GRID_W = 64
ROPE_HALF = HEAD_DIM // 2
ROPE_BASE = 10000.0
DIFF_HEADS = D // (2 * HEAD_DIM)
SWA_Q_HEADS = D // HEAD_DIM
SWA_KV_HEADS = 4
SWA_GROUP = SWA_Q_HEADS // SWA_KV_HEADS
SWA_WINDOW = 128
CHUNK = 128
CM_WIDTH = 2 * D
CM_GROUPS = 8
CM_GW = CM_WIDTH // CM_GROUPS
FFN_DIM = 2816
N_EXPERTS = 8
EPS = 1e-6

F32 = jnp.float32
BF16 = jnp.bfloat16
HIGHEST = lax.Precision.HIGHEST
LOG2E = math.log2(math.e)

TM = 512
CN = 256
MOE_RB = 512
ATT_TQ = 512
ATT_TK = 1024
SWA_TQ = 256
ROUTE_W = 128
MAX_UNSHIFTED_LOGIT = 60.0
VMEM_LIMIT = 56 << 20


def _params(n_grid):
    return pltpu.CompilerParams(dimension_semantics=("arbitrary",) * n_grid, vmem_limit_bytes=VMEM_LIMIT)


def _resident(shape):
    zeros = (0,) * len(shape)
    return pl.BlockSpec(shape, lambda *_: zeros, pipeline_mode=pl.Buffered(1))


def _sigmoid(x):
    return 1.0 / (1.0 + jnp.exp(-x))


def _gelu_tanh(x):
    return 0.5 * x * (1.0 + jnp.tanh(math.sqrt(2.0 / math.pi) * (x + 0.044715 * (x * x * x))))


def _modnorm(x, g, shift, scale):
    y = x * lax.rsqrt(jnp.mean(x * x, axis=-1, keepdims=True) + EPS)
    return (y * g) * (1.0 + scale) + shift


def _dot(a, b):
    return jnp.dot(a, b, preferred_element_type=F32)


def _mix_mod(m):
    return m[6:7], m[0:1], m[1:2]


def _ffn_mod(m):
    return m[7:8], m[3:4], m[4:5]


def _ada_kernel(c_ref, w_ref, b_ref, o_ref):
    c = c_ref[...]
    s = c * _sigmoid(c)
    o_ref[0] = jnp.dot(s, w_ref[0], precision=HIGHEST, preferred_element_type=F32) + b_ref[0]


def _ada_all(cvec, ada_w, ada_b):
    depth, _, n = ada_w.shape
    tn = 1536
    rows = cvec.shape[0]
    return pl.pallas_call(
        _ada_kernel,
        grid=(depth, n // tn),
        in_specs=[pl.BlockSpec((rows, D), lambda l, j: (0, 0)),
                  pl.BlockSpec((1, D, tn), lambda l, j: (l, 0, j)),
                  pl.BlockSpec((1, 1, tn), lambda l, j: (l, 0, j))],
        out_specs=pl.BlockSpec((1, rows, tn), lambda l, j: (l, 0, j)),
        out_shape=jax.ShapeDtypeStruct((depth, rows, n), F32),
        compiler_params=_params(2),
        name="adaln",
    )(cvec, ada_w, ada_b.reshape(depth, 1, n))


class _Rows:
    def __init__(self, batch, seq, ctx_len):
        self.batch, self.seq, self.ctx_len = batch, seq, ctx_len
        self.n_lat = batch * seq
        self.n_ctx = batch * ctx_len
        self.n_all = self.n_lat + self.n_ctx
        assert seq % TM == 0 and self.n_ctx % TM == 0 and TM % ctx_len == 0
        self.lat_tiles = self.n_lat // TM
        self.all_tiles = self.n_all // TM
        self.tiles_per_seq = seq // TM

    def mod_spec(self):
        tps, batch = self.tiles_per_seq, self.batch
        return pl.BlockSpec((1, 8, D), lambda i, *_: (jnp.minimum(i // tps, batch), 0, 0))

    def row_spec(self, width):
        return pl.BlockSpec((TM, width), lambda i, *_: (i, 0))

    def split_specs(self, width):
        lat_tiles = self.lat_tiles
        return [pl.BlockSpec((TM, width), lambda i, *_: (jnp.minimum(i, lat_tiles - 1), 0)),
                pl.BlockSpec((TM, width), lambda i, *_: (jnp.maximum(i - lat_tiles, 0), 0))]


def _conv_in_kernel(lat_tiles, x_ref, ctx_ref, mod_ref, w_ref, bg_ref, y_ref):
    h = jnp.where(pl.program_id(0) < lat_tiles, x_ref[...], ctx_ref[...])
    a = _modnorm(h, *_mix_mod(mod_ref[0])).astype(BF16)
    for c in range(D // CN):
        lo = c * CN
        bg_ref[:, lo:lo + CN] = _dot(a, w_ref[:, lo:lo + CN]).astype(BF16)
        cg = _dot(a, w_ref[:, D + lo:D + lo + CN])
        xv = _dot(a, w_ref[:, 2 * D + lo:2 * D + lo + CN])
        y_ref[:, lo:lo + CN] = (cg * xv).astype(BF16)


def _conv_in(rows, x, ctx, modg, w_in):
    out = jax.ShapeDtypeStruct((rows.n_all, D), BF16)
    return pl.pallas_call(
        functools.partial(_conv_in_kernel, rows.lat_tiles),
        grid=(rows.all_tiles,),
        in_specs=rows.split_specs(D) + [rows.mod_spec(), _resident((D, 3 * D))],
        out_specs=[rows.row_spec(D), rows.row_spec(D)],
        out_shape=[out, out],
        compiler_params=_params(1),
        name="conv_in",
    )(x, ctx, modg, w_in)


HALO = 16


def _conv_out_kernel(n_lat, seq, ctx_len, x_ref, ctx_ref, mod_ref, bg_ref, y_ref, yp_ref, yn_ref, cw_ref, w_ref,
                     o_ref):
    i = pl.program_id(0)
    h = jnp.where(i * TM < n_lat, x_ref[...], ctx_ref[...])
    m = mod_ref[0]
    row = lax.broadcasted_iota(jnp.int32, (TM, 1), 0)
    grow = row + i * TM
    seq_len = jnp.where(grow < n_lat, seq, ctx_len)
    pos = grow & (seq_len - 1)
    out = None
    for c in range(D // CN):
        cols = slice(c * CN, (c + 1) * CN)
        y = y_ref[:, cols].astype(F32)
        prev_row = yp_ref[HALO - 1:HALO, cols].astype(F32)
        next_row = yn_ref[0:1, cols].astype(F32)
        y_m1 = jnp.where(row == 0, prev_row, pltpu.roll(y, 1, 0))
        y_m1 = jnp.where(pos == 0, 0.0, y_m1)
        y_p1 = jnp.where(row == TM - 1, next_row, pltpu.roll(y, TM - 1, 0))
        y_p1 = jnp.where(pos == seq_len - 1, 0.0, y_p1)
        conv = cw_ref[0:1, cols] * y_m1 + cw_ref[1:2, cols] * y + cw_ref[2:3, cols] * y_p1
        z = (bg_ref[:, cols].astype(F32) * conv).astype(BF16)
        part = _dot(z, w_ref[cols, :])
        out = part if out is None else out + part
    o_ref[...] = h + m[2:3] * out


def _conv_out(rows, x, ctx, modg, bg, y, conv_w, w_out):
    hb = TM // HALO
    last = rows.n_all // HALO - 1
    return pl.pallas_call(
        functools.partial(_conv_out_kernel, rows.n_lat, rows.seq, rows.ctx_len),
        grid=(rows.all_tiles,),
        in_specs=rows.split_specs(D) + [rows.mod_spec(), rows.row_spec(D), rows.row_spec(D),
                  pl.BlockSpec((HALO, D), lambda i: (jnp.maximum(i * hb - 1, 0), 0)),
                  pl.BlockSpec((HALO, D), lambda i: (jnp.minimum((i + 1) * hb, last), 0)),
                  _resident((8, D)), _resident((D, D))],
        out_specs=rows.row_spec(D),
        out_shape=jax.ShapeDtypeStruct((rows.n_all, D), F32),
        compiler_params=_params(1),
        name="conv_out",
    )(x, ctx, modg, bg, y, y, y, conv_w, w_out)


W2_KC = 768


def _swiglu(a, w1_ref, w3_ref, w2_ref, t_ref):
    for c in range(FFN_DIM // CN):
        lo = c * CN
        h1 = _dot(a, w1_ref[:, lo:lo + CN].astype(BF16))
        h3 = _dot(a, w3_ref[:, lo:lo + CN].astype(BF16))
        t_ref[:, lo:lo + CN] = (h1 * _sigmoid(h1) * h3).astype(BF16)
    out = None
    for lo in range(0, FFN_DIM, W2_KC):
        hi = min(lo + W2_KC, FFN_DIM)
        part = _dot(t_ref[:, lo:hi], w2_ref[lo:hi, :].astype(BF16))
        out = part if out is None else out + part
    return out


def _ffn_kernel(h_ref, mod_ref, w1_ref, w3_ref, w2_ref, o_ref, t_ref):
    m = mod_ref[0]
    x = h_ref[...]
    a = _modnorm(x, *_ffn_mod(m)).astype(BF16)
    o_ref[...] = x + m[5:6] * _swiglu(a, w1_ref, w3_ref, w2_ref, t_ref)


def _ffn(rows, n_tiles, h, modg, layer, w1, w3, w2):
    def slab(d0, d1):
        return pl.BlockSpec((None, d0, d1), lambda i: (layer, 0, 0), pipeline_mode=pl.Buffered(1))

    return pl.pallas_call(
        _ffn_kernel,
        grid=(n_tiles,),
        in_specs=[rows.row_spec(D), rows.mod_spec(), slab(D, FFN_DIM), slab(D, FFN_DIM), slab(FFN_DIM, D)],
        out_specs=rows.row_spec(D),
        out_shape=jax.ShapeDtypeStruct((n_tiles * TM, D), F32),
        scratch_shapes=[pltpu.VMEM((TM, FFN_DIM), BF16)],
        compiler_params=_params(1),
        name="ffn_dense",
    )(h, modg, w1, w3, w2)


def _norm_rope(x, gmat_ref, cos, sin):
    ms = _dot((x * x).astype(BF16), gmat_ref[...]) * (1.0 / HEAD_DIM)
    xn = x * lax.rsqrt(ms + EPS)
    lane = lax.broadcasted_iota(jnp.int32, (1, CN), 1)
    first_half = (lane & (ROPE_HALF - 1)) < (ROPE_HALF // 2)
    partner = jnp.where(first_half, pltpu.roll(xn, CN - ROPE_HALF // 2, 1), pltpu.roll(xn, ROPE_HALF // 2, 1))
    return xn * cos + partner * sin


def _qkv_kernel(n_qk_chunks, n_q_chunks, h_ref, mod_ref, w_ref, gmat_ref, qcos_ref, qsin_ref, kcos_ref, ksin_ref,
                o_ref):
    a = _modnorm(h_ref[...], *_mix_mod(mod_ref[0])).astype(BF16)
    n_chunks = w_ref.shape[1] // CN

    def project(c):
        return _dot(a, w_ref[:, c * CN:(c + 1) * CN])

    nxt = project(0)
    for c in range(n_chunks):
        acc = nxt
        if c + 1 < n_chunks:
            nxt = project(c + 1)
        if c < n_q_chunks:
            acc = _norm_rope(acc, gmat_ref, qcos_ref[...], qsin_ref[...])
        elif c < n_qk_chunks:
            acc = _norm_rope(acc, gmat_ref, kcos_ref[...], ksin_ref[...])
        o_ref[:, c * CN:(c + 1) * CN] = acc.astype(BF16)


def _rope_tables(rows, gain, scale):
    seq = rows.seq
    pos = jnp.arange(seq)
    n_freq = ROPE_HALF // 2
    inv = ROPE_BASE ** (-jnp.arange(n_freq, dtype=F32) / n_freq)
    ang_r = (pos // GRID_W).astype(F32)[:, None] * inv
    ang_c = (pos % GRID_W).astype(F32)[:, None] * inv
    cos = jnp.concatenate([jnp.cos(ang_r)] * 2 + [jnp.cos(ang_c)] * 2, axis=-1)
    sin = jnp.concatenate([-jnp.sin(ang_r), jnp.sin(ang_r), -jnp.sin(ang_c), jnp.sin(ang_c)], axis=-1)
    cos = jnp.concatenate([cos, jnp.ones((TM, HEAD_DIM), F32)], axis=0)
    sin = jnp.concatenate([sin, jnp.zeros((TM, HEAD_DIM), F32)], axis=0)
    dim = jnp.arange(HEAD_DIM)
    partner = jnp.where((dim % ROPE_HALF) < n_freq, dim + n_freq, dim - n_freq)
    g = gain.astype(F32) * scale
    reps = CN // HEAD_DIM
    return jnp.tile(cos * g[None, :], (1, reps)), jnp.tile(sin * g[partner][None, :], (1, reps))


def _qkv(rows, h, modg, w, q_gain, k_gain, n_q_chunks, n_k_chunks):
    width = w.shape[1]
    head = jnp.arange(CN) // HEAD_DIM
    gmat = (head[:, None] == head[None, :]).astype(BF16)
    tables = _rope_tables(rows, q_gain, LOG2E * HEAD_DIM ** -0.5) + _rope_tables(rows, k_gain, 1.0)
    tps, lat_tiles = rows.tiles_per_seq, rows.lat_tiles
    tab_spec = pl.BlockSpec((TM, CN), lambda i: (jnp.where(i < lat_tiles, i % tps, tps), 0))
    return pl.pallas_call(
        functools.partial(_qkv_kernel, n_q_chunks + n_k_chunks, n_q_chunks),
        grid=(rows.all_tiles,),
        in_specs=[rows.row_spec(D), rows.mod_spec(), _resident((D, width)), _resident((CN, CN))] + [tab_spec] * 4,
        out_specs=rows.row_spec(width),
        out_shape=jax.ShapeDtypeStruct((rows.n_all, width), BF16),
        compiler_params=_params(1),
        name="qkv_proj",
    )(h, modg, w, gmat, *tables)


DH2 = 2 * HEAD_DIM


def _diff_attn_kernel(lam_init, n_lat_chunks, q_ref, kc_ref, vc_ref, kl_ref, vl_ref, lam_ref, subg_ref, o_ref,
                      m_ref, acc_ref, vca_ref, vla_ref, kn_ref):
    def max_norm_sq(x):
        xf = x.astype(F32)
        return jnp.max(jnp.sum(xf * xf, axis=-1, keepdims=True), axis=0, keepdims=True)

    def per_head_setup():
        vca_ref[:, :DH2] = vc_ref[...]
        vca_ref[:, DH2:] = jnp.ones((vc_ref.shape[0], DH2), BF16)
        kn = max_norm_sq(kc_ref[...])
        if n_lat_chunks:
            vla_ref[:, :DH2] = vl_ref[...]
            vla_ref[:, DH2:] = jnp.ones((vl_ref.shape[0], DH2), BF16)
            kn = jnp.maximum(kn, max_norm_sq(kl_ref[...]))
        kn_ref[...] = jnp.broadcast_to(kn, kn_ref.shape)

    if n_lat_chunks:
        pl.when(pl.program_id(2) == 0)(per_head_setup)
    else:
        per_head_setup()

    q = q_ref[...]
    lane = lax.broadcasted_iota(jnp.int32, (1, DH2), 1)
    zero = jnp.zeros((), BF16)
    qs = (jnp.where(lane < HEAD_DIM, q, zero), jnp.where(lane >= HEAD_DIM, q, zero))
    nt = (((1,), (1,)), ((), ()))

    def over_keys(update):
        update(kc_ref[...], vca_ref[...])
        if n_lat_chunks:
            tk = kl_ref.shape[0] // n_lat_chunks

            def body(c, carry):
                start = pl.multiple_of(c * tk, tk)
                update(kl_ref[pl.ds(start, tk), :], vla_ref[pl.ds(start, tk), :])
                return carry
            lax.fori_loop(0, n_lat_chunks, body, 0)

    lp = lam_ref[...]
    lam = (jnp.exp(jnp.sum(lp[0:1] * lp[1:2], axis=-1, keepdims=True))
           - jnp.exp(jnp.sum(lp[2:3] * lp[3:4], axis=-1, keepdims=True)) + lam_init)

    def unshifted():
        n_sub = 2
        sub = q.shape[0] // n_sub

        def logits(r):
            out = []
            for mi in range(2):
                qm = qs[mi][r * sub:(r + 1) * sub]
                s_c = lax.dot_general(qm, kc_ref[...], nt, preferred_element_type=F32)
                s_l = lax.dot_general(qm, kl_ref[...], nt, preferred_element_type=F32) if n_lat_chunks else None
                out.append((s_c, s_l))
            return out

        all_logits = [logits(r) for r in range(n_sub)]
        for r in range(n_sub):
            p_ctx, p_lat, sums = [], [], []
            for s_c, s_l in all_logits[r]:
                e = jnp.exp2(s_c)
                total = jnp.sum(e, axis=-1, keepdims=True)
                p_ctx.append(e)
                if s_l is not None:
                    e = jnp.exp2(s_l)
                    total = total + jnp.sum(e, axis=-1, keepdims=True)
                    p_lat.append(e)
                sums.append(total)
            r0 = 1.0 / sums[0]
            r1 = lam / sums[1]

            def weights(p):
                return (p[0] * r0 - p[1] * r1).astype(BF16)

            o = _dot(weights(p_ctx), vc_ref[...])
            if n_lat_chunks:
                o = o + _dot(weights(p_lat), vl_ref[...])
            acc_ref[0, r * sub:(r + 1) * sub, :DH2] = o

    def update_online(k, va):
        reps = k.shape[0] // DH2
        for mi in range(2):
            s = lax.dot_general(qs[mi], k, nt, preferred_element_type=F32)
            m_old = m_ref[mi]
            m_new = jnp.maximum(m_old, jnp.max(s, axis=-1, keepdims=True))
            alpha = jnp.exp2(m_old - m_new)
            p = jnp.exp2(s - jnp.concatenate([m_new] * reps, axis=-1))
            acc_ref[mi] = jnp.concatenate([alpha, alpha], axis=-1) * acc_ref[mi] + _dot(p.astype(BF16), va)
            m_ref[mi] = m_new

    def online():
        m_ref[...] = jnp.full(m_ref.shape, -jnp.inf, F32)
        acc_ref[...] = jnp.zeros(acc_ref.shape, F32)
        over_keys(update_online)
        acc0 = acc_ref[0]
        acc1 = acc_ref[1]
        acc_ref[0, :, :DH2] = acc0[:, :DH2] / acc0[:, DH2:] - lam * (acc1[:, :DH2] / acc1[:, DH2:])

    bound_sq = max_norm_sq(q) * kn_ref[0:1, 0:1]
    small_logits = bound_sq[0, 0] <= MAX_UNSHIFTED_LOGIT ** 2
    pl.when(small_logits)(unshifted)
    pl.when(jnp.logical_not(small_logits))(online)

    o = acc_ref[0, :, :DH2]
    o = o * lax.rsqrt(jnp.mean(o * o, axis=-1, keepdims=True) + EPS) * subg_ref[...]
    o_ref[...] = (o * (1.0 - lam_init)).astype(BF16)


def _diff_attn(rows, qkv, lam_p, sub_g, lam_init):
    batch, seq, ctx_len = rows.batch, rows.seq, rows.ctx_len
    nq = seq // ATT_TQ
    nh = DIFF_HEADS
    ctx0 = rows.n_lat // ctx_len
    sub_g = sub_g.reshape(1, DH2)
    small = [_resident((4, HEAD_DIM)), _resident((1, DH2))]
    kc_spec = pl.BlockSpec((ctx_len, DH2), lambda b, h, *_: (ctx0 + b, nh + h))
    vc_spec = pl.BlockSpec((ctx_len, DH2), lambda b, h, *_: (ctx0 + b, 2 * nh + h))

    def scratch(tq, n_lat_keys):
        return [pltpu.VMEM((2, tq, DH2), F32), pltpu.VMEM((2, tq, 2 * DH2), F32),
                pltpu.VMEM((ctx_len, 2 * DH2), BF16), pltpu.VMEM((n_lat_keys, 2 * DH2), BF16),
                pltpu.VMEM((8, DH2), F32)]

    o_lat = pl.pallas_call(
        functools.partial(_diff_attn_kernel, lam_init, pl.cdiv(seq, ATT_TK)),
        grid=(batch, nh, nq),
        in_specs=[pl.BlockSpec((ATT_TQ, DH2), lambda b, h, i: (b * nq + i, h)),
                  kc_spec, vc_spec,
                  pl.BlockSpec((seq, DH2), lambda b, h, i: (b, nh + h)),
                  pl.BlockSpec((seq, DH2), lambda b, h, i: (b, 2 * nh + h))] + small,
        out_specs=pl.BlockSpec((ATT_TQ, DH2), lambda b, h, i: (b * nq + i, h)),
        out_shape=jax.ShapeDtypeStruct((rows.n_lat, D), BF16),
        scratch_shapes=scratch(ATT_TQ, seq),
        compiler_params=_params(3),
        name="diff_attn_latent",
    )(qkv, qkv, qkv, qkv, qkv, lam_p, sub_g)

    def ctx_kernel(q_ref, kc_ref, vc_ref, lam_ref, subg_ref, o_ref, *scratch_refs):
        _diff_attn_kernel(lam_init, 0, q_ref, kc_ref, vc_ref, None, None, lam_ref, subg_ref, o_ref, *scratch_refs)

    o_ctx = pl.pallas_call(
        ctx_kernel,
        grid=(batch, nh),
        in_specs=[pl.BlockSpec((ctx_len, DH2), lambda b, h: (ctx0 + b, h)), kc_spec, vc_spec] + small,
        out_specs=pl.BlockSpec((ctx_len, DH2), lambda b, h: (b, h)),
        out_shape=jax.ShapeDtypeStruct((rows.n_ctx, D), BF16),
        scratch_shapes=scratch(ctx_len, 16),
        compiler_params=_params(2),
        name="diff_attn_context",
    )(qkv, qkv, qkv, lam_p, sub_g)
    return o_lat, o_ctx


def _gmlp_kernel(h_ref, mod_ref, win_ref, bin_ref, vg_ref, ws_ref, bs_ref, wout_ref, o_ref, u_ref, v_ref, t_ref):
    m = mod_ref[0]
    x = h_ref[...]
    a = _modnorm(x, *_mix_mod(m)).astype(BF16)
    n_half = CM_WIDTH // CN

    def in_proj(c):
        lo = c * CN
        return _gelu_tanh(_dot(a, win_ref[:, lo:lo + CN]) + bin_ref[:, lo:lo + CN])

    ssq = jnp.zeros((TM, 1), F32)
    for c in range(n_half):
        z = in_proj(n_half + c)
        v_ref[:, c * CN:(c + 1) * CN] = z
        ssq = ssq + jnp.sum(z * z, axis=-1, keepdims=True)
    for c in range(n_half):
        u_ref[:, c * CN:(c + 1) * CN] = in_proj(c)
    inv = lax.rsqrt(ssq * (1.0 / CM_WIDTH) + EPS)

    def mix(g):
        lo = g * CM_GW
        vn = (v_ref[:, lo:lo + CM_GW] * inv * vg_ref[:, lo:lo + CM_GW]).astype(BF16)
        return [_dot(ws_ref[g], vn[r * CHUNK:(r + 1) * CHUNK]) + bs_ref[:, g:g + 1] for r in range(TM // CHUNK)]

    out = None
    nxt = mix(0)
    for g in range(CM_GROUPS):
        lo = g * CM_GW
        sv = nxt
        if g + 1 < CM_GROUPS:
            nxt = mix(g + 1)
        for r in range(TM // CHUNK):
            r0 = r * CHUNK
            t_ref[r0:r0 + CHUNK, lo:lo + CM_GW] = (u_ref[r0:r0 + CHUNK, lo:lo + CM_GW] * sv[r]).astype(BF16)
        part = _dot(t_ref[:, lo:lo + CM_GW], wout_ref[lo:lo + CM_GW, :])
        out = part if out is None else out + part
    o_ref[...] = x + m[2:3] * out


def _gmlp(rows, h, modg, w_in, b_in, v_g, w_s, b_s, w_out):
    return pl.pallas_call(
        _gmlp_kernel,
        grid=(rows.all_tiles,),
        in_specs=[rows.row_spec(D), rows.mod_spec(), _resident((D, 2 * CM_WIDTH)), _resident((1, 2 * CM_WIDTH)),
                  _resident((1, CM_WIDTH)), _resident((CM_GROUPS, CHUNK, CHUNK)), _resident((CHUNK, CM_GROUPS)),
                  _resident((CM_WIDTH, D))],
        out_specs=rows.row_spec(D),
        out_shape=jax.ShapeDtypeStruct((rows.n_all, D), F32),
        scratch_shapes=[pltpu.VMEM((TM, CM_WIDTH), F32), pltpu.VMEM((TM, CM_WIDTH), F32),
                        pltpu.VMEM((TM, CM_WIDTH), BF16)],
        compiler_params=_params(1),
        name="gmlp",
    )(h, modg, w_in, b_in, v_g, w_s, b_s, w_out)


SWA_BAND = SWA_TQ + 2 * SWA_WINDOW


SWA_PAIR = 2 * HEAD_DIM
N_KV_VARIANTS = 2 * SWA_KV_HEADS


def _swa_kernel(seq, sink_ref, q_ref, kc_ref, vc_ref, kl_ref, vl_ref, o_ref, kcv_ref, vcv_ref, klv_ref, vlv_ref,
                kn_ref):
    qi = pl.program_id(1)
    lane = lax.broadcasted_iota(jnp.int32, (1, SWA_PAIR), 1)
    lo_half = lane < HEAD_DIM

    def max_norm_sq(x):
        xf = x.astype(F32)
        return jnp.max(jnp.sum(xf * xf, axis=-1, keepdims=True), axis=0, keepdims=True)

    @pl.when(qi == 0)
    def _per_batch_setup():
        kn = jnp.zeros((1, 1), F32)
        for src_ref, dst_ref in ((kc_ref, kcv_ref), (vc_ref, vcv_ref), (kl_ref, klv_ref), (vl_ref, vlv_ref)):
            for half in range(SWA_KV_HEADS // 2):
                x = src_ref[:, half * SWA_PAIR:(half + 1) * SWA_PAIR].astype(F32)
                xr = pltpu.roll(x, HEAD_DIM, 1)
                j0, j1 = 2 * half, 2 * half + 1
                dst_ref[2 * j0] = jnp.where(lo_half, x, 0.0).astype(BF16)
                dst_ref[2 * j0 + 1] = jnp.where(lo_half, 0.0, xr).astype(BF16)
                dst_ref[2 * j1] = jnp.where(lo_half, xr, 0.0).astype(BF16)
                dst_ref[2 * j1 + 1] = jnp.where(lo_half, 0.0, x).astype(BF16)
                if src_ref is kc_ref or src_ref is kl_ref:
                    kn = jnp.maximum(kn, max_norm_sq(x))
        kn_ref[...] = jnp.broadcast_to(kn, kn_ref.shape)

    q0 = qi * SWA_TQ
    start = jnp.clip(q0 - SWA_WINDOW, 0, seq - SWA_BAND)
    start = pl.multiple_of(start, SWA_WINDOW)
    qpos = q0 + lax.broadcasted_iota(jnp.int32, (SWA_TQ, SWA_BAND), 0)
    kpos = start + lax.broadcasted_iota(jnp.int32, (SWA_TQ, SWA_BAND), 1)
    in_band = jnp.abs(qpos - kpos) <= SWA_WINDOW
    nt = (((1,), (1,)), ((), ()))

    def unshifted():
        in_band2 = jnp.concatenate([in_band, in_band], axis=0)
        top = lax.broadcasted_iota(jnp.int32, (2 * SWA_TQ, 1), 0) < SWA_TQ
        def logits(v):
            lo = (v // 2) * SWA_GROUP * HEAD_DIM
            q2 = jnp.concatenate([q_ref[:, lo:lo + SWA_PAIR], q_ref[:, lo + SWA_PAIR:lo + 2 * SWA_PAIR]], axis=0)
            return (lax.dot_general(q2, kcv_ref[v], nt, preferred_element_type=F32),
                    lax.dot_general(q2, klv_ref[v, pl.ds(start, SWA_BAND), :], nt, preferred_element_type=F32))

        nxt = logits(0)
        for j in range(SWA_KV_HEADS):
            lo = j * SWA_GROUP * HEAD_DIM
            out = jnp.zeros((2 * SWA_TQ, SWA_PAIR), F32)
            for var in range(2):
                v = 2 * j + var
                s_c, s_b = nxt
                if v + 1 < N_KV_VARIANTS:
                    nxt = logits(v + 1)
                p_c = jnp.exp2(s_c)
                p_b = jnp.where(in_band2, jnp.exp2(s_b), 0.0)
                hq = j * SWA_GROUP + var
                sink = jnp.where(top, sink_ref[hq] * LOG2E, sink_ref[hq + 2] * LOG2E)
                denom = (jnp.sum(p_c, axis=-1, keepdims=True) + jnp.sum(p_b, axis=-1, keepdims=True)
                         + jnp.exp2(sink))
                o = (_dot(p_c.astype(BF16), vcv_ref[v])
                     + _dot(p_b.astype(BF16), vlv_ref[v, pl.ds(start, SWA_BAND), :]))
                out = out + o / denom
            o_ref[:, lo:lo + SWA_PAIR] = out[:SWA_TQ].astype(BF16)
            o_ref[:, lo + SWA_PAIR:lo + 2 * SWA_PAIR] = out[SWA_TQ:].astype(BF16)

    def shifted():
        kb = kl_ref[pl.ds(start, SWA_BAND), :]
        vb = vl_ref[pl.ds(start, SWA_BAND), :]
        kc = kc_ref[...]
        vc = vc_ref[...]
        for j in range(SWA_KV_HEADS):
            kj = slice(j * HEAD_DIM, (j + 1) * HEAD_DIM)
            kbj, vbj, kcj, vcj = kb[:, kj], vb[:, kj], kc[:, kj], vc[:, kj]
            outs = []
            for g in range(SWA_GROUP):
                hq = j * SWA_GROUP + g
                qh = q_ref[:, hq * HEAD_DIM:(hq + 1) * HEAD_DIM]
                s_c = lax.dot_general(qh, kcj, nt, preferred_element_type=F32)
                s_b = lax.dot_general(qh, kbj, nt, preferred_element_type=F32)
                s_b = jnp.where(in_band, s_b, -jnp.inf)
                sink = sink_ref[hq] * LOG2E
                mx = jnp.maximum(jnp.maximum(jnp.max(s_c, axis=-1, keepdims=True),
                                             jnp.max(s_b, axis=-1, keepdims=True)), sink)
                p_c = jnp.exp2(s_c - mx)
                p_b = jnp.exp2(s_b - mx)
                denom = (jnp.sum(p_c, axis=-1, keepdims=True) + jnp.sum(p_b, axis=-1, keepdims=True)
                         + jnp.exp2(sink - mx))
                o = _dot(p_c.astype(BF16), vcj) + _dot(p_b.astype(BF16), vbj)
                outs.append(o / denom)
            lo = j * SWA_GROUP * HEAD_DIM
            o_ref[:, lo:lo + SWA_GROUP * HEAD_DIM] = jnp.concatenate(outs, axis=-1).astype(BF16)

    qn = jnp.zeros((1, 1), F32)
    for pair in range(SWA_Q_HEADS // 2):
        qn = jnp.maximum(qn, max_norm_sq(q_ref[:, pair * SWA_PAIR:(pair + 1) * SWA_PAIR]))
    max_sink = sink_ref[0]
    for hq in range(1, SWA_Q_HEADS):
        max_sink = jnp.maximum(max_sink, sink_ref[hq])
    small_logits = jnp.logical_and((qn * kn_ref[0:1, 0:1])[0, 0] <= MAX_UNSHIFTED_LOGIT ** 2,
                                   max_sink * LOG2E <= MAX_UNSHIFTED_LOGIT)
    pl.when(small_logits)(unshifted)
    pl.when(jnp.logical_not(small_logits))(shifted)


def _swa(rows, qkv, sink):
    batch, seq, ctx_len = rows.batch, rows.seq, rows.ctx_len
    nq = seq // SWA_TQ
    kvw = SWA_KV_HEADS * HEAD_DIM
    k_col = D // kvw
    ctx0 = rows.n_lat // ctx_len
    return pl.pallas_call(
        functools.partial(_swa_kernel, seq),
        grid=(batch, nq),
        in_specs=[pl.BlockSpec(memory_space=pltpu.SMEM),
                  pl.BlockSpec((SWA_TQ, D), lambda b, i: (b * nq + i, 0)),
                  pl.BlockSpec((ctx_len, kvw), lambda b, i: (ctx0 + b, k_col)),
                  pl.BlockSpec((ctx_len, kvw), lambda b, i: (ctx0 + b, k_col + 1)),
                  pl.BlockSpec((seq, kvw), lambda b, i: (b, k_col)),
                  pl.BlockSpec((seq, kvw), lambda b, i: (b, k_col + 1))],
        out_specs=pl.BlockSpec((SWA_TQ, D), lambda b, i: (b * nq + i, 0)),
        out_shape=jax.ShapeDtypeStruct((rows.n_lat, D), BF16),
        scratch_shapes=[pltpu.VMEM((N_KV_VARIANTS, ctx_len, SWA_PAIR), BF16),
                        pltpu.VMEM((N_KV_VARIANTS, ctx_len, SWA_PAIR), BF16),
                        pltpu.VMEM((N_KV_VARIANTS, seq, SWA_PAIR), BF16),
                        pltpu.VMEM((N_KV_VARIANTS, seq, SWA_PAIR), BF16),
                        pltpu.VMEM((8, SWA_PAIR), F32)],
        compiler_params=_params(2),
        name="swa_attn",
    )(sink, qkv, qkv, qkv, qkv, qkv)


def _proj_route_kernel(lat_tiles, h_ref, mod_ref, zl_ref, zc_ref, w_ref, wr_ref, br_ref, hm_ref, a_ref, route_ref,
                       cnt_ref):
    i = pl.program_id(0)
    m = mod_ref[0]

    def project(z_ref):
        hm_ref[...] = h_ref[...] + m[2:3] * _dot(z_ref[...], w_ref[...])

    pl.when(i < lat_tiles)(lambda: project(zl_ref))
    pl.when(i >= lat_tiles)(lambda: project(zc_ref))
    a = _modnorm(hm_ref[...], *_ffn_mod(m))
    a_hi = a.astype(BF16)
    a_ref[...] = a_hi
    a_lo = (a - a_hi.astype(F32)).astype(BF16)
    hi_part = _dot(a_hi, wr_ref[...])
    logits = (hi_part[:, :ROUTE_W] + hi_part[:, ROUTE_W:]) + _dot(a_lo, wr_ref[:, :ROUTE_W]) + br_ref[...]
    lane = lax.broadcasted_iota(jnp.int32, (TM, ROUTE_W), 1)
    m1 = jnp.max(logits, axis=-1, keepdims=True)
    i1 = jnp.min(jnp.where(logits == m1, lane, ROUTE_W), axis=-1, keepdims=True)
    rest = jnp.where(lane == i1, -jnp.inf, logits)
    m2 = jnp.max(rest, axis=-1, keepdims=True)
    i2 = jnp.min(jnp.where(rest == m2, lane, ROUTE_W), axis=-1, keepdims=True)
    e2 = jnp.exp(m2 - m1)
    gate1 = 1.0 / (1.0 + e2)
    gate2 = e2 / (1.0 + e2)
    rec = jnp.where(lane == 0, i1.astype(F32), 0.0)
    rec = jnp.where(lane == 1, i2.astype(F32), rec)
    rec = jnp.where(lane == 2, gate1, rec)
    route_ref[...] = jnp.where(lane == 3, gate2, rec)
    chosen = jnp.where((lane == i1) | (lane == i2), 1.0, 0.0)
    cnt_ref[0] = jnp.broadcast_to(jnp.sum(chosen, axis=0, keepdims=True), (8, ROUTE_W))


def _proj_route(rows, n_tiles, h, modg, z_lat, z_ctx, w_out, w_r, b_r):
    lat_tiles = rows.lat_tiles
    pad = ROUTE_W - N_EXPERTS
    w_pad = jnp.pad(w_r, ((0, 0), (0, pad)))
    w_hi = w_pad.astype(BF16)
    w_lo = (w_pad - w_hi.astype(F32)).astype(BF16)
    w_split = jnp.concatenate([w_hi, w_lo], axis=1)
    b_pad = jnp.concatenate([b_r.astype(F32), jnp.full((pad,), -1e30, F32)]).reshape(1, ROUTE_W)
    return pl.pallas_call(
        functools.partial(_proj_route_kernel, lat_tiles),
        grid=(n_tiles,),
        in_specs=[rows.row_spec(D), rows.mod_spec(),
                  pl.BlockSpec((TM, D), lambda i: (jnp.minimum(i, lat_tiles - 1), 0)),
                  pl.BlockSpec((TM, D), lambda i: (jnp.maximum(i - lat_tiles, 0), 0)),
                  _resident((D, D)), _resident((D, 2 * ROUTE_W)), _resident((1, ROUTE_W))],
        out_specs=[rows.row_spec(D), rows.row_spec(D), rows.row_spec(ROUTE_W),
                   pl.BlockSpec((1, 8, ROUTE_W), lambda i: (i, 0, 0))],
        out_shape=[jax.ShapeDtypeStruct((n_tiles * TM, D), F32), jax.ShapeDtypeStruct((n_tiles * TM, D), BF16),
                   jax.ShapeDtypeStruct((n_tiles * TM, ROUTE_W), F32),
                   jax.ShapeDtypeStruct((n_tiles, 8, ROUTE_W), F32)],
        compiler_params=_params(1),
        name="out_proj_router",
    )(h, modg, z_lat, z_ctx, w_out, w_split, b_pad)


SEG_ALIGN = 16
SEG_BITS = tuple(1 << b for b in range(TM.bit_length() - 1, SEG_ALIGN.bit_length() - 2, -1))
LROWS = -(-(2 * TM + N_EXPERTS * (SEG_ALIGN - 1)) // 128) * 128
XS_W = D + 128
META_W = 3 * N_EXPERTS


def _segment_copies(meta_ref, tile, local_ref, slots_hbm, sem, to_slots):
    for e in range(N_EXPERTS):
        loff = meta_ref[tile * META_W + e]
        goff = meta_ref[tile * META_W + N_EXPERTS + e]
        seg = meta_ref[tile * META_W + 2 * N_EXPERTS + e]
        for bit in SEG_BITS:
            done = seg & ~(2 * bit - 1)
            lo = local_ref.at[pl.ds(pl.multiple_of(loff + done, SEG_ALIGN), bit)]
            gl = slots_hbm.at[pl.ds(pl.multiple_of(goff + done, SEG_ALIGN), bit)]
            copy = pltpu.make_async_copy(lo, gl, sem) if to_slots else pltpu.make_async_copy(gl, lo, sem)
            yield (seg & bit) != 0, copy


def _start_segment_copies(*args):
    for cond, copy in _segment_copies(*args):
        pl.when(cond)(copy.start)


def _wait_segment_copies(*args):
    for cond, copy in _segment_copies(*args):
        pl.when(cond)(copy.wait)


def _split3(x):
    hi = x.astype(BF16).astype(F32)
    mid = (x - hi).astype(BF16).astype(F32)
    lo = ((x - hi) - mid).astype(BF16).astype(F32)
    return hi, mid, lo


def _dispatch_kernel(n_blocks, meta_ref, fill_ref, a_ref, route_ref, tri_ref, xs_hbm, slot_ref, local_ref, zero_ref,
                     sem, zsem):
    i = pl.program_id(0)

    @pl.when(i == 0)
    def _zero_unwritten_slots():
        zero_ref[...] = jnp.zeros(zero_ref.shape, BF16)
        tails = []
        for e in range(N_EXPERTS):
            off, length = fill_ref[e], fill_ref[N_EXPERTS + e]
            for bit in (b for b in SEG_BITS if b < MOE_RB):
                done = length & ~(2 * bit - 1)
                dst = xs_hbm.at[pl.ds(pl.multiple_of(off + done, SEG_ALIGN), bit)]
                tails.append(((length & bit) != 0, pltpu.make_async_copy(zero_ref.at[pl.ds(0, bit)], dst, zsem)))
        for cond, copy in tails:
            pl.when(cond)(copy.start)

        def block_copy(blk):
            dst = xs_hbm.at[pl.ds(pl.multiple_of(blk * MOE_RB, MOE_RB), MOE_RB)]
            return pltpu.make_async_copy(zero_ref, dst, zsem)

        n_used = fill_ref[2 * N_EXPERTS]
        lax.fori_loop(n_used, n_blocks, lambda blk, c: (block_copy(blk).start(), c)[1], 0)
        for cond, copy in tails:
            pl.when(cond)(copy.wait)
        lax.fori_loop(n_used, n_blocks, lambda blk, c: (block_copy(blk).wait(), c)[1], 0)

    route = route_ref[...]
    lane = lax.broadcasted_iota(jnp.int32, (TM, ROUTE_W), 1)
    pick1 = lane == route[:, 0:1].astype(jnp.int32)
    pick2 = lane == route[:, 1:2].astype(jnp.int32)
    chosen = jnp.where(pick1 | pick2, 1.0, 0.0).astype(BF16)
    rank = _dot(tri_ref[...], chosen)
    lane1 = lax.broadcasted_iota(jnp.int32, (1, ROUTE_W), 1)
    loff = jnp.zeros((1, ROUTE_W), F32)
    for e in range(N_EXPERTS):
        loff = jnp.where(lane1 == e, meta_ref[i * META_W + e].astype(F32), loff)
    place = rank + loff
    slot1 = jnp.sum(jnp.where(pick1, place, 0.0), axis=-1, keepdims=True)
    slot2 = jnp.sum(jnp.where(pick2, place, 0.0), axis=-1, keepdims=True)
    slot_ref[...] = jnp.where(lane == 0, slot1, jnp.where(lane == 1, slot2, 0.0))

    row1 = jnp.transpose(jnp.broadcast_to(slot1, (TM, 128)))[0:1, :]
    row2 = jnp.transpose(jnp.broadcast_to(slot2, (TM, 128)))[0:1, :]
    lrow = lax.broadcasted_iota(jnp.int32, (LROWS, TM), 0).astype(F32)
    p1 = jnp.where(lrow == row1, 1.0, 0.0).astype(BF16)
    p2 = jnp.where(lrow == row2, 1.0, 0.0).astype(BF16)
    local_ref[:, :D] = _dot(p1 + p2, a_ref[...]).astype(BF16)
    gates = []
    for col in (2, 3):
        pieces = _split3(route[:, col:col + 1])
        g = jnp.zeros((TM, ROUTE_W), F32)
        for k, piece in enumerate(pieces):
            g = jnp.where(lane == k, piece, g)
        gates.append(g.astype(BF16))
    local_ref[:, D:] = (_dot(p1, gates[0]) + _dot(p2, gates[1])).astype(BF16)
    _start_segment_copies(meta_ref, i, local_ref, xs_hbm, sem, True)
    _wait_segment_copies(meta_ref, i, local_ref, xs_hbm, sem, True)


def _dispatch(n_tiles, n_blocks, a, route, meta, fill):
    tri = (jnp.arange(TM)[:, None] > jnp.arange(TM)[None, :]).astype(BF16)
    return pl.pallas_call(
        functools.partial(_dispatch_kernel, n_blocks),
        grid_spec=pltpu.PrefetchScalarGridSpec(
            num_scalar_prefetch=2,
            grid=(n_tiles,),
            in_specs=[pl.BlockSpec((TM, D), lambda i, *_: (i, 0)),
                      pl.BlockSpec((TM, ROUTE_W), lambda i, *_: (i, 0)),
                      _resident((TM, TM))],
            out_specs=[pl.BlockSpec(memory_space=pl.ANY), pl.BlockSpec((TM, ROUTE_W), lambda i, *_: (i, 0))],
            scratch_shapes=[pltpu.VMEM((LROWS, XS_W), BF16), pltpu.VMEM((MOE_RB, XS_W), BF16),
                            pltpu.SemaphoreType.DMA(()), pltpu.SemaphoreType.DMA(())],
        ),
        out_shape=[jax.ShapeDtypeStruct((n_blocks * MOE_RB, XS_W), BF16),
                   jax.ShapeDtypeStruct((n_tiles * TM, ROUTE_W), F32)],
        compiler_params=_params(1),
        name="moe_dispatch",
    )(meta, fill, a, route, tri)


def _expert_kernel(be_ref, fill_ref, xs_ref, w1_ref, w3_ref, w2_ref, ys_ref, t_ref):
    del be_ref
    used = pl.program_id(0) < fill_ref[2 * N_EXPERTS]

    @pl.when(used)
    def _ffn():
        y = _swiglu(xs_ref[:, :D], w1_ref, w3_ref, w2_ref, t_ref)
        gate = jnp.sum(xs_ref[:, D:].astype(F32), axis=-1, keepdims=True)
        ys_ref[...] = (gate * y).astype(BF16)

    @pl.when(jnp.logical_not(used))
    def _idle():
        ys_ref[...] = jnp.zeros(ys_ref.shape, BF16)


def _experts(xs, blk_exp, fill, layer, w1, w3, w2):
    def slab(d0, d1):
        return pl.BlockSpec((None, None, d0, d1), lambda i, be, fill: (layer, be[i], 0, 0),
                            pipeline_mode=pl.Buffered(1))

    n_blocks = blk_exp.shape[0]
    last_used = lambda i, fill: jnp.minimum(i, fill[2 * N_EXPERTS] - 1)
    return pl.pallas_call(
        _expert_kernel,
        grid_spec=pltpu.PrefetchScalarGridSpec(
            num_scalar_prefetch=2,
            grid=(n_blocks,),
            in_specs=[pl.BlockSpec((MOE_RB, XS_W), lambda i, be, fill: (last_used(i, fill), 0)),
                      slab(D, FFN_DIM), slab(D, FFN_DIM), slab(FFN_DIM, D)],
            out_specs=pl.BlockSpec((MOE_RB, D), lambda i, be, fill: (i, 0)),
            scratch_shapes=[pltpu.VMEM((MOE_RB, FFN_DIM), BF16)],
        ),
        out_shape=jax.ShapeDtypeStruct((n_blocks * MOE_RB, D), BF16),
        compiler_params=_params(1),
        name="moe_experts",
    )(blk_exp, fill, xs, w1, w3, w2)


def _combine_kernel(meta_ref, h_ref, mod_ref, slot_ref, ys_hbm, o_ref, local_ref, sem):
    i = pl.program_id(0)

    buf = i % 2

    def fetch(tile, b):
        return (meta_ref, tile, local_ref.at[b], ys_hbm, sem.at[b], False)

    @pl.when(i == 0)
    def _init():
        local_ref[...] = jnp.zeros(local_ref.shape, BF16)
        _start_segment_copies(*fetch(0, 0))

    @pl.when(i + 1 < pl.num_programs(0))
    def _prefetch():
        _start_segment_copies(*fetch(i + 1, 1 - buf))

    _wait_segment_copies(*fetch(i, buf))
    slots = slot_ref[...]
    lcol = lax.broadcasted_iota(jnp.int32, (TM, LROWS), 1).astype(F32)
    gather = jnp.where((lcol == slots[:, 0:1]) | (lcol == slots[:, 1:2]), 1.0, 0.0).astype(BF16)
    o_ref[...] = h_ref[...] + mod_ref[0][5:6] * _dot(gather, local_ref[buf])


def _combine(rows, n_tiles, h, modg, slots, ys, meta):
    tps, batch = rows.tiles_per_seq, rows.batch
    return pl.pallas_call(
        _combine_kernel,
        grid_spec=pltpu.PrefetchScalarGridSpec(
            num_scalar_prefetch=1,
            grid=(n_tiles,),
            in_specs=[pl.BlockSpec((TM, D), lambda i, m: (i, 0)),
                      pl.BlockSpec((1, 8, D), lambda i, m: (jnp.minimum(i // tps, batch), 0, 0)),
                      pl.BlockSpec((TM, ROUTE_W), lambda i, m: (i, 0)),
                      pl.BlockSpec(memory_space=pl.ANY)],
            out_specs=pl.BlockSpec((TM, D), lambda i, m: (i, 0)),
            scratch_shapes=[pltpu.VMEM((2, LROWS, D), BF16), pltpu.SemaphoreType.DMA((2,))],
        ),
        out_shape=jax.ShapeDtypeStruct((n_tiles * TM, D), F32),
        compiler_params=_params(1),
        name="moe_combine",
    )(meta, h, modg, slots, ys)


def _mixer_out_moe(rows, n_tiles, h, modg, z_lat, z_ctx, w_out, w_r, b_r, layer, w1, w3, w2):
    h, a, route, cnt = _proj_route(rows, n_tiles, h, modg, z_lat, z_ctx, w_out, w_r, b_r)
    cnt = cnt[:, 0, :N_EXPERTS].astype(jnp.int32)
    seg = (cnt + SEG_ALIGN - 1) // SEG_ALIGN * SEG_ALIGN
    loff = jnp.cumsum(seg, axis=1) - seg
    total = jnp.sum(seg, axis=0)
    region = (total + MOE_RB - 1) // MOE_RB * MOE_RB
    region_end = jnp.cumsum(region)
    region_start = region_end - region
    goff = region_start[None, :] + jnp.cumsum(seg, axis=0) - seg
    meta = jnp.concatenate([loff, goff, seg], axis=1).reshape(-1)
    n_blocks = pl.cdiv(2 * n_tiles * TM + n_tiles * N_EXPERTS * (SEG_ALIGN - 1), MOE_RB) + N_EXPERTS
    n_used = region_end[-1] // MOE_RB
    fill = jnp.concatenate([region_start + total, region - total, n_used[None]]).astype(jnp.int32)
    blk = jnp.minimum(jnp.arange(n_blocks, dtype=jnp.int32), n_used - 1)
    blk_exp = jnp.minimum(jnp.searchsorted(region_end, blk * MOE_RB, side="right"), N_EXPERTS - 1).astype(jnp.int32)
    xs, slots = _dispatch(n_tiles, n_blocks, a, route, meta, fill)
    ys = _experts(xs, blk_exp, fill, layer, w1, w3, w2)
    return _combine(rows, n_tiles, h, modg, slots, ys, meta)


def kernel(x, c, ctx, c_ctx, ada_w, ada_b, norm_mix_g, norm_ffn_g, sc_in_w, sc_conv_w, sc_out_w, da_qkv_w, da_out_w,
           da_q_norm_g, da_k_norm_g, da_lambda, da_sub_norm_g, cm_in_w, cm_in_b, cm_v_norm_g, cm_ws, cm_bs, cm_out_w,
           sw_qkv_w, sw_out_w, sw_q_norm_g, sw_k_norm_g, sw_sink, ffn_w1, ffn_w3, ffn_w2, moe_router_w, moe_router_b,
           moe_w1, moe_w3, moe_w2):
    batch, seq, _ = x.shape
    ctx_len = ctx.shape[1]
    depth = ada_w.shape[0]
    assert depth == 4 and batch + 1 <= 16
    rows = _Rows(batch, seq, ctx_len)

    cvec = jnp.concatenate([c, c_ctx[None, :], jnp.zeros((16 - batch - 1, D), F32)], axis=0)
    mod = _ada_all(cvec, ada_w, ada_b)[:, :batch + 1].reshape(depth, batch + 1, 6, D)
    gains = jnp.stack([norm_mix_g, norm_ffn_g], axis=1)[:, None]
    modg = jnp.concatenate([mod, jnp.broadcast_to(gains, (depth, batch + 1, 2, D))], axis=2)

    bf = lambda w: w.astype(BF16)
    x2, ctx2 = x.reshape(-1, D), ctx.reshape(-1, D)

    bg, y = _conv_in(rows, x2, ctx2, modg[0], bf(sc_in_w[0]))
    conv_w = jnp.pad(sc_conv_w[0], ((0, 5), (0, 0)))
    h = _conv_out(rows, x2, ctx2, modg[0], bg, y, conv_w, bf(sc_out_w[0]))
    h = _ffn(rows, rows.all_tiles, h, modg[0], 0, ffn_w1, ffn_w3, ffn_w2)

    nq_chunks = D // CN
    qkv = _qkv(rows, h, modg[1], bf(da_qkv_w[0]), da_q_norm_g[0], da_k_norm_g[0], nq_chunks, nq_chunks)
    lam_init = 0.8 - 0.6 * math.exp(-0.3 * 1)
    o_lat, o_ctx = _diff_attn(rows, qkv, da_lambda[0], da_sub_norm_g[0], lam_init)
    h = _mixer_out_moe(rows, rows.all_tiles, h, modg[1], o_lat, o_ctx, bf(da_out_w[0]), moe_router_w[0],
                       moe_router_b[0], 0, moe_w1, moe_w3, moe_w2)

    h = _gmlp(rows, h, modg[2], bf(cm_in_w[0]), cm_in_b[0].reshape(1, -1), cm_v_norm_g[0].reshape(1, -1),
              bf(cm_ws[0]), cm_bs[0].T, bf(cm_out_w[0]))
    h = _ffn(rows, rows.all_tiles, h, modg[2], 1, ffn_w1, ffn_w3, ffn_w2)

    kv_chunks = SWA_KV_HEADS * HEAD_DIM // CN
    qkv = _qkv(rows, h, modg[3], bf(sw_qkv_w[0]), sw_q_norm_g[0], sw_k_norm_g[0], nq_chunks, kv_chunks)
    o = _swa(rows, qkv, sw_sink[0])
    h = _mixer_out_moe(rows, rows.lat_tiles, h, modg[3], o, o, bf(sw_out_w[0]), moe_router_w[1], moe_router_b[1], 1,
                       moe_w1, moe_w3, moe_w2)
    return h.reshape(batch, seq, D)
```

```python
import functools
import math

import jax
import jax.numpy as jnp
from jax import lax
from jax.experimental import pallas as pl
from jax.experimental.pallas import tpu as pltpu

D = 1024
HEAD_DIM = 64
GRID_W = 64
ROPE_HALF = HEAD_DIM // 2
ROPE_BASE = 10000.0
DIFF_HEADS = D // (2 * HEAD_DIM)
SWA_Q_HEADS = D // HEAD_DIM
SWA_KV_HEADS = 4
SWA_GROUP = SWA_Q_HEADS // SWA_KV_HEADS
SWA_WINDOW = 128
CHUNK = 128
CM_WIDTH = 2 * D
CM_GROUPS = 8
CM_GW = CM_WIDTH // CM_GROUPS
FFN_DIM = 2816
N_EXPERTS = 8
EPS = 1e-6

F32 = jnp.float32
BF16 = jnp.bfloat16
HIGHEST = lax.Precision.HIGHEST
LOG2E = math.log2(math.e)

TM = 512
CN = 256
MOE_RB = 512
ATT_TQ = 512
ATT_TK = 1024
SWA_TQ = 256
ROUTE_W = 128
MAX_UNSHIFTED_LOGIT = 60.0
VMEM_LIMIT = 56 << 20


def _params(n_grid):
    return pltpu.CompilerParams(dimension_semantics=("arbitrary",) * n_grid, vmem_limit_bytes=VMEM_LIMIT)


def _resident(shape):
    zeros = (0,) * len(shape)
    return pl.BlockSpec(shape, lambda *_: zeros, pipeline_mode=pl.Buffered(1))


def _sigmoid(x):
    return 1.0 / (1.0 + jnp.exp(-x))


def _gelu_tanh(x):
    return 0.5 * x * (1.0 + jnp.tanh(math.sqrt(2.0 / math.pi) * (x + 0.044715 * (x * x * x))))


def _modnorm(x, g, shift, scale):
    y = x * lax.rsqrt(jnp.mean(x * x, axis=-1, keepdims=True) + EPS)
    return (y * g) * (1.0 + scale) + shift


def _dot(a, b):
    return jnp.dot(a, b, preferred_element_type=F32)


def _mix_mod(m):
    return m[6:7], m[0:1], m[1:2]


def _ffn_mod(m):
    return m[7:8], m[3:4], m[4:5]


def _ada_kernel(c_ref, w_ref, b_ref, o_ref):
    c = c_ref[...]
    s = c * _sigmoid(c)
    o_ref[0] = jnp.dot(s, w_ref[0], precision=HIGHEST, preferred_element_type=F32) + b_ref[0]


def _ada_all(cvec, ada_w, ada_b):
    depth, _, n = ada_w.shape
    tn = 1536
    rows = cvec.shape[0]
    return pl.pallas_call(
        _ada_kernel,
        grid=(depth, n // tn),
        in_specs=[pl.BlockSpec((rows, D), lambda l, j: (0, 0)),
                  pl.BlockSpec((1, D, tn), lambda l, j: (l, 0, j)),
                  pl.BlockSpec((1, 1, tn), lambda l, j: (l, 0, j))],
        out_specs=pl.BlockSpec((1, rows, tn), lambda l, j: (l, 0, j)),
        out_shape=jax.ShapeDtypeStruct((depth, rows, n), F32),
        compiler_params=_params(2),
        name="adaln",
    )(cvec, ada_w, ada_b.reshape(depth, 1, n))


class _Rows:
    def __init__(self, batch, seq, ctx_len):
        self.batch, self.seq, self.ctx_len = batch, seq, ctx_len
        self.n_lat = batch * seq
        self.n_ctx = batch * ctx_len
        self.n_all = self.n_lat + self.n_ctx
        assert seq % TM == 0 and self.n_ctx % TM == 0 and TM % ctx_len == 0
        self.lat_tiles = self.n_lat // TM
        self.all_tiles = self.n_all // TM
        self.tiles_per_seq = seq // TM

    def mod_spec(self):
        tps, batch = self.tiles_per_seq, self.batch
        return pl.BlockSpec((1, 8, D), lambda i, *_: (jnp.minimum(i // tps, batch), 0, 0))

    def row_spec(self, width):
        return pl.BlockSpec((TM, width), lambda i, *_: (i, 0))

    def split_specs(self, width):
        lat_tiles = self.lat_tiles
        return [pl.BlockSpec((TM, width), lambda i, *_: (jnp.minimum(i, lat_tiles - 1), 0)),
                pl.BlockSpec((TM, width), lambda i, *_: (jnp.maximum(i - lat_tiles, 0), 0))]


def _conv_in_kernel(lat_tiles, x_ref, ctx_ref, mod_ref, w_ref, bg_ref, y_ref):
    h = jnp.where(pl.program_id(0) < lat_tiles, x_ref[...], ctx_ref[...])
    a = _modnorm(h, *_mix_mod(mod_ref[0])).astype(BF16)
    for c in range(D // CN):
        lo = c * CN
        bg_ref[:, lo:lo + CN] = _dot(a, w_ref[:, lo:lo + CN]).astype(BF16)
        cg = _dot(a, w_ref[:, D + lo:D + lo + CN])
        xv = _dot(a, w_ref[:, 2 * D + lo:2 * D + lo + CN])
        y_ref[:, lo:lo + CN] = (cg * xv).astype(BF16)


def _conv_in(rows, x, ctx, modg, w_in):
    out = jax.ShapeDtypeStruct((rows.n_all, D), BF16)
    return pl.pallas_call(
        functools.partial(_conv_in_kernel, rows.lat_tiles),
        grid=(rows.all_tiles,),
        in_specs=rows.split_specs(D) + [rows.mod_spec(), _resident((D, 3 * D))],
        out_specs=[rows.row_spec(D), rows.row_spec(D)],
        out_shape=[out, out],
        compiler_params=_params(1),
        name="conv_in",
    )(x, ctx, modg, w_in)


HALO = 16


def _conv_out_kernel(n_lat, seq, ctx_len, x_ref, ctx_ref, mod_ref, bg_ref, y_ref, yp_ref, yn_ref, cw_ref, w_ref,
                     o_ref):
    i = pl.program_id(0)
    h = jnp.where(i * TM < n_lat, x_ref[...], ctx_ref[...])
    m = mod_ref[0]
    row = lax.broadcasted_iota(jnp.int32, (TM, 1), 0)
    grow = row + i * TM
    seq_len = jnp.where(grow < n_lat, seq, ctx_len)
    pos = grow & (seq_len - 1)
    out = None
    for c in range(D // CN):
        cols = slice(c * CN, (c + 1) * CN)
        y = y_ref[:, cols].astype(F32)
        prev_row = yp_ref[HALO - 1:HALO, cols].astype(F32)
        next_row = yn_ref[0:1, cols].astype(F32)
        y_m1 = jnp.where(row == 0, prev_row, pltpu.roll(y, 1, 0))
        y_m1 = jnp.where(pos == 0, 0.0, y_m1)
        y_p1 = jnp.where(row == TM - 1, next_row, pltpu.roll(y, TM - 1, 0))
        y_p1 = jnp.where(pos == seq_len - 1, 0.0, y_p1)
        conv = cw_ref[0:1, cols] * y_m1 + cw_ref[1:2, cols] * y + cw_ref[2:3, cols] * y_p1
        z = (bg_ref[:, cols].astype(F32) * conv).astype(BF16)
        part = _dot(z, w_ref[cols, :])
        out = part if out is None else out + part
    o_ref[...] = h + m[2:3] * out


def _conv_out(rows, x, ctx, modg, bg, y, conv_w, w_out):
    hb = TM // HALO
    last = rows.n_all // HALO - 1
    return pl.pallas_call(
        functools.partial(_conv_out_kernel, rows.n_lat, rows.seq, rows.ctx_len),
        grid=(rows.all_tiles,),
        in_specs=rows.split_specs(D) + [rows.mod_spec(), rows.row_spec(D), rows.row_spec(D),
                  pl.BlockSpec((HALO, D), lambda i: (jnp.maximum(i * hb - 1, 0), 0)),
                  pl.BlockSpec((HALO, D), lambda i: (jnp.minimum((i + 1) * hb, last), 0)),
                  _resident((8, D)), _resident((D, D))],
        out_specs=rows.row_spec(D),
        out_shape=jax.ShapeDtypeStruct((rows.n_all, D), F32),
        compiler_params=_params(1),
        name="conv_out",
    )(x, ctx, modg, bg, y, y, y, conv_w, w_out)


W2_KC = 768


def _swiglu(a, w1_ref, w3_ref, w2_ref, t_ref):
    for c in range(FFN_DIM // CN):
        lo = c * CN
        h1 = _dot(a, w1_ref[:, lo:lo + CN].astype(BF16))
        h3 = _dot(a, w3_ref[:, lo:lo + CN].astype(BF16))
        t_ref[:, lo:lo + CN] = (h1 * _sigmoid(h1) * h3).astype(BF16)
    out = None
    for lo in range(0, FFN_DIM, W2_KC):
        hi = min(lo + W2_KC, FFN_DIM)
        part = _dot(t_ref[:, lo:hi], w2_ref[lo:hi, :].astype(BF16))
        out = part if out is None else out + part
    return out


def _ffn_kernel(h_ref, mod_ref, w1_ref, w3_ref, w2_ref, o_ref, t_ref):
    m = mod_ref[0]
    x = h_ref[...]
    a = _modnorm(x, *_ffn_mod(m)).astype(BF16)
    o_ref[...] = x + m[5:6] * _swiglu(a, w1_ref, w3_ref, w2_ref, t_ref)


def _ffn(rows, n_tiles, h, modg, layer, w1, w3, w2):
    def slab(d0, d1):
        return pl.BlockSpec((None, d0, d1), lambda i: (layer, 0, 0), pipeline_mode=pl.Buffered(1))

    return pl.pallas_call(
        _ffn_kernel,
        grid=(n_tiles,),
        in_specs=[rows.row_spec(D), rows.mod_spec(), slab(D, FFN_DIM), slab(D, FFN_DIM), slab(FFN_DIM, D)],
        out_specs=rows.row_spec(D),
        out_shape=jax.ShapeDtypeStruct((n_tiles * TM, D), F32),
        scratch_shapes=[pltpu.VMEM((TM, FFN_DIM), BF16)],
        compiler_params=_params(1),
        name="ffn_dense",
    )(h, modg, w1, w3, w2)


def _norm_rope(x, gmat_ref, cos, sin):
    ms = _dot((x * x).astype(BF16), gmat_ref[...]) * (1.0 / HEAD_DIM)
    xn = x * lax.rsqrt(ms + EPS)
    lane = lax.broadcasted_iota(jnp.int32, (1, CN), 1)
    first_half = (lane & (ROPE_HALF - 1)) < (ROPE_HALF // 2)
    partner = jnp.where(first_half, pltpu.roll(xn, CN - ROPE_HALF // 2, 1), pltpu.roll(xn, ROPE_HALF // 2, 1))
    return xn * cos + partner * sin


def _qkv_kernel(n_qk_chunks, n_q_chunks, h_ref, mod_ref, w_ref, gmat_ref, qcos_ref, qsin_ref, kcos_ref, ksin_ref,
                o_ref):
    a = _modnorm(h_ref[...], *_mix_mod(mod_ref[0])).astype(BF16)
    n_chunks = w_ref.shape[1] // CN

    def project(c):
        return _dot(a, w_ref[:, c * CN:(c + 1) * CN])

    nxt = project(0)
    for c in range(n_chunks):
        acc = nxt
        if c + 1 < n_chunks:
            nxt = project(c + 1)
        if c < n_q_chunks:
            acc = _norm_rope(acc, gmat_ref, qcos_ref[...], qsin_ref[...])
        elif c < n_qk_chunks:
            acc = _norm_rope(acc, gmat_ref, kcos_ref[...], ksin_ref[...])
        o_ref[:, c * CN:(c + 1) * CN] = acc.astype(BF16)


def _rope_tables(rows, gain, scale):
    seq = rows.seq
    pos = jnp.arange(seq)
    n_freq = ROPE_HALF // 2
    inv = ROPE_BASE ** (-jnp.arange(n_freq, dtype=F32) / n_freq)
    ang_r = (pos // GRID_W).astype(F32)[:, None] * inv
    ang_c = (pos % GRID_W).astype(F32)[:, None] * inv
    cos = jnp.concatenate([jnp.cos(ang_r)] * 2 + [jnp.cos(ang_c)] * 2, axis=-1)
    sin = jnp.concatenate([-jnp.sin(ang_r), jnp.sin(ang_r), -jnp.sin(ang_c), jnp.sin(ang_c)], axis=-1)
    cos = jnp.concatenate([cos, jnp.ones((TM, HEAD_DIM), F32)], axis=0)
    sin = jnp.concatenate([sin, jnp.zeros((TM, HEAD_DIM), F32)], axis=0)
    dim = jnp.arange(HEAD_DIM)
    partner = jnp.where((dim % ROPE_HALF) < n_freq, dim + n_freq, dim - n_freq)
    g = gain.astype(F32) * scale
    reps = CN // HEAD_DIM
    return jnp.tile(cos * g[None, :], (1, reps)), jnp.tile(sin * g[partner][None, :], (1, reps))


def _qkv(rows, h, modg, w, q_gain, k_gain, n_q_chunks, n_k_chunks):
    width = w.shape[1]
    head = jnp.arange(CN) // HEAD_DIM
    gmat = (head[:, None] == head[None, :]).astype(BF16)
    tables = _rope_tables(rows, q_gain, LOG2E * HEAD_DIM ** -0.5) + _rope_tables(rows, k_gain, 1.0)
    tps, lat_tiles = rows.tiles_per_seq, rows.lat_tiles
    tab_spec = pl.BlockSpec((TM, CN), lambda i: (jnp.where(i < lat_tiles, i % tps, tps), 0))
    return pl.pallas_call(
        functools.partial(_qkv_kernel, n_q_chunks + n_k_chunks, n_q_chunks),
        grid=(rows.all_tiles,),
        in_specs=[rows.row_spec(D), rows.mod_spec(), _resident((D, width)), _resident((CN, CN))] + [tab_spec] * 4,
        out_specs=rows.row_spec(width),
        out_shape=jax.ShapeDtypeStruct((rows.n_all, width), BF16),
        compiler_params=_params(1),
        name="qkv_proj",
    )(h, modg, w, gmat, *tables)


DH2 = 2 * HEAD_DIM


def _diff_attn_kernel(lam_init, n_lat_chunks, q_ref, kc_ref, vc_ref, kl_ref, vl_ref, lam_ref, subg_ref, o_ref,
                      m_ref, acc_ref, vca_ref, vla_ref, kn_ref):
    def max_norm_sq(x):
        xf = x.astype(F32)
        return jnp.max(jnp.sum(xf * xf, axis=-1, keepdims=True), axis=0, keepdims=True)

    def per_head_setup():
        vca_ref[:, :DH2] = vc_ref[...]
        vca_ref[:, DH2:] = jnp.ones((vc_ref.shape[0], DH2), BF16)
        kn = max_norm_sq(kc_ref[...])
        if n_lat_chunks:
            vla_ref[:, :DH2] = vl_ref[...]
            vla_ref[:, DH2:] = jnp.ones((vl_ref.shape[0], DH2), BF16)
            kn = jnp.maximum(kn, max_norm_sq(kl_ref[...]))
        kn_ref[...] = jnp.broadcast_to(kn, kn_ref.shape)

    if n_lat_chunks:
        pl.when(pl.program_id(2) == 0)(per_head_setup)
    else:
        per_head_setup()

    q = q_ref[...]
    lane = lax.broadcasted_iota(jnp.int32, (1, DH2), 1)
    zero = jnp.zeros((), BF16)
    qs = (jnp.where(lane < HEAD_DIM, q, zero), jnp.where(lane >= HEAD_DIM, q, zero))
    nt = (((1,), (1,)), ((), ()))

    def over_keys(update):
        update(kc_ref[...], vca_ref[...])
        if n_lat_chunks:
            tk = kl_ref.shape[0] // n_lat_chunks

            def body(c, carry):
                start = pl.multiple_of(c * tk, tk)
                update(kl_ref[pl.ds(start, tk), :], vla_ref[pl.ds(start, tk), :])
                return carry
            lax.fori_loop(0, n_lat_chunks, body, 0)

    lp = lam_ref[...]
    lam = (jnp.exp(jnp.sum(lp[0:1] * lp[1:2], axis=-1, keepdims=True))
           - jnp.exp(jnp.sum(lp[2:3] * lp[3:4], axis=-1, keepdims=True)) + lam_init)

    def unshifted():
        n_sub = 2
        sub = q.shape[0] // n_sub

        def logits(r):
            out = []
            for mi in range(2):
                qm = qs[mi][r * sub:(r + 1) * sub]
                s_c = lax.dot_general(qm, kc_ref[...], nt, preferred_element_type=F32)
                s_l = lax.dot_general(qm, kl_ref[...], nt, preferred_element_type=F32) if n_lat_chunks else None
                out.append((s_c, s_l))
            return out

        all_logits = [logits(r) for r in range(n_sub)]
        for r in range(n_sub):
            p_ctx, p_lat, sums = [], [], []
            for s_c, s_l in all_logits[r]:
                e = jnp.exp2(s_c)
                total = jnp.sum(e, axis=-1, keepdims=True)
                p_ctx.append(e)
                if s_l is not None:
                    e = jnp.exp2(s_l)
                    total = total + jnp.sum(e, axis=-1, keepdims=True)
                    p_lat.append(e)
                sums.append(total)
            r0 = 1.0 / sums[0]
            r1 = lam / sums[1]

            def weights(p):
                return (p[0] * r0 - p[1] * r1).astype(BF16)

            o = _dot(weights(p_ctx), vc_ref[...])
            if n_lat_chunks:
                o = o + _dot(weights(p_lat), vl_ref[...])
            acc_ref[0, r * sub:(r + 1) * sub, :DH2] = o

    def update_online(k, va):
        reps = k.shape[0] // DH2
        for mi in range(2):
            s = lax.dot_general(qs[mi], k, nt, preferred_element_type=F32)
            m_old = m_ref[mi]
            m_new = jnp.maximum(m_old, jnp.max(s, axis=-1, keepdims=True))
            alpha = jnp.exp2(m_old - m_new)
            p = jnp.exp2(s - jnp.concatenate([m_new] * reps, axis=-1))
            acc_ref[mi] = jnp.concatenate([alpha, alpha], axis=-1) * acc_ref[mi] + _dot(p.astype(BF16), va)
            m_ref[mi] = m_new

    def online():
        m_ref[...] = jnp.full(m_ref.shape, -jnp.inf, F32)
        acc_ref[...] = jnp.zeros(acc_ref.shape, F32)
        over_keys(update_online)
        acc0 = acc_ref[0]
        acc1 = acc_ref[1]
        acc_ref[0, :, :DH2] = acc0[:, :DH2] / acc0[:, DH2:] - lam * (acc1[:, :DH2] / acc1[:, DH2:])

    bound_sq = max_norm_sq(q) * kn_ref[0:1, 0:1]
    small_logits = bound_sq[0, 0] <= MAX_UNSHIFTED_LOGIT ** 2
    pl.when(small_logits)(unshifted)
    pl.when(jnp.logical_not(small_logits))(online)

    o = acc_ref[0, :, :DH2]
    o = o * lax.rsqrt(jnp.mean(o * o, axis=-1, keepdims=True) + EPS) * subg_ref[...]
    o_ref[...] = (o * (1.0 - lam_init)).astype(BF16)


def _diff_attn(rows, qkv, lam_p, sub_g, lam_init):
    batch, seq, ctx_len = rows.batch, rows.seq, rows.ctx_len
    nq = seq // ATT_TQ
    nh = DIFF_HEADS
    ctx0 = rows.n_lat // ctx_len
    sub_g = sub_g.reshape(1, DH2)
    small = [_resident((4, HEAD_DIM)), _resident((1, DH2))]
    kc_spec = pl.BlockSpec((ctx_len, DH2), lambda b, h, *_: (ctx0 + b, nh + h))
    vc_spec = pl.BlockSpec((ctx_len, DH2), lambda b, h, *_: (ctx0 + b, 2 * nh + h))

    def scratch(tq, n_lat_keys):
        return [pltpu.VMEM((2, tq, DH2), F32), pltpu.VMEM((2, tq, 2 * DH2), F32),
                pltpu.VMEM((ctx_len, 2 * DH2), BF16), pltpu.VMEM((n_lat_keys, 2 * DH2), BF16),
                pltpu.VMEM((8, DH2), F32)]

    o_lat = pl.pallas_call(
        functools.partial(_diff_attn_kernel, lam_init, pl.cdiv(seq, ATT_TK)),
        grid=(batch, nh, nq),
        in_specs=[pl.BlockSpec((ATT_TQ, DH2), lambda b, h, i: (b * nq + i, h)),
                  kc_spec, vc_spec,
                  pl.BlockSpec((seq, DH2), lambda b, h, i: (b, nh + h)),
                  pl.BlockSpec((seq, DH2), lambda b, h, i: (b, 2 * nh + h))] + small,
        out_specs=pl.BlockSpec((ATT_TQ, DH2), lambda b, h, i: (b * nq + i, h)),
        out_shape=jax.ShapeDtypeStruct((rows.n_lat, D), BF16),
        scratch_shapes=scratch(ATT_TQ, seq),
        compiler_params=_params(3),
        name="diff_attn_latent",
    )(qkv, qkv, qkv, qkv, qkv, lam_p, sub_g)

    def ctx_kernel(q_ref, kc_ref, vc_ref, lam_ref, subg_ref, o_ref, *scratch_refs):
        _diff_attn_kernel(lam_init, 0, q_ref, kc_ref, vc_ref, None, None, lam_ref, subg_ref, o_ref, *scratch_refs)

    o_ctx = pl.pallas_call(
        ctx_kernel,
        grid=(batch, nh),
        in_specs=[pl.BlockSpec((ctx_len, DH2), lambda b, h: (ctx0 + b, h)), kc_spec, vc_spec] + small,
        out_specs=pl.BlockSpec((ctx_len, DH2), lambda b, h: (b, h)),
        out_shape=jax.ShapeDtypeStruct((rows.n_ctx, D), BF16),
        scratch_shapes=scratch(ctx_len, 16),
        compiler_params=_params(2),
        name="diff_attn_context",
    )(qkv, qkv, qkv, lam_p, sub_g)
    return o_lat, o_ctx


def _gmlp_kernel(meta_ref, h_ref, prev_mod_ref, slot_ref, ys_hbm, mod_ref, win_ref, bin_ref, vg_ref, ws_ref, bs_ref,
                 wout_ref, o_ref, u_ref, v_ref, t_ref, local_ref, sem):
    x = _combined_rows(meta_ref, h_ref, prev_mod_ref, slot_ref, ys_hbm, local_ref, sem)
    m = mod_ref[0]
    a = _modnorm(x, *_mix_mod(m)).astype(BF16)
    n_half = CM_WIDTH // CN

    def in_proj(c):
        lo = c * CN
        return _gelu_tanh(_dot(a, win_ref[:, lo:lo + CN]) + bin_ref[:, lo:lo + CN])

    ssq = jnp.zeros((TM, 1), F32)
    for c in range(n_half):
        z = in_proj(n_half + c)
        v_ref[:, c * CN:(c + 1) * CN] = z
        ssq = ssq + jnp.sum(z * z, axis=-1, keepdims=True)
    for c in range(n_half):
        u_ref[:, c * CN:(c + 1) * CN] = in_proj(c)
    inv = lax.rsqrt(ssq * (1.0 / CM_WIDTH) + EPS)

    def mix(g):
        lo = g * CM_GW
        vn = (v_ref[:, lo:lo + CM_GW] * inv * vg_ref[:, lo:lo + CM_GW]).astype(BF16)
        return [_dot(ws_ref[g], vn[r * CHUNK:(r + 1) * CHUNK]) + bs_ref[:, g:g + 1] for r in range(TM // CHUNK)]

    out = None
    nxt = mix(0)
    for g in range(CM_GROUPS):
        lo = g * CM_GW
        sv = nxt
        if g + 1 < CM_GROUPS:
            nxt = mix(g + 1)
        for r in range(TM // CHUNK):
            r0 = r * CHUNK
            t_ref[r0:r0 + CHUNK, lo:lo + CM_GW] = (u_ref[r0:r0 + CHUNK, lo:lo + CM_GW] * sv[r]).astype(BF16)
        part = _dot(t_ref[:, lo:lo + CM_GW], wout_ref[lo:lo + CM_GW, :])
        out = part if out is None else out + part
    o_ref[...] = x + m[2:3] * out


def _gmlp(rows, pending, modg, w_in, b_in, v_g, w_s, b_s, w_out):
    h, prev_modg, slots, ys, meta = pending
    return pl.pallas_call(
        _gmlp_kernel,
        grid_spec=pltpu.PrefetchScalarGridSpec(
            num_scalar_prefetch=1,
            grid=(rows.all_tiles,),
            in_specs=[rows.row_spec(D), rows.mod_spec(), rows.row_spec(ROUTE_W), pl.BlockSpec(memory_space=pl.ANY),
                      rows.mod_spec(), _resident((D, 2 * CM_WIDTH)), _resident((1, 2 * CM_WIDTH)),
                      _resident((1, CM_WIDTH)), _resident((CM_GROUPS, CHUNK, CHUNK)), _resident((CHUNK, CM_GROUPS)),
                      _resident((CM_WIDTH, D))],
            out_specs=rows.row_spec(D),
            scratch_shapes=[pltpu.VMEM((TM, CM_WIDTH), F32), pltpu.VMEM((TM, CM_WIDTH), F32),
                            pltpu.VMEM((TM, CM_WIDTH), BF16)] + _combine_scratch(),
        ),
        out_shape=jax.ShapeDtypeStruct((rows.n_all, D), F32),
        compiler_params=_params(1),
        name="moe_combine_gmlp",
    )(meta, h, prev_modg, slots, ys, modg, w_in, b_in, v_g, w_s, b_s, w_out)


SWA_BAND = SWA_TQ + 2 * SWA_WINDOW


SWA_PAIR = 2 * HEAD_DIM
N_KV_VARIANTS = 2 * SWA_KV_HEADS


def _swa_kernel(seq, sink_ref, q_ref, kc_ref, vc_ref, kl_ref, vl_ref, o_ref, kcv_ref, vcv_ref, klv_ref, vlv_ref,
                kn_ref):
    qi = pl.program_id(1)
    lane = lax.broadcasted_iota(jnp.int32, (1, SWA_PAIR), 1)
    lo_half = lane < HEAD_DIM

    def max_norm_sq(x):
        xf = x.astype(F32)
        return jnp.max(jnp.sum(xf * xf, axis=-1, keepdims=True), axis=0, keepdims=True)

    @pl.when(qi == 0)
    def _per_batch_setup():
        kn = jnp.zeros((1, 1), F32)
        for src_ref, dst_ref in ((kc_ref, kcv_ref), (vc_ref, vcv_ref), (kl_ref, klv_ref), (vl_ref, vlv_ref)):
            for half in range(SWA_KV_HEADS // 2):
                x = src_ref[:, half * SWA_PAIR:(half + 1) * SWA_PAIR].astype(F32)
                xr = pltpu.roll(x, HEAD_DIM, 1)
                j0, j1 = 2 * half, 2 * half + 1
                dst_ref[2 * j0] = jnp.where(lo_half, x, 0.0).astype(BF16)
                dst_ref[2 * j0 + 1] = jnp.where(lo_half, 0.0, xr).astype(BF16)
                dst_ref[2 * j1] = jnp.where(lo_half, xr, 0.0).astype(BF16)
                dst_ref[2 * j1 + 1] = jnp.where(lo_half, 0.0, x).astype(BF16)
                if src_ref is kc_ref or src_ref is kl_ref:
                    kn = jnp.maximum(kn, max_norm_sq(x))
        kn_ref[...] = jnp.broadcast_to(kn, kn_ref.shape)

    q0 = qi * SWA_TQ
    start = jnp.clip(q0 - SWA_WINDOW, 0, seq - SWA_BAND)
    start = pl.multiple_of(start, SWA_WINDOW)
    qpos = q0 + lax.broadcasted_iota(jnp.int32, (SWA_TQ, SWA_BAND), 0)
    kpos = start + lax.broadcasted_iota(jnp.int32, (SWA_TQ, SWA_BAND), 1)
    in_band = jnp.abs(qpos - kpos) <= SWA_WINDOW
    nt = (((1,), (1,)), ((), ()))

    def unshifted():
        in_band2 = jnp.concatenate([in_band, in_band], axis=0)
        top = lax.broadcasted_iota(jnp.int32, (2 * SWA_TQ, 1), 0) < SWA_TQ
        def logits(v):
            lo = (v // 2) * SWA_GROUP * HEAD_DIM
            q2 = jnp.concatenate([q_ref[:, lo:lo + SWA_PAIR], q_ref[:, lo + SWA_PAIR:lo + 2 * SWA_PAIR]], axis=0)
            return (lax.dot_general(q2, kcv_ref[v], nt, preferred_element_type=F32),
                    lax.dot_general(q2, klv_ref[v, pl.ds(start, SWA_BAND), :], nt, preferred_element_type=F32))

        nxt = logits(0)
        for j in range(SWA_KV_HEADS):
            lo = j * SWA_GROUP * HEAD_DIM
            out = jnp.zeros((2 * SWA_TQ, SWA_PAIR), F32)
            for var in range(2):
                v = 2 * j + var
                s_c, s_b = nxt
                if v + 1 < N_KV_VARIANTS:
                    nxt = logits(v + 1)
                p_c = jnp.exp2(s_c)
                p_b = jnp.where(in_band2, jnp.exp2(s_b), 0.0)
                hq = j * SWA_GROUP + var
                sink = jnp.where(top, sink_ref[hq] * LOG2E, sink_ref[hq + 2] * LOG2E)
                denom = (jnp.sum(p_c, axis=-1, keepdims=True) + jnp.sum(p_b, axis=-1, keepdims=True)
                         + jnp.exp2(sink))
                o = (_dot(p_c.astype(BF16), vcv_ref[v])
                     + _dot(p_b.astype(BF16), vlv_ref[v, pl.ds(start, SWA_BAND), :]))
                out = out + o / denom
            o_ref[:, lo:lo + SWA_PAIR] = out[:SWA_TQ].astype(BF16)
            o_ref[:, lo + SWA_PAIR:lo + 2 * SWA_PAIR] = out[SWA_TQ:].astype(BF16)

    def shifted():
        kb = kl_ref[pl.ds(start, SWA_BAND), :]
        vb = vl_ref[pl.ds(start, SWA_BAND), :]
        kc = kc_ref[...]
        vc = vc_ref[...]
        for j in range(SWA_KV_HEADS):
            kj = slice(j * HEAD_DIM, (j + 1) * HEAD_DIM)
            kbj, vbj, kcj, vcj = kb[:, kj], vb[:, kj], kc[:, kj], vc[:, kj]
            outs = []
            for g in range(SWA_GROUP):
                hq = j * SWA_GROUP + g
                qh = q_ref[:, hq * HEAD_DIM:(hq + 1) * HEAD_DIM]
                s_c = lax.dot_general(qh, kcj, nt, preferred_element_type=F32)
                s_b = lax.dot_general(qh, kbj, nt, preferred_element_type=F32)
                s_b = jnp.where(in_band, s_b, -jnp.inf)
                sink = sink_ref[hq] * LOG2E
                mx = jnp.maximum(jnp.maximum(jnp.max(s_c, axis=-1, keepdims=True),
                                             jnp.max(s_b, axis=-1, keepdims=True)), sink)
                p_c = jnp.exp2(s_c - mx)
                p_b = jnp.exp2(s_b - mx)
                denom = (jnp.sum(p_c, axis=-1, keepdims=True) + jnp.sum(p_b, axis=-1, keepdims=True)
                         + jnp.exp2(sink - mx))
                o = _dot(p_c.astype(BF16), vcj) + _dot(p_b.astype(BF16), vbj)
                outs.append(o / denom)
            lo = j * SWA_GROUP * HEAD_DIM
            o_ref[:, lo:lo + SWA_GROUP * HEAD_DIM] = jnp.concatenate(outs, axis=-1).astype(BF16)

    qn = jnp.zeros((1, 1), F32)
    for pair in range(SWA_Q_HEADS // 2):
        qn = jnp.maximum(qn, max_norm_sq(q_ref[:, pair * SWA_PAIR:(pair + 1) * SWA_PAIR]))
    max_sink = sink_ref[0]
    for hq in range(1, SWA_Q_HEADS):
        max_sink = jnp.maximum(max_sink, sink_ref[hq])
    small_logits = jnp.logical_and((qn * kn_ref[0:1, 0:1])[0, 0] <= MAX_UNSHIFTED_LOGIT ** 2,
                                   max_sink * LOG2E <= MAX_UNSHIFTED_LOGIT)
    pl.when(small_logits)(unshifted)
    pl.when(jnp.logical_not(small_logits))(shifted)


def _swa(rows, qkv, sink):
    batch, seq, ctx_len = rows.batch, rows.seq, rows.ctx_len
    nq = seq // SWA_TQ
    kvw = SWA_KV_HEADS * HEAD_DIM
    k_col = D // kvw
    ctx0 = rows.n_lat // ctx_len
    return pl.pallas_call(
        functools.partial(_swa_kernel, seq),
        grid=(batch, nq),
        in_specs=[pl.BlockSpec(memory_space=pltpu.SMEM),
                  pl.BlockSpec((SWA_TQ, D), lambda b, i: (b * nq + i, 0)),
                  pl.BlockSpec((ctx_len, kvw), lambda b, i: (ctx0 + b, k_col)),
                  pl.BlockSpec((ctx_len, kvw), lambda b, i: (ctx0 + b, k_col + 1)),
                  pl.BlockSpec((seq, kvw), lambda b, i: (b, k_col)),
                  pl.BlockSpec((seq, kvw), lambda b, i: (b, k_col + 1))],
        out_specs=pl.BlockSpec((SWA_TQ, D), lambda b, i: (b * nq + i, 0)),
        out_shape=jax.ShapeDtypeStruct((rows.n_lat, D), BF16),
        scratch_shapes=[pltpu.VMEM((N_KV_VARIANTS, ctx_len, SWA_PAIR), BF16),
                        pltpu.VMEM((N_KV_VARIANTS, ctx_len, SWA_PAIR), BF16),
                        pltpu.VMEM((N_KV_VARIANTS, seq, SWA_PAIR), BF16),
                        pltpu.VMEM((N_KV_VARIANTS, seq, SWA_PAIR), BF16),
                        pltpu.VMEM((8, SWA_PAIR), F32)],
        compiler_params=_params(2),
        name="swa_attn",
    )(sink, qkv, qkv, qkv, qkv, qkv)


def _proj_route_kernel(lat_tiles, h_ref, mod_ref, zl_ref, zc_ref, w_ref, wr_ref, br_ref, hm_ref, a_ref, route_ref,
                       cnt_ref):
    i = pl.program_id(0)
    m = mod_ref[0]

    def project(z_ref):
        hm_ref[...] = h_ref[...] + m[2:3] * _dot(z_ref[...], w_ref[...])

    pl.when(i < lat_tiles)(lambda: project(zl_ref))
    pl.when(i >= lat_tiles)(lambda: project(zc_ref))
    a = _modnorm(hm_ref[...], *_ffn_mod(m))
    a_hi = a.astype(BF16)
    a_ref[...] = a_hi
    a_lo = (a - a_hi.astype(F32)).astype(BF16)
    hi_part = _dot(a_hi, wr_ref[...])
    logits = (hi_part[:, :ROUTE_W] + hi_part[:, ROUTE_W:]) + _dot(a_lo, wr_ref[:, :ROUTE_W]) + br_ref[...]
    lane = lax.broadcasted_iota(jnp.int32, (TM, ROUTE_W), 1)
    m1 = jnp.max(logits, axis=-1, keepdims=True)
    i1 = jnp.min(jnp.where(logits == m1, lane, ROUTE_W), axis=-1, keepdims=True)
    rest = jnp.where(lane == i1, -jnp.inf, logits)
    m2 = jnp.max(rest, axis=-1, keepdims=True)
    i2 = jnp.min(jnp.where(rest == m2, lane, ROUTE_W), axis=-1, keepdims=True)
    e2 = jnp.exp(m2 - m1)
    gate1 = 1.0 / (1.0 + e2)
    gate2 = e2 / (1.0 + e2)
    rec = jnp.where(lane == 0, i1.astype(F32), 0.0)
    rec = jnp.where(lane == 1, i2.astype(F32), rec)
    rec = jnp.where(lane == 2, gate1, rec)
    route_ref[...] = jnp.where(lane == 3, gate2, rec)
    chosen = jnp.where((lane == i1) | (lane == i2), 1.0, 0.0)
    cnt_ref[0] = jnp.broadcast_to(jnp.sum(chosen, axis=0, keepdims=True), (8, ROUTE_W))


def _proj_route(rows, n_tiles, h, modg, z_lat, z_ctx, w_out, w_r, b_r):
    lat_tiles = rows.lat_tiles
    pad = ROUTE_W - N_EXPERTS
    w_pad = jnp.pad(w_r, ((0, 0), (0, pad)))
    w_hi = w_pad.astype(BF16)
    w_lo = (w_pad - w_hi.astype(F32)).astype(BF16)
    w_split = jnp.concatenate([w_hi, w_lo], axis=1)
    b_pad = jnp.concatenate([b_r.astype(F32), jnp.full((pad,), -1e30, F32)]).reshape(1, ROUTE_W)
    return pl.pallas_call(
        functools.partial(_proj_route_kernel, lat_tiles),
        grid=(n_tiles,),
        in_specs=[rows.row_spec(D), rows.mod_spec(),
                  pl.BlockSpec((TM, D), lambda i: (jnp.minimum(i, lat_tiles - 1), 0)),
                  pl.BlockSpec((TM, D), lambda i: (jnp.maximum(i - lat_tiles, 0), 0)),
                  _resident((D, D)), _resident((D, 2 * ROUTE_W)), _resident((1, ROUTE_W))],
        out_specs=[rows.row_spec(D), rows.row_spec(D), rows.row_spec(ROUTE_W),
                   pl.BlockSpec((1, 8, ROUTE_W), lambda i: (i, 0, 0))],
        out_shape=[jax.ShapeDtypeStruct((n_tiles * TM, D), F32), jax.ShapeDtypeStruct((n_tiles * TM, D), BF16),
                   jax.ShapeDtypeStruct((n_tiles * TM, ROUTE_W), F32),
                   jax.ShapeDtypeStruct((n_tiles, 8, ROUTE_W), F32)],
        compiler_params=_params(1),
        name="out_proj_router",
    )(h, modg, z_lat, z_ctx, w_out, w_split, b_pad)


SEG_ALIGN = 16
SEG_BITS = tuple(1 << b for b in range(TM.bit_length() - 1, SEG_ALIGN.bit_length() - 2, -1))
LROWS = -(-(2 * TM + N_EXPERTS * (SEG_ALIGN - 1)) // 128) * 128
XS_W = D + 128
META_W = 3 * N_EXPERTS


def _segment_copies(meta_ref, tile, local_ref, slots_hbm, sem, to_slots):
    for e in range(N_EXPERTS):
        loff = meta_ref[tile * META_W + e]
        goff = meta_ref[tile * META_W + N_EXPERTS + e]
        seg = meta_ref[tile * META_W + 2 * N_EXPERTS + e]
        for bit in SEG_BITS:
            done = seg & ~(2 * bit - 1)
            lo = local_ref.at[pl.ds(pl.multiple_of(loff + done, SEG_ALIGN), bit)]
            gl = slots_hbm.at[pl.ds(pl.multiple_of(goff + done, SEG_ALIGN), bit)]
            copy = pltpu.make_async_copy(lo, gl, sem) if to_slots else pltpu.make_async_copy(gl, lo, sem)
            yield (seg & bit) != 0, copy


def _start_segment_copies(*args):
    for cond, copy in _segment_copies(*args):
        pl.when(cond)(copy.start)


def _wait_segment_copies(*args):
    for cond, copy in _segment_copies(*args):
        pl.when(cond)(copy.wait)


def _split3(x):
    hi = x.astype(BF16).astype(F32)
    mid = (x - hi).astype(BF16).astype(F32)
    lo = ((x - hi) - mid).astype(BF16).astype(F32)
    return hi, mid, lo


def _dispatch_kernel(n_blocks, meta_ref, fill_ref, a_ref, route_ref, tri_ref, xs_hbm, slot_ref, local_ref, zero_ref,
                     sem, zsem):
    i = pl.program_id(0)

    @pl.when(i == 0)
    def _zero_unwritten_slots():
        zero_ref[...] = jnp.zeros(zero_ref.shape, BF16)
        tails = []
        for e in range(N_EXPERTS):
            off, length = fill_ref[e], fill_ref[N_EXPERTS + e]
            for bit in (b for b in SEG_BITS if b < MOE_RB):
                done = length & ~(2 * bit - 1)
                dst = xs_hbm.at[pl.ds(pl.multiple_of(off + done, SEG_ALIGN), bit)]
                tails.append(((length & bit) != 0, pltpu.make_async_copy(zero_ref.at[pl.ds(0, bit)], dst, zsem)))
        for cond, copy in tails:
            pl.when(cond)(copy.start)

        def block_copy(blk):
            dst = xs_hbm.at[pl.ds(pl.multiple_of(blk * MOE_RB, MOE_RB), MOE_RB)]
            return pltpu.make_async_copy(zero_ref, dst, zsem)

        n_used = fill_ref[2 * N_EXPERTS]
        lax.fori_loop(n_used, n_blocks, lambda blk, c: (block_copy(blk).start(), c)[1], 0)
        for cond, copy in tails:
            pl.when(cond)(copy.wait)
        lax.fori_loop(n_used, n_blocks, lambda blk, c: (block_copy(blk).wait(), c)[1], 0)

    route = route_ref[...]
    lane = lax.broadcasted_iota(jnp.int32, (TM, ROUTE_W), 1)
    pick1 = lane == route[:, 0:1].astype(jnp.int32)
    pick2 = lane == route[:, 1:2].astype(jnp.int32)
    chosen = jnp.where(pick1 | pick2, 1.0, 0.0).astype(BF16)
    rank = _dot(tri_ref[...], chosen)
    lane1 = lax.broadcasted_iota(jnp.int32, (1, ROUTE_W), 1)
    loff = jnp.zeros((1, ROUTE_W), F32)
    for e in range(N_EXPERTS):
        loff = jnp.where(lane1 == e, meta_ref[i * META_W + e].astype(F32), loff)
    place = rank + loff
    slot1 = jnp.sum(jnp.where(pick1, place, 0.0), axis=-1, keepdims=True)
    slot2 = jnp.sum(jnp.where(pick2, place, 0.0), axis=-1, keepdims=True)
    slot_ref[...] = jnp.where(lane == 0, slot1, jnp.where(lane == 1, slot2, 0.0))

    row1 = jnp.transpose(jnp.broadcast_to(slot1, (TM, 128)))[0:1, :]
    row2 = jnp.transpose(jnp.broadcast_to(slot2, (TM, 128)))[0:1, :]
    lrow = lax.broadcasted_iota(jnp.int32, (LROWS, TM), 0).astype(F32)
    p1 = jnp.where(lrow == row1, 1.0, 0.0).astype(BF16)
    p2 = jnp.where(lrow == row2, 1.0, 0.0).astype(BF16)
    local_ref[:, :D] = _dot(p1 + p2, a_ref[...]).astype(BF16)
    gates = []
    for col in (2, 3):
        pieces = _split3(route[:, col:col + 1])
        g = jnp.zeros((TM, ROUTE_W), F32)
        for k, piece in enumerate(pieces):
            g = jnp.where(lane == k, piece, g)
        gates.append(g.astype(BF16))
    local_ref[:, D:] = (_dot(p1, gates[0]) + _dot(p2, gates[1])).astype(BF16)
    _start_segment_copies(meta_ref, i, local_ref, xs_hbm, sem, True)
    _wait_segment_copies(meta_ref, i, local_ref, xs_hbm, sem, True)


def _dispatch(n_tiles, n_blocks, a, route, meta, fill):
    tri = (jnp.arange(TM)[:, None] > jnp.arange(TM)[None, :]).astype(BF16)
    return pl.pallas_call(
        functools.partial(_dispatch_kernel, n_blocks),
        grid_spec=pltpu.PrefetchScalarGridSpec(
            num_scalar_prefetch=2,
            grid=(n_tiles,),
            in_specs=[pl.BlockSpec((TM, D), lambda i, *_: (i, 0)),
                      pl.BlockSpec((TM, ROUTE_W), lambda i, *_: (i, 0)),
                      _resident((TM, TM))],
            out_specs=[pl.BlockSpec(memory_space=pl.ANY), pl.BlockSpec((TM, ROUTE_W), lambda i, *_: (i, 0))],
            scratch_shapes=[pltpu.VMEM((LROWS, XS_W), BF16), pltpu.VMEM((MOE_RB, XS_W), BF16),
                            pltpu.SemaphoreType.DMA(()), pltpu.SemaphoreType.DMA(())],
        ),
        out_shape=[jax.ShapeDtypeStruct((n_blocks * MOE_RB, XS_W), BF16),
                   jax.ShapeDtypeStruct((n_tiles * TM, ROUTE_W), F32)],
        compiler_params=_params(1),
        name="moe_dispatch",
    )(meta, fill, a, route, tri)


def _expert_kernel(be_ref, fill_ref, xs_ref, w1_ref, w3_ref, w2_ref, ys_ref, t_ref):
    del be_ref
    used = pl.program_id(0) < fill_ref[2 * N_EXPERTS]

    @pl.when(used)
    def _ffn():
        y = _swiglu(xs_ref[:, :D], w1_ref, w3_ref, w2_ref, t_ref)
        gate = jnp.sum(xs_ref[:, D:].astype(F32), axis=-1, keepdims=True)
        ys_ref[...] = (gate * y).astype(BF16)

    @pl.when(jnp.logical_not(used))
    def _idle():
        ys_ref[...] = jnp.zeros(ys_ref.shape, BF16)


def _experts(xs, blk_exp, fill, layer, w1, w3, w2):
    def slab(d0, d1):
        return pl.BlockSpec((None, None, d0, d1), lambda i, be, fill: (layer, be[i], 0, 0),
                            pipeline_mode=pl.Buffered(1))

    n_blocks = blk_exp.shape[0]
    last_used = lambda i, fill: jnp.minimum(i, fill[2 * N_EXPERTS] - 1)
    return pl.pallas_call(
        _expert_kernel,
        grid_spec=pltpu.PrefetchScalarGridSpec(
            num_scalar_prefetch=2,
            grid=(n_blocks,),
            in_specs=[pl.BlockSpec((MOE_RB, XS_W), lambda i, be, fill: (last_used(i, fill), 0)),
                      slab(D, FFN_DIM), slab(D, FFN_DIM), slab(FFN_DIM, D)],
            out_specs=pl.BlockSpec((MOE_RB, D), lambda i, be, fill: (i, 0)),
            scratch_shapes=[pltpu.VMEM((MOE_RB, FFN_DIM), BF16)],
        ),
        out_shape=jax.ShapeDtypeStruct((n_blocks * MOE_RB, D), BF16),
        compiler_params=_params(1),
        name="moe_experts",
    )(blk_exp, fill, xs, w1, w3, w2)


def _combine_scratch():
    return [pltpu.VMEM((2, LROWS, D), BF16), pltpu.SemaphoreType.DMA((2,))]


def _combined_rows(meta_ref, h_ref, mod_ref, slot_ref, ys_hbm, local_ref, sem):
    i = pl.program_id(0)

    buf = i % 2

    def fetch(tile, b):
        return (meta_ref, tile, local_ref.at[b], ys_hbm, sem.at[b], False)

    @pl.when(i == 0)
    def _init():
        local_ref[...] = jnp.zeros(local_ref.shape, BF16)
        _start_segment_copies(*fetch(0, 0))

    @pl.when(i + 1 < pl.num_programs(0))
    def _prefetch():
        _start_segment_copies(*fetch(i + 1, 1 - buf))

    _wait_segment_copies(*fetch(i, buf))
    slots = slot_ref[...]
    lcol = lax.broadcasted_iota(jnp.int32, (TM, LROWS), 1).astype(F32)
    gather = jnp.where((lcol == slots[:, 0:1]) | (lcol == slots[:, 1:2]), 1.0, 0.0).astype(BF16)
    return h_ref[...] + mod_ref[0][5:6] * _dot(gather, local_ref[buf])


def _combine_kernel(meta_ref, h_ref, mod_ref, slot_ref, ys_hbm, o_ref, local_ref, sem):
    o_ref[...] = _combined_rows(meta_ref, h_ref, mod_ref, slot_ref, ys_hbm, local_ref, sem)


def _combine(rows, n_tiles, h, modg, slots, ys, meta):
    tps, batch = rows.tiles_per_seq, rows.batch
    return pl.pallas_call(
        _combine_kernel,
        grid_spec=pltpu.PrefetchScalarGridSpec(
            num_scalar_prefetch=1,
            grid=(n_tiles,),
            in_specs=[pl.BlockSpec((TM, D), lambda i, m: (i, 0)),
                      pl.BlockSpec((1, 8, D), lambda i, m: (jnp.minimum(i // tps, batch), 0, 0)),
                      pl.BlockSpec((TM, ROUTE_W), lambda i, m: (i, 0)),
                      pl.BlockSpec(memory_space=pl.ANY)],
            out_specs=pl.BlockSpec((TM, D), lambda i, m: (i, 0)),
            scratch_shapes=_combine_scratch(),
        ),
        out_shape=jax.ShapeDtypeStruct((n_tiles * TM, D), F32),
        compiler_params=_params(1),
        name="moe_combine",
    )(meta, h, modg, slots, ys)


def _mixer_out_moe(rows, n_tiles, h, modg, z_lat, z_ctx, w_out, w_r, b_r, layer, w1, w3, w2, defer_combine):
    h, a, route, cnt = _proj_route(rows, n_tiles, h, modg, z_lat, z_ctx, w_out, w_r, b_r)
    cnt = cnt[:, 0, :N_EXPERTS].astype(jnp.int32)
    seg = (cnt + SEG_ALIGN - 1) // SEG_ALIGN * SEG_ALIGN
    loff = jnp.cumsum(seg, axis=1) - seg
    total = jnp.sum(seg, axis=0)
    region = (total + MOE_RB - 1) // MOE_RB * MOE_RB
    region_end = jnp.cumsum(region)
    region_start = region_end - region
    goff = region_start[None, :] + jnp.cumsum(seg, axis=0) - seg
    meta = jnp.concatenate([loff, goff, seg], axis=1).reshape(-1)
    n_blocks = pl.cdiv(2 * n_tiles * TM + n_tiles * N_EXPERTS * (SEG_ALIGN - 1), MOE_RB) + N_EXPERTS
    n_used = region_end[-1] // MOE_RB
    fill = jnp.concatenate([region_start + total, region - total, n_used[None]]).astype(jnp.int32)
    blk = jnp.minimum(jnp.arange(n_blocks, dtype=jnp.int32), n_used - 1)
    blk_exp = jnp.sum((region_end[None, :] <= (blk * MOE_RB)[:, None]).astype(jnp.int32), axis=1)
    blk_exp = jnp.minimum(blk_exp, N_EXPERTS - 1)
    xs, slots = _dispatch(n_tiles, n_blocks, a, route, meta, fill)
    ys = _experts(xs, blk_exp, fill, layer, w1, w3, w2)
    if defer_combine:
        return h, modg, slots, ys, meta
    return _combine(rows, n_tiles, h, modg, slots, ys, meta)


def kernel(x, c, ctx, c_ctx, ada_w, ada_b, norm_mix_g, norm_ffn_g, sc_in_w, sc_conv_w, sc_out_w, da_qkv_w, da_out_w,
           da_q_norm_g, da_k_norm_g, da_lambda, da_sub_norm_g, cm_in_w, cm_in_b, cm_v_norm_g, cm_ws, cm_bs, cm_out_w,
           sw_qkv_w, sw_out_w, sw_q_norm_g, sw_k_norm_g, sw_sink, ffn_w1, ffn_w3, ffn_w2, moe_router_w, moe_router_b,
           moe_w1, moe_w3, moe_w2):
    batch, seq, _ = x.shape
    ctx_len = ctx.shape[1]
    depth = ada_w.shape[0]
    assert depth == 4 and batch + 1 <= 16
    rows = _Rows(batch, seq, ctx_len)

    cvec = jnp.concatenate([c, c_ctx[None, :], jnp.zeros((16 - batch - 1, D), F32)], axis=0)
    mod = _ada_all(cvec, ada_w, ada_b)[:, :batch + 1].reshape(depth, batch + 1, 6, D)
    gains = jnp.stack([norm_mix_g, norm_ffn_g], axis=1)[:, None]
    modg = jnp.concatenate([mod, jnp.broadcast_to(gains, (depth, batch + 1, 2, D))], axis=2)

    bf = lambda w: w.astype(BF16)
    x2, ctx2 = x.reshape(-1, D), ctx.reshape(-1, D)

    bg, y = _conv_in(rows, x2, ctx2, modg[0], bf(sc_in_w[0]))
    conv_w = jnp.pad(sc_conv_w[0], ((0, 5), (0, 0)))
    h = _conv_out(rows, x2, ctx2, modg[0], bg, y, conv_w, bf(sc_out_w[0]))
    h = _ffn(rows, rows.all_tiles, h, modg[0], 0, ffn_w1, ffn_w3, ffn_w2)

    nq_chunks = D // CN
    qkv = _qkv(rows, h, modg[1], bf(da_qkv_w[0]), da_q_norm_g[0], da_k_norm_g[0], nq_chunks, nq_chunks)
    lam_init = 0.8 - 0.6 * math.exp(-0.3 * 1)
    o_lat, o_ctx = _diff_attn(rows, qkv, da_lambda[0], da_sub_norm_g[0], lam_init)
    pending = _mixer_out_moe(rows, rows.all_tiles, h, modg[1], o_lat, o_ctx, bf(da_out_w[0]), moe_router_w[0],
                             moe_router_b[0], 0, moe_w1, moe_w3, moe_w2, defer_combine=True)

    h = _gmlp(rows, pending, modg[2], bf(cm_in_w[0]), cm_in_b[0].reshape(1, -1), cm_v_norm_g[0].reshape(1, -1),
              bf(cm_ws[0]), cm_bs[0].T, bf(cm_out_w[0]))
    h = _ffn(rows, rows.all_tiles, h, modg[2], 1, ffn_w1, ffn_w3, ffn_w2)

    kv_chunks = SWA_KV_HEADS * HEAD_DIM // CN
    qkv = _qkv(rows, h, modg[3], bf(sw_qkv_w[0]), sw_q_norm_g[0], sw_k_norm_g[0], nq_chunks, kv_chunks)
    o = _swa(rows, qkv, sw_sink[0])
    h = _mixer_out_moe(rows, rows.lat_tiles, h, modg[3], o, o, bf(sw_out_w[0]), moe_router_w[1], moe_router_b[1], 1,
                       moe_w1, moe_w3, moe_w2, defer_combine=False)
    return h.reshape(batch, seq, D)
```

```python
import functools
import math

import jax
import jax.numpy as jnp
from jax import lax
from jax.experimental import pallas as pl
from jax.experimental.pallas import tpu as pltpu

D = 1024
HEAD_DIM = 64
GRID_W = 64
ROPE_HALF = HEAD_DIM // 2
ROPE_BASE = 10000.0
DIFF_HEADS = D // (2 * HEAD_DIM)
SWA_Q_HEADS = D // HEAD_DIM
SWA_KV_HEADS = 4
SWA_GROUP = SWA_Q_HEADS // SWA_KV_HEADS
SWA_WINDOW = 128
CHUNK = 128
CM_WIDTH = 2 * D
CM_GROUPS = 8
CM_GW = CM_WIDTH // CM_GROUPS
FFN_DIM = 2816
N_EXPERTS = 8
EPS = 1e-6

F32 = jnp.float32
BF16 = jnp.bfloat16
HIGHEST = lax.Precision.HIGHEST
LOG2E = math.log2(math.e)

TM = 512
CN = 256
MOE_RB = 512
ATT_TQ = 512
ATT_TK = 1024
SWA_TQ = 256
ROUTE_W = 128
MAX_UNSHIFTED_LOGIT = 60.0
VMEM_LIMIT = 56 << 20


def _params(n_grid):
    return pltpu.CompilerParams(dimension_semantics=("arbitrary",) * n_grid, vmem_limit_bytes=VMEM_LIMIT)


def _resident(shape):
    zeros = (0,) * len(shape)
    return pl.BlockSpec(shape, lambda *_: zeros, pipeline_mode=pl.Buffered(1))


def _sigmoid(x):
    return 1.0 / (1.0 + jnp.exp(-x))


def _gelu_tanh(x):
    return 0.5 * x * (1.0 + jnp.tanh(math.sqrt(2.0 / math.pi) * (x + 0.044715 * (x * x * x))))


def _modnorm(x, g, shift, scale):
    y = x * lax.rsqrt(jnp.mean(x * x, axis=-1, keepdims=True) + EPS)
    return (y * g) * (1.0 + scale) + shift


def _dot(a, b):
    return jnp.dot(a, b, preferred_element_type=F32)


def _mix_mod(m):
    return m[6:7], m[0:1], m[1:2]


def _ffn_mod(m):
    return m[7:8], m[3:4], m[4:5]


def _ada_kernel(c_ref, w_ref, b_ref, o_ref):
    c = c_ref[...]
    s = c * _sigmoid(c)
    o_ref[0] = jnp.dot(s, w_ref[0], precision=HIGHEST, preferred_element_type=F32) + b_ref[0]


def _ada_all(cvec, ada_w, ada_b):
    depth, _, n = ada_w.shape
    tn = 1536
    rows = cvec.shape[0]
    return pl.pallas_call(
        _ada_kernel,
        grid=(depth, n // tn),
        in_specs=[pl.BlockSpec((rows, D), lambda l, j: (0, 0)),
                  pl.BlockSpec((1, D, tn), lambda l, j: (l, 0, j)),
                  pl.BlockSpec((1, 1, tn), lambda l, j: (l, 0, j))],
        out_specs=pl.BlockSpec((1, rows, tn), lambda l, j: (l, 0, j)),
        out_shape=jax.ShapeDtypeStruct((depth, rows, n), F32),
        compiler_params=_params(2),
        name="adaln",
    )(cvec, ada_w, ada_b.reshape(depth, 1, n))


class _Rows:
    def __init__(self, batch, seq, ctx_len):
        self.batch, self.seq, self.ctx_len = batch, seq, ctx_len
        self.n_lat = batch * seq
        self.n_ctx = batch * ctx_len
        self.n_all = self.n_lat + self.n_ctx
        assert seq % TM == 0 and self.n_ctx % TM == 0 and TM % ctx_len == 0
        self.lat_tiles = self.n_lat // TM
        self.all_tiles = self.n_all // TM
        self.tiles_per_seq = seq // TM

    def mod_spec(self):
        tps, batch = self.tiles_per_seq, self.batch
        return pl.BlockSpec((1, 8, D), lambda i, *_: (jnp.minimum(i // tps, batch), 0, 0))

    def row_spec(self, width):
        return pl.BlockSpec((TM, width), lambda i, *_: (i, 0))

    def split_specs(self, width):
        lat_tiles = self.lat_tiles
        return [pl.BlockSpec((TM, width), lambda i, *_: (jnp.minimum(i, lat_tiles - 1), 0)),
                pl.BlockSpec((TM, width), lambda i, *_: (jnp.maximum(i - lat_tiles, 0), 0))]


def _conv_in_kernel(lat_tiles, x_ref, ctx_ref, mod_ref, w_ref, bg_ref, y_ref):
    h = jnp.where(pl.program_id(0) < lat_tiles, x_ref[...], ctx_ref[...])
    a = _modnorm(h, *_mix_mod(mod_ref[0])).astype(BF16)
    for c in range(D // CN):
        lo = c * CN
        bg_ref[:, lo:lo + CN] = _dot(a, w_ref[:, lo:lo + CN]).astype(BF16)
        cg = _dot(a, w_ref[:, D + lo:D + lo + CN])
        xv = _dot(a, w_ref[:, 2 * D + lo:2 * D + lo + CN])
        y_ref[:, lo:lo + CN] = (cg * xv).astype(BF16)


def _conv_in(rows, x, ctx, modg, w_in):
    out = jax.ShapeDtypeStruct((rows.n_all, D), BF16)
    return pl.pallas_call(
        functools.partial(_conv_in_kernel, rows.lat_tiles),
        grid=(rows.all_tiles,),
        in_specs=rows.split_specs(D) + [rows.mod_spec(), _resident((D, 3 * D))],
        out_specs=[rows.row_spec(D), rows.row_spec(D)],
        out_shape=[out, out],
        compiler_params=_params(1),
        name="conv_in",
    )(x, ctx, modg, w_in)


HALO = 16


def _conv_out_kernel(n_lat, seq, ctx_len, x_ref, ctx_ref, mod_ref, bg_ref, y_ref, yp_ref, yn_ref, cw_ref, w_ref,
                     o_ref):
    i = pl.program_id(0)
    h = jnp.where(i * TM < n_lat, x_ref[...], ctx_ref[...])
    m = mod_ref[0]
    row = lax.broadcasted_iota(jnp.int32, (TM, 1), 0)
    grow = row + i * TM
    seq_len = jnp.where(grow < n_lat, seq, ctx_len)
    pos = grow & (seq_len - 1)
    out = None
    for c in range(D // CN):
        cols = slice(c * CN, (c + 1) * CN)
        y = y_ref[:, cols].astype(F32)
        prev_row = yp_ref[HALO - 1:HALO, cols].astype(F32)
        next_row = yn_ref[0:1, cols].astype(F32)
        y_m1 = jnp.where(row == 0, prev_row, pltpu.roll(y, 1, 0))
        y_m1 = jnp.where(pos == 0, 0.0, y_m1)
        y_p1 = jnp.where(row == TM - 1, next_row, pltpu.roll(y, TM - 1, 0))
        y_p1 = jnp.where(pos == seq_len - 1, 0.0, y_p1)
        conv = cw_ref[0:1, cols] * y_m1 + cw_ref[1:2, cols] * y + cw_ref[2:3, cols] * y_p1
        z = (bg_ref[:, cols].astype(F32) * conv).astype(BF16)
        part = _dot(z, w_ref[cols, :])
        out = part if out is None else out + part
    o_ref[...] = h + m[2:3] * out


def _conv_out(rows, x, ctx, modg, bg, y, conv_w, w_out):
    hb = TM // HALO
    last = rows.n_all // HALO - 1
    return pl.pallas_call(
        functools.partial(_conv_out_kernel, rows.n_lat, rows.seq, rows.ctx_len),
        grid=(rows.all_tiles,),
        in_specs=rows.split_specs(D) + [rows.mod_spec(), rows.row_spec(D), rows.row_spec(D),
                  pl.BlockSpec((HALO, D), lambda i: (jnp.maximum(i * hb - 1, 0), 0)),
                  pl.BlockSpec((HALO, D), lambda i: (jnp.minimum((i + 1) * hb, last), 0)),
                  _resident((8, D)), _resident((D, D))],
        out_specs=rows.row_spec(D),
        out_shape=jax.ShapeDtypeStruct((rows.n_all, D), F32),
        compiler_params=_params(1),
        name="conv_out",
    )(x, ctx, modg, bg, y, y, y, conv_w, w_out)


W2_KC = 768


def _swiglu(a, w1_ref, w3_ref, w2_ref, t_ref):
    for c in range(FFN_DIM // CN):
        lo = c * CN
        h1 = _dot(a, w1_ref[:, lo:lo + CN].astype(BF16))
        h3 = _dot(a, w3_ref[:, lo:lo + CN].astype(BF16))
        t_ref[:, lo:lo + CN] = (h1 * _sigmoid(h1) * h3).astype(BF16)
    out = None
    for lo in range(0, FFN_DIM, W2_KC):
        hi = min(lo + W2_KC, FFN_DIM)
        part = _dot(t_ref[:, lo:hi], w2_ref[lo:hi, :].astype(BF16))
        out = part if out is None else out + part
    return out


def _ffn_kernel(h_ref, mod_ref, w1_ref, w3_ref, w2_ref, o_ref, t_ref):
    m = mod_ref[0]
    x = h_ref[...]
    a = _modnorm(x, *_ffn_mod(m)).astype(BF16)
    o_ref[...] = x + m[5:6] * _swiglu(a, w1_ref, w3_ref, w2_ref, t_ref)


def _ffn(rows, n_tiles, h, modg, layer, w1, w3, w2):
    def slab(d0, d1):
        return pl.BlockSpec((None, d0, d1), lambda i: (layer, 0, 0), pipeline_mode=pl.Buffered(1))

    return pl.pallas_call(
        _ffn_kernel,
        grid=(n_tiles,),
        in_specs=[rows.row_spec(D), rows.mod_spec(), slab(D, FFN_DIM), slab(D, FFN_DIM), slab(FFN_DIM, D)],
        out_specs=rows.row_spec(D),
        out_shape=jax.ShapeDtypeStruct((n_tiles * TM, D), F32),
        scratch_shapes=[pltpu.VMEM((TM, FFN_DIM), BF16)],
        compiler_params=_params(1),
        name="ffn_dense",
    )(h, modg, w1, w3, w2)


def _norm_rope(x, gmat_ref, cos, sin):
    ms = _dot((x * x).astype(BF16), gmat_ref[...]) * (1.0 / HEAD_DIM)
    xn = x * lax.rsqrt(ms + EPS)
    lane = lax.broadcasted_iota(jnp.int32, (1, CN), 1)
    first_half = (lane & (ROPE_HALF - 1)) < (ROPE_HALF // 2)
    partner = jnp.where(first_half, pltpu.roll(xn, CN - ROPE_HALF // 2, 1), pltpu.roll(xn, ROPE_HALF // 2, 1))
    return xn * cos + partner * sin


def _qkv_kernel(n_qk_chunks, n_q_chunks, h_ref, mod_ref, w_ref, gmat_ref, qcos_ref, qsin_ref, kcos_ref, ksin_ref,
                o_ref):
    a = _modnorm(h_ref[...], *_mix_mod(mod_ref[0])).astype(BF16)
    n_chunks = w_ref.shape[1] // CN

    def project(c):
        return _dot(a, w_ref[:, c * CN:(c + 1) * CN])

    nxt = project(0)
    for c in range(n_chunks):
        acc = nxt
        if c + 1 < n_chunks:
            nxt = project(c + 1)
        if c < n_q_chunks:
            acc = _norm_rope(acc, gmat_ref, qcos_ref[...], qsin_ref[...])
        elif c < n_qk_chunks:
            acc = _norm_rope(acc, gmat_ref, kcos_ref[...], ksin_ref[...])
        o_ref[:, c * CN:(c + 1) * CN] = acc.astype(BF16)


def _rope_tables(rows, gain, scale):
    seq = rows.seq
    pos = jnp.arange(seq)
    n_freq = ROPE_HALF // 2
    inv = ROPE_BASE ** (-jnp.arange(n_freq, dtype=F32) / n_freq)
    ang_r = (pos // GRID_W).astype(F32)[:, None] * inv
    ang_c = (pos % GRID_W).astype(F32)[:, None] * inv
    cos = jnp.concatenate([jnp.cos(ang_r)] * 2 + [jnp.cos(ang_c)] * 2, axis=-1)
    sin = jnp.concatenate([-jnp.sin(ang_r), jnp.sin(ang_r), -jnp.sin(ang_c), jnp.sin(ang_c)], axis=-1)
    cos = jnp.concatenate([cos, jnp.ones((TM, HEAD_DIM), F32)], axis=0)
    sin = jnp.concatenate([sin, jnp.zeros((TM, HEAD_DIM), F32)], axis=0)
    dim = jnp.arange(HEAD_DIM)
    partner = jnp.where((dim % ROPE_HALF) < n_freq, dim + n_freq, dim - n_freq)
    g = gain.astype(F32) * scale
    reps = CN // HEAD_DIM
    return jnp.tile(cos * g[None, :], (1, reps)), jnp.tile(sin * g[partner][None, :], (1, reps))


def _qkv(rows, h, modg, w, q_gain, k_gain, n_q_chunks, n_k_chunks):
    width = w.shape[1]
    head = jnp.arange(CN) // HEAD_DIM
    gmat = (head[:, None] == head[None, :]).astype(BF16)
    tables = _rope_tables(rows, q_gain, LOG2E * HEAD_DIM ** -0.5) + _rope_tables(rows, k_gain, 1.0)
    tps, lat_tiles = rows.tiles_per_seq, rows.lat_tiles
    tab_spec = pl.BlockSpec((TM, CN), lambda i: (jnp.where(i < lat_tiles, i % tps, tps), 0))
    return pl.pallas_call(
        functools.partial(_qkv_kernel, n_q_chunks + n_k_chunks, n_q_chunks),
        grid=(rows.all_tiles,),
        in_specs=[rows.row_spec(D), rows.mod_spec(), _resident((D, width)), _resident((CN, CN))] + [tab_spec] * 4,
        out_specs=rows.row_spec(width),
        out_shape=jax.ShapeDtypeStruct((rows.n_all, width), BF16),
        compiler_params=_params(1),
        name="qkv_proj",
    )(h, modg, w, gmat, *tables)


DH2 = 2 * HEAD_DIM


def _diff_attn_kernel(lam_init, n_lat_chunks, q_ref, kc_ref, vc_ref, kl_ref, vl_ref, lam_ref, subg_ref, small_ref,
                      o_ref, m_ref, acc_ref, vca_ref, vla_ref):
    small_logits = small_ref[0] != 0

    def online_setup():
        vca_ref[:, :DH2] = vc_ref[...]
        vca_ref[:, DH2:] = jnp.ones((vc_ref.shape[0], DH2), BF16)
        if n_lat_chunks:
            vla_ref[:, :DH2] = vl_ref[...]
            vla_ref[:, DH2:] = jnp.ones((vl_ref.shape[0], DH2), BF16)

    first_query_tile = pl.program_id(2) == 0 if n_lat_chunks else True
    pl.when(jnp.logical_and(first_query_tile, jnp.logical_not(small_logits)))(online_setup)

    q = q_ref[...]
    lane = lax.broadcasted_iota(jnp.int32, (1, DH2), 1)
    zero = jnp.zeros((), BF16)
    qs = (jnp.where(lane < HEAD_DIM, q, zero), jnp.where(lane >= HEAD_DIM, q, zero))
    nt = (((1,), (1,)), ((), ()))

    def over_keys(update):
        update(kc_ref[...], vca_ref[...])
        if n_lat_chunks:
            tk = kl_ref.shape[0] // n_lat_chunks

            def body(c, carry):
                start = pl.multiple_of(c * tk, tk)
                update(kl_ref[pl.ds(start, tk), :], vla_ref[pl.ds(start, tk), :])
                return carry
            lax.fori_loop(0, n_lat_chunks, body, 0)

    lp = lam_ref[...]
    lam = (jnp.exp(jnp.sum(lp[0:1] * lp[1:2], axis=-1, keepdims=True))
           - jnp.exp(jnp.sum(lp[2:3] * lp[3:4], axis=-1, keepdims=True)) + lam_init)

    def unshifted():
        n_sub = 2
        sub = q.shape[0] // n_sub

        def logits(r):
            out = []
            for mi in range(2):
                qm = qs[mi][r * sub:(r + 1) * sub]
                s_c = lax.dot_general(qm, kc_ref[...], nt, preferred_element_type=F32)
                s_l = lax.dot_general(qm, kl_ref[...], nt, preferred_element_type=F32) if n_lat_chunks else None
                out.append((s_c, s_l))
            return out

        all_logits = [logits(r) for r in range(n_sub)]
        for r in range(n_sub):
            p_ctx, p_lat, sums = [], [], []
            for s_c, s_l in all_logits[r]:
                e = jnp.exp2(s_c)
                total = jnp.sum(e, axis=-1, keepdims=True)
                p_ctx.append(e)
                if s_l is not None:
                    e = jnp.exp2(s_l)
                    total = total + jnp.sum(e, axis=-1, keepdims=True)
                    p_lat.append(e)
                sums.append(total)
            r0 = 1.0 / sums[0]
            r1 = lam / sums[1]

            def weights(p):
                return (p[0] * r0 - p[1] * r1).astype(BF16)

            o = _dot(weights(p_ctx), vc_ref[...])
            if n_lat_chunks:
                o = o + _dot(weights(p_lat), vl_ref[...])
            acc_ref[0, r * sub:(r + 1) * sub, :DH2] = o

    def update_online(k, va):
        reps = k.shape[0] // DH2
        for mi in range(2):
            s = lax.dot_general(qs[mi], k, nt, preferred_element_type=F32)
            m_old = m_ref[mi]
            m_new = jnp.maximum(m_old, jnp.max(s, axis=-1, keepdims=True))
            alpha = jnp.exp2(m_old - m_new)
            p = jnp.exp2(s - jnp.concatenate([m_new] * reps, axis=-1))
            acc_ref[mi] = jnp.concatenate([alpha, alpha], axis=-1) * acc_ref[mi] + _dot(p.astype(BF16), va)
            m_ref[mi] = m_new

    def online():
        m_ref[...] = jnp.full(m_ref.shape, -jnp.inf, F32)
        acc_ref[...] = jnp.zeros(acc_ref.shape, F32)
        over_keys(update_online)
        acc0 = acc_ref[0]
        acc1 = acc_ref[1]
        acc_ref[0, :, :DH2] = acc0[:, :DH2] / acc0[:, DH2:] - lam * (acc1[:, :DH2] / acc1[:, DH2:])

    pl.when(small_logits)(unshifted)
    pl.when(jnp.logical_not(small_logits))(online)

    o = acc_ref[0, :, :DH2]
    o = o * lax.rsqrt(jnp.mean(o * o, axis=-1, keepdims=True) + EPS) * subg_ref[...]
    o_ref[...] = (o * (1.0 - lam_init)).astype(BF16)


def _small_logits(q_gain, k_gain, sink=None):
    bound = 2 * HEAD_DIM * (LOG2E * HEAD_DIM ** -0.5) * jnp.max(jnp.abs(q_gain)) * jnp.max(jnp.abs(k_gain))
    ok = bound <= MAX_UNSHIFTED_LOGIT
    if sink is not None:
        ok = jnp.logical_and(ok, jnp.max(sink) * LOG2E <= MAX_UNSHIFTED_LOGIT)
    return ok.astype(jnp.int32).reshape(1)


def _diff_attn(rows, qkv, lam_p, sub_g, lam_init, small_logits):
    batch, seq, ctx_len = rows.batch, rows.seq, rows.ctx_len
    nq = seq // ATT_TQ
    nh = DIFF_HEADS
    ctx0 = rows.n_lat // ctx_len
    sub_g = sub_g.reshape(1, DH2)
    small = [_resident((4, HEAD_DIM)), _resident((1, DH2)), pl.BlockSpec(memory_space=pltpu.SMEM)]
    kc_spec = pl.BlockSpec((ctx_len, DH2), lambda b, h, *_: (ctx0 + b, nh + h))
    vc_spec = pl.BlockSpec((ctx_len, DH2), lambda b, h, *_: (ctx0 + b, 2 * nh + h))

    def scratch(tq, n_lat_keys):
        return [pltpu.VMEM((2, tq, DH2), F32), pltpu.VMEM((2, tq, 2 * DH2), F32),
                pltpu.VMEM((ctx_len, 2 * DH2), BF16), pltpu.VMEM((n_lat_keys, 2 * DH2), BF16)]

    o_lat = pl.pallas_call(
        functools.partial(_diff_attn_kernel, lam_init, pl.cdiv(seq, ATT_TK)),
        grid=(batch, nh, nq),
        in_specs=[pl.BlockSpec((ATT_TQ, DH2), lambda b, h, i: (b * nq + i, h)),
                  kc_spec, vc_spec,
                  pl.BlockSpec((seq, DH2), lambda b, h, i: (b, nh + h)),
                  pl.BlockSpec((seq, DH2), lambda b, h, i: (b, 2 * nh + h))] + small,
        out_specs=pl.BlockSpec((ATT_TQ, DH2), lambda b, h, i: (b * nq + i, h)),
        out_shape=jax.ShapeDtypeStruct((rows.n_lat, D), BF16),
        scratch_shapes=scratch(ATT_TQ, seq),
        compiler_params=_params(3),
        name="diff_attn_latent",
    )(qkv, qkv, qkv, qkv, qkv, lam_p, sub_g, small_logits)

    def ctx_kernel(q_ref, kc_ref, vc_ref, lam_ref, subg_ref, small_ref, o_ref, *scratch_refs):
        _diff_attn_kernel(lam_init, 0, q_ref, kc_ref, vc_ref, None, None, lam_ref, subg_ref, small_ref, o_ref,
                          *scratch_refs)

    o_ctx = pl.pallas_call(
        ctx_kernel,
        grid=(batch, nh),
        in_specs=[pl.BlockSpec((ctx_len, DH2), lambda b, h: (ctx0 + b, h)), kc_spec, vc_spec] + small,
        out_specs=pl.BlockSpec((ctx_len, DH2), lambda b, h: (b, h)),
        out_shape=jax.ShapeDtypeStruct((rows.n_ctx, D), BF16),
        scratch_shapes=scratch(ctx_len, 16),
        compiler_params=_params(2),
        name="diff_attn_context",
    )(qkv, qkv, qkv, lam_p, sub_g, small_logits)
    return o_lat, o_ctx


def _gmlp_kernel(meta_ref, h_ref, prev_mod_ref, slot_ref, ys_hbm, mod_ref, win_ref, bin_ref, vg_ref, ws_ref, bs_ref,
                 wout_ref, o_ref, u_ref, v_ref, t_ref, local_ref, sem):
    x = _combined_rows(meta_ref, h_ref, prev_mod_ref, slot_ref, ys_hbm, local_ref, sem)
    m = mod_ref[0]
    a = _modnorm(x, *_mix_mod(m)).astype(BF16)
    n_half = CM_WIDTH // CN

    def in_proj(c):
        lo = c * CN
        return _gelu_tanh(_dot(a, win_ref[:, lo:lo + CN]) + bin_ref[:, lo:lo + CN])

    ssq = jnp.zeros((TM, 1), F32)
    for c in range(n_half):
        z = in_proj(n_half + c)
        v_ref[:, c * CN:(c + 1) * CN] = z
        ssq = ssq + jnp.sum(z * z, axis=-1, keepdims=True)
    for c in range(n_half):
        u_ref[:, c * CN:(c + 1) * CN] = in_proj(c)
    inv = lax.rsqrt(ssq * (1.0 / CM_WIDTH) + EPS)

    def mix(g):
        lo = g * CM_GW
        vn = (v_ref[:, lo:lo + CM_GW] * inv * vg_ref[:, lo:lo + CM_GW]).astype(BF16)
        return [_dot(ws_ref[g], vn[r * CHUNK:(r + 1) * CHUNK]) + bs_ref[:, g:g + 1] for r in range(TM // CHUNK)]

    out = None
    nxt = mix(0)
    for g in range(CM_GROUPS):
        lo = g * CM_GW
        sv = nxt
        if g + 1 < CM_GROUPS:
            nxt = mix(g + 1)
        for r in range(TM // CHUNK):
            r0 = r * CHUNK
            t_ref[r0:r0 + CHUNK, lo:lo + CM_GW] = (u_ref[r0:r0 + CHUNK, lo:lo + CM_GW] * sv[r]).astype(BF16)
        part = _dot(t_ref[:, lo:lo + CM_GW], wout_ref[lo:lo + CM_GW, :])
        out = part if out is None else out + part
    o_ref[...] = x + m[2:3] * out


def _gmlp(rows, pending, modg, w_in, b_in, v_g, w_s, b_s, w_out):
    h, prev_modg, slots, ys, meta = pending
    return pl.pallas_call(
        _gmlp_kernel,
        grid_spec=pltpu.PrefetchScalarGridSpec(
            num_scalar_prefetch=1,
            grid=(rows.all_tiles,),
            in_specs=[rows.row_spec(D), rows.mod_spec(), rows.row_spec(ROUTE_W), pl.BlockSpec(memory_space=pl.ANY),
                      rows.mod_spec(), _resident((D, 2 * CM_WIDTH)), _resident((1, 2 * CM_WIDTH)),
                      _resident((1, CM_WIDTH)), _resident((CM_GROUPS, CHUNK, CHUNK)), _resident((CHUNK, CM_GROUPS)),
                      _resident((CM_WIDTH, D))],
            out_specs=rows.row_spec(D),
            scratch_shapes=[pltpu.VMEM((TM, CM_WIDTH), F32), pltpu.VMEM((TM, CM_WIDTH), F32),
                            pltpu.VMEM((TM, CM_WIDTH), BF16)] + _combine_scratch(),
        ),
        out_shape=jax.ShapeDtypeStruct((rows.n_all, D), F32),
        compiler_params=_params(1),
        name="moe_combine_gmlp",
    )(meta, h, prev_modg, slots, ys, modg, w_in, b_in, v_g, w_s, b_s, w_out)


SWA_BAND = SWA_TQ + 2 * SWA_WINDOW


SWA_PAIR = 2 * HEAD_DIM
N_KV_VARIANTS = 2 * SWA_KV_HEADS


def _swa_kernel(seq, sink_ref, small_ref, q_ref, kc_ref, vc_ref, kl_ref, vl_ref, o_ref, kcv_ref, vcv_ref, klv_ref,
                vlv_ref):
    qi = pl.program_id(1)
    lane = lax.broadcasted_iota(jnp.int32, (1, SWA_PAIR), 1)
    lo_half = lane < HEAD_DIM

    small_logits = small_ref[0] != 0

    @pl.when(jnp.logical_and(qi == 0, small_logits))
    def _per_batch_setup():
        for src_ref, dst_ref in ((kc_ref, kcv_ref), (vc_ref, vcv_ref), (kl_ref, klv_ref), (vl_ref, vlv_ref)):
            for half in range(SWA_KV_HEADS // 2):
                x = src_ref[:, half * SWA_PAIR:(half + 1) * SWA_PAIR].astype(F32)
                xr = pltpu.roll(x, HEAD_DIM, 1)
                j0, j1 = 2 * half, 2 * half + 1
                dst_ref[2 * j0] = jnp.where(lo_half, x, 0.0).astype(BF16)
                dst_ref[2 * j0 + 1] = jnp.where(lo_half, 0.0, xr).astype(BF16)
                dst_ref[2 * j1] = jnp.where(lo_half, xr, 0.0).astype(BF16)
                dst_ref[2 * j1 + 1] = jnp.where(lo_half, 0.0, x).astype(BF16)

    q0 = qi * SWA_TQ
    start = jnp.clip(q0 - SWA_WINDOW, 0, seq - SWA_BAND)
    start = pl.multiple_of(start, SWA_WINDOW)
    qpos = q0 + lax.broadcasted_iota(jnp.int32, (SWA_TQ, SWA_BAND), 0)
    kpos = start + lax.broadcasted_iota(jnp.int32, (SWA_TQ, SWA_BAND), 1)
    in_band = jnp.abs(qpos - kpos) <= SWA_WINDOW
    nt = (((1,), (1,)), ((), ()))

    def unshifted():
        in_band2 = jnp.concatenate([in_band, in_band], axis=0)
        top = lax.broadcasted_iota(jnp.int32, (2 * SWA_TQ, 1), 0) < SWA_TQ
        def logits(v):
            lo = (v // 2) * SWA_GROUP * HEAD_DIM
            q2 = jnp.concatenate([q_ref[:, lo:lo + SWA_PAIR], q_ref[:, lo + SWA_PAIR:lo + 2 * SWA_PAIR]], axis=0)
            return (lax.dot_general(q2, kcv_ref[v], nt, preferred_element_type=F32),
                    lax.dot_general(q2, klv_ref[v, pl.ds(start, SWA_BAND), :], nt, preferred_element_type=F32))

        nxt = logits(0)
        for j in range(SWA_KV_HEADS):
            lo = j * SWA_GROUP * HEAD_DIM
            out = jnp.zeros((2 * SWA_TQ, SWA_PAIR), F32)
            for var in range(2):
                v = 2 * j + var
                s_c, s_b = nxt
                if v + 1 < N_KV_VARIANTS:
                    nxt = logits(v + 1)
                p_c = jnp.exp2(s_c)
                p_b = jnp.where(in_band2, jnp.exp2(s_b), 0.0)
                hq = j * SWA_GROUP + var
                sink = jnp.where(top, sink_ref[hq] * LOG2E, sink_ref[hq + 2] * LOG2E)
                denom = (jnp.sum(p_c, axis=-1, keepdims=True) + jnp.sum(p_b, axis=-1, keepdims=True)
                         + jnp.exp2(sink))
                o = (_dot(p_c.astype(BF16), vcv_ref[v])
                     + _dot(p_b.astype(BF16), vlv_ref[v, pl.ds(start, SWA_BAND), :]))
                out = out + o / denom
            o_ref[:, lo:lo + SWA_PAIR] = out[:SWA_TQ].astype(BF16)
            o_ref[:, lo + SWA_PAIR:lo + 2 * SWA_PAIR] = out[SWA_TQ:].astype(BF16)

    def shifted():
        kb = kl_ref[pl.ds(start, SWA_BAND), :]
        vb = vl_ref[pl.ds(start, SWA_BAND), :]
        kc = kc_ref[...]
        vc = vc_ref[...]
        for j in range(SWA_KV_HEADS):
            kj = slice(j * HEAD_DIM, (j + 1) * HEAD_DIM)
            kbj, vbj, kcj, vcj = kb[:, kj], vb[:, kj], kc[:, kj], vc[:, kj]
            outs = []
            for g in range(SWA_GROUP):
                hq = j * SWA_GROUP + g
                qh = q_ref[:, hq * HEAD_DIM:(hq + 1) * HEAD_DIM]
                s_c = lax.dot_general(qh, kcj, nt, preferred_element_type=F32)
                s_b = lax.dot_general(qh, kbj, nt, preferred_element_type=F32)
                s_b = jnp.where(in_band, s_b, -jnp.inf)
                sink = sink_ref[hq] * LOG2E
                mx = jnp.maximum(jnp.maximum(jnp.max(s_c, axis=-1, keepdims=True),
                                             jnp.max(s_b, axis=-1, keepdims=True)), sink)
                p_c = jnp.exp2(s_c - mx)
                p_b = jnp.exp2(s_b - mx)
                denom = (jnp.sum(p_c, axis=-1, keepdims=True) + jnp.sum(p_b, axis=-1, keepdims=True)
                         + jnp.exp2(sink - mx))
                o = _dot(p_c.astype(BF16), vcj) + _dot(p_b.astype(BF16), vbj)
                outs.append(o / denom)
            lo = j * SWA_GROUP * HEAD_DIM
            o_ref[:, lo:lo + SWA_GROUP * HEAD_DIM] = jnp.concatenate(outs, axis=-1).astype(BF16)

    pl.when(small_logits)(unshifted)
    pl.when(jnp.logical_not(small_logits))(shifted)


def _swa(rows, qkv, sink, small_logits):
    batch, seq, ctx_len = rows.batch, rows.seq, rows.ctx_len
    nq = seq // SWA_TQ
    kvw = SWA_KV_HEADS * HEAD_DIM
    k_col = D // kvw
    ctx0 = rows.n_lat // ctx_len
    return pl.pallas_call(
        functools.partial(_swa_kernel, seq),
        grid=(batch, nq),
        in_specs=[pl.BlockSpec(memory_space=pltpu.SMEM), pl.BlockSpec(memory_space=pltpu.SMEM),
                  pl.BlockSpec((SWA_TQ, D), lambda b, i: (b * nq + i, 0)),
                  pl.BlockSpec((ctx_len, kvw), lambda b, i: (ctx0 + b, k_col)),
                  pl.BlockSpec((ctx_len, kvw), lambda b, i: (ctx0 + b, k_col + 1)),
                  pl.BlockSpec((seq, kvw), lambda b, i: (b, k_col)),
                  pl.BlockSpec((seq, kvw), lambda b, i: (b, k_col + 1))],
        out_specs=pl.BlockSpec((SWA_TQ, D), lambda b, i: (b * nq + i, 0)),
        out_shape=jax.ShapeDtypeStruct((rows.n_lat, D), BF16),
        scratch_shapes=[pltpu.VMEM((N_KV_VARIANTS, ctx_len, SWA_PAIR), BF16),
                        pltpu.VMEM((N_KV_VARIANTS, ctx_len, SWA_PAIR), BF16),
                        pltpu.VMEM((N_KV_VARIANTS, seq, SWA_PAIR), BF16),
                        pltpu.VMEM((N_KV_VARIANTS, seq, SWA_PAIR), BF16)],
        compiler_params=_params(2),
        name="swa_attn",
    )(sink, small_logits, qkv, qkv, qkv, qkv, qkv)


def _proj_route_kernel(lat_tiles, h_ref, mod_ref, zl_ref, zc_ref, w_ref, wr_ref, br_ref, hm_ref, a_ref, route_ref,
                       cnt_ref):
    i = pl.program_id(0)
    m = mod_ref[0]

    def project(z_ref):
        hm_ref[...] = h_ref[...] + m[2:3] * _dot(z_ref[...], w_ref[...])

    pl.when(i < lat_tiles)(lambda: project(zl_ref))
    pl.when(i >= lat_tiles)(lambda: project(zc_ref))
    a = _modnorm(hm_ref[...], *_ffn_mod(m))
    a_hi = a.astype(BF16)
    a_ref[...] = a_hi
    a_lo = (a - a_hi.astype(F32)).astype(BF16)
    hi_part = _dot(a_hi, wr_ref[...])
    logits = (hi_part[:, :ROUTE_W] + hi_part[:, ROUTE_W:]) + _dot(a_lo, wr_ref[:, :ROUTE_W]) + br_ref[...]
    lane = lax.broadcasted_iota(jnp.int32, (TM, ROUTE_W), 1)
    m1 = jnp.max(logits, axis=-1, keepdims=True)
    i1 = jnp.min(jnp.where(logits == m1, lane, ROUTE_W), axis=-1, keepdims=True)
    rest = jnp.where(lane == i1, -jnp.inf, logits)
    m2 = jnp.max(rest, axis=-1, keepdims=True)
    i2 = jnp.min(jnp.where(rest == m2, lane, ROUTE_W), axis=-1, keepdims=True)
    e2 = jnp.exp(m2 - m1)
    gate1 = 1.0 / (1.0 + e2)
    gate2 = e2 / (1.0 + e2)
    rec = jnp.where(lane == 0, i1.astype(F32), 0.0)
    rec = jnp.where(lane == 1, i2.astype(F32), rec)
    rec = jnp.where(lane == 2, gate1, rec)
    route_ref[...] = jnp.where(lane == 3, gate2, rec)
    chosen = jnp.where((lane == i1) | (lane == i2), 1.0, 0.0)
    cnt_ref[0] = jnp.broadcast_to(jnp.sum(chosen, axis=0, keepdims=True), (8, ROUTE_W))


def _proj_route(rows, n_tiles, h, modg, z_lat, z_ctx, w_out, w_r, b_r):
    lat_tiles = rows.lat_tiles
    pad = ROUTE_W - N_EXPERTS
    w_pad = jnp.pad(w_r, ((0, 0), (0, pad)))
    w_hi = w_pad.astype(BF16)
    w_lo = (w_pad - w_hi.astype(F32)).astype(BF16)
    w_split = jnp.concatenate([w_hi, w_lo], axis=1)
    b_pad = jnp.concatenate([b_r.astype(F32), jnp.full((pad,), -1e30, F32)]).reshape(1, ROUTE_W)
    return pl.pallas_call(
        functools.partial(_proj_route_kernel, lat_tiles),
        grid=(n_tiles,),
        in_specs=[rows.row_spec(D), rows.mod_spec(),
                  pl.BlockSpec((TM, D), lambda i: (jnp.minimum(i, lat_tiles - 1), 0)),
                  pl.BlockSpec((TM, D), lambda i: (jnp.maximum(i - lat_tiles, 0), 0)),
                  _resident((D, D)), _resident((D, 2 * ROUTE_W)), _resident((1, ROUTE_W))],
        out_specs=[rows.row_spec(D), rows.row_spec(D), rows.row_spec(ROUTE_W),
                   pl.BlockSpec((1, 8, ROUTE_W), lambda i: (i, 0, 0))],
        out_shape=[jax.ShapeDtypeStruct((n_tiles * TM, D), F32), jax.ShapeDtypeStruct((n_tiles * TM, D), BF16),
                   jax.ShapeDtypeStruct((n_tiles * TM, ROUTE_W), F32),
                   jax.ShapeDtypeStruct((n_tiles, 8, ROUTE_W), F32)],
        compiler_params=_params(1),
        name="out_proj_router",
    )(h, modg, z_lat, z_ctx, w_out, w_split, b_pad)


SEG_ALIGN = 16
SEG_BITS = tuple(1 << b for b in range(TM.bit_length() - 1, SEG_ALIGN.bit_length() - 2, -1))
LROWS = -(-(2 * TM + N_EXPERTS * (SEG_ALIGN - 1)) // 128) * 128
XS_W = D + 128
META_W = 3 * N_EXPERTS


def _segment_copies(meta_ref, tile, local_ref, slots_hbm, sem, to_slots):
    for e in range(N_EXPERTS):
        loff = meta_ref[tile * META_W + e]
        goff = meta_ref[tile * META_W + N_EXPERTS + e]
        seg = meta_ref[tile * META_W + 2 * N_EXPERTS + e]
        for bit in SEG_BITS:
            done = seg & ~(2 * bit - 1)
            lo = local_ref.at[pl.ds(pl.multiple_of(loff + done, SEG_ALIGN), bit)]
            gl = slots_hbm.at[pl.ds(pl.multiple_of(goff + done, SEG_ALIGN), bit)]
            copy = pltpu.make_async_copy(lo, gl, sem) if to_slots else pltpu.make_async_copy(gl, lo, sem)
            yield (seg & bit) != 0, copy


def _start_segment_copies(*args):
    for cond, copy in _segment_copies(*args):
        pl.when(cond)(copy.start)


def _wait_segment_copies(*args):
    for cond, copy in _segment_copies(*args):
        pl.when(cond)(copy.wait)


def _split3(x):
    hi = x.astype(BF16).astype(F32)
    mid = (x - hi).astype(BF16).astype(F32)
    lo = ((x - hi) - mid).astype(BF16).astype(F32)
    return hi, mid, lo


def _dispatch_kernel(n_blocks, meta_ref, fill_ref, a_ref, route_ref, tri_ref, xs_hbm, slot_ref, local_ref, zero_ref,
                     sem, zsem):
    i = pl.program_id(0)

    @pl.when(i == 0)
    def _zero_unwritten_slots():
        zero_ref[...] = jnp.zeros(zero_ref.shape, BF16)
        tails = []
        for e in range(N_EXPERTS):
            off, length = fill_ref[e], fill_ref[N_EXPERTS + e]
            for bit in (b for b in SEG_BITS if b < MOE_RB):
                done = length & ~(2 * bit - 1)
                dst = xs_hbm.at[pl.ds(pl.multiple_of(off + done, SEG_ALIGN), bit)]
                tails.append(((length & bit) != 0, pltpu.make_async_copy(zero_ref.at[pl.ds(0, bit)], dst, zsem)))
        for cond, copy in tails:
            pl.when(cond)(copy.start)

        def block_copy(blk):
            dst = xs_hbm.at[pl.ds(pl.multiple_of(blk * MOE_RB, MOE_RB), MOE_RB)]
            return pltpu.make_async_copy(zero_ref, dst, zsem)

        n_used = fill_ref[2 * N_EXPERTS]
        lax.fori_loop(n_used, n_blocks, lambda blk, c: (block_copy(blk).start(), c)[1], 0)
        for cond, copy in tails:
            pl.when(cond)(copy.wait)
        lax.fori_loop(n_used, n_blocks, lambda blk, c: (block_copy(blk).wait(), c)[1], 0)

    route = route_ref[...]
    lane = lax.broadcasted_iota(jnp.int32, (TM, ROUTE_W), 1)
    pick1 = lane == route[:, 0:1].astype(jnp.int32)
    pick2 = lane == route[:, 1:2].astype(jnp.int32)
    chosen = jnp.where(pick1 | pick2, 1.0, 0.0).astype(BF16)
    rank = _dot(tri_ref[...], chosen)
    lane1 = lax.broadcasted_iota(jnp.int32, (1, ROUTE_W), 1)
    loff = jnp.zeros((1, ROUTE_W), F32)
    for e in range(N_EXPERTS):
        loff = jnp.where(lane1 == e, meta_ref[i * META_W + e].astype(F32), loff)
    place = rank + loff
    slot1 = jnp.sum(jnp.where(pick1, place, 0.0), axis=-1, keepdims=True)
    slot2 = jnp.sum(jnp.where(pick2, place, 0.0), axis=-1, keepdims=True)
    slot_ref[...] = jnp.where(lane == 0, slot1, jnp.where(lane == 1, slot2, 0.0))

    row1 = jnp.transpose(jnp.broadcast_to(slot1, (TM, 128)))[0:1, :]
    row2 = jnp.transpose(jnp.broadcast_to(slot2, (TM, 128)))[0:1, :]
    lrow = lax.broadcasted_iota(jnp.int32, (LROWS, TM), 0).astype(F32)
    p1 = jnp.where(lrow == row1, 1.0, 0.0).astype(BF16)
    p2 = jnp.where(lrow == row2, 1.0, 0.0).astype(BF16)
    local_ref[:, :D] = _dot(p1 + p2, a_ref[...]).astype(BF16)
    gates = []
    for col in (2, 3):
        pieces = _split3(route[:, col:col + 1])
        g = jnp.zeros((TM, ROUTE_W), F32)
        for k, piece in enumerate(pieces):
            g = jnp.where(lane == k, piece, g)
        gates.append(g.astype(BF16))
    local_ref[:, D:] = (_dot(p1, gates[0]) + _dot(p2, gates[1])).astype(BF16)
    _start_segment_copies(meta_ref, i, local_ref, xs_hbm, sem, True)
    _wait_segment_copies(meta_ref, i, local_ref, xs_hbm, sem, True)


def _dispatch(n_tiles, n_blocks, a, route, meta, fill):
    tri = (jnp.arange(TM)[:, None] > jnp.arange(TM)[None, :]).astype(BF16)
    return pl.pallas_call(
        functools.partial(_dispatch_kernel, n_blocks),
        grid_spec=pltpu.PrefetchScalarGridSpec(
            num_scalar_prefetch=2,
            grid=(n_tiles,),
            in_specs=[pl.BlockSpec((TM, D), lambda i, *_: (i, 0)),
                      pl.BlockSpec((TM, ROUTE_W), lambda i, *_: (i, 0)),
                      _resident((TM, TM))],
            out_specs=[pl.BlockSpec(memory_space=pl.ANY), pl.BlockSpec((TM, ROUTE_W), lambda i, *_: (i, 0))],
            scratch_shapes=[pltpu.VMEM((LROWS, XS_W), BF16), pltpu.VMEM((MOE_RB, XS_W), BF16),
                            pltpu.SemaphoreType.DMA(()), pltpu.SemaphoreType.DMA(())],
        ),
        out_shape=[jax.ShapeDtypeStruct((n_blocks * MOE_RB, XS_W), BF16),
                   jax.ShapeDtypeStruct((n_tiles * TM, ROUTE_W), F32)],
        compiler_params=_params(1),
        name="moe_dispatch",
    )(meta, fill, a, route, tri)


def _expert_kernel(be_ref, fill_ref, xs_ref, w1_ref, w3_ref, w2_ref, ys_ref, t_ref):
    del be_ref
    used = pl.program_id(0) < fill_ref[2 * N_EXPERTS]

    @pl.when(used)
    def _ffn():
        y = _swiglu(xs_ref[:, :D], w1_ref, w3_ref, w2_ref, t_ref)
        gate = jnp.sum(xs_ref[:, D:].astype(F32), axis=-1, keepdims=True)
        ys_ref[...] = (gate * y).astype(BF16)

    @pl.when(jnp.logical_not(used))
    def _idle():
        ys_ref[...] = jnp.zeros(ys_ref.shape, BF16)


def _experts(xs, blk_exp, fill, layer, w1, w3, w2):
    def slab(d0, d1):
        return pl.BlockSpec((None, None, d0, d1), lambda i, be, fill: (layer, be[i], 0, 0),
                            pipeline_mode=pl.Buffered(1))

    n_blocks = blk_exp.shape[0]
    last_used = lambda i, fill: jnp.minimum(i, fill[2 * N_EXPERTS] - 1)
    return pl.pallas_call(
        _expert_kernel,
        grid_spec=pltpu.PrefetchScalarGridSpec(
            num_scalar_prefetch=2,
            grid=(n_blocks,),
            in_specs=[pl.BlockSpec((MOE_RB, XS_W), lambda i, be, fill: (last_used(i, fill), 0)),
                      slab(D, FFN_DIM), slab(D, FFN_DIM), slab(FFN_DIM, D)],
            out_specs=pl.BlockSpec((MOE_RB, D), lambda i, be, fill: (i, 0)),
            scratch_shapes=[pltpu.VMEM((MOE_RB, FFN_DIM), BF16)],
        ),
        out_shape=jax.ShapeDtypeStruct((n_blocks * MOE_RB, D), BF16),
        compiler_params=_params(1),
        name="moe_experts",
    )(blk_exp, fill, xs, w1, w3, w2)


def _combine_scratch():
    return [pltpu.VMEM((2, LROWS, D), BF16), pltpu.SemaphoreType.DMA((2,))]


def _combined_rows(meta_ref, h_ref, mod_ref, slot_ref, ys_hbm, local_ref, sem):
    i = pl.program_id(0)

    buf = i % 2

    def fetch(tile, b):
        return (meta_ref, tile, local_ref.at[b], ys_hbm, sem.at[b], False)

    @pl.when(i == 0)
    def _init():
        local_ref[...] = jnp.zeros(local_ref.shape, BF16)
        _start_segment_copies(*fetch(0, 0))

    @pl.when(i + 1 < pl.num_programs(0))
    def _prefetch():
        _start_segment_copies(*fetch(i + 1, 1 - buf))

    _wait_segment_copies(*fetch(i, buf))
    slots = slot_ref[...]
    lcol = lax.broadcasted_iota(jnp.int32, (TM, LROWS), 1).astype(F32)
    gather = jnp.where((lcol == slots[:, 0:1]) | (lcol == slots[:, 1:2]), 1.0, 0.0).astype(BF16)
    return h_ref[...] + mod_ref[0][5:6] * _dot(gather, local_ref[buf])


def _combine_kernel(meta_ref, h_ref, mod_ref, slot_ref, ys_hbm, o_ref, local_ref, sem):
    o_ref[...] = _combined_rows(meta_ref, h_ref, mod_ref, slot_ref, ys_hbm, local_ref, sem)


def _combine(rows, n_tiles, h, modg, slots, ys, meta):
    tps, batch = rows.tiles_per_seq, rows.batch
    return pl.pallas_call(
        _combine_kernel,
        grid_spec=pltpu.PrefetchScalarGridSpec(
            num_scalar_prefetch=1,
            grid=(n_tiles,),
            in_specs=[pl.BlockSpec((TM, D), lambda i, m: (i, 0)),
                      pl.BlockSpec((1, 8, D), lambda i, m: (jnp.minimum(i // tps, batch), 0, 0)),
                      pl.BlockSpec((TM, ROUTE_W), lambda i, m: (i, 0)),
                      pl.BlockSpec(memory_space=pl.ANY)],
            out_specs=pl.BlockSpec((TM, D), lambda i, m: (i, 0)),
            scratch_shapes=_combine_scratch(),
        ),
        out_shape=jax.ShapeDtypeStruct((n_tiles * TM, D), F32),
        compiler_params=_params(1),
        name="moe_combine",
    )(meta, h, modg, slots, ys)


def _mixer_out_moe(rows, n_tiles, h, modg, z_lat, z_ctx, w_out, w_r, b_r, layer, w1, w3, w2, defer_combine):
    h, a, route, cnt = _proj_route(rows, n_tiles, h, modg, z_lat, z_ctx, w_out, w_r, b_r)
    cnt = cnt[:, 0, :N_EXPERTS].astype(jnp.int32)
    seg = (cnt + SEG_ALIGN - 1) // SEG_ALIGN * SEG_ALIGN
    loff = jnp.cumsum(seg, axis=1) - seg
    total = jnp.sum(seg, axis=0)
    region = (total + MOE_RB - 1) // MOE_RB * MOE_RB
    region_end = jnp.cumsum(region)
    region_start = region_end - region
    goff = region_start[None, :] + jnp.cumsum(seg, axis=0) - seg
    meta = jnp.concatenate([loff, goff, seg], axis=1).reshape(-1)
    n_blocks = pl.cdiv(2 * n_tiles * TM + n_tiles * N_EXPERTS * (SEG_ALIGN - 1), MOE_RB) + N_EXPERTS
    n_used = region_end[-1] // MOE_RB
    fill = jnp.concatenate([region_start + total, region - total, n_used[None]]).astype(jnp.int32)
    blk = jnp.minimum(jnp.arange(n_blocks, dtype=jnp.int32), n_used - 1)
    blk_exp = jnp.sum((region_end[None, :] <= (blk * MOE_RB)[:, None]).astype(jnp.int32), axis=1)
    blk_exp = jnp.minimum(blk_exp, N_EXPERTS - 1)
    xs, slots = _dispatch(n_tiles, n_blocks, a, route, meta, fill)
    ys = _experts(xs, blk_exp, fill, layer, w1, w3, w2)
    if defer_combine:
        return h, modg, slots, ys, meta
    return _combine(rows, n_tiles, h, modg, slots, ys, meta)


def kernel(x, c, ctx, c_ctx, ada_w, ada_b, norm_mix_g, norm_ffn_g, sc_in_w, sc_conv_w, sc_out_w, da_qkv_w, da_out_w,
           da_q_norm_g, da_k_norm_g, da_lambda, da_sub_norm_g, cm_in_w, cm_in_b, cm_v_norm_g, cm_ws, cm_bs, cm_out_w,
           sw_qkv_w, sw_out_w, sw_q_norm_g, sw_k_norm_g, sw_sink, ffn_w1, ffn_w3, ffn_w2, moe_router_w, moe_router_b,
           moe_w1, moe_w3, moe_w2):
    batch, seq, _ = x.shape
    ctx_len = ctx.shape[1]
    depth = ada_w.shape[0]
    assert depth == 4 and batch + 1 <= 16
    rows = _Rows(batch, seq, ctx_len)

    cvec = jnp.concatenate([c, c_ctx[None, :], jnp.zeros((16 - batch - 1, D), F32)], axis=0)
    mod = _ada_all(cvec, ada_w, ada_b)[:, :batch + 1].reshape(depth, batch + 1, 6, D)
    gains = jnp.stack([norm_mix_g, norm_ffn_g], axis=1)[:, None]
    modg = jnp.concatenate([mod, jnp.broadcast_to(gains, (depth, batch + 1, 2, D))], axis=2)

    bf = lambda w: w.astype(BF16)
    x2, ctx2 = x.reshape(-1, D), ctx.reshape(-1, D)

    bg, y = _conv_in(rows, x2, ctx2, modg[0], bf(sc_in_w[0]))
    conv_w = jnp.pad(sc_conv_w[0], ((0, 5), (0, 0)))
    h = _conv_out(rows, x2, ctx2, modg[0], bg, y, conv_w, bf(sc_out_w[0]))
    h = _ffn(rows, rows.all_tiles, h, modg[0], 0, ffn_w1, ffn_w3, ffn_w2)

    nq_chunks = D // CN
    qkv = _qkv(rows, h, modg[1], bf(da_qkv_w[0]), da_q_norm_g[0], da_k_norm_g[0], nq_chunks, nq_chunks)
    lam_init = 0.8 - 0.6 * math.exp(-0.3 * 1)
    o_lat, o_ctx = _diff_attn(rows, qkv, da_lambda[0], da_sub_norm_g[0], lam_init,
                              _small_logits(da_q_norm_g[0], da_k_norm_g[0]))
    pending = _mixer_out_moe(rows, rows.all_tiles, h, modg[1], o_lat, o_ctx, bf(da_out_w[0]), moe_router_w[0],
                             moe_router_b[0], 0, moe_w1, moe_w3, moe_w2, defer_combine=True)

    h = _gmlp(rows, pending, modg[2], bf(cm_in_w[0]), cm_in_b[0].reshape(1, -1), cm_v_norm_g[0].reshape(1, -1),
              bf(cm_ws[0]), cm_bs[0].T, bf(cm_out_w[0]))
    h = _ffn(rows, rows.all_tiles, h, modg[2], 1, ffn_w1, ffn_w3, ffn_w2)

    kv_chunks = SWA_KV_HEADS * HEAD_DIM // CN
    qkv = _qkv(rows, h, modg[3], bf(sw_qkv_w[0]), sw_q_norm_g[0], sw_k_norm_g[0], nq_chunks, kv_chunks)
    o = _swa(rows, qkv, sw_sink[0], _small_logits(sw_q_norm_g[0], sw_k_norm_g[0], sw_sink[0]))
    h = _mixer_out_moe(rows, rows.lat_tiles, h, modg[3], o, o, bf(sw_out_w[0]), moe_router_w[1], moe_router_b[1], 1,
                       moe_w1, moe_w3, moe_w2, defer_combine=False)
    return h.reshape(batch, seq, D)
```

```python
import functools
import math

import jax
import jax.numpy as jnp
from jax import lax
from jax.experimental import pallas as pl
from jax.experimental.pallas import tpu as pltpu

D = 1024
HEAD_DIM = 64
GRID_W = 64
ROPE_HALF = HEAD_DIM // 2
ROPE_BASE = 10000.0
DIFF_HEADS = D // (2 * HEAD_DIM)
SWA_Q_HEADS = D // HEAD_DIM
SWA_KV_HEADS = 4
SWA_GROUP = SWA_Q_HEADS // SWA_KV_HEADS
SWA_WINDOW = 128
CHUNK = 128
CM_WIDTH = 2 * D
CM_GROUPS = 8
CM_GW = CM_WIDTH // CM_GROUPS
FFN_DIM = 2816
N_EXPERTS = 8
EPS = 1e-6

F32 = jnp.float32
BF16 = jnp.bfloat16
HIGHEST = lax.Precision.HIGHEST
LOG2E = math.log2(math.e)

TM = 512
CN = 256
MOE_RB = 512
ATT_TQ = 512
ATT_TK = 1024
SWA_TQ = 256
ROUTE_W = 128
MAX_UNSHIFTED_LOGIT = 60.0
VMEM_LIMIT = 56 << 20


def _params(n_grid):
    return pltpu.CompilerParams(dimension_semantics=("arbitrary",) * n_grid, vmem_limit_bytes=VMEM_LIMIT)


def _resident(shape):
    zeros = (0,) * len(shape)
    return pl.BlockSpec(shape, lambda *_: zeros, pipeline_mode=pl.Buffered(1))


def _sigmoid(x):
    return 1.0 / (1.0 + jnp.exp(-x))


def _gelu_tanh(x):
    return 0.5 * x * (1.0 + jnp.tanh(math.sqrt(2.0 / math.pi) * (x + 0.044715 * (x * x * x))))


def _modnorm(x, g, shift, scale):
    y = x * lax.rsqrt(jnp.mean(x * x, axis=-1, keepdims=True) + EPS)
    return (y * g) * (1.0 + scale) + shift


def _dot(a, b):
    return jnp.dot(a, b, preferred_element_type=F32)


def _mix_mod(m):
    return m[6:7], m[0:1], m[1:2]


def _ffn_mod(m):
    return m[7:8], m[3:4], m[4:5]


def _ada_kernel(c_ref, w_ref, b_ref, o_ref):
    c = c_ref[...]
    s = c * _sigmoid(c)
    o_ref[0] = jnp.dot(s, w_ref[0], precision=HIGHEST, preferred_element_type=F32) + b_ref[0]


def _ada_all(cvec, ada_w, ada_b):
    depth, _, n = ada_w.shape
    tn = 1536
    rows = cvec.shape[0]
    return pl.pallas_call(
        _ada_kernel,
        grid=(depth, n // tn),
        in_specs=[pl.BlockSpec((rows, D), lambda l, j: (0, 0)),
                  pl.BlockSpec((1, D, tn), lambda l, j: (l, 0, j)),
                  pl.BlockSpec((1, 1, tn), lambda l, j: (l, 0, j))],
        out_specs=pl.BlockSpec((1, rows, tn), lambda l, j: (l, 0, j)),
        out_shape=jax.ShapeDtypeStruct((depth, rows, n), F32),
        compiler_params=_params(2),
        name="adaln",
    )(cvec, ada_w, ada_b.reshape(depth, 1, n))


class _Rows:
    def __init__(self, batch, seq, ctx_len):
        self.batch, self.seq, self.ctx_len = batch, seq, ctx_len
        self.n_lat = batch * seq
        self.n_ctx = batch * ctx_len
        self.n_all = self.n_lat + self.n_ctx
        assert seq % TM == 0 and self.n_ctx % TM == 0 and TM % ctx_len == 0
        self.lat_tiles = self.n_lat // TM
        self.all_tiles = self.n_all // TM
        self.tiles_per_seq = seq // TM

    def mod_spec(self):
        tps, batch = self.tiles_per_seq, self.batch
        return pl.BlockSpec((1, 8, D), lambda i, *_: (jnp.minimum(i // tps, batch), 0, 0))

    def row_spec(self, width):
        return pl.BlockSpec((TM, width), lambda i, *_: (i, 0))

    def split_specs(self, width):
        lat_tiles = self.lat_tiles
        return [pl.BlockSpec((TM, width), lambda i, *_: (jnp.minimum(i, lat_tiles - 1), 0)),
                pl.BlockSpec((TM, width), lambda i, *_: (jnp.maximum(i - lat_tiles, 0), 0))]


def _conv_in_kernel(lat_tiles, x_ref, ctx_ref, mod_ref, w_ref, bg_ref, y_ref):
    h = jnp.where(pl.program_id(0) < lat_tiles, x_ref[...], ctx_ref[...])
    a = _modnorm(h, *_mix_mod(mod_ref[0])).astype(BF16)
    for c in range(D // CN):
        lo = c * CN
        bg_ref[:, lo:lo + CN] = _dot(a, w_ref[:, lo:lo + CN]).astype(BF16)
        cg = _dot(a, w_ref[:, D + lo:D + lo + CN])
        xv = _dot(a, w_ref[:, 2 * D + lo:2 * D + lo + CN])
        y_ref[:, lo:lo + CN] = (cg * xv).astype(BF16)


def _conv_in(rows, x, ctx, modg, w_in):
    out = jax.ShapeDtypeStruct((rows.n_all, D), BF16)
    return pl.pallas_call(
        functools.partial(_conv_in_kernel, rows.lat_tiles),
        grid=(rows.all_tiles,),
        in_specs=rows.split_specs(D) + [rows.mod_spec(), _resident((D, 3 * D))],
        out_specs=[rows.row_spec(D), rows.row_spec(D)],
        out_shape=[out, out],
        compiler_params=_params(1),
        name="conv_in",
    )(x, ctx, modg, w_in)


HALO = 16


def _conv_out_kernel(n_lat, seq, ctx_len, x_ref, ctx_ref, mod_ref, bg_ref, y_ref, yp_ref, yn_ref, cw_ref, w_ref,
                     o_ref):
    i = pl.program_id(0)
    h = jnp.where(i * TM < n_lat, x_ref[...], ctx_ref[...])
    m = mod_ref[0]
    row = lax.broadcasted_iota(jnp.int32, (TM, 1), 0)
    grow = row + i * TM
    seq_len = jnp.where(grow < n_lat, seq, ctx_len)
    pos = grow & (seq_len - 1)
    out = None
    for c in range(D // CN):
        cols = slice(c * CN, (c + 1) * CN)
        y = y_ref[:, cols].astype(F32)
        prev_row = yp_ref[HALO - 1:HALO, cols].astype(F32)
        next_row = yn_ref[0:1, cols].astype(F32)
        y_m1 = jnp.where(row == 0, prev_row, pltpu.roll(y, 1, 0))
        y_m1 = jnp.where(pos == 0, 0.0, y_m1)
        y_p1 = jnp.where(row == TM - 1, next_row, pltpu.roll(y, TM - 1, 0))
        y_p1 = jnp.where(pos == seq_len - 1, 0.0, y_p1)
        conv = cw_ref[0:1, cols] * y_m1 + cw_ref[1:2, cols] * y + cw_ref[2:3, cols] * y_p1
        z = (bg_ref[:, cols].astype(F32) * conv).astype(BF16)
        part = _dot(z, w_ref[cols, :])
        out = part if out is None else out + part
    o_ref[...] = h + m[2:3] * out


def _conv_out(rows, x, ctx, modg, bg, y, conv_w, w_out):
    hb = TM // HALO
    last = rows.n_all // HALO - 1
    return pl.pallas_call(
        functools.partial(_conv_out_kernel, rows.n_lat, rows.seq, rows.ctx_len),
        grid=(rows.all_tiles,),
        in_specs=rows.split_specs(D) + [rows.mod_spec(), rows.row_spec(D), rows.row_spec(D),
                  pl.BlockSpec((HALO, D), lambda i: (jnp.maximum(i * hb - 1, 0), 0)),
                  pl.BlockSpec((HALO, D), lambda i: (jnp.minimum((i + 1) * hb, last), 0)),
                  _resident((8, D)), _resident((D, D))],
        out_specs=rows.row_spec(D),
        out_shape=jax.ShapeDtypeStruct((rows.n_all, D), F32),
        compiler_params=_params(1),
        name="conv_out",
    )(x, ctx, modg, bg, y, y, y, conv_w, w_out)


W2_KC = 768


def _swiglu(a, w1_ref, w3_ref, w2_ref, t_ref):
    for c in range(FFN_DIM // CN):
        lo = c * CN
        h1 = _dot(a, w1_ref[:, lo:lo + CN].astype(BF16))
        h3 = _dot(a, w3_ref[:, lo:lo + CN].astype(BF16))
        t_ref[:, lo:lo + CN] = (h1 * _sigmoid(h1) * h3).astype(BF16)
    out = None
    for lo in range(0, FFN_DIM, W2_KC):
        hi = min(lo + W2_KC, FFN_DIM)
        part = _dot(t_ref[:, lo:hi], w2_ref[lo:hi, :].astype(BF16))
        out = part if out is None else out + part
    return out


def _ffn_kernel(h_ref, mod_ref, w1_ref, w3_ref, w2_ref, o_ref, t_ref):
    m = mod_ref[0]
    x = h_ref[...]
    a = _modnorm(x, *_ffn_mod(m)).astype(BF16)
    o_ref[...] = x + m[5:6] * _swiglu(a, w1_ref, w3_ref, w2_ref, t_ref)


def _ffn(rows, n_tiles, h, modg, layer, w1, w3, w2):
    def slab(d0, d1):
        return pl.BlockSpec((None, d0, d1), lambda i: (layer, 0, 0), pipeline_mode=pl.Buffered(1))

    return pl.pallas_call(
        _ffn_kernel,
        grid=(n_tiles,),
        in_specs=[rows.row_spec(D), rows.mod_spec(), slab(D, FFN_DIM), slab(D, FFN_DIM), slab(FFN_DIM, D)],
        out_specs=rows.row_spec(D),
        out_shape=jax.ShapeDtypeStruct((n_tiles * TM, D), F32),
        scratch_shapes=[pltpu.VMEM((TM, FFN_DIM), BF16)],
        compiler_params=_params(1),
        name="ffn_dense",
    )(h, modg, w1, w3, w2)


def _norm_rope(x, gmat_ref, cos, sin):
    ms = _dot((x * x).astype(BF16), gmat_ref[...]) * (1.0 / HEAD_DIM)
    xn = x * lax.rsqrt(ms + EPS)
    lane = lax.broadcasted_iota(jnp.int32, (1, CN), 1)
    first_half = (lane & (ROPE_HALF - 1)) < (ROPE_HALF // 2)
    partner = jnp.where(first_half, pltpu.roll(xn, CN - ROPE_HALF // 2, 1), pltpu.roll(xn, ROPE_HALF // 2, 1))
    return xn * cos + partner * sin


def _qkv_kernel(n_qk_chunks, n_q_chunks, h_ref, mod_ref, w_ref, gmat_ref, qcos_ref, qsin_ref, kcos_ref, ksin_ref,
                o_ref):
    a = _modnorm(h_ref[...], *_mix_mod(mod_ref[0])).astype(BF16)
    n_chunks = w_ref.shape[1] // CN

    def project(c):
        return _dot(a, w_ref[:, c * CN:(c + 1) * CN])

    nxt = project(0)
    for c in range(n_chunks):
        acc = nxt
        if c + 1 < n_chunks:
            nxt = project(c + 1)
        if c < n_q_chunks:
            acc = _norm_rope(acc, gmat_ref, qcos_ref[...], qsin_ref[...])
        elif c < n_qk_chunks:
            acc = _norm_rope(acc, gmat_ref, kcos_ref[...], ksin_ref[...])
        o_ref[:, c * CN:(c + 1) * CN] = acc.astype(BF16)


def _rope_tables(rows, gain, scale):
    seq = rows.seq
    pos = jnp.arange(seq)
    n_freq = ROPE_HALF // 2
    inv = ROPE_BASE ** (-jnp.arange(n_freq, dtype=F32) / n_freq)
    ang_r = (pos // GRID_W).astype(F32)[:, None] * inv
    ang_c = (pos % GRID_W).astype(F32)[:, None] * inv
    cos = jnp.concatenate([jnp.cos(ang_r)] * 2 + [jnp.cos(ang_c)] * 2, axis=-1)
    sin = jnp.concatenate([-jnp.sin(ang_r), jnp.sin(ang_r), -jnp.sin(ang_c), jnp.sin(ang_c)], axis=-1)
    cos = jnp.concatenate([cos, jnp.ones((TM, HEAD_DIM), F32)], axis=0)
    sin = jnp.concatenate([sin, jnp.zeros((TM, HEAD_DIM), F32)], axis=0)
    dim = jnp.arange(HEAD_DIM)
    partner = jnp.where((dim % ROPE_HALF) < n_freq, dim + n_freq, dim - n_freq)
    g = gain.astype(F32) * scale
    reps = CN // HEAD_DIM
    return jnp.tile(cos * g[None, :], (1, reps)), jnp.tile(sin * g[partner][None, :], (1, reps))


def _qkv(rows, h, modg, w, q_gain, k_gain, n_q_chunks, n_k_chunks):
    width = w.shape[1]
    head = jnp.arange(CN) // HEAD_DIM
    gmat = (head[:, None] == head[None, :]).astype(BF16)
    tables = _rope_tables(rows, q_gain, LOG2E * HEAD_DIM ** -0.5) + _rope_tables(rows, k_gain, 1.0)
    tps, lat_tiles = rows.tiles_per_seq, rows.lat_tiles
    tab_spec = pl.BlockSpec((TM, CN), lambda i: (jnp.where(i < lat_tiles, i % tps, tps), 0))
    return pl.pallas_call(
        functools.partial(_qkv_kernel, n_q_chunks + n_k_chunks, n_q_chunks),
        grid=(rows.all_tiles,),
        in_specs=[rows.row_spec(D), rows.mod_spec(), _resident((D, width)), _resident((CN, CN))] + [tab_spec] * 4,
        out_specs=rows.row_spec(width),
        out_shape=jax.ShapeDtypeStruct((rows.n_all, width), BF16),
        compiler_params=_params(1),
        name="qkv_proj",
    )(h, modg, w, gmat, *tables)


DH2 = 2 * HEAD_DIM


def _diff_attn_kernel(lam_init, n_lat_chunks, q_ref, kc_ref, vc_ref, kl_ref, vl_ref, lam_ref, subg_ref, small_ref,
                      o_ref, m_ref, acc_ref, vca_ref, vla_ref):
    small_logits = small_ref[0] != 0

    def online_setup():
        vca_ref[:, :DH2] = vc_ref[...]
        vca_ref[:, DH2:] = jnp.ones((vc_ref.shape[0], DH2), BF16)
        if n_lat_chunks:
            vla_ref[:, :DH2] = vl_ref[...]
            vla_ref[:, DH2:] = jnp.ones((vl_ref.shape[0], DH2), BF16)

    first_query_tile = pl.program_id(2) == 0 if n_lat_chunks else True
    pl.when(jnp.logical_and(first_query_tile, jnp.logical_not(small_logits)))(online_setup)

    q = q_ref[...]
    lane = lax.broadcasted_iota(jnp.int32, (1, DH2), 1)
    zero = jnp.zeros((), BF16)
    qs = (jnp.where(lane < HEAD_DIM, q, zero), jnp.where(lane >= HEAD_DIM, q, zero))
    nt = (((1,), (1,)), ((), ()))

    def over_keys(update):
        update(kc_ref[...], vca_ref[...])
        if n_lat_chunks:
            tk = kl_ref.shape[0] // n_lat_chunks

            def body(c, carry):
                start = pl.multiple_of(c * tk, tk)
                update(kl_ref[pl.ds(start, tk), :], vla_ref[pl.ds(start, tk), :])
                return carry
            lax.fori_loop(0, n_lat_chunks, body, 0)

    lp = lam_ref[...]
    lam = (jnp.exp(jnp.sum(lp[0:1] * lp[1:2], axis=-1, keepdims=True))
           - jnp.exp(jnp.sum(lp[2:3] * lp[3:4], axis=-1, keepdims=True)) + lam_init)

    def unshifted():
        n_sub = 2
        sub = q.shape[0] // n_sub

        def logits(r):
            out = []
            for mi in range(2):
                qm = qs[mi][r * sub:(r + 1) * sub]
                s_c = lax.dot_general(qm, kc_ref[...], nt, preferred_element_type=F32)
                s_l = lax.dot_general(qm, kl_ref[...], nt, preferred_element_type=F32) if n_lat_chunks else None
                out.append((s_c, s_l))
            return out

        all_logits = [logits(r) for r in range(n_sub)]
        for r in range(n_sub):
            p_ctx, p_lat, sums = [], [], []
            for s_c, s_l in all_logits[r]:
                e = jnp.exp2(s_c)
                total = jnp.sum(e, axis=-1, keepdims=True)
                p_ctx.append(e)
                if s_l is not None:
                    e = jnp.exp2(s_l)
                    total = total + jnp.sum(e, axis=-1, keepdims=True)
                    p_lat.append(e)
                sums.append(total)
            r0 = 1.0 / sums[0]
            r1 = lam / sums[1]

            def weights(p):
                return (p[0] * r0 - p[1] * r1).astype(BF16)

            o = _dot(weights(p_ctx), vc_ref[...])
            if n_lat_chunks:
                o = o + _dot(weights(p_lat), vl_ref[...])
            acc_ref[0, r * sub:(r + 1) * sub, :DH2] = o

    def update_online(k, va):
        reps = k.shape[0] // DH2
        for mi in range(2):
            s = lax.dot_general(qs[mi], k, nt, preferred_element_type=F32)
            m_old = m_ref[mi]
            m_new = jnp.maximum(m_old, jnp.max(s, axis=-1, keepdims=True))
            alpha = jnp.exp2(m_old - m_new)
            p = jnp.exp2(s - jnp.concatenate([m_new] * reps, axis=-1))
            acc_ref[mi] = jnp.concatenate([alpha, alpha], axis=-1) * acc_ref[mi] + _dot(p.astype(BF16), va)
            m_ref[mi] = m_new

    def online():
        m_ref[...] = jnp.full(m_ref.shape, -jnp.inf, F32)
        acc_ref[...] = jnp.zeros(acc_ref.shape, F32)
        over_keys(update_online)
        acc0 = acc_ref[0]
        acc1 = acc_ref[1]
        acc_ref[0, :, :DH2] = acc0[:, :DH2] / acc0[:, DH2:] - lam * (acc1[:, :DH2] / acc1[:, DH2:])

    pl.when(small_logits)(unshifted)
    pl.when(jnp.logical_not(small_logits))(online)

    o = acc_ref[0, :, :DH2]
    o = o * lax.rsqrt(jnp.mean(o * o, axis=-1, keepdims=True) + EPS) * subg_ref[...]
    o_ref[...] = (o * (1.0 - lam_init)).astype(BF16)


def _small_logits(q_gain, k_gain, sink=None):
    bound = 2 * HEAD_DIM * (LOG2E * HEAD_DIM ** -0.5) * jnp.max(jnp.abs(q_gain)) * jnp.max(jnp.abs(k_gain))
    ok = bound <= MAX_UNSHIFTED_LOGIT
    if sink is not None:
        ok = jnp.logical_and(ok, jnp.max(sink) * LOG2E <= MAX_UNSHIFTED_LOGIT)
    return ok.astype(jnp.int32).reshape(1)


def _diff_attn(rows, qkv, lam_p, sub_g, lam_init, small_logits):
    batch, seq, ctx_len = rows.batch, rows.seq, rows.ctx_len
    nq = seq // ATT_TQ
    nh = DIFF_HEADS
    ctx0 = rows.n_lat // ctx_len
    sub_g = sub_g.reshape(1, DH2)
    small = [_resident((4, HEAD_DIM)), _resident((1, DH2)), pl.BlockSpec(memory_space=pltpu.SMEM)]
    kc_spec = pl.BlockSpec((ctx_len, DH2), lambda b, h, *_: (ctx0 + b, nh + h))
    vc_spec = pl.BlockSpec((ctx_len, DH2), lambda b, h, *_: (ctx0 + b, 2 * nh + h))

    def scratch(tq, n_lat_keys):
        return [pltpu.VMEM((2, tq, DH2), F32), pltpu.VMEM((2, tq, 2 * DH2), F32),
                pltpu.VMEM((ctx_len, 2 * DH2), BF16), pltpu.VMEM((n_lat_keys, 2 * DH2), BF16)]

    o_lat = pl.pallas_call(
        functools.partial(_diff_attn_kernel, lam_init, pl.cdiv(seq, ATT_TK)),
        grid=(batch, nh, nq),
        in_specs=[pl.BlockSpec((ATT_TQ, DH2), lambda b, h, i: (b * nq + i, h)),
                  kc_spec, vc_spec,
                  pl.BlockSpec((seq, DH2), lambda b, h, i: (b, nh + h)),
                  pl.BlockSpec((seq, DH2), lambda b, h, i: (b, 2 * nh + h))] + small,
        out_specs=pl.BlockSpec((ATT_TQ, DH2), lambda b, h, i: (b * nq + i, h)),
        out_shape=jax.ShapeDtypeStruct((rows.n_lat, D), BF16),
        scratch_shapes=scratch(ATT_TQ, seq),
        compiler_params=_params(3),
        name="diff_attn_latent",
    )(qkv, qkv, qkv, qkv, qkv, lam_p, sub_g, small_logits)

    def ctx_kernel(q_ref, kc_ref, vc_ref, lam_ref, subg_ref, small_ref, o_ref, *scratch_refs):
        _diff_attn_kernel(lam_init, 0, q_ref, kc_ref, vc_ref, None, None, lam_ref, subg_ref, small_ref, o_ref,
                          *scratch_refs)

    o_ctx = pl.pallas_call(
        ctx_kernel,
        grid=(batch, nh),
        in_specs=[pl.BlockSpec((ctx_len, DH2), lambda b, h: (ctx0 + b, h)), kc_spec, vc_spec] + small,
        out_specs=pl.BlockSpec((ctx_len, DH2), lambda b, h: (b, h)),
        out_shape=jax.ShapeDtypeStruct((rows.n_ctx, D), BF16),
        scratch_shapes=scratch(ctx_len, 16),
        compiler_params=_params(2),
        name="diff_attn_context",
    )(qkv, qkv, qkv, lam_p, sub_g, small_logits)
    return o_lat, o_ctx


def _gmlp_kernel(meta_ref, h_ref, prev_mod_ref, slot_ref, ys_hbm, mod_ref, win_ref, bin_ref, vg_ref, ws_ref, bs_ref,
                 wout_ref, o_ref, u_ref, v_ref, t_ref, local_ref, sem):
    x = _combined_rows(meta_ref, h_ref, prev_mod_ref, slot_ref, ys_hbm, local_ref, sem)
    m = mod_ref[0]
    a = _modnorm(x, *_mix_mod(m)).astype(BF16)
    n_half = CM_WIDTH // CN

    def in_proj(c):
        lo = c * CN
        return _gelu_tanh(_dot(a, win_ref[:, lo:lo + CN]) + bin_ref[:, lo:lo + CN])

    ssq = jnp.zeros((TM, 1), F32)
    for c in range(n_half):
        z = in_proj(n_half + c)
        v_ref[:, c * CN:(c + 1) * CN] = z
        ssq = ssq + jnp.sum(z * z, axis=-1, keepdims=True)
    for c in range(n_half):
        u_ref[:, c * CN:(c + 1) * CN] = in_proj(c)
    inv = lax.rsqrt(ssq * (1.0 / CM_WIDTH) + EPS)

    def mix(g):
        lo = g * CM_GW
        vn = (v_ref[:, lo:lo + CM_GW] * inv * vg_ref[:, lo:lo + CM_GW]).astype(BF16)
        return [_dot(ws_ref[g], vn[r * CHUNK:(r + 1) * CHUNK]) + bs_ref[:, g:g + 1] for r in range(TM // CHUNK)]

    out = None
    nxt = mix(0)
    for g in range(CM_GROUPS):
        lo = g * CM_GW
        sv = nxt
        if g + 1 < CM_GROUPS:
            nxt = mix(g + 1)
        for r in range(TM // CHUNK):
            r0 = r * CHUNK
            t_ref[r0:r0 + CHUNK, lo:lo + CM_GW] = (u_ref[r0:r0 + CHUNK, lo:lo + CM_GW] * sv[r]).astype(BF16)
        part = _dot(t_ref[:, lo:lo + CM_GW], wout_ref[lo:lo + CM_GW, :])
        out = part if out is None else out + part
    o_ref[...] = x + m[2:3] * out


def _gmlp(rows, pending, modg, w_in, b_in, v_g, w_s, b_s, w_out):
    h, prev_modg, slots, ys, meta = pending
    return pl.pallas_call(
        _gmlp_kernel,
        grid_spec=pltpu.PrefetchScalarGridSpec(
            num_scalar_prefetch=1,
            grid=(rows.all_tiles,),
            in_specs=[rows.row_spec(D), rows.mod_spec(), rows.row_spec(ROUTE_W), pl.BlockSpec(memory_space=pl.ANY),
                      rows.mod_spec(), _resident((D, 2 * CM_WIDTH)), _resident((1, 2 * CM_WIDTH)),
                      _resident((1, CM_WIDTH)), _resident((CM_GROUPS, CHUNK, CHUNK)), _resident((CHUNK, CM_GROUPS)),
                      _resident((CM_WIDTH, D))],
            out_specs=rows.row_spec(D),
            scratch_shapes=[pltpu.VMEM((TM, CM_WIDTH), F32), pltpu.VMEM((TM, CM_WIDTH), F32),
                            pltpu.VMEM((TM, CM_WIDTH), BF16)] + _combine_scratch(),
        ),
        out_shape=jax.ShapeDtypeStruct((rows.n_all, D), F32),
        compiler_params=_params(1),
        name="moe_combine_gmlp",
    )(meta, h, prev_modg, slots, ys, modg, w_in, b_in, v_g, w_s, b_s, w_out)


SWA_BAND = SWA_TQ + 2 * SWA_WINDOW


SWA_PAIR = 2 * HEAD_DIM
N_KV_VARIANTS = 2 * SWA_KV_HEADS


def _swa_kernel(seq, sink_ref, small_ref, q_ref, kc_ref, vc_ref, kl_ref, vl_ref, o_ref, kcv_ref, vcv_ref, klv_ref,
                vlv_ref):
    qi = pl.program_id(1)
    lane = lax.broadcasted_iota(jnp.int32, (1, SWA_PAIR), 1)
    lo_half = lane < HEAD_DIM

    small_logits = small_ref[0] != 0

    @pl.when(jnp.logical_and(qi == 0, small_logits))
    def _per_batch_setup():
        for src_ref, dst_ref in ((kc_ref, kcv_ref), (vc_ref, vcv_ref), (kl_ref, klv_ref), (vl_ref, vlv_ref)):
            for half in range(SWA_KV_HEADS // 2):
                x = src_ref[:, half * SWA_PAIR:(half + 1) * SWA_PAIR].astype(F32)
                xr = pltpu.roll(x, HEAD_DIM, 1)
                j0, j1 = 2 * half, 2 * half + 1
                dst_ref[2 * j0] = jnp.where(lo_half, x, 0.0).astype(BF16)
                dst_ref[2 * j0 + 1] = jnp.where(lo_half, 0.0, xr).astype(BF16)
                dst_ref[2 * j1] = jnp.where(lo_half, xr, 0.0).astype(BF16)
                dst_ref[2 * j1 + 1] = jnp.where(lo_half, 0.0, x).astype(BF16)

    q0 = qi * SWA_TQ
    start = jnp.clip(q0 - SWA_WINDOW, 0, seq - SWA_BAND)
    start = pl.multiple_of(start, SWA_WINDOW)
    qpos = q0 + lax.broadcasted_iota(jnp.int32, (SWA_TQ, SWA_BAND), 0)
    kpos = start + lax.broadcasted_iota(jnp.int32, (SWA_TQ, SWA_BAND), 1)
    in_band = jnp.abs(qpos - kpos) <= SWA_WINDOW
    nt = (((1,), (1,)), ((), ()))

    def unshifted():
        in_band2 = jnp.concatenate([in_band, in_band], axis=0)
        top = lax.broadcasted_iota(jnp.int32, (2 * SWA_TQ, 1), 0) < SWA_TQ
        def logits(v):
            lo = (v // 2) * SWA_GROUP * HEAD_DIM
            q2 = jnp.concatenate([q_ref[:, lo:lo + SWA_PAIR], q_ref[:, lo + SWA_PAIR:lo + 2 * SWA_PAIR]], axis=0)
            return (lax.dot_general(q2, kcv_ref[v], nt, preferred_element_type=F32),
                    lax.dot_general(q2, klv_ref[v, pl.ds(start, SWA_BAND), :], nt, preferred_element_type=F32))

        nxt = logits(0)
        for j in range(SWA_KV_HEADS):
            lo = j * SWA_GROUP * HEAD_DIM
            out = jnp.zeros((2 * SWA_TQ, SWA_PAIR), F32)
            for var in range(2):
                v = 2 * j + var
                s_c, s_b = nxt
                if v + 1 < N_KV_VARIANTS:
                    nxt = logits(v + 1)
                p_c = jnp.exp2(s_c)
                p_b = jnp.where(in_band2, jnp.exp2(s_b), 0.0)
                hq = j * SWA_GROUP + var
                sink = jnp.where(top, sink_ref[hq] * LOG2E, sink_ref[hq + 2] * LOG2E)
                denom = (jnp.sum(p_c, axis=-1, keepdims=True) + jnp.sum(p_b, axis=-1, keepdims=True)
                         + jnp.exp2(sink))
                o = (_dot(p_c.astype(BF16), vcv_ref[v])
                     + _dot(p_b.astype(BF16), vlv_ref[v, pl.ds(start, SWA_BAND), :]))
                out = out + o / denom
            o_ref[:, lo:lo + SWA_PAIR] = out[:SWA_TQ].astype(BF16)
            o_ref[:, lo + SWA_PAIR:lo + 2 * SWA_PAIR] = out[SWA_TQ:].astype(BF16)

    def shifted():
        kb = kl_ref[pl.ds(start, SWA_BAND), :]
        vb = vl_ref[pl.ds(start, SWA_BAND), :]
        kc = kc_ref[...]
        vc = vc_ref[...]
        for j in range(SWA_KV_HEADS):
            kj = slice(j * HEAD_DIM, (j + 1) * HEAD_DIM)
            kbj, vbj, kcj, vcj = kb[:, kj], vb[:, kj], kc[:, kj], vc[:, kj]
            outs = []
            for g in range(SWA_GROUP):
                hq = j * SWA_GROUP + g
                qh = q_ref[:, hq * HEAD_DIM:(hq + 1) * HEAD_DIM]
                s_c = lax.dot_general(qh, kcj, nt, preferred_element_type=F32)
                s_b = lax.dot_general(qh, kbj, nt, preferred_element_type=F32)
                s_b = jnp.where(in_band, s_b, -jnp.inf)
                sink = sink_ref[hq] * LOG2E
                mx = jnp.maximum(jnp.maximum(jnp.max(s_c, axis=-1, keepdims=True),
                                             jnp.max(s_b, axis=-1, keepdims=True)), sink)
                p_c = jnp.exp2(s_c - mx)
                p_b = jnp.exp2(s_b - mx)
                denom = (jnp.sum(p_c, axis=-1, keepdims=True) + jnp.sum(p_b, axis=-1, keepdims=True)
                         + jnp.exp2(sink - mx))
                o = _dot(p_c.astype(BF16), vcj) + _dot(p_b.astype(BF16), vbj)
                outs.append(o / denom)
            lo = j * SWA_GROUP * HEAD_DIM
            o_ref[:, lo:lo + SWA_GROUP * HEAD_DIM] = jnp.concatenate(outs, axis=-1).astype(BF16)

    pl.when(small_logits)(unshifted)
    pl.when(jnp.logical_not(small_logits))(shifted)


def _swa(rows, qkv, sink, small_logits):
    batch, seq, ctx_len = rows.batch, rows.seq, rows.ctx_len
    nq = seq // SWA_TQ
    kvw = SWA_KV_HEADS * HEAD_DIM
    k_col = D // kvw
    ctx0 = rows.n_lat // ctx_len
    return pl.pallas_call(
        functools.partial(_swa_kernel, seq),
        grid=(batch, nq),
        in_specs=[pl.BlockSpec(memory_space=pltpu.SMEM), pl.BlockSpec(memory_space=pltpu.SMEM),
                  pl.BlockSpec((SWA_TQ, D), lambda b, i: (b * nq + i, 0)),
                  pl.BlockSpec((ctx_len, kvw), lambda b, i: (ctx0 + b, k_col)),
                  pl.BlockSpec((ctx_len, kvw), lambda b, i: (ctx0 + b, k_col + 1)),
                  pl.BlockSpec((seq, kvw), lambda b, i: (b, k_col)),
                  pl.BlockSpec((seq, kvw), lambda b, i: (b, k_col + 1))],
        out_specs=pl.BlockSpec((SWA_TQ, D), lambda b, i: (b * nq + i, 0)),
        out_shape=jax.ShapeDtypeStruct((rows.n_lat, D), BF16),
        scratch_shapes=[pltpu.VMEM((N_KV_VARIANTS, ctx_len, SWA_PAIR), BF16),
                        pltpu.VMEM((N_KV_VARIANTS, ctx_len, SWA_PAIR), BF16),
                        pltpu.VMEM((N_KV_VARIANTS, seq, SWA_PAIR), BF16),
                        pltpu.VMEM((N_KV_VARIANTS, seq, SWA_PAIR), BF16)],
        compiler_params=_params(2),
        name="swa_attn",
    )(sink, small_logits, qkv, qkv, qkv, qkv, qkv)


def _proj_route_kernel(lat_tiles, h_ref, mod_ref, zl_ref, zc_ref, w_ref, wr_ref, br_ref, tri_ref, hm_ref, a_ref,
                       slot_ref, slot_t_ref, gate_ref, cnt_ref):
    i = pl.program_id(0)
    m = mod_ref[0]

    def project(z_ref):
        hm_ref[...] = h_ref[...] + m[2:3] * _dot(z_ref[...], w_ref[...])

    pl.when(i < lat_tiles)(lambda: project(zl_ref))
    pl.when(i >= lat_tiles)(lambda: project(zc_ref))
    a = _modnorm(hm_ref[...], *_ffn_mod(m))
    a_hi = a.astype(BF16)
    a_ref[...] = a_hi
    a_lo = (a - a_hi.astype(F32)).astype(BF16)
    hi_part = _dot(a_hi, wr_ref[...])
    logits = (hi_part[:, :ROUTE_W] + hi_part[:, ROUTE_W:]) + _dot(a_lo, wr_ref[:, :ROUTE_W]) + br_ref[...]
    lane = lax.broadcasted_iota(jnp.int32, (TM, ROUTE_W), 1)
    m1 = jnp.max(logits, axis=-1, keepdims=True)
    i1 = jnp.min(jnp.where(logits == m1, lane, ROUTE_W), axis=-1, keepdims=True)
    rest = jnp.where(lane == i1, -jnp.inf, logits)
    m2 = jnp.max(rest, axis=-1, keepdims=True)
    i2 = jnp.min(jnp.where(rest == m2, lane, ROUTE_W), axis=-1, keepdims=True)
    e2 = jnp.exp(m2 - m1)
    gate1 = 1.0 / (1.0 + e2)
    gate2 = e2 / (1.0 + e2)
    pick1 = lane == i1
    pick2 = lane == i2
    chosen = jnp.where(pick1 | pick2, 1.0, 0.0)
    cnt = jnp.broadcast_to(jnp.sum(chosen, axis=0, keepdims=True), (8, ROUTE_W))
    cnt_ref[0] = cnt
    seg = jnp.ceil(cnt * (1.0 / SEG_ALIGN)) * SEG_ALIGN
    lane8 = lax.broadcasted_iota(jnp.int32, (8, ROUTE_W), 1)
    incl = seg
    for shift in (1, 2, 4):
        incl = incl + jnp.where(lane8 >= shift, pltpu.roll(incl, shift, 1), 0.0)
    loff = (incl - seg)[0:1]
    place = _dot(tri_ref[...], chosen.astype(BF16)) + loff
    slot1 = jnp.sum(jnp.where(pick1, place, 0.0), axis=-1, keepdims=True)
    slot2 = jnp.sum(jnp.where(pick2, place, 0.0), axis=-1, keepdims=True)
    slots = jnp.where(lane == 0, slot1, jnp.where(lane == 1, slot2, 0.0))
    slot_ref[...] = slots
    slot_t_ref[0] = jnp.transpose(slots)[0:8, :]
    halves = []
    for gate in (gate1, gate2):
        g = jnp.zeros((TM, ROUTE_W), F32)
        for k, piece in enumerate(_split3(gate)):
            g = jnp.where(lane == k, piece, g)
        halves.append(g.astype(BF16))
    gate_ref[...] = jnp.concatenate(halves, axis=-1)


def _proj_route(rows, n_tiles, h, modg, z_lat, z_ctx, w_out, w_r, b_r):
    lat_tiles = rows.lat_tiles
    pad = ROUTE_W - N_EXPERTS
    w_pad = jnp.pad(w_r, ((0, 0), (0, pad)))
    w_hi = w_pad.astype(BF16)
    w_lo = (w_pad - w_hi.astype(F32)).astype(BF16)
    w_split = jnp.concatenate([w_hi, w_lo], axis=1)
    b_pad = jnp.concatenate([b_r.astype(F32), jnp.full((pad,), -1e30, F32)]).reshape(1, ROUTE_W)
    tri = (jnp.arange(TM)[:, None] > jnp.arange(TM)[None, :]).astype(BF16)
    return pl.pallas_call(
        functools.partial(_proj_route_kernel, lat_tiles),
        grid=(n_tiles,),
        in_specs=[rows.row_spec(D), rows.mod_spec(),
                  pl.BlockSpec((TM, D), lambda i: (jnp.minimum(i, lat_tiles - 1), 0)),
                  pl.BlockSpec((TM, D), lambda i: (jnp.maximum(i - lat_tiles, 0), 0)),
                  _resident((D, D)), _resident((D, 2 * ROUTE_W)), _resident((1, ROUTE_W)), _resident((TM, TM))],
        out_specs=[rows.row_spec(D), rows.row_spec(D), rows.row_spec(ROUTE_W),
                   pl.BlockSpec((1, 8, TM), lambda i: (i, 0, 0)), rows.row_spec(2 * ROUTE_W),
                   pl.BlockSpec((1, 8, ROUTE_W), lambda i: (i, 0, 0))],
        out_shape=[jax.ShapeDtypeStruct((n_tiles * TM, D), F32), jax.ShapeDtypeStruct((n_tiles * TM, D), BF16),
                   jax.ShapeDtypeStruct((n_tiles * TM, ROUTE_W), F32),
                   jax.ShapeDtypeStruct((n_tiles, 8, TM), F32),
                   jax.ShapeDtypeStruct((n_tiles * TM, 2 * ROUTE_W), BF16),
                   jax.ShapeDtypeStruct((n_tiles, 8, ROUTE_W), F32)],
        compiler_params=_params(1),
        name="out_proj_router",
    )(h, modg, z_lat, z_ctx, w_out, w_split, b_pad, tri)


SEG_ALIGN = 16
SEG_BITS = tuple(1 << b for b in range(TM.bit_length() - 1, SEG_ALIGN.bit_length() - 2, -1))
LROWS = -(-(2 * TM + N_EXPERTS * (SEG_ALIGN - 1)) // 128) * 128
XS_W = D + 128
META_W = 3 * N_EXPERTS


def _segment_copies(meta_ref, tile, local_ref, slots_hbm, sem, to_slots):
    for e in range(N_EXPERTS):
        loff = meta_ref[tile * META_W + e]
        goff = meta_ref[tile * META_W + N_EXPERTS + e]
        seg = meta_ref[tile * META_W + 2 * N_EXPERTS + e]
        for bit in SEG_BITS:
            done = seg & ~(2 * bit - 1)
            lo = local_ref.at[pl.ds(pl.multiple_of(loff + done, SEG_ALIGN), bit)]
            gl = slots_hbm.at[pl.ds(pl.multiple_of(goff + done, SEG_ALIGN), bit)]
            copy = pltpu.make_async_copy(lo, gl, sem) if to_slots else pltpu.make_async_copy(gl, lo, sem)
            yield (seg & bit) != 0, copy


def _start_segment_copies(*args):
    for cond, copy in _segment_copies(*args):
        pl.when(cond)(copy.start)


def _wait_segment_copies(*args):
    for cond, copy in _segment_copies(*args):
        pl.when(cond)(copy.wait)


def _split3(x):
    hi = x.astype(BF16).astype(F32)
    mid = (x - hi).astype(BF16).astype(F32)
    lo = ((x - hi) - mid).astype(BF16).astype(F32)
    return hi, mid, lo


def _dispatch_kernel(n_blocks, meta_ref, fill_ref, a_ref, slot_t_ref, gate_ref, xs_hbm, local_ref, zero_ref, sem,
                     zsem):
    i = pl.program_id(0)

    @pl.when(i == 0)
    def _zero_unwritten_slots():
        zero_ref[...] = jnp.zeros(zero_ref.shape, BF16)
        tails = []
        for e in range(N_EXPERTS):
            off, length = fill_ref[e], fill_ref[N_EXPERTS + e]
            for bit in (b for b in SEG_BITS if b < MOE_RB):
                done = length & ~(2 * bit - 1)
                dst = xs_hbm.at[pl.ds(pl.multiple_of(off + done, SEG_ALIGN), bit)]
                tails.append(((length & bit) != 0, pltpu.make_async_copy(zero_ref.at[pl.ds(0, bit)], dst, zsem)))
        for cond, copy in tails:
            pl.when(cond)(copy.start)

        def block_copy(blk):
            dst = xs_hbm.at[pl.ds(pl.multiple_of(blk * MOE_RB, MOE_RB), MOE_RB)]
            return pltpu.make_async_copy(zero_ref, dst, zsem)

        n_used = fill_ref[2 * N_EXPERTS]
        lax.fori_loop(n_used, n_blocks, lambda blk, c: (block_copy(blk).start(), c)[1], 0)
        for cond, copy in tails:
            pl.when(cond)(copy.wait)
        lax.fori_loop(n_used, n_blocks, lambda blk, c: (block_copy(blk).wait(), c)[1], 0)

    half = TM // 2
    main = gate = None
    for hh in range(2):
        cols = slice(hh * half, (hh + 1) * half)
        lrow = lax.broadcasted_iota(jnp.int32, (LROWS, half), 0).astype(F32)
        p1 = jnp.where(lrow == slot_t_ref[0, 0:1, cols], 1.0, 0.0).astype(BF16)
        p2 = jnp.where(lrow == slot_t_ref[0, 1:2, cols], 1.0, 0.0).astype(BF16)
        part = _dot(p1 + p2, a_ref[cols, :])
        gpart = _dot(p1, gate_ref[cols, :ROUTE_W]) + _dot(p2, gate_ref[cols, ROUTE_W:])
        main = part if main is None else main + part
        gate = gpart if gate is None else gate + gpart
    local_ref[:, :D] = main.astype(BF16)
    local_ref[:, D:] = gate.astype(BF16)
    _start_segment_copies(meta_ref, i, local_ref, xs_hbm, sem, True)
    _wait_segment_copies(meta_ref, i, local_ref, xs_hbm, sem, True)


def _dispatch(n_tiles, n_blocks, a, slots_t, gates, meta, fill):
    return pl.pallas_call(
        functools.partial(_dispatch_kernel, n_blocks),
        grid_spec=pltpu.PrefetchScalarGridSpec(
            num_scalar_prefetch=2,
            grid=(n_tiles,),
            in_specs=[pl.BlockSpec((TM, D), lambda i, *_: (i, 0)),
                      pl.BlockSpec((1, 8, TM), lambda i, *_: (i, 0, 0)),
                      pl.BlockSpec((TM, 2 * ROUTE_W), lambda i, *_: (i, 0))],
            out_specs=pl.BlockSpec(memory_space=pl.ANY),
            scratch_shapes=[pltpu.VMEM((LROWS, XS_W), BF16), pltpu.VMEM((MOE_RB, XS_W), BF16),
                            pltpu.SemaphoreType.DMA(()), pltpu.SemaphoreType.DMA(())],
        ),
        out_shape=jax.ShapeDtypeStruct((n_blocks * MOE_RB, XS_W), BF16),
        compiler_params=_params(1),
        name="moe_dispatch",
    )(meta, fill, a, slots_t, gates)


def _expert_kernel(be_ref, fill_ref, xs_ref, w1_ref, w3_ref, w2_ref, ys_ref, t_ref):
    del be_ref
    used = pl.program_id(0) < fill_ref[2 * N_EXPERTS]

    @pl.when(used)
    def _ffn():
        y = _swiglu(xs_ref[:, :D], w1_ref, w3_ref, w2_ref, t_ref)
        gate = jnp.sum(xs_ref[:, D:].astype(F32), axis=-1, keepdims=True)
        ys_ref[...] = (gate * y).astype(BF16)

    @pl.when(jnp.logical_not(used))
    def _idle():
        ys_ref[...] = jnp.zeros(ys_ref.shape, BF16)


def _experts(xs, blk_exp, fill, layer, w1, w3, w2):
    def slab(d0, d1):
        return pl.BlockSpec((None, None, d0, d1), lambda i, be, fill: (layer, be[i], 0, 0),
                            pipeline_mode=pl.Buffered(1))

    n_blocks = blk_exp.shape[0]
    last_used = lambda i, fill: jnp.minimum(i, fill[2 * N_EXPERTS] - 1)
    return pl.pallas_call(
        _expert_kernel,
        grid_spec=pltpu.PrefetchScalarGridSpec(
            num_scalar_prefetch=2,
            grid=(n_blocks,),
            in_specs=[pl.BlockSpec((MOE_RB, XS_W), lambda i, be, fill: (last_used(i, fill), 0)),
                      slab(D, FFN_DIM), slab(D, FFN_DIM), slab(FFN_DIM, D)],
            out_specs=pl.BlockSpec((MOE_RB, D), lambda i, be, fill: (i, 0)),
            scratch_shapes=[pltpu.VMEM((MOE_RB, FFN_DIM), BF16)],
        ),
        out_shape=jax.ShapeDtypeStruct((n_blocks * MOE_RB, D), BF16),
        compiler_params=_params(1),
        name="moe_experts",
    )(blk_exp, fill, xs, w1, w3, w2)


def _combine_scratch():
    return [pltpu.VMEM((2, LROWS, D), BF16), pltpu.SemaphoreType.DMA((2,))]


def _combined_rows(meta_ref, h_ref, mod_ref, slot_ref, ys_hbm, local_ref, sem):
    i = pl.program_id(0)

    buf = i % 2

    def fetch(tile, b):
        return (meta_ref, tile, local_ref.at[b], ys_hbm, sem.at[b], False)

    @pl.when(i == 0)
    def _init():
        local_ref[...] = jnp.zeros(local_ref.shape, BF16)
        _start_segment_copies(*fetch(0, 0))

    @pl.when(i + 1 < pl.num_programs(0))
    def _prefetch():
        _start_segment_copies(*fetch(i + 1, 1 - buf))

    _wait_segment_copies(*fetch(i, buf))
    slots = slot_ref[...]
    lcol = lax.broadcasted_iota(jnp.int32, (TM, LROWS), 1).astype(F32)
    gather = jnp.where((lcol == slots[:, 0:1]) | (lcol == slots[:, 1:2]), 1.0, 0.0).astype(BF16)
    return h_ref[...] + mod_ref[0][5:6] * _dot(gather, local_ref[buf])


def _combine_kernel(meta_ref, h_ref, mod_ref, slot_ref, ys_hbm, o_ref, local_ref, sem):
    o_ref[...] = _combined_rows(meta_ref, h_ref, mod_ref, slot_ref, ys_hbm, local_ref, sem)


def _combine(rows, n_tiles, h, modg, slots, ys, meta):
    tps, batch = rows.tiles_per_seq, rows.batch
    return pl.pallas_call(
        _combine_kernel,
        grid_spec=pltpu.PrefetchScalarGridSpec(
            num_scalar_prefetch=1,
            grid=(n_tiles,),
            in_specs=[pl.BlockSpec((TM, D), lambda i, m: (i, 0)),
                      pl.BlockSpec((1, 8, D), lambda i, m: (jnp.minimum(i // tps, batch), 0, 0)),
                      pl.BlockSpec((TM, ROUTE_W), lambda i, m: (i, 0)),
                      pl.BlockSpec(memory_space=pl.ANY)],
            out_specs=pl.BlockSpec((TM, D), lambda i, m: (i, 0)),
            scratch_shapes=_combine_scratch(),
        ),
        out_shape=jax.ShapeDtypeStruct((n_tiles * TM, D), F32),
        compiler_params=_params(1),
        name="moe_combine",
    )(meta, h, modg, slots, ys)


def _mixer_out_moe(rows, n_tiles, h, modg, z_lat, z_ctx, w_out, w_r, b_r, layer, w1, w3, w2, defer_combine):
    h, a, slots, slots_t, gates, cnt = _proj_route(rows, n_tiles, h, modg, z_lat, z_ctx, w_out, w_r, b_r)
    cnt = cnt[:, 0, :N_EXPERTS].astype(jnp.int32)
    seg = (cnt + SEG_ALIGN - 1) // SEG_ALIGN * SEG_ALIGN
    loff = jnp.cumsum(seg, axis=1) - seg
    total = jnp.sum(seg, axis=0)
    region = (total + MOE_RB - 1) // MOE_RB * MOE_RB
    region_end = jnp.cumsum(region)
    region_start = region_end - region
    goff = region_start[None, :] + jnp.cumsum(seg, axis=0) - seg
    meta = jnp.concatenate([loff, goff, seg], axis=1).reshape(-1)
    n_blocks = pl.cdiv(2 * n_tiles * TM + n_tiles * N_EXPERTS * (SEG_ALIGN - 1), MOE_RB) + N_EXPERTS
    n_used = region_end[-1] // MOE_RB
    fill = jnp.concatenate([region_start + total, region - total, n_used[None]]).astype(jnp.int32)
    blk = jnp.minimum(jnp.arange(n_blocks, dtype=jnp.int32), n_used - 1)
    blk_exp = jnp.sum((region_end[None, :] <= (blk * MOE_RB)[:, None]).astype(jnp.int32), axis=1)
    blk_exp = jnp.minimum(blk_exp, N_EXPERTS - 1)
    xs = _dispatch(n_tiles, n_blocks, a, slots_t, gates, meta, fill)
    ys = _experts(xs, blk_exp, fill, layer, w1, w3, w2)
    if defer_combine:
        return h, modg, slots, ys, meta
    return _combine(rows, n_tiles, h, modg, slots, ys, meta)


def kernel(x, c, ctx, c_ctx, ada_w, ada_b, norm_mix_g, norm_ffn_g, sc_in_w, sc_conv_w, sc_out_w, da_qkv_w, da_out_w,
           da_q_norm_g, da_k_norm_g, da_lambda, da_sub_norm_g, cm_in_w, cm_in_b, cm_v_norm_g, cm_ws, cm_bs, cm_out_w,
           sw_qkv_w, sw_out_w, sw_q_norm_g, sw_k_norm_g, sw_sink, ffn_w1, ffn_w3, ffn_w2, moe_router_w, moe_router_b,
           moe_w1, moe_w3, moe_w2):
    batch, seq, _ = x.shape
    ctx_len = ctx.shape[1]
    depth = ada_w.shape[0]
    assert depth == 4 and batch + 1 <= 16
    rows = _Rows(batch, seq, ctx_len)

    cvec = jnp.concatenate([c, c_ctx[None, :], jnp.zeros((16 - batch - 1, D), F32)], axis=0)
    mod = _ada_all(cvec, ada_w, ada_b)[:, :batch + 1].reshape(depth, batch + 1, 6, D)
    gains = jnp.stack([norm_mix_g, norm_ffn_g], axis=1)[:, None]
    modg = jnp.concatenate([mod, jnp.broadcast_to(gains, (depth, batch + 1, 2, D))], axis=2)

    bf = lambda w: w.astype(BF16)
    x2, ctx2 = x.reshape(-1, D), ctx.reshape(-1, D)

    bg, y = _conv_in(rows, x2, ctx2, modg[0], bf(sc_in_w[0]))
    conv_w = jnp.pad(sc_conv_w[0], ((0, 5), (0, 0)))
    h = _conv_out(rows, x2, ctx2, modg[0], bg, y, conv_w, bf(sc_out_w[0]))
    h = _ffn(rows, rows.all_tiles, h, modg[0], 0, ffn_w1, ffn_w3, ffn_w2)

    nq_chunks = D // CN
    qkv = _qkv(rows, h, modg[1], bf(da_qkv_w[0]), da_q_norm_g[0], da_k_norm_g[0], nq_chunks, nq_chunks)
    lam_init = 0.8 - 0.6 * math.exp(-0.3 * 1)
    o_lat, o_ctx = _diff_attn(rows, qkv, da_lambda[0], da_sub_norm_g[0], lam_init,
                              _small_logits(da_q_norm_g[0], da_k_norm_g[0]))
    pending = _mixer_out_moe(rows, rows.all_tiles, h, modg[1], o_lat, o_ctx, bf(da_out_w[0]), moe_router_w[0],
                             moe_router_b[0], 0, moe_w1, moe_w3, moe_w2, defer_combine=True)

    h = _gmlp(rows, pending, modg[2], bf(cm_in_w[0]), cm_in_b[0].reshape(1, -1), cm_v_norm_g[0].reshape(1, -1),
              bf(cm_ws[0]), cm_bs[0].T, bf(cm_out_w[0]))
    h = _ffn(rows, rows.all_tiles, h, modg[2], 1, ffn_w1, ffn_w3, ffn_w2)

    kv_chunks = SWA_KV_HEADS * HEAD_DIM // CN
    qkv = _qkv(rows, h, modg[3], bf(sw_qkv_w[0]), sw_q_norm_g[0], sw_k_norm_g[0], nq_chunks, kv_chunks)
    o = _swa(rows, qkv, sw_sink[0], _small_logits(sw_q_norm_g[0], sw_k_norm_g[0], sw_sink[0]))
    h = _mixer_out_moe(rows, rows.lat_tiles, h, modg[3], o, o, bf(sw_out_w[0]), moe_router_w[1], moe_router_b[1], 1,
                       moe_w1, moe_w3, moe_w2, defer_combine=False)
    return h.reshape(batch, seq, D)
```

```python
import functools
import math

import jax
import jax.numpy as jnp
from jax import lax
from jax.experimental import pallas as pl
from jax.experimental.pallas import tpu as pltpu

D = 1024
HEAD_DIM = 64
GRID_W = 64
ROPE_HALF = HEAD_DIM // 2
ROPE_BASE = 10000.0
DIFF_HEADS = D // (2 * HEAD_DIM)
SWA_Q_HEADS = D // HEAD_DIM
SWA_KV_HEADS = 4
SWA_GROUP = SWA_Q_HEADS // SWA_KV_HEADS
SWA_WINDOW = 128
CHUNK = 128
CM_WIDTH = 2 * D
CM_GROUPS = 8
CM_GW = CM_WIDTH // CM_GROUPS
FFN_DIM = 2816
N_EXPERTS = 8
EPS = 1e-6

F32 = jnp.float32
BF16 = jnp.bfloat16
HIGHEST = lax.Precision.HIGHEST
LOG2E = math.log2(math.e)

TM = 512
CN = 256
MOE_RB = 512
ATT_TQ = 1024
ATT_SUB = 256
ATT_TK = 1024
SWA_TQ = 256
ROUTE_W = 128
MAX_UNSHIFTED_LOGIT = 60.0
VMEM_LIMIT = 56 << 20


def _params(n_grid):
    return pltpu.CompilerParams(dimension_semantics=("arbitrary",) * n_grid, vmem_limit_bytes=VMEM_LIMIT)


def _resident(shape):
    zeros = (0,) * len(shape)
    return pl.BlockSpec(shape, lambda *_: zeros, pipeline_mode=pl.Buffered(1))


def _sigmoid(x):
    return 1.0 / (1.0 + jnp.exp(-x))


def _gelu_tanh(x):
    return 0.5 * x * (1.0 + jnp.tanh(math.sqrt(2.0 / math.pi) * (x + 0.044715 * (x * x * x))))


def _modnorm(x, g, shift, scale):
    y = x * lax.rsqrt(jnp.mean(x * x, axis=-1, keepdims=True) + EPS)
    return (y * g) * (1.0 + scale) + shift


def _dot(a, b):
    return jnp.dot(a, b, preferred_element_type=F32)


def _mix_mod(m):
    return m[6:7], m[0:1], m[1:2]


def _ffn_mod(m):
    return m[7:8], m[3:4], m[4:5]


def _ada_kernel(c_ref, w_ref, b_ref, o_ref):
    c = c_ref[...]
    s = c * _sigmoid(c)
    o_ref[0] = jnp.dot(s, w_ref[0], precision=HIGHEST, preferred_element_type=F32) + b_ref[0]


def _ada_all(cvec, ada_w, ada_b):
    depth, _, n = ada_w.shape
    tn = 1536
    rows = cvec.shape[0]
    return pl.pallas_call(
        _ada_kernel,
        grid=(depth, n // tn),
        in_specs=[pl.BlockSpec((rows, D), lambda l, j: (0, 0)),
                  pl.BlockSpec((1, D, tn), lambda l, j: (l, 0, j)),
                  pl.BlockSpec((1, 1, tn), lambda l, j: (l, 0, j))],
        out_specs=pl.BlockSpec((1, rows, tn), lambda l, j: (l, 0, j)),
        out_shape=jax.ShapeDtypeStruct((depth, rows, n), F32),
        compiler_params=_params(2),
        name="adaln",
    )(cvec, ada_w, ada_b.reshape(depth, 1, n))


class _Rows:
    def __init__(self, batch, seq, ctx_len):
        self.batch, self.seq, self.ctx_len = batch, seq, ctx_len
        self.n_lat = batch * seq
        self.n_ctx = batch * ctx_len
        self.n_all = self.n_lat + self.n_ctx
        assert seq % TM == 0 and self.n_ctx % TM == 0 and TM % ctx_len == 0
        self.lat_tiles = self.n_lat // TM
        self.all_tiles = self.n_all // TM
        self.tiles_per_seq = seq // TM

    def mod_spec(self):
        tps, batch = self.tiles_per_seq, self.batch
        return pl.BlockSpec((1, 8, D), lambda i, *_: (jnp.minimum(i // tps, batch), 0, 0))

    def row_spec(self, width):
        return pl.BlockSpec((TM, width), lambda i, *_: (i, 0))

    def split_specs(self, width):
        lat_tiles = self.lat_tiles
        return [pl.BlockSpec((TM, width), lambda i, *_: (jnp.minimum(i, lat_tiles - 1), 0)),
                pl.BlockSpec((TM, width), lambda i, *_: (jnp.maximum(i - lat_tiles, 0), 0))]


def _conv_in_kernel(lat_tiles, x_ref, ctx_ref, mod_ref, w_ref, bg_ref, y_ref):
    h = jnp.where(pl.program_id(0) < lat_tiles, x_ref[...], ctx_ref[...])
    a = _modnorm(h, *_mix_mod(mod_ref[0])).astype(BF16)
    for c in range(D // CN):
        lo = c * CN
        bg_ref[:, lo:lo + CN] = _dot(a, w_ref[:, lo:lo + CN]).astype(BF16)
        cg = _dot(a, w_ref[:, D + lo:D + lo + CN])
        xv = _dot(a, w_ref[:, 2 * D + lo:2 * D + lo + CN])
        y_ref[:, lo:lo + CN] = (cg * xv).astype(BF16)


def _conv_in(rows, x, ctx, modg, w_in):
    out = jax.ShapeDtypeStruct((rows.n_all, D), BF16)
    return pl.pallas_call(
        functools.partial(_conv_in_kernel, rows.lat_tiles),
        grid=(rows.all_tiles,),
        in_specs=rows.split_specs(D) + [rows.mod_spec(), _resident((D, 3 * D))],
        out_specs=[rows.row_spec(D), rows.row_spec(D)],
        out_shape=[out, out],
        compiler_params=_params(1),
        name="conv_in",
    )(x, ctx, modg, w_in)


HALO = 16


def _conv_out_kernel(n_lat, seq, ctx_len, x_ref, ctx_ref, mod_ref, bg_ref, y_ref, yp_ref, yn_ref, cw_ref, w_ref,
                     o_ref):
    i = pl.program_id(0)
    h = jnp.where(i * TM < n_lat, x_ref[...], ctx_ref[...])
    m = mod_ref[0]
    row = lax.broadcasted_iota(jnp.int32, (TM, 1), 0)
    grow = row + i * TM
    seq_len = jnp.where(grow < n_lat, seq, ctx_len)
    pos = grow & (seq_len - 1)
    out = None
    for c in range(D // CN):
        cols = slice(c * CN, (c + 1) * CN)
        y = y_ref[:, cols].astype(F32)
        prev_row = yp_ref[HALO - 1:HALO, cols].astype(F32)
        next_row = yn_ref[0:1, cols].astype(F32)
        y_m1 = jnp.where(row == 0, prev_row, pltpu.roll(y, 1, 0))
        y_m1 = jnp.where(pos == 0, 0.0, y_m1)
        y_p1 = jnp.where(row == TM - 1, next_row, pltpu.roll(y, TM - 1, 0))
        y_p1 = jnp.where(pos == seq_len - 1, 0.0, y_p1)
        conv = cw_ref[0:1, cols] * y_m1 + cw_ref[1:2, cols] * y + cw_ref[2:3, cols] * y_p1
        z = (bg_ref[:, cols].astype(F32) * conv).astype(BF16)
        part = _dot(z, w_ref[cols, :])
        out = part if out is None else out + part
    o_ref[...] = h + m[2:3] * out


def _conv_out(rows, x, ctx, modg, bg, y, conv_w, w_out):
    hb = TM // HALO
    last = rows.n_all // HALO - 1
    return pl.pallas_call(
        functools.partial(_conv_out_kernel, rows.n_lat, rows.seq, rows.ctx_len),
        grid=(rows.all_tiles,),
        in_specs=rows.split_specs(D) + [rows.mod_spec(), rows.row_spec(D), rows.row_spec(D),
                  pl.BlockSpec((HALO, D), lambda i: (jnp.maximum(i * hb - 1, 0), 0)),
                  pl.BlockSpec((HALO, D), lambda i: (jnp.minimum((i + 1) * hb, last), 0)),
                  _resident((8, D)), _resident((D, D))],
        out_specs=rows.row_spec(D),
        out_shape=jax.ShapeDtypeStruct((rows.n_all, D), F32),
        compiler_params=_params(1),
        name="conv_out",
    )(x, ctx, modg, bg, y, y, y, conv_w, w_out)


W2_KC = 768


def _swiglu(a, w1_ref, w3_ref, w2_ref, t_ref):
    for c in range(FFN_DIM // CN):
        lo = c * CN
        h1 = _dot(a, w1_ref[:, lo:lo + CN].astype(BF16))
        h3 = _dot(a, w3_ref[:, lo:lo + CN].astype(BF16))
        t_ref[:, lo:lo + CN] = (h1 * _sigmoid(h1) * h3).astype(BF16)
    out = None
    for lo in range(0, FFN_DIM, W2_KC):
        hi = min(lo + W2_KC, FFN_DIM)
        part = _dot(t_ref[:, lo:hi], w2_ref[lo:hi, :].astype(BF16))
        out = part if out is None else out + part
    return out


def _ffn_kernel(h_ref, mod_ref, w1_ref, w3_ref, w2_ref, o_ref, t_ref):
    m = mod_ref[0]
    x = h_ref[...]
    a = _modnorm(x, *_ffn_mod(m)).astype(BF16)
    o_ref[...] = x + m[5:6] * _swiglu(a, w1_ref, w3_ref, w2_ref, t_ref)


def _ffn(rows, n_tiles, h, modg, layer, w1, w3, w2):
    def slab(d0, d1):
        return pl.BlockSpec((None, d0, d1), lambda i: (layer, 0, 0), pipeline_mode=pl.Buffered(1))

    return pl.pallas_call(
        _ffn_kernel,
        grid=(n_tiles,),
        in_specs=[rows.row_spec(D), rows.mod_spec(), slab(D, FFN_DIM), slab(D, FFN_DIM), slab(FFN_DIM, D)],
        out_specs=rows.row_spec(D),
        out_shape=jax.ShapeDtypeStruct((n_tiles * TM, D), F32),
        scratch_shapes=[pltpu.VMEM((TM, FFN_DIM), BF16)],
        compiler_params=_params(1),
        name="ffn_dense",
    )(h, modg, w1, w3, w2)


def _norm_rope(x, gmat_ref, cos, sin):
    ms = _dot((x * x).astype(BF16), gmat_ref[...]) * (1.0 / HEAD_DIM)
    xn = x * lax.rsqrt(ms + EPS)
    lane = lax.broadcasted_iota(jnp.int32, (1, CN), 1)
    first_half = (lane & (ROPE_HALF - 1)) < (ROPE_HALF // 2)
    partner = jnp.where(first_half, pltpu.roll(xn, CN - ROPE_HALF // 2, 1), pltpu.roll(xn, ROPE_HALF // 2, 1))
    return xn * cos + partner * sin


def _qkv_kernel(n_qk_chunks, n_q_chunks, h_ref, mod_ref, w_ref, gmat_ref, qcos_ref, qsin_ref, kcos_ref, ksin_ref,
                o_ref):
    a = _modnorm(h_ref[...], *_mix_mod(mod_ref[0])).astype(BF16)
    n_chunks = w_ref.shape[1] // CN

    def project(c):
        return _dot(a, w_ref[:, c * CN:(c + 1) * CN])

    nxt = project(0)
    for c in range(n_chunks):
        acc = nxt
        if c + 1 < n_chunks:
            nxt = project(c + 1)
        if c < n_q_chunks:
            acc = _norm_rope(acc, gmat_ref, qcos_ref[...], qsin_ref[...])
        elif c < n_qk_chunks:
            acc = _norm_rope(acc, gmat_ref, kcos_ref[...], ksin_ref[...])
        o_ref[:, c * CN:(c + 1) * CN] = acc.astype(BF16)


def _rope_tables(rows, gain, scale):
    seq = rows.seq
    pos = jnp.arange(seq)
    n_freq = ROPE_HALF // 2
    inv = ROPE_BASE ** (-jnp.arange(n_freq, dtype=F32) / n_freq)
    ang_r = (pos // GRID_W).astype(F32)[:, None] * inv
    ang_c = (pos % GRID_W).astype(F32)[:, None] * inv
    cos = jnp.concatenate([jnp.cos(ang_r)] * 2 + [jnp.cos(ang_c)] * 2, axis=-1)
    sin = jnp.concatenate([-jnp.sin(ang_r), jnp.sin(ang_r), -jnp.sin(ang_c), jnp.sin(ang_c)], axis=-1)
    cos = jnp.concatenate([cos, jnp.ones((TM, HEAD_DIM), F32)], axis=0)
    sin = jnp.concatenate([sin, jnp.zeros((TM, HEAD_DIM), F32)], axis=0)
    dim = jnp.arange(HEAD_DIM)
    partner = jnp.where((dim % ROPE_HALF) < n_freq, dim + n_freq, dim - n_freq)
    g = gain.astype(F32) * scale
    reps = CN // HEAD_DIM
    return jnp.tile(cos * g[None, :], (1, reps)), jnp.tile(sin * g[partner][None, :], (1, reps))


def _qkv(rows, h, modg, w, q_gain, k_gain, n_q_chunks, n_k_chunks):
    width = w.shape[1]
    head = jnp.arange(CN) // HEAD_DIM
    gmat = (head[:, None] == head[None, :]).astype(BF16)
    tables = _rope_tables(rows, q_gain, LOG2E * HEAD_DIM ** -0.5) + _rope_tables(rows, k_gain, 1.0)
    tps, lat_tiles = rows.tiles_per_seq, rows.lat_tiles
    tab_spec = pl.BlockSpec((TM, CN), lambda i: (jnp.where(i < lat_tiles, i % tps, tps), 0))
    return pl.pallas_call(
        functools.partial(_qkv_kernel, n_q_chunks + n_k_chunks, n_q_chunks),
        grid=(rows.all_tiles,),
        in_specs=[rows.row_spec(D), rows.mod_spec(), _resident((D, width)), _resident((CN, CN))] + [tab_spec] * 4,
        out_specs=rows.row_spec(width),
        out_shape=jax.ShapeDtypeStruct((rows.n_all, width), BF16),
        compiler_params=_params(1),
        name="qkv_proj",
    )(h, modg, w, gmat, *tables)


DH2 = 2 * HEAD_DIM


def _diff_attn_kernel(lam_init, n_lat_chunks, q_ref, kc_ref, vc_ref, kl_ref, vl_ref, lam_ref, subg_ref, small_ref,
                      o_ref, m_ref, acc_ref, vca_ref, vla_ref):
    small_logits = small_ref[0] != 0

    def online_setup():
        vca_ref[:, :DH2] = vc_ref[...]
        vca_ref[:, DH2:] = jnp.ones((vc_ref.shape[0], DH2), BF16)
        if n_lat_chunks:
            vla_ref[:, :DH2] = vl_ref[...]
            vla_ref[:, DH2:] = jnp.ones((vl_ref.shape[0], DH2), BF16)

    first_query_tile = pl.program_id(2) == 0 if n_lat_chunks else True
    pl.when(jnp.logical_and(first_query_tile, jnp.logical_not(small_logits)))(online_setup)

    q = q_ref[...]
    lane = lax.broadcasted_iota(jnp.int32, (1, DH2), 1)
    zero = jnp.zeros((), BF16)
    qs = (jnp.where(lane < HEAD_DIM, q, zero), jnp.where(lane >= HEAD_DIM, q, zero))
    nt = (((1,), (1,)), ((), ()))

    def over_keys(update):
        update(kc_ref[...], vca_ref[...])
        if n_lat_chunks:
            tk = kl_ref.shape[0] // n_lat_chunks

            def body(c, carry):
                start = pl.multiple_of(c * tk, tk)
                update(kl_ref[pl.ds(start, tk), :], vla_ref[pl.ds(start, tk), :])
                return carry
            lax.fori_loop(0, n_lat_chunks, body, 0)

    lp = lam_ref[...]
    lam = (jnp.exp(jnp.sum(lp[0:1] * lp[1:2], axis=-1, keepdims=True))
           - jnp.exp(jnp.sum(lp[2:3] * lp[3:4], axis=-1, keepdims=True)) + lam_init)

    def unshifted():
        sub = min(ATT_SUB, q.shape[0] // 2)
        n_sub = q.shape[0] // sub

        def logits(r):
            out = []
            for mi in range(2):
                qm = qs[mi][r * sub:(r + 1) * sub]
                s_c = lax.dot_general(qm, kc_ref[...], nt, preferred_element_type=F32)
                s_l = lax.dot_general(qm, kl_ref[...], nt, preferred_element_type=F32) if n_lat_chunks else None
                out.append((s_c, s_l))
            return out

        ahead = [logits(0), logits(1)]
        for r in range(n_sub):
            p_ctx, p_lat, sums = [], [], []
            for s_c, s_l in ahead.pop(0):
                e = jnp.exp2(s_c)
                total = jnp.sum(e, axis=-1, keepdims=True)
                p_ctx.append(e)
                if s_l is not None:
                    e = jnp.exp2(s_l)
                    total = total + jnp.sum(e, axis=-1, keepdims=True)
                    p_lat.append(e)
                sums.append(total)
            r0 = 1.0 / sums[0]
            r1 = lam / sums[1]

            def weights(p):
                return (p[0] * r0 - p[1] * r1).astype(BF16)

            o = _dot(weights(p_ctx), vc_ref[...])
            if n_lat_chunks:
                o = o + _dot(weights(p_lat), vl_ref[...])
            acc_ref[0, r * sub:(r + 1) * sub, :DH2] = o
            if r + 2 < n_sub:
                ahead.append(logits(r + 2))

    def update_online(k, va):
        reps = k.shape[0] // DH2
        for mi in range(2):
            s = lax.dot_general(qs[mi], k, nt, preferred_element_type=F32)
            m_old = m_ref[mi]
            m_new = jnp.maximum(m_old, jnp.max(s, axis=-1, keepdims=True))
            alpha = jnp.exp2(m_old - m_new)
            p = jnp.exp2(s - jnp.concatenate([m_new] * reps, axis=-1))
            acc_ref[mi] = jnp.concatenate([alpha, alpha], axis=-1) * acc_ref[mi] + _dot(p.astype(BF16), va)
            m_ref[mi] = m_new

    def online():
        m_ref[...] = jnp.full(m_ref.shape, -jnp.inf, F32)
        acc_ref[...] = jnp.zeros(acc_ref.shape, F32)
        over_keys(update_online)
        acc0 = acc_ref[0]
        acc1 = acc_ref[1]
        acc_ref[0, :, :DH2] = acc0[:, :DH2] / acc0[:, DH2:] - lam * (acc1[:, :DH2] / acc1[:, DH2:])

    pl.when(small_logits)(unshifted)
    pl.when(jnp.logical_not(small_logits))(online)

    o = acc_ref[0, :, :DH2]
    o = o * lax.rsqrt(jnp.mean(o * o, axis=-1, keepdims=True) + EPS) * subg_ref[...]
    o_ref[...] = (o * (1.0 - lam_init)).astype(BF16)


def _small_logits(q_gain, k_gain, sink=None):
    bound = 2 * HEAD_DIM * (LOG2E * HEAD_DIM ** -0.5) * jnp.max(jnp.abs(q_gain)) * jnp.max(jnp.abs(k_gain))
    ok = bound <= MAX_UNSHIFTED_LOGIT
    if sink is not None:
        ok = jnp.logical_and(ok, jnp.max(sink) * LOG2E <= MAX_UNSHIFTED_LOGIT)
    return ok.astype(jnp.int32).reshape(1)


def _diff_attn(rows, qkv, lam_p, sub_g, lam_init, small_logits):
    batch, seq, ctx_len = rows.batch, rows.seq, rows.ctx_len
    tq = min(ATT_TQ, seq)
    nq = seq // tq
    nh = DIFF_HEADS
    ctx0 = rows.n_lat // ctx_len
    sub_g = sub_g.reshape(1, DH2)
    small = [_resident((4, HEAD_DIM)), _resident((1, DH2)), pl.BlockSpec(memory_space=pltpu.SMEM)]
    kc_spec = pl.BlockSpec((ctx_len, DH2), lambda b, h, *_: (ctx0 + b, nh + h))
    vc_spec = pl.BlockSpec((ctx_len, DH2), lambda b, h, *_: (ctx0 + b, 2 * nh + h))

    def scratch(tq, n_lat_keys):
        return [pltpu.VMEM((2, tq, DH2), F32), pltpu.VMEM((2, tq, 2 * DH2), F32),
                pltpu.VMEM((ctx_len, 2 * DH2), BF16), pltpu.VMEM((n_lat_keys, 2 * DH2), BF16)]

    o_lat = pl.pallas_call(
        functools.partial(_diff_attn_kernel, lam_init, pl.cdiv(seq, ATT_TK)),
        grid=(batch, nh, nq),
        in_specs=[pl.BlockSpec((tq, DH2), lambda b, h, i: (b * nq + i, h)),
                  kc_spec, vc_spec,
                  pl.BlockSpec((seq, DH2), lambda b, h, i: (b, nh + h)),
                  pl.BlockSpec((seq, DH2), lambda b, h, i: (b, 2 * nh + h))] + small,
        out_specs=pl.BlockSpec((tq, DH2), lambda b, h, i: (b * nq + i, h)),
        out_shape=jax.ShapeDtypeStruct((rows.n_lat, D), BF16),
        scratch_shapes=scratch(tq, seq),
        compiler_params=_params(3),
        name="diff_attn_latent",
    )(qkv, qkv, qkv, qkv, qkv, lam_p, sub_g, small_logits)

    def ctx_kernel(q_ref, kc_ref, vc_ref, lam_ref, subg_ref, small_ref, o_ref, *scratch_refs):
        _diff_attn_kernel(lam_init, 0, q_ref, kc_ref, vc_ref, None, None, lam_ref, subg_ref, small_ref, o_ref,
                          *scratch_refs)

    o_ctx = pl.pallas_call(
        ctx_kernel,
        grid=(batch, nh),
        in_specs=[pl.BlockSpec((ctx_len, DH2), lambda b, h: (ctx0 + b, h)), kc_spec, vc_spec] + small,
        out_specs=pl.BlockSpec((ctx_len, DH2), lambda b, h: (b, h)),
        out_shape=jax.ShapeDtypeStruct((rows.n_ctx, D), BF16),
        scratch_shapes=scratch(ctx_len, 16),
        compiler_params=_params(2),
        name="diff_attn_context",
    )(qkv, qkv, qkv, lam_p, sub_g, small_logits)
    return o_lat, o_ctx


def _gmlp_kernel(meta_ref, h_ref, prev_mod_ref, slot_ref, ys_hbm, mod_ref, win_ref, bin_ref, vg_ref, ws_ref, bs_ref,
                 wout_ref, o_ref, u_ref, v_ref, t_ref, local_ref, sem):
    x = _combined_rows(meta_ref, h_ref, prev_mod_ref, slot_ref, ys_hbm, local_ref, sem)
    m = mod_ref[0]
    a = _modnorm(x, *_mix_mod(m)).astype(BF16)
    n_half = CM_WIDTH // CN

    def in_proj(c):
        lo = c * CN
        return _gelu_tanh(_dot(a, win_ref[:, lo:lo + CN]) + bin_ref[:, lo:lo + CN])

    ssq = jnp.zeros((TM, 1), F32)
    for c in range(n_half):
        z = in_proj(n_half + c)
        v_ref[:, c * CN:(c + 1) * CN] = z
        ssq = ssq + jnp.sum(z * z, axis=-1, keepdims=True)
    for c in range(n_half):
        u_ref[:, c * CN:(c + 1) * CN] = in_proj(c)
    inv = lax.rsqrt(ssq * (1.0 / CM_WIDTH) + EPS)

    def mix(g):
        lo = g * CM_GW
        vn = (v_ref[:, lo:lo + CM_GW] * inv * vg_ref[:, lo:lo + CM_GW]).astype(BF16)
        return [_dot(ws_ref[g], vn[r * CHUNK:(r + 1) * CHUNK]) + bs_ref[:, g:g + 1] for r in range(TM // CHUNK)]

    out = None
    nxt = mix(0)
    for g in range(CM_GROUPS):
        lo = g * CM_GW
        sv = nxt
        if g + 1 < CM_GROUPS:
            nxt = mix(g + 1)
        for r in range(TM // CHUNK):
            r0 = r * CHUNK
            t_ref[r0:r0 + CHUNK, lo:lo + CM_GW] = (u_ref[r0:r0 + CHUNK, lo:lo + CM_GW] * sv[r]).astype(BF16)
        part = _dot(t_ref[:, lo:lo + CM_GW], wout_ref[lo:lo + CM_GW, :])
        out = part if out is None else out + part
    o_ref[...] = x + m[2:3] * out


def _gmlp(rows, pending, modg, w_in, b_in, v_g, w_s, b_s, w_out):
    h, prev_modg, slots, ys, meta = pending
    return pl.pallas_call(
        _gmlp_kernel,
        grid_spec=pltpu.PrefetchScalarGridSpec(
            num_scalar_prefetch=1,
            grid=(rows.all_tiles,),
            in_specs=[rows.row_spec(D), rows.mod_spec(), rows.row_spec(ROUTE_W), pl.BlockSpec(memory_space=pl.ANY),
                      rows.mod_spec(), _resident((D, 2 * CM_WIDTH)), _resident((1, 2 * CM_WIDTH)),
                      _resident((1, CM_WIDTH)), _resident((CM_GROUPS, CHUNK, CHUNK)), _resident((CHUNK, CM_GROUPS)),
                      _resident((CM_WIDTH, D))],
            out_specs=rows.row_spec(D),
            scratch_shapes=[pltpu.VMEM((TM, CM_WIDTH), F32), pltpu.VMEM((TM, CM_WIDTH), F32),
                            pltpu.VMEM((TM, CM_WIDTH), BF16)] + _combine_scratch(),
        ),
        out_shape=jax.ShapeDtypeStruct((rows.n_all, D), F32),
        compiler_params=_params(1),
        name="moe_combine_gmlp",
    )(meta, h, prev_modg, slots, ys, modg, w_in, b_in, v_g, w_s, b_s, w_out)


SWA_BAND = SWA_TQ + 2 * SWA_WINDOW


SWA_PAIR = 2 * HEAD_DIM
N_KV_VARIANTS = 2 * SWA_KV_HEADS


def _swa_kernel(seq, sink_ref, small_ref, q_ref, kc_ref, vc_ref, kl_ref, vl_ref, o_ref, kcv_ref, vcv_ref, klv_ref,
                vlv_ref):
    qi = pl.program_id(1)
    lane = lax.broadcasted_iota(jnp.int32, (1, SWA_PAIR), 1)
    lo_half = lane < HEAD_DIM

    small_logits = small_ref[0] != 0

    @pl.when(jnp.logical_and(qi == 0, small_logits))
    def _per_batch_setup():
        for src_ref, dst_ref in ((kc_ref, kcv_ref), (vc_ref, vcv_ref), (kl_ref, klv_ref), (vl_ref, vlv_ref)):
            for half in range(SWA_KV_HEADS // 2):
                x = src_ref[:, half * SWA_PAIR:(half + 1) * SWA_PAIR].astype(F32)
                xr = pltpu.roll(x, HEAD_DIM, 1)
                j0, j1 = 2 * half, 2 * half + 1
                dst_ref[2 * j0] = jnp.where(lo_half, x, 0.0).astype(BF16)
                dst_ref[2 * j0 + 1] = jnp.where(lo_half, 0.0, xr).astype(BF16)
                dst_ref[2 * j1] = jnp.where(lo_half, xr, 0.0).astype(BF16)
                dst_ref[2 * j1 + 1] = jnp.where(lo_half, 0.0, x).astype(BF16)

    q0 = qi * SWA_TQ
    start = jnp.clip(q0 - SWA_WINDOW, 0, seq - SWA_BAND)
    start = pl.multiple_of(start, SWA_WINDOW)
    qpos = q0 + lax.broadcasted_iota(jnp.int32, (SWA_TQ, SWA_BAND), 0)
    kpos = start + lax.broadcasted_iota(jnp.int32, (SWA_TQ, SWA_BAND), 1)
    in_band = jnp.abs(qpos - kpos) <= SWA_WINDOW
    nt = (((1,), (1,)), ((), ()))

    def unshifted():
        in_band2 = jnp.concatenate([in_band, in_band], axis=0)
        top = lax.broadcasted_iota(jnp.int32, (2 * SWA_TQ, 1), 0) < SWA_TQ
        def logits(v):
            lo = (v // 2) * SWA_GROUP * HEAD_DIM
            q2 = jnp.concatenate([q_ref[:, lo:lo + SWA_PAIR], q_ref[:, lo + SWA_PAIR:lo + 2 * SWA_PAIR]], axis=0)
            return (lax.dot_general(q2, kcv_ref[v], nt, preferred_element_type=F32),
                    lax.dot_general(q2, klv_ref[v, pl.ds(start, SWA_BAND), :], nt, preferred_element_type=F32))

        nxt = logits(0)
        for j in range(SWA_KV_HEADS):
            lo = j * SWA_GROUP * HEAD_DIM
            out = jnp.zeros((2 * SWA_TQ, SWA_PAIR), F32)
            for var in range(2):
                v = 2 * j + var
                s_c, s_b = nxt
                if v + 1 < N_KV_VARIANTS:
                    nxt = logits(v + 1)
                p_c = jnp.exp2(s_c)
                p_b = jnp.where(in_band2, jnp.exp2(s_b), 0.0)
                hq = j * SWA_GROUP + var
                sink = jnp.where(top, sink_ref[hq] * LOG2E, sink_ref[hq + 2] * LOG2E)
                denom = (jnp.sum(p_c, axis=-1, keepdims=True) + jnp.sum(p_b, axis=-1, keepdims=True)
                         + jnp.exp2(sink))
                o = (_dot(p_c.astype(BF16), vcv_ref[v])
                     + _dot(p_b.astype(BF16), vlv_ref[v, pl.ds(start, SWA_BAND), :]))
                out = out + o / denom
            o_ref[:, lo:lo + SWA_PAIR] = out[:SWA_TQ].astype(BF16)
            o_ref[:, lo + SWA_PAIR:lo + 2 * SWA_PAIR] = out[SWA_TQ:].astype(BF16)

    def shifted():
        kb = kl_ref[pl.ds(start, SWA_BAND), :]
        vb = vl_ref[pl.ds(start, SWA_BAND), :]
        kc = kc_ref[...]
        vc = vc_ref[...]
        for j in range(SWA_KV_HEADS):
            kj = slice(j * HEAD_DIM, (j + 1) * HEAD_DIM)
            kbj, vbj, kcj, vcj = kb[:, kj], vb[:, kj], kc[:, kj], vc[:, kj]
            outs = []
            for g in range(SWA_GROUP):
                hq = j * SWA_GROUP + g
                qh = q_ref[:, hq * HEAD_DIM:(hq + 1) * HEAD_DIM]
                s_c = lax.dot_general(qh, kcj, nt, preferred_element_type=F32)
                s_b = lax.dot_general(qh, kbj, nt, preferred_element_type=F32)
                s_b = jnp.where(in_band, s_b, -jnp.inf)
                sink = sink_ref[hq] * LOG2E
                mx = jnp.maximum(jnp.maximum(jnp.max(s_c, axis=-1, keepdims=True),
                                             jnp.max(s_b, axis=-1, keepdims=True)), sink)
                p_c = jnp.exp2(s_c - mx)
                p_b = jnp.exp2(s_b - mx)
                denom = (jnp.sum(p_c, axis=-1, keepdims=True) + jnp.sum(p_b, axis=-1, keepdims=True)
                         + jnp.exp2(sink - mx))
                o = _dot(p_c.astype(BF16), vcj) + _dot(p_b.astype(BF16), vbj)
                outs.append(o / denom)
            lo = j * SWA_GROUP * HEAD_DIM
            o_ref[:, lo:lo + SWA_GROUP * HEAD_DIM] = jnp.concatenate(outs, axis=-1).astype(BF16)

    pl.when(small_logits)(unshifted)
    pl.when(jnp.logical_not(small_logits))(shifted)


def _swa(rows, qkv, sink, small_logits):
    batch, seq, ctx_len = rows.batch, rows.seq, rows.ctx_len
    nq = seq // SWA_TQ
    kvw = SWA_KV_HEADS * HEAD_DIM
    k_col = D // kvw
    ctx0 = rows.n_lat // ctx_len
    return pl.pallas_call(
        functools.partial(_swa_kernel, seq),
        grid=(batch, nq),
        in_specs=[pl.BlockSpec(memory_space=pltpu.SMEM), pl.BlockSpec(memory_space=pltpu.SMEM),
                  pl.BlockSpec((SWA_TQ, D), lambda b, i: (b * nq + i, 0)),
                  pl.BlockSpec((ctx_len, kvw), lambda b, i: (ctx0 + b, k_col)),
                  pl.BlockSpec((ctx_len, kvw), lambda b, i: (ctx0 + b, k_col + 1)),
                  pl.BlockSpec((seq, kvw), lambda b, i: (b, k_col)),
                  pl.BlockSpec((seq, kvw), lambda b, i: (b, k_col + 1))],
        out_specs=pl.BlockSpec((SWA_TQ, D), lambda b, i: (b * nq + i, 0)),
        out_shape=jax.ShapeDtypeStruct((rows.n_lat, D), BF16),
        scratch_shapes=[pltpu.VMEM((N_KV_VARIANTS, ctx_len, SWA_PAIR), BF16),
                        pltpu.VMEM((N_KV_VARIANTS, ctx_len, SWA_PAIR), BF16),
                        pltpu.VMEM((N_KV_VARIANTS, seq, SWA_PAIR), BF16),
                        pltpu.VMEM((N_KV_VARIANTS, seq, SWA_PAIR), BF16)],
        compiler_params=_params(2),
        name="swa_attn",
    )(sink, small_logits, qkv, qkv, qkv, qkv, qkv)


def _proj_route_kernel(lat_tiles, h_ref, mod_ref, zl_ref, zc_ref, w_ref, wr_ref, br_ref, hm_ref, a_ref, route_ref,
                       cnt_ref):
    i = pl.program_id(0)
    m = mod_ref[0]

    def project(z_ref):
        hm_ref[...] = h_ref[...] + m[2:3] * _dot(z_ref[...], w_ref[...])

    pl.when(i < lat_tiles)(lambda: project(zl_ref))
    pl.when(i >= lat_tiles)(lambda: project(zc_ref))
    a = _modnorm(hm_ref[...], *_ffn_mod(m))
    a_hi = a.astype(BF16)
    a_ref[...] = a_hi
    a_lo = (a - a_hi.astype(F32)).astype(BF16)
    hi_part = _dot(a_hi, wr_ref[...])
    logits = (hi_part[:, :ROUTE_W] + hi_part[:, ROUTE_W:]) + _dot(a_lo, wr_ref[:, :ROUTE_W]) + br_ref[...]
    lane = lax.broadcasted_iota(jnp.int32, (TM, ROUTE_W), 1)
    m1 = jnp.max(logits, axis=-1, keepdims=True)
    i1 = jnp.min(jnp.where(logits == m1, lane, ROUTE_W), axis=-1, keepdims=True)
    rest = jnp.where(lane == i1, -jnp.inf, logits)
    m2 = jnp.max(rest, axis=-1, keepdims=True)
    i2 = jnp.min(jnp.where(rest == m2, lane, ROUTE_W), axis=-1, keepdims=True)
    e2 = jnp.exp(m2 - m1)
    gate1 = 1.0 / (1.0 + e2)
    gate2 = e2 / (1.0 + e2)
    rec = jnp.where(lane == 0, i1.astype(F32), 0.0)
    rec = jnp.where(lane == 1, i2.astype(F32), rec)
    rec = jnp.where(lane == 2, gate1, rec)
    route_ref[...] = jnp.where(lane == 3, gate2, rec)
    chosen = jnp.where((lane == i1) | (lane == i2), 1.0, 0.0)
    cnt_ref[0] = jnp.broadcast_to(jnp.sum(chosen, axis=0, keepdims=True), (8, ROUTE_W))


def _proj_route(rows, n_tiles, h, modg, z_lat, z_ctx, w_out, w_r, b_r):
    lat_tiles = rows.lat_tiles
    pad = ROUTE_W - N_EXPERTS
    w_pad = jnp.pad(w_r, ((0, 0), (0, pad)))
    w_hi = w_pad.astype(BF16)
    w_lo = (w_pad - w_hi.astype(F32)).astype(BF16)
    w_split = jnp.concatenate([w_hi, w_lo], axis=1)
    b_pad = jnp.concatenate([b_r.astype(F32), jnp.full((pad,), -1e30, F32)]).reshape(1, ROUTE_W)
    return pl.pallas_call(
        functools.partial(_proj_route_kernel, lat_tiles),
        grid=(n_tiles,),
        in_specs=[rows.row_spec(D), rows.mod_spec(),
                  pl.BlockSpec((TM, D), lambda i: (jnp.minimum(i, lat_tiles - 1), 0)),
                  pl.BlockSpec((TM, D), lambda i: (jnp.maximum(i - lat_tiles, 0), 0)),
                  _resident((D, D)), _resident((D, 2 * ROUTE_W)), _resident((1, ROUTE_W))],
        out_specs=[rows.row_spec(D), rows.row_spec(D), rows.row_spec(ROUTE_W),
                   pl.BlockSpec((1, 8, ROUTE_W), lambda i: (i, 0, 0))],
        out_shape=[jax.ShapeDtypeStruct((n_tiles * TM, D), F32), jax.ShapeDtypeStruct((n_tiles * TM, D), BF16),
                   jax.ShapeDtypeStruct((n_tiles * TM, ROUTE_W), F32),
                   jax.ShapeDtypeStruct((n_tiles, 8, ROUTE_W), F32)],
        compiler_params=_params(1),
        name="out_proj_router",
    )(h, modg, z_lat, z_ctx, w_out, w_split, b_pad)


SEG_ALIGN = 16
SEG_BITS = tuple(1 << b for b in range(TM.bit_length() - 1, SEG_ALIGN.bit_length() - 2, -1))
LROWS = -(-(2 * TM + N_EXPERTS * (SEG_ALIGN - 1)) // 128) * 128
XS_W = D + 128
META_W = 3 * N_EXPERTS


def _segment_copies(meta_ref, tile, local_ref, slots_hbm, sem, to_slots):
    for e in range(N_EXPERTS):
        loff = meta_ref[tile * META_W + e]
        goff = meta_ref[tile * META_W + N_EXPERTS + e]
        seg = meta_ref[tile * META_W + 2 * N_EXPERTS + e]
        for bit in SEG_BITS:
            done = seg & ~(2 * bit - 1)
            lo = local_ref.at[pl.ds(pl.multiple_of(loff + done, SEG_ALIGN), bit)]
            gl = slots_hbm.at[pl.ds(pl.multiple_of(goff + done, SEG_ALIGN), bit)]
            copy = pltpu.make_async_copy(lo, gl, sem) if to_slots else pltpu.make_async_copy(gl, lo, sem)
            yield (seg & bit) != 0, copy


def _start_segment_copies(*args):
    for cond, copy in _segment_copies(*args):
        pl.when(cond)(copy.start)


def _wait_segment_copies(*args):
    for cond, copy in _segment_copies(*args):
        pl.when(cond)(copy.wait)


def _split3(x):
    hi = x.astype(BF16).astype(F32)
    mid = (x - hi).astype(BF16).astype(F32)
    lo = ((x - hi) - mid).astype(BF16).astype(F32)
    return hi, mid, lo


def _dispatch_kernel(n_blocks, meta_ref, fill_ref, a_ref, route_ref, tri_ref, xs_hbm, slot_ref, local_ref, zero_ref,
                     sem, zsem):
    i = pl.program_id(0)

    @pl.when(i == 0)
    def _zero_unwritten_slots():
        zero_ref[...] = jnp.zeros(zero_ref.shape, BF16)
        tails = []
        for e in range(N_EXPERTS):
            off, length = fill_ref[e], fill_ref[N_EXPERTS + e]
            for bit in (b for b in SEG_BITS if b < MOE_RB):
                done = length & ~(2 * bit - 1)
                dst = xs_hbm.at[pl.ds(pl.multiple_of(off + done, SEG_ALIGN), bit)]
                tails.append(((length & bit) != 0, pltpu.make_async_copy(zero_ref.at[pl.ds(0, bit)], dst, zsem)))
        for cond, copy in tails:
            pl.when(cond)(copy.start)

        def block_copy(blk):
            dst = xs_hbm.at[pl.ds(pl.multiple_of(blk * MOE_RB, MOE_RB), MOE_RB)]
            return pltpu.make_async_copy(zero_ref, dst, zsem)

        n_used = fill_ref[2 * N_EXPERTS]
        lax.fori_loop(n_used, n_blocks, lambda blk, c: (block_copy(blk).start(), c)[1], 0)
        for cond, copy in tails:
            pl.when(cond)(copy.wait)
        lax.fori_loop(n_used, n_blocks, lambda blk, c: (block_copy(blk).wait(), c)[1], 0)

    route = route_ref[...]
    lane = lax.broadcasted_iota(jnp.int32, (TM, ROUTE_W), 1)
    pick1 = lane == route[:, 0:1].astype(jnp.int32)
    pick2 = lane == route[:, 1:2].astype(jnp.int32)
    chosen = jnp.where(pick1 | pick2, 1.0, 0.0).astype(BF16)
    rank = _dot(tri_ref[...], chosen)
    lane1 = lax.broadcasted_iota(jnp.int32, (1, ROUTE_W), 1)
    loff = jnp.zeros((1, ROUTE_W), F32)
    for e in range(N_EXPERTS):
        loff = jnp.where(lane1 == e, meta_ref[i * META_W + e].astype(F32), loff)
    place = rank + loff
    slot1 = jnp.sum(jnp.where(pick1, place, 0.0), axis=-1, keepdims=True)
    slot2 = jnp.sum(jnp.where(pick2, place, 0.0), axis=-1, keepdims=True)
    slot_ref[...] = jnp.where(lane == 0, slot1, jnp.where(lane == 1, slot2, 0.0))

    row1 = jnp.transpose(jnp.broadcast_to(slot1, (TM, 128)))[0:1, :]
    row2 = jnp.transpose(jnp.broadcast_to(slot2, (TM, 128)))[0:1, :]
    lrow = lax.broadcasted_iota(jnp.int32, (LROWS, TM), 0).astype(F32)
    p1 = jnp.where(lrow == row1, 1.0, 0.0).astype(BF16)
    p2 = jnp.where(lrow == row2, 1.0, 0.0).astype(BF16)
    local_ref[:, :D] = _dot(p1 + p2, a_ref[...]).astype(BF16)
    gates = []
    for col in (2, 3):
        pieces = _split3(route[:, col:col + 1])
        g = jnp.zeros((TM, ROUTE_W), F32)
        for k, piece in enumerate(pieces):
            g = jnp.where(lane == k, piece, g)
        gates.append(g.astype(BF16))
    local_ref[:, D:] = (_dot(p1, gates[0]) + _dot(p2, gates[1])).astype(BF16)
    _start_segment_copies(meta_ref, i, local_ref, xs_hbm, sem, True)
    _wait_segment_copies(meta_ref, i, local_ref, xs_hbm, sem, True)


def _dispatch(n_tiles, n_blocks, a, route, meta, fill):
    tri = (jnp.arange(TM)[:, None] > jnp.arange(TM)[None, :]).astype(BF16)
    return pl.pallas_call(
        functools.partial(_dispatch_kernel, n_blocks),
        grid_spec=pltpu.PrefetchScalarGridSpec(
            num_scalar_prefetch=2,
            grid=(n_tiles,),
            in_specs=[pl.BlockSpec((TM, D), lambda i, *_: (i, 0)),
                      pl.BlockSpec((TM, ROUTE_W), lambda i, *_: (i, 0)),
                      _resident((TM, TM))],
            out_specs=[pl.BlockSpec(memory_space=pl.ANY), pl.BlockSpec((TM, ROUTE_W), lambda i, *_: (i, 0))],
            scratch_shapes=[pltpu.VMEM((LROWS, XS_W), BF16), pltpu.VMEM((MOE_RB, XS_W), BF16),
                            pltpu.SemaphoreType.DMA(()), pltpu.SemaphoreType.DMA(())],
        ),
        out_shape=[jax.ShapeDtypeStruct((n_blocks * MOE_RB, XS_W), BF16),
                   jax.ShapeDtypeStruct((n_tiles * TM, ROUTE_W), F32)],
        compiler_params=_params(1),
        name="moe_dispatch",
    )(meta, fill, a, route, tri)


def _expert_kernel(be_ref, fill_ref, xs_ref, w1_ref, w3_ref, w2_ref, ys_ref, t_ref):
    del be_ref
    used = pl.program_id(0) < fill_ref[2 * N_EXPERTS]

    @pl.when(used)
    def _ffn():
        y = _swiglu(xs_ref[:, :D], w1_ref, w3_ref, w2_ref, t_ref)
        gate = jnp.sum(xs_ref[:, D:].astype(F32), axis=-1, keepdims=True)
        ys_ref[...] = (gate * y).astype(BF16)

    @pl.when(jnp.logical_not(used))
    def _idle():
        ys_ref[...] = jnp.zeros(ys_ref.shape, BF16)


def _experts(xs, blk_exp, fill, layer, w1, w3, w2):
    def slab(d0, d1):
        return pl.BlockSpec((None, None, d0, d1), lambda i, be, fill: (layer, be[i], 0, 0),
                            pipeline_mode=pl.Buffered(1))

    n_blocks = blk_exp.shape[0]
    last_used = lambda i, fill: jnp.minimum(i, fill[2 * N_EXPERTS] - 1)
    return pl.pallas_call(
        _expert_kernel,
        grid_spec=pltpu.PrefetchScalarGridSpec(
            num_scalar_prefetch=2,
            grid=(n_blocks,),
            in_specs=[pl.BlockSpec((MOE_RB, XS_W), lambda i, be, fill: (last_used(i, fill), 0)),
                      slab(D, FFN_DIM), slab(D, FFN_DIM), slab(FFN_DIM, D)],
            out_specs=pl.BlockSpec((MOE_RB, D), lambda i, be, fill: (i, 0)),
            scratch_shapes=[pltpu.VMEM((MOE_RB, FFN_DIM), BF16)],
        ),
        out_shape=jax.ShapeDtypeStruct((n_blocks * MOE_RB, D), BF16),
        compiler_params=_params(1),
        name="moe_experts",
    )(blk_exp, fill, xs, w1, w3, w2)


def _combine_scratch():
    return [pltpu.VMEM((2, LROWS, D), BF16), pltpu.SemaphoreType.DMA((2,))]


def _combined_rows(meta_ref, h_ref, mod_ref, slot_ref, ys_hbm, local_ref, sem):
    i = pl.program_id(0)

    buf = i % 2

    def fetch(tile, b):
        return (meta_ref, tile, local_ref.at[b], ys_hbm, sem.at[b], False)

    @pl.when(i == 0)
    def _init():
        local_ref[...] = jnp.zeros(local_ref.shape, BF16)
        _start_segment_copies(*fetch(0, 0))

    @pl.when(i + 1 < pl.num_programs(0))
    def _prefetch():
        _start_segment_copies(*fetch(i + 1, 1 - buf))

    _wait_segment_copies(*fetch(i, buf))
    slots = slot_ref[...]
    lcol = lax.broadcasted_iota(jnp.int32, (TM, LROWS), 1).astype(F32)
    gather = jnp.where((lcol == slots[:, 0:1]) | (lcol == slots[:, 1:2]), 1.0, 0.0).astype(BF16)
    return h_ref[...] + mod_ref[0][5:6] * _dot(gather, local_ref[buf])


def _combine_kernel(meta_ref, h_ref, mod_ref, slot_ref, ys_hbm, o_ref, local_ref, sem):
    o_ref[...] = _combined_rows(meta_ref, h_ref, mod_ref, slot_ref, ys_hbm, local_ref, sem)


def _combine(rows, n_tiles, h, modg, slots, ys, meta):
    tps, batch = rows.tiles_per_seq, rows.batch
    return pl.pallas_call(
        _combine_kernel,
        grid_spec=pltpu.PrefetchScalarGridSpec(
            num_scalar_prefetch=1,
            grid=(n_tiles,),
            in_specs=[pl.BlockSpec((TM, D), lambda i, m: (i, 0)),
                      pl.BlockSpec((1, 8, D), lambda i, m: (jnp.minimum(i // tps, batch), 0, 0)),
                      pl.BlockSpec((TM, ROUTE_W), lambda i, m: (i, 0)),
                      pl.BlockSpec(memory_space=pl.ANY)],
            out_specs=pl.BlockSpec((TM, D), lambda i, m: (i, 0)),
            scratch_shapes=_combine_scratch(),
        ),
        out_shape=jax.ShapeDtypeStruct((n_tiles * TM, D), F32),
        compiler_params=_params(1),
        name="moe_combine",
    )(meta, h, modg, slots, ys)


def _mixer_out_moe(rows, n_tiles, h, modg, z_lat, z_ctx, w_out, w_r, b_r, layer, w1, w3, w2, defer_combine):
    h, a, route, cnt = _proj_route(rows, n_tiles, h, modg, z_lat, z_ctx, w_out, w_r, b_r)
    cnt = cnt[:, 0, :N_EXPERTS].astype(jnp.int32)
    seg = (cnt + SEG_ALIGN - 1) // SEG_ALIGN * SEG_ALIGN
    loff = jnp.cumsum(seg, axis=1) - seg
    total = jnp.sum(seg, axis=0)
    region = (total + MOE_RB - 1) // MOE_RB * MOE_RB
    region_end = jnp.cumsum(region)
    region_start = region_end - region
    goff = region_start[None, :] + jnp.cumsum(seg, axis=0) - seg
    meta = jnp.concatenate([loff, goff, seg], axis=1).reshape(-1)
    n_blocks = pl.cdiv(2 * n_tiles * TM + n_tiles * N_EXPERTS * (SEG_ALIGN - 1), MOE_RB) + N_EXPERTS
    n_used = region_end[-1] // MOE_RB
    fill = jnp.concatenate([region_start + total, region - total, n_used[None]]).astype(jnp.int32)
    blk = jnp.minimum(jnp.arange(n_blocks, dtype=jnp.int32), n_used - 1)
    blk_exp = jnp.sum((region_end[None, :] <= (blk * MOE_RB)[:, None]).astype(jnp.int32), axis=1)
    blk_exp = jnp.minimum(blk_exp, N_EXPERTS - 1)
    xs, slots = _dispatch(n_tiles, n_blocks, a, route, meta, fill)
    ys = _experts(xs, blk_exp, fill, layer, w1, w3, w2)
    if defer_combine:
        return h, modg, slots, ys, meta
    return _combine(rows, n_tiles, h, modg, slots, ys, meta)


def kernel(x, c, ctx, c_ctx, ada_w, ada_b, norm_mix_g, norm_ffn_g, sc_in_w, sc_conv_w, sc_out_w, da_qkv_w, da_out_w,
           da_q_norm_g, da_k_norm_g, da_lambda, da_sub_norm_g, cm_in_w, cm_in_b, cm_v_norm_g, cm_ws, cm_bs, cm_out_w,
           sw_qkv_w, sw_out_w, sw_q_norm_g, sw_k_norm_g, sw_sink, ffn_w1, ffn_w3, ffn_w2, moe_router_w, moe_router_b,
           moe_w1, moe_w3, moe_w2):
    batch, seq, _ = x.shape
    ctx_len = ctx.shape[1]
    depth = ada_w.shape[0]
    assert depth == 4 and batch + 1 <= 16
    rows = _Rows(batch, seq, ctx_len)

    cvec = jnp.concatenate([c, c_ctx[None, :], jnp.zeros((16 - batch - 1, D), F32)], axis=0)
    mod = _ada_all(cvec, ada_w, ada_b)[:, :batch + 1].reshape(depth, batch + 1, 6, D)
    gains = jnp.stack([norm_mix_g, norm_ffn_g], axis=1)[:, None]
    modg = jnp.concatenate([mod, jnp.broadcast_to(gains, (depth, batch + 1, 2, D))], axis=2)

    bf = lambda w: w.astype(BF16)
    x2, ctx2 = x.reshape(-1, D), ctx.reshape(-1, D)

    bg, y = _conv_in(rows, x2, ctx2, modg[0], bf(sc_in_w[0]))
    conv_w = jnp.pad(sc_conv_w[0], ((0, 5), (0, 0)))
    h = _conv_out(rows, x2, ctx2, modg[0], bg, y, conv_w, bf(sc_out_w[0]))
    h = _ffn(rows, rows.all_tiles, h, modg[0], 0, ffn_w1, ffn_w3, ffn_w2)

    nq_chunks = D // CN
    qkv = _qkv(rows, h, modg[1], bf(da_qkv_w[0]), da_q_norm_g[0], da_k_norm_g[0], nq_chunks, nq_chunks)
    lam_init = 0.8 - 0.6 * math.exp(-0.3 * 1)
    o_lat, o_ctx = _diff_attn(rows, qkv, da_lambda[0], da_sub_norm_g[0], lam_init,
                              _small_logits(da_q_norm_g[0], da_k_norm_g[0]))
    pending = _mixer_out_moe(rows, rows.all_tiles, h, modg[1], o_lat, o_ctx, bf(da_out_w[0]), moe_router_w[0],
                             moe_router_b[0], 0, moe_w1, moe_w3, moe_w2, defer_combine=True)

    h = _gmlp(rows, pending, modg[2], bf(cm_in_w[0]), cm_in_b[0].reshape(1, -1), cm_v_norm_g[0].reshape(1, -1),
              bf(cm_ws[0]), cm_bs[0].T, bf(cm_out_w[0]))
    h = _ffn(rows, rows.all_tiles, h, modg[2], 1, ffn_w1, ffn_w3, ffn_w2)

    kv_chunks = SWA_KV_HEADS * HEAD_DIM // CN
    qkv = _qkv(rows, h, modg[3], bf(sw_qkv_w[0]), sw_q_norm_g[0], sw_k_norm_g[0], nq_chunks, kv_chunks)
    o = _swa(rows, qkv, sw_sink[0], _small_logits(sw_q_norm_g[0], sw_k_norm_g[0], sw_sink[0]))
    h = _mixer_out_moe(rows, rows.lat_tiles, h, modg[3], o, o, bf(sw_out_w[0]), moe_router_w[1], moe_router_b[1], 1,
                       moe_w1, moe_w3, moe_w2, defer_combine=False)
    return h.reshape(batch, seq, D)
```

```python
import functools
import math

import jax
import jax.numpy as jnp
from jax import lax
from jax.experimental import pallas as pl
from jax.experimental.pallas import tpu as pltpu

D = 1024
HEAD_DIM = 64
GRID_W = 64
ROPE_HALF = HEAD_DIM // 2
ROPE_BASE = 10000.0
DIFF_HEADS = D // (2 * HEAD_DIM)
SWA_Q_HEADS = D // HEAD_DIM
SWA_KV_HEADS = 4
SWA_GROUP = SWA_Q_HEADS // SWA_KV_HEADS
SWA_WINDOW = 128
CHUNK = 128
CM_WIDTH = 2 * D
CM_GROUPS = 8
CM_GW = CM_WIDTH // CM_GROUPS
FFN_DIM = 2816
N_EXPERTS = 8
EPS = 1e-6

F32 = jnp.float32
BF16 = jnp.bfloat16
HIGHEST = lax.Precision.HIGHEST
LOG2E = math.log2(math.e)

TM = 512
CN = 256
MOE_RB = 512
ATT_TQ = 1024
ATT_SUB = 256
ATT_TK = 1024
SWA_TQ = 256
ROUTE_W = 128
MAX_UNSHIFTED_LOGIT = 60.0
VMEM_LIMIT = 56 << 20


def _params(n_grid):
    return pltpu.CompilerParams(dimension_semantics=("arbitrary",) * n_grid, vmem_limit_bytes=VMEM_LIMIT)


def _resident(shape):
    zeros = (0,) * len(shape)
    return pl.BlockSpec(shape, lambda *_: zeros, pipeline_mode=pl.Buffered(1))


def _sigmoid(x):
    return 1.0 / (1.0 + jnp.exp(-x))


def _gelu_tanh(x):
    return 0.5 * x * (1.0 + jnp.tanh(math.sqrt(2.0 / math.pi) * (x + 0.044715 * (x * x * x))))


def _modnorm(x, g, shift, scale):
    y = x * lax.rsqrt(jnp.mean(x * x, axis=-1, keepdims=True) + EPS)
    return (y * g) * (1.0 + scale) + shift


def _dot(a, b):
    return jnp.dot(a, b, preferred_element_type=F32)


def _mix_mod(m):
    return m[6:7], m[0:1], m[1:2]


def _ffn_mod(m):
    return m[7:8], m[3:4], m[4:5]


def _ada_kernel(c_ref, w_ref, b_ref, o_ref):
    c = c_ref[...]
    s = c * _sigmoid(c)
    o_ref[0] = jnp.dot(s, w_ref[0], precision=HIGHEST, preferred_element_type=F32) + b_ref[0]


def _ada_all(cvec, ada_w, ada_b):
    depth, _, n = ada_w.shape
    tn = 1536
    rows = cvec.shape[0]
    return pl.pallas_call(
        _ada_kernel,
        grid=(depth, n // tn),
        in_specs=[pl.BlockSpec((rows, D), lambda l, j: (0, 0)),
                  pl.BlockSpec((1, D, tn), lambda l, j: (l, 0, j)),
                  pl.BlockSpec((1, 1, tn), lambda l, j: (l, 0, j))],
        out_specs=pl.BlockSpec((1, rows, tn), lambda l, j: (l, 0, j)),
        out_shape=jax.ShapeDtypeStruct((depth, rows, n), F32),
        compiler_params=_params(2),
        name="adaln",
    )(cvec, ada_w, ada_b.reshape(depth, 1, n))


class _Rows:
    def __init__(self, batch, seq, ctx_len):
        self.batch, self.seq, self.ctx_len = batch, seq, ctx_len
        self.n_lat = batch * seq
        self.n_ctx = batch * ctx_len
        self.n_all = self.n_lat + self.n_ctx
        assert seq % TM == 0 and self.n_ctx % TM == 0 and TM % ctx_len == 0
        self.lat_tiles = self.n_lat // TM
        self.all_tiles = self.n_all // TM
        self.tiles_per_seq = seq // TM

    def mod_spec(self):
        tps, batch = self.tiles_per_seq, self.batch
        return pl.BlockSpec((1, 8, D), lambda i, *_: (jnp.minimum(i // tps, batch), 0, 0))

    def row_spec(self, width):
        return pl.BlockSpec((TM, width), lambda i, *_: (i, 0))

    def split_specs(self, width):
        lat_tiles = self.lat_tiles
        return [pl.BlockSpec((TM, width), lambda i, *_: (jnp.minimum(i, lat_tiles - 1), 0)),
                pl.BlockSpec((TM, width), lambda i, *_: (jnp.maximum(i - lat_tiles, 0), 0))]


def _conv_in_kernel(lat_tiles, x_ref, ctx_ref, mod_ref, w_ref, bg_ref, y_ref):
    h = jnp.where(pl.program_id(0) < lat_tiles, x_ref[...], ctx_ref[...])
    a = _modnorm(h, *_mix_mod(mod_ref[0])).astype(BF16)
    for c in range(D // CN):
        lo = c * CN
        bg_ref[:, lo:lo + CN] = _dot(a, w_ref[:, lo:lo + CN]).astype(BF16)
        cg = _dot(a, w_ref[:, D + lo:D + lo + CN])
        xv = _dot(a, w_ref[:, 2 * D + lo:2 * D + lo + CN])
        y_ref[:, lo:lo + CN] = (cg * xv).astype(BF16)


def _conv_in(rows, x, ctx, modg, w_in):
    out = jax.ShapeDtypeStruct((rows.n_all, D), BF16)
    return pl.pallas_call(
        functools.partial(_conv_in_kernel, rows.lat_tiles),
        grid=(rows.all_tiles,),
        in_specs=rows.split_specs(D) + [rows.mod_spec(), _resident((D, 3 * D))],
        out_specs=[rows.row_spec(D), rows.row_spec(D)],
        out_shape=[out, out],
        compiler_params=_params(1),
        name="conv_in",
    )(x, ctx, modg, w_in)


HALO = 16


def _conv_out_kernel(n_lat, seq, ctx_len, x_ref, ctx_ref, mod_ref, bg_ref, y_ref, yp_ref, yn_ref, cw_ref, w_ref,
                     o_ref):
    i = pl.program_id(0)
    h = jnp.where(i * TM < n_lat, x_ref[...], ctx_ref[...])
    m = mod_ref[0]
    row = lax.broadcasted_iota(jnp.int32, (TM, 1), 0)
    grow = row + i * TM
    seq_len = jnp.where(grow < n_lat, seq, ctx_len)
    pos = grow & (seq_len - 1)
    out = None
    for c in range(D // CN):
        cols = slice(c * CN, (c + 1) * CN)
        y = y_ref[:, cols].astype(F32)
        prev_row = yp_ref[HALO - 1:HALO, cols].astype(F32)
        next_row = yn_ref[0:1, cols].astype(F32)
        y_m1 = jnp.where(row == 0, prev_row, pltpu.roll(y, 1, 0))
        y_m1 = jnp.where(pos == 0, 0.0, y_m1)
        y_p1 = jnp.where(row == TM - 1, next_row, pltpu.roll(y, TM - 1, 0))
        y_p1 = jnp.where(pos == seq_len - 1, 0.0, y_p1)
        conv = cw_ref[0:1, cols] * y_m1 + cw_ref[1:2, cols] * y + cw_ref[2:3, cols] * y_p1
        z = (bg_ref[:, cols].astype(F32) * conv).astype(BF16)
        part = _dot(z, w_ref[cols, :])
        out = part if out is None else out + part
    o_ref[...] = h + m[2:3] * out


def _conv_out(rows, x, ctx, modg, bg, y, conv_w, w_out):
    hb = TM // HALO
    last = rows.n_all // HALO - 1
    return pl.pallas_call(
        functools.partial(_conv_out_kernel, rows.n_lat, rows.seq, rows.ctx_len),
        grid=(rows.all_tiles,),
        in_specs=rows.split_specs(D) + [rows.mod_spec(), rows.row_spec(D), rows.row_spec(D),
                  pl.BlockSpec((HALO, D), lambda i: (jnp.maximum(i * hb - 1, 0), 0)),
                  pl.BlockSpec((HALO, D), lambda i: (jnp.minimum((i + 1) * hb, last), 0)),
                  _resident((8, D)), _resident((D, D))],
        out_specs=rows.row_spec(D),
        out_shape=jax.ShapeDtypeStruct((rows.n_all, D), F32),
        compiler_params=_params(1),
        name="conv_out",
    )(x, ctx, modg, bg, y, y, y, conv_w, w_out)


W2_KC = 768


def _swiglu(a, w1_ref, w3_ref, w2_ref, t_ref):
    for c in range(FFN_DIM // CN):
        lo = c * CN
        h1 = _dot(a, w1_ref[:, lo:lo + CN].astype(BF16))
        h3 = _dot(a, w3_ref[:, lo:lo + CN].astype(BF16))
        t_ref[:, lo:lo + CN] = (h1 * _sigmoid(h1) * h3).astype(BF16)
    out = None
    for lo in range(0, FFN_DIM, W2_KC):
        hi = min(lo + W2_KC, FFN_DIM)
        part = _dot(t_ref[:, lo:hi], w2_ref[lo:hi, :].astype(BF16))
        out = part if out is None else out + part
    return out


def _ffn_kernel(h_ref, mod_ref, w1_ref, w3_ref, w2_ref, o_ref, t_ref):
    m = mod_ref[0]
    x = h_ref[...]
    a = _modnorm(x, *_ffn_mod(m)).astype(BF16)
    o_ref[...] = x + m[5:6] * _swiglu(a, w1_ref, w3_ref, w2_ref, t_ref)


def _ffn(rows, n_tiles, h, modg, layer, w1, w3, w2):
    def slab(d0, d1):
        return pl.BlockSpec((None, d0, d1), lambda i: (layer, 0, 0), pipeline_mode=pl.Buffered(1))

    return pl.pallas_call(
        _ffn_kernel,
        grid=(n_tiles,),
        in_specs=[rows.row_spec(D), rows.mod_spec(), slab(D, FFN_DIM), slab(D, FFN_DIM), slab(FFN_DIM, D)],
        out_specs=rows.row_spec(D),
        out_shape=jax.ShapeDtypeStruct((n_tiles * TM, D), F32),
        scratch_shapes=[pltpu.VMEM((TM, FFN_DIM), BF16)],
        compiler_params=_params(1),
        name="ffn_dense",
    )(h, modg, w1, w3, w2)


def _norm_rope(x, gmat_ref, cos, sin):
    ms = _dot((x * x).astype(BF16), gmat_ref[...]) * (1.0 / HEAD_DIM)
    xn = x * lax.rsqrt(ms + EPS)
    lane = lax.broadcasted_iota(jnp.int32, (1, CN), 1)
    first_half = (lane & (ROPE_HALF - 1)) < (ROPE_HALF // 2)
    partner = jnp.where(first_half, pltpu.roll(xn, CN - ROPE_HALF // 2, 1), pltpu.roll(xn, ROPE_HALF // 2, 1))
    return xn * cos + partner * sin


def _qkv_kernel(n_qk_chunks, n_q_chunks, h_ref, mod_ref, w_ref, gmat_ref, qcos_ref, qsin_ref, kcos_ref, ksin_ref,
                o_ref):
    a = _modnorm(h_ref[...], *_mix_mod(mod_ref[0])).astype(BF16)
    n_chunks = w_ref.shape[1] // CN

    def project(c):
        return _dot(a, w_ref[:, c * CN:(c + 1) * CN])

    nxt = project(0)
    for c in range(n_chunks):
        acc = nxt
        if c + 1 < n_chunks:
            nxt = project(c + 1)
        if c < n_q_chunks:
            acc = _norm_rope(acc, gmat_ref, qcos_ref[...], qsin_ref[...])
        elif c < n_qk_chunks:
            acc = _norm_rope(acc, gmat_ref, kcos_ref[...], ksin_ref[...])
        o_ref[:, c * CN:(c + 1) * CN] = acc.astype(BF16)


def _rope_tables(rows, gain, scale):
    seq = rows.seq
    pos = jnp.arange(seq)
    n_freq = ROPE_HALF // 2
    inv = ROPE_BASE ** (-jnp.arange(n_freq, dtype=F32) / n_freq)
    ang_r = (pos // GRID_W).astype(F32)[:, None] * inv
    ang_c = (pos % GRID_W).astype(F32)[:, None] * inv
    cos = jnp.concatenate([jnp.cos(ang_r)] * 2 + [jnp.cos(ang_c)] * 2, axis=-1)
    sin = jnp.concatenate([-jnp.sin(ang_r), jnp.sin(ang_r), -jnp.sin(ang_c), jnp.sin(ang_c)], axis=-1)
    cos = jnp.concatenate([cos, jnp.ones((TM, HEAD_DIM), F32)], axis=0)
    sin = jnp.concatenate([sin, jnp.zeros((TM, HEAD_DIM), F32)], axis=0)
    dim = jnp.arange(HEAD_DIM)
    partner = jnp.where((dim % ROPE_HALF) < n_freq, dim + n_freq, dim - n_freq)
    g = gain.astype(F32) * scale
    reps = CN // HEAD_DIM
    return jnp.tile(cos * g[None, :], (1, reps)), jnp.tile(sin * g[partner][None, :], (1, reps))


def _qkv(rows, h, modg, w, q_gain, k_gain, n_q_chunks, n_k_chunks):
    width = w.shape[1]
    head = jnp.arange(CN) // HEAD_DIM
    gmat = (head[:, None] == head[None, :]).astype(BF16)
    tables = _rope_tables(rows, q_gain, LOG2E * HEAD_DIM ** -0.5) + _rope_tables(rows, k_gain, 1.0)
    tps, lat_tiles = rows.tiles_per_seq, rows.lat_tiles
    tab_spec = pl.BlockSpec((TM, CN), lambda i: (jnp.where(i < lat_tiles, i % tps, tps), 0))
    return pl.pallas_call(
        functools.partial(_qkv_kernel, n_q_chunks + n_k_chunks, n_q_chunks),
        grid=(rows.all_tiles,),
        in_specs=[rows.row_spec(D), rows.mod_spec(), _resident((D, width)), _resident((CN, CN))] + [tab_spec] * 4,
        out_specs=rows.row_spec(width),
        out_shape=jax.ShapeDtypeStruct((rows.n_all, width), BF16),
        compiler_params=_params(1),
        name="qkv_proj",
    )(h, modg, w, gmat, *tables)


DH2 = 2 * HEAD_DIM


def _diff_attn_kernel(lam_init, n_lat_chunks, q_ref, kc_ref, vc_ref, kl_ref, vl_ref, lam_ref, subg_ref, small_ref,
                      o_ref, m_ref, acc_ref, vca_ref, vla_ref):
    small_logits = small_ref[0] != 0

    def online_setup():
        vca_ref[:, :DH2] = vc_ref[...]
        vca_ref[:, DH2:] = jnp.ones((vc_ref.shape[0], DH2), BF16)
        if n_lat_chunks:
            vla_ref[:, :DH2] = vl_ref[...]
            vla_ref[:, DH2:] = jnp.ones((vl_ref.shape[0], DH2), BF16)

    first_query_tile = pl.program_id(2) == 0 if n_lat_chunks else True
    pl.when(jnp.logical_and(first_query_tile, jnp.logical_not(small_logits)))(online_setup)

    q = q_ref[...]
    lane = lax.broadcasted_iota(jnp.int32, (1, DH2), 1)
    zero = jnp.zeros((), BF16)
    qs = (jnp.where(lane < HEAD_DIM, q, zero), jnp.where(lane >= HEAD_DIM, q, zero))
    nt = (((1,), (1,)), ((), ()))

    def over_keys(update):
        update(kc_ref[...], vca_ref[...])
        if n_lat_chunks:
            tk = kl_ref.shape[0] // n_lat_chunks

            def body(c, carry):
                start = pl.multiple_of(c * tk, tk)
                update(kl_ref[pl.ds(start, tk), :], vla_ref[pl.ds(start, tk), :])
                return carry
            lax.fori_loop(0, n_lat_chunks, body, 0)

    lp = lam_ref[...]
    lam = (jnp.exp(jnp.sum(lp[0:1] * lp[1:2], axis=-1, keepdims=True))
           - jnp.exp(jnp.sum(lp[2:3] * lp[3:4], axis=-1, keepdims=True)) + lam_init)

    def unshifted():
        sub = min(ATT_SUB, q.shape[0] // 2)
        n_sub = q.shape[0] // sub

        def logits(r):
            out = []
            for mi in range(2):
                qm = qs[mi][r * sub:(r + 1) * sub]
                s_c = lax.dot_general(qm, kc_ref[...], nt, preferred_element_type=F32)
                s_l = lax.dot_general(qm, kl_ref[...], nt, preferred_element_type=F32) if n_lat_chunks else None
                out.append((s_c, s_l))
            return out

        ahead = [logits(0), logits(1)]
        for r in range(n_sub):
            p_ctx, p_lat, sums = [], [], []
            for s_c, s_l in ahead.pop(0):
                e = jnp.exp2(s_c)
                total = jnp.sum(e, axis=-1, keepdims=True)
                p_ctx.append(e.astype(BF16))
                if s_l is not None:
                    e = jnp.exp2(s_l)
                    total = total + jnp.sum(e, axis=-1, keepdims=True)
                    p_lat.append(e.astype(BF16))
                sums.append(total)
            r0 = (1.0 / sums[0]).astype(BF16)
            r1 = (lam / sums[1]).astype(BF16)

            def weights(p):
                return p[0] * r0 - p[1] * r1

            o = _dot(weights(p_ctx), vc_ref[...])
            if n_lat_chunks:
                o = o + _dot(weights(p_lat), vl_ref[...])
            acc_ref[0, r * sub:(r + 1) * sub, :DH2] = o
            if r + 2 < n_sub:
                ahead.append(logits(r + 2))

    def update_online(k, va):
        reps = k.shape[0] // DH2
        for mi in range(2):
            s = lax.dot_general(qs[mi], k, nt, preferred_element_type=F32)
            m_old = m_ref[mi]
            m_new = jnp.maximum(m_old, jnp.max(s, axis=-1, keepdims=True))
            alpha = jnp.exp2(m_old - m_new)
            p = jnp.exp2(s - jnp.concatenate([m_new] * reps, axis=-1))
            acc_ref[mi] = jnp.concatenate([alpha, alpha], axis=-1) * acc_ref[mi] + _dot(p.astype(BF16), va)
            m_ref[mi] = m_new

    def online():
        m_ref[...] = jnp.full(m_ref.shape, -jnp.inf, F32)
        acc_ref[...] = jnp.zeros(acc_ref.shape, F32)
        over_keys(update_online)
        acc0 = acc_ref[0]
        acc1 = acc_ref[1]
        acc_ref[0, :, :DH2] = acc0[:, :DH2] / acc0[:, DH2:] - lam * (acc1[:, :DH2] / acc1[:, DH2:])

    pl.when(small_logits)(unshifted)
    pl.when(jnp.logical_not(small_logits))(online)

    o = acc_ref[0, :, :DH2]
    o = o * lax.rsqrt(jnp.mean(o * o, axis=-1, keepdims=True) + EPS) * subg_ref[...]
    o_ref[...] = (o * (1.0 - lam_init)).astype(BF16)


def _small_logits(q_gain, k_gain, sink=None):
    bound = 2 * HEAD_DIM * (LOG2E * HEAD_DIM ** -0.5) * jnp.max(jnp.abs(q_gain)) * jnp.max(jnp.abs(k_gain))
    ok = bound <= MAX_UNSHIFTED_LOGIT
    if sink is not None:
        ok = jnp.logical_and(ok, jnp.max(sink) * LOG2E <= MAX_UNSHIFTED_LOGIT)
    return ok.astype(jnp.int32).reshape(1)


def _diff_attn(rows, qkv, lam_p, sub_g, lam_init, small_logits):
    batch, seq, ctx_len = rows.batch, rows.seq, rows.ctx_len
    tq = min(ATT_TQ, seq)
    nq = seq // tq
    nh = DIFF_HEADS
    ctx0 = rows.n_lat // ctx_len
    sub_g = sub_g.reshape(1, DH2)
    small = [_resident((4, HEAD_DIM)), _resident((1, DH2)), pl.BlockSpec(memory_space=pltpu.SMEM)]
    kc_spec = pl.BlockSpec((ctx_len, DH2), lambda b, h, *_: (ctx0 + b, nh + h))
    vc_spec = pl.BlockSpec((ctx_len, DH2), lambda b, h, *_: (ctx0 + b, 2 * nh + h))

    def scratch(tq, n_lat_keys):
        return [pltpu.VMEM((2, tq, DH2), F32), pltpu.VMEM((2, tq, 2 * DH2), F32),
                pltpu.VMEM((ctx_len, 2 * DH2), BF16), pltpu.VMEM((n_lat_keys, 2 * DH2), BF16)]

    o_lat = pl.pallas_call(
        functools.partial(_diff_attn_kernel, lam_init, pl.cdiv(seq, ATT_TK)),
        grid=(batch, nh, nq),
        in_specs=[pl.BlockSpec((tq, DH2), lambda b, h, i: (b * nq + i, h)),
                  kc_spec, vc_spec,
                  pl.BlockSpec((seq, DH2), lambda b, h, i: (b, nh + h)),
                  pl.BlockSpec((seq, DH2), lambda b, h, i: (b, 2 * nh + h))] + small,
        out_specs=pl.BlockSpec((tq, DH2), lambda b, h, i: (b * nq + i, h)),
        out_shape=jax.ShapeDtypeStruct((rows.n_lat, D), BF16),
        scratch_shapes=scratch(tq, seq),
        compiler_params=_params(3),
        name="diff_attn_latent",
    )(qkv, qkv, qkv, qkv, qkv, lam_p, sub_g, small_logits)

    def ctx_kernel(q_ref, kc_ref, vc_ref, lam_ref, subg_ref, small_ref, o_ref, *scratch_refs):
        _diff_attn_kernel(lam_init, 0, q_ref, kc_ref, vc_ref, None, None, lam_ref, subg_ref, small_ref, o_ref,
                          *scratch_refs)

    o_ctx = pl.pallas_call(
        ctx_kernel,
        grid=(batch, nh),
        in_specs=[pl.BlockSpec((ctx_len, DH2), lambda b, h: (ctx0 + b, h)), kc_spec, vc_spec] + small,
        out_specs=pl.BlockSpec((ctx_len, DH2), lambda b, h: (b, h)),
        out_shape=jax.ShapeDtypeStruct((rows.n_ctx, D), BF16),
        scratch_shapes=scratch(ctx_len, 16),
        compiler_params=_params(2),
        name="diff_attn_context",
    )(qkv, qkv, qkv, lam_p, sub_g, small_logits)
    return o_lat, o_ctx


def _gmlp_kernel(meta_ref, h_ref, prev_mod_ref, slot_ref, ys_hbm, mod_ref, win_ref, bin_ref, vg_ref, ws_ref, bs_ref,
                 wout_ref, o_ref, u_ref, v_ref, t_ref, local_ref, sem):
    x = _combined_rows(meta_ref, h_ref, prev_mod_ref, slot_ref, ys_hbm, local_ref, sem)
    m = mod_ref[0]
    a = _modnorm(x, *_mix_mod(m)).astype(BF16)
    n_half = CM_WIDTH // CN

    def in_proj(c):
        lo = c * CN
        return _gelu_tanh(_dot(a, win_ref[:, lo:lo + CN]) + bin_ref[:, lo:lo + CN])

    ssq = jnp.zeros((TM, 1), F32)
    for c in range(n_half):
        z = in_proj(n_half + c)
        v_ref[:, c * CN:(c + 1) * CN] = z
        ssq = ssq + jnp.sum(z * z, axis=-1, keepdims=True)
    for c in range(n_half):
        u_ref[:, c * CN:(c + 1) * CN] = in_proj(c)
    inv = lax.rsqrt(ssq * (1.0 / CM_WIDTH) + EPS)

    def mix(g):
        lo = g * CM_GW
        vn = (v_ref[:, lo:lo + CM_GW] * inv * vg_ref[:, lo:lo + CM_GW]).astype(BF16)
        return [_dot(ws_ref[g], vn[r * CHUNK:(r + 1) * CHUNK]) + bs_ref[:, g:g + 1] for r in range(TM // CHUNK)]

    out = None
    nxt = mix(0)
    for g in range(CM_GROUPS):
        lo = g * CM_GW
        sv = nxt
        if g + 1 < CM_GROUPS:
            nxt = mix(g + 1)
        for r in range(TM // CHUNK):
            r0 = r * CHUNK
            t_ref[r0:r0 + CHUNK, lo:lo + CM_GW] = (u_ref[r0:r0 + CHUNK, lo:lo + CM_GW] * sv[r]).astype(BF16)
        part = _dot(t_ref[:, lo:lo + CM_GW], wout_ref[lo:lo + CM_GW, :])
        out = part if out is None else out + part
    o_ref[...] = x + m[2:3] * out


def _gmlp(rows, pending, modg, w_in, b_in, v_g, w_s, b_s, w_out):
    h, prev_modg, slots, ys, meta = pending
    return pl.pallas_call(
        _gmlp_kernel,
        grid_spec=pltpu.PrefetchScalarGridSpec(
            num_scalar_prefetch=1,
            grid=(rows.all_tiles,),
            in_specs=[rows.row_spec(D), rows.mod_spec(), rows.row_spec(ROUTE_W), pl.BlockSpec(memory_space=pl.ANY),
                      rows.mod_spec(), _resident((D, 2 * CM_WIDTH)), _resident((1, 2 * CM_WIDTH)),
                      _resident((1, CM_WIDTH)), _resident((CM_GROUPS, CHUNK, CHUNK)), _resident((CHUNK, CM_GROUPS)),
                      _resident((CM_WIDTH, D))],
            out_specs=rows.row_spec(D),
            scratch_shapes=[pltpu.VMEM((TM, CM_WIDTH), F32), pltpu.VMEM((TM, CM_WIDTH), F32),
                            pltpu.VMEM((TM, CM_WIDTH), BF16)] + _combine_scratch(),
        ),
        out_shape=jax.ShapeDtypeStruct((rows.n_all, D), F32),
        compiler_params=_params(1),
        name="moe_combine_gmlp",
    )(meta, h, prev_modg, slots, ys, modg, w_in, b_in, v_g, w_s, b_s, w_out)


SWA_BAND = SWA_TQ + 2 * SWA_WINDOW


SWA_PAIR = 2 * HEAD_DIM
N_KV_VARIANTS = 2 * SWA_KV_HEADS


def _swa_kernel(seq, sink_ref, small_ref, q_ref, kc_ref, vc_ref, kl_ref, vl_ref, o_ref, kcv_ref, vcv_ref, klv_ref,
                vlv_ref):
    qi = pl.program_id(1)
    lane = lax.broadcasted_iota(jnp.int32, (1, SWA_PAIR), 1)
    lo_half = lane < HEAD_DIM

    small_logits = small_ref[0] != 0

    @pl.when(jnp.logical_and(qi == 0, small_logits))
    def _per_batch_setup():
        for src_ref, dst_ref in ((kc_ref, kcv_ref), (vc_ref, vcv_ref), (kl_ref, klv_ref), (vl_ref, vlv_ref)):
            for half in range(SWA_KV_HEADS // 2):
                x = src_ref[:, half * SWA_PAIR:(half + 1) * SWA_PAIR].astype(F32)
                xr = pltpu.roll(x, HEAD_DIM, 1)
                j0, j1 = 2 * half, 2 * half + 1
                dst_ref[2 * j0] = jnp.where(lo_half, x, 0.0).astype(BF16)
                dst_ref[2 * j0 + 1] = jnp.where(lo_half, 0.0, xr).astype(BF16)
                dst_ref[2 * j1] = jnp.where(lo_half, xr, 0.0).astype(BF16)
                dst_ref[2 * j1 + 1] = jnp.where(lo_half, 0.0, x).astype(BF16)

    q0 = qi * SWA_TQ
    start = jnp.clip(q0 - SWA_WINDOW, 0, seq - SWA_BAND)
    start = pl.multiple_of(start, SWA_WINDOW)
    qpos = q0 + lax.broadcasted_iota(jnp.int32, (SWA_TQ, SWA_BAND), 0)
    kpos = start + lax.broadcasted_iota(jnp.int32, (SWA_TQ, SWA_BAND), 1)
    in_band = jnp.abs(qpos - kpos) <= SWA_WINDOW
    nt = (((1,), (1,)), ((), ()))

    def unshifted():
        in_band2 = jnp.concatenate([in_band, in_band], axis=0)
        top = lax.broadcasted_iota(jnp.int32, (2 * SWA_TQ, 1), 0) < SWA_TQ
        def logits(v):
            lo = (v // 2) * SWA_GROUP * HEAD_DIM
            q2 = jnp.concatenate([q_ref[:, lo:lo + SWA_PAIR], q_ref[:, lo + SWA_PAIR:lo + 2 * SWA_PAIR]], axis=0)
            return (lax.dot_general(q2, kcv_ref[v], nt, preferred_element_type=F32),
                    lax.dot_general(q2, klv_ref[v, pl.ds(start, SWA_BAND), :], nt, preferred_element_type=F32))

        nxt = logits(0)
        for j in range(SWA_KV_HEADS):
            lo = j * SWA_GROUP * HEAD_DIM
            out = jnp.zeros((2 * SWA_TQ, SWA_PAIR), F32)
            for var in range(2):
                v = 2 * j + var
                s_c, s_b = nxt
                if v + 1 < N_KV_VARIANTS:
                    nxt = logits(v + 1)
                p_c = jnp.exp2(s_c)
                p_b = jnp.where(in_band2, jnp.exp2(s_b), 0.0)
                hq = j * SWA_GROUP + var
                sink = jnp.where(top, sink_ref[hq] * LOG2E, sink_ref[hq + 2] * LOG2E)
                denom = (jnp.sum(p_c, axis=-1, keepdims=True) + jnp.sum(p_b, axis=-1, keepdims=True)
                         + jnp.exp2(sink))
                o = (_dot(p_c.astype(BF16), vcv_ref[v])
                     + _dot(p_b.astype(BF16), vlv_ref[v, pl.ds(start, SWA_BAND), :]))
                out = out + o / denom
            o_ref[:, lo:lo + SWA_PAIR] = out[:SWA_TQ].astype(BF16)
            o_ref[:, lo + SWA_PAIR:lo + 2 * SWA_PAIR] = out[SWA_TQ:].astype(BF16)

    def shifted():
        kb = kl_ref[pl.ds(start, SWA_BAND), :]
        vb = vl_ref[pl.ds(start, SWA_BAND), :]
        kc = kc_ref[...]
        vc = vc_ref[...]
        for j in range(SWA_KV_HEADS):
            kj = slice(j * HEAD_DIM, (j + 1) * HEAD_DIM)
            kbj, vbj, kcj, vcj = kb[:, kj], vb[:, kj], kc[:, kj], vc[:, kj]
            outs = []
            for g in range(SWA_GROUP):
                hq = j * SWA_GROUP + g
                qh = q_ref[:, hq * HEAD_DIM:(hq + 1) * HEAD_DIM]
                s_c = lax.dot_general(qh, kcj, nt, preferred_element_type=F32)
                s_b = lax.dot_general(qh, kbj, nt, preferred_element_type=F32)
                s_b = jnp.where(in_band, s_b, -jnp.inf)
                sink = sink_ref[hq] * LOG2E
                mx = jnp.maximum(jnp.maximum(jnp.max(s_c, axis=-1, keepdims=True),
                                             jnp.max(s_b, axis=-1, keepdims=True)), sink)
                p_c = jnp.exp2(s_c - mx)
                p_b = jnp.exp2(s_b - mx)
                denom = (jnp.sum(p_c, axis=-1, keepdims=True) + jnp.sum(p_b, axis=-1, keepdims=True)
                         + jnp.exp2(sink - mx))
                o = _dot(p_c.astype(BF16), vcj) + _dot(p_b.astype(BF16), vbj)
                outs.append(o / denom)
            lo = j * SWA_GROUP * HEAD_DIM
            o_ref[:, lo:lo + SWA_GROUP * HEAD_DIM] = jnp.concatenate(outs, axis=-1).astype(BF16)

    pl.when(small_logits)(unshifted)
    pl.when(jnp.logical_not(small_logits))(shifted)


def _swa(rows, qkv, sink, small_logits):
    batch, seq, ctx_len = rows.batch, rows.seq, rows.ctx_len
    nq = seq // SWA_TQ
    kvw = SWA_KV_HEADS * HEAD_DIM
    k_col = D // kvw
    ctx0 = rows.n_lat // ctx_len
    return pl.pallas_call(
        functools.partial(_swa_kernel, seq),
        grid=(batch, nq),
        in_specs=[pl.BlockSpec(memory_space=pltpu.SMEM), pl.BlockSpec(memory_space=pltpu.SMEM),
                  pl.BlockSpec((SWA_TQ, D), lambda b, i: (b * nq + i, 0)),
                  pl.BlockSpec((ctx_len, kvw), lambda b, i: (ctx0 + b, k_col)),
                  pl.BlockSpec((ctx_len, kvw), lambda b, i: (ctx0 + b, k_col + 1)),
                  pl.BlockSpec((seq, kvw), lambda b, i: (b, k_col)),
                  pl.BlockSpec((seq, kvw), lambda b, i: (b, k_col + 1))],
        out_specs=pl.BlockSpec((SWA_TQ, D), lambda b, i: (b * nq + i, 0)),
        out_shape=jax.ShapeDtypeStruct((rows.n_lat, D), BF16),
        scratch_shapes=[pltpu.VMEM((N_KV_VARIANTS, ctx_len, SWA_PAIR), BF16),
                        pltpu.VMEM((N_KV_VARIANTS, ctx_len, SWA_PAIR), BF16),
                        pltpu.VMEM((N_KV_VARIANTS, seq, SWA_PAIR), BF16),
                        pltpu.VMEM((N_KV_VARIANTS, seq, SWA_PAIR), BF16)],
        compiler_params=_params(2),
        name="swa_attn",
    )(sink, small_logits, qkv, qkv, qkv, qkv, qkv)


def _proj_route_kernel(lat_tiles, h_ref, mod_ref, zl_ref, zc_ref, w_ref, wr_ref, br_ref, hm_ref, a_ref, route_ref,
                       cnt_ref):
    i = pl.program_id(0)
    m = mod_ref[0]

    def project(z_ref):
        hm_ref[...] = h_ref[...] + m[2:3] * _dot(z_ref[...], w_ref[...])

    pl.when(i < lat_tiles)(lambda: project(zl_ref))
    pl.when(i >= lat_tiles)(lambda: project(zc_ref))
    a = _modnorm(hm_ref[...], *_ffn_mod(m))
    a_hi = a.astype(BF16)
    a_ref[...] = a_hi
    a_lo = (a - a_hi.astype(F32)).astype(BF16)
    hi_part = _dot(a_hi, wr_ref[...])
    logits = (hi_part[:, :ROUTE_W] + hi_part[:, ROUTE_W:]) + _dot(a_lo, wr_ref[:, :ROUTE_W]) + br_ref[...]
    lane = lax.broadcasted_iota(jnp.int32, (TM, ROUTE_W), 1)
    m1 = jnp.max(logits, axis=-1, keepdims=True)
    i1 = jnp.min(jnp.where(logits == m1, lane, ROUTE_W), axis=-1, keepdims=True)
    rest = jnp.where(lane == i1, -jnp.inf, logits)
    m2 = jnp.max(rest, axis=-1, keepdims=True)
    i2 = jnp.min(jnp.where(rest == m2, lane, ROUTE_W), axis=-1, keepdims=True)
    e2 = jnp.exp(m2 - m1)
    gate1 = 1.0 / (1.0 + e2)
    gate2 = e2 / (1.0 + e2)
    rec = jnp.where(lane == 0, i1.astype(F32), 0.0)
    rec = jnp.where(lane == 1, i2.astype(F32), rec)
    rec = jnp.where(lane == 2, gate1, rec)
    route_ref[...] = jnp.where(lane == 3, gate2, rec)
    chosen = jnp.where((lane == i1) | (lane == i2), 1.0, 0.0)
    cnt_ref[0] = jnp.broadcast_to(jnp.sum(chosen, axis=0, keepdims=True), (8, ROUTE_W))


def _proj_route(rows, n_tiles, h, modg, z_lat, z_ctx, w_out, w_r, b_r):
    lat_tiles = rows.lat_tiles
    pad = ROUTE_W - N_EXPERTS
    w_pad = jnp.pad(w_r, ((0, 0), (0, pad)))
    w_hi = w_pad.astype(BF16)
    w_lo = (w_pad - w_hi.astype(F32)).astype(BF16)
    w_split = jnp.concatenate([w_hi, w_lo], axis=1)
    b_pad = jnp.concatenate([b_r.astype(F32), jnp.full((pad,), -1e30, F32)]).reshape(1, ROUTE_W)
    return pl.pallas_call(
        functools.partial(_proj_route_kernel, lat_tiles),
        grid=(n_tiles,),
        in_specs=[rows.row_spec(D), rows.mod_spec(),
                  pl.BlockSpec((TM, D), lambda i: (jnp.minimum(i, lat_tiles - 1), 0)),
                  pl.BlockSpec((TM, D), lambda i: (jnp.maximum(i - lat_tiles, 0), 0)),
                  _resident((D, D)), _resident((D, 2 * ROUTE_W)), _resident((1, ROUTE_W))],
        out_specs=[rows.row_spec(D), rows.row_spec(D), rows.row_spec(ROUTE_W),
                   pl.BlockSpec((1, 8, ROUTE_W), lambda i: (i, 0, 0))],
        out_shape=[jax.ShapeDtypeStruct((n_tiles * TM, D), F32), jax.ShapeDtypeStruct((n_tiles * TM, D), BF16),
                   jax.ShapeDtypeStruct((n_tiles * TM, ROUTE_W), F32),
                   jax.ShapeDtypeStruct((n_tiles, 8, ROUTE_W), F32)],
        compiler_params=_params(1),
        name="out_proj_router",
    )(h, modg, z_lat, z_ctx, w_out, w_split, b_pad)


SEG_ALIGN = 16
SEG_BITS = tuple(1 << b for b in range(TM.bit_length() - 1, SEG_ALIGN.bit_length() - 2, -1))
LROWS = -(-(2 * TM + N_EXPERTS * (SEG_ALIGN - 1)) // 128) * 128
XS_W = D + 128
META_W = 3 * N_EXPERTS


def _segment_copies(meta_ref, tile, local_ref, slots_hbm, sem, to_slots):
    for e in range(N_EXPERTS):
        loff = meta_ref[tile * META_W + e]
        goff = meta_ref[tile * META_W + N_EXPERTS + e]
        seg = meta_ref[tile * META_W + 2 * N_EXPERTS + e]
        for bit in SEG_BITS:
            done = seg & ~(2 * bit - 1)
            lo = local_ref.at[pl.ds(pl.multiple_of(loff + done, SEG_ALIGN), bit)]
            gl = slots_hbm.at[pl.ds(pl.multiple_of(goff + done, SEG_ALIGN), bit)]
            copy = pltpu.make_async_copy(lo, gl, sem) if to_slots else pltpu.make_async_copy(gl, lo, sem)
            yield (seg & bit) != 0, copy


def _start_segment_copies(*args):
    for cond, copy in _segment_copies(*args):
        pl.when(cond)(copy.start)


def _wait_segment_copies(*args):
    for cond, copy in _segment_copies(*args):
        pl.when(cond)(copy.wait)


def _split3(x):
    hi = x.astype(BF16).astype(F32)
    mid = (x - hi).astype(BF16).astype(F32)
    lo = ((x - hi) - mid).astype(BF16).astype(F32)
    return hi, mid, lo


def _dispatch_kernel(n_blocks, meta_ref, fill_ref, a_ref, route_ref, tri_ref, xs_hbm, slot_ref, local_ref, zero_ref,
                     sem, zsem):
    i = pl.program_id(0)

    @pl.when(i == 0)
    def _zero_unwritten_slots():
        zero_ref[...] = jnp.zeros(zero_ref.shape, BF16)
        tails = []
        for e in range(N_EXPERTS):
            off, length = fill_ref[e], fill_ref[N_EXPERTS + e]
            for bit in (b for b in SEG_BITS if b < MOE_RB):
                done = length & ~(2 * bit - 1)
                dst = xs_hbm.at[pl.ds(pl.multiple_of(off + done, SEG_ALIGN), bit)]
                tails.append(((length & bit) != 0, pltpu.make_async_copy(zero_ref.at[pl.ds(0, bit)], dst, zsem)))
        for cond, copy in tails:
            pl.when(cond)(copy.start)

        def block_copy(blk):
            dst = xs_hbm.at[pl.ds(pl.multiple_of(blk * MOE_RB, MOE_RB), MOE_RB)]
            return pltpu.make_async_copy(zero_ref, dst, zsem)

        n_used = fill_ref[2 * N_EXPERTS]
        lax.fori_loop(n_used, n_blocks, lambda blk, c: (block_copy(blk).start(), c)[1], 0)
        for cond, copy in tails:
            pl.when(cond)(copy.wait)
        lax.fori_loop(n_used, n_blocks, lambda blk, c: (block_copy(blk).wait(), c)[1], 0)

    route = route_ref[...]
    lane = lax.broadcasted_iota(jnp.int32, (TM, ROUTE_W), 1)
    pick1 = lane == route[:, 0:1].astype(jnp.int32)
    pick2 = lane == route[:, 1:2].astype(jnp.int32)
    chosen = jnp.where(pick1 | pick2, 1.0, 0.0).astype(BF16)
    rank = _dot(tri_ref[...], chosen)
    lane1 = lax.broadcasted_iota(jnp.int32, (1, ROUTE_W), 1)
    loff = jnp.zeros((1, ROUTE_W), F32)
    for e in range(N_EXPERTS):
        loff = jnp.where(lane1 == e, meta_ref[i * META_W + e].astype(F32), loff)
    place = rank + loff
    slot1 = jnp.sum(jnp.where(pick1, place, 0.0), axis=-1, keepdims=True)
    slot2 = jnp.sum(jnp.where(pick2, place, 0.0), axis=-1, keepdims=True)
    slot_ref[...] = jnp.where(lane == 0, slot1, jnp.where(lane == 1, slot2, 0.0))

    row1 = jnp.transpose(jnp.broadcast_to(slot1, (TM, 128)))[0:1, :]
    row2 = jnp.transpose(jnp.broadcast_to(slot2, (TM, 128)))[0:1, :]
    lrow = lax.broadcasted_iota(jnp.int32, (LROWS, TM), 0).astype(F32)
    p1 = jnp.where(lrow == row1, 1.0, 0.0).astype(BF16)
    p2 = jnp.where(lrow == row2, 1.0, 0.0).astype(BF16)
    local_ref[:, :D] = _dot(p1 + p2, a_ref[...]).astype(BF16)
    gates = []
    for col in (2, 3):
        pieces = _split3(route[:, col:col + 1])
        g = jnp.zeros((TM, ROUTE_W), F32)
        for k, piece in enumerate(pieces):
            g = jnp.where(lane == k, piece, g)
        gates.append(g.astype(BF16))
    local_ref[:, D:] = (_dot(p1, gates[0]) + _dot(p2, gates[1])).astype(BF16)
    _start_segment_copies(meta_ref, i, local_ref, xs_hbm, sem, True)
    _wait_segment_copies(meta_ref, i, local_ref, xs_hbm, sem, True)


def _dispatch(n_tiles, n_blocks, a, route, meta, fill):
    tri = (jnp.arange(TM)[:, None] > jnp.arange(TM)[None, :]).astype(BF16)
    return pl.pallas_call(
        functools.partial(_dispatch_kernel, n_blocks),
        grid_spec=pltpu.PrefetchScalarGridSpec(
            num_scalar_prefetch=2,
            grid=(n_tiles,),
            in_specs=[pl.BlockSpec((TM, D), lambda i, *_: (i, 0)),
                      pl.BlockSpec((TM, ROUTE_W), lambda i, *_: (i, 0)),
                      _resident((TM, TM))],
            out_specs=[pl.BlockSpec(memory_space=pl.ANY), pl.BlockSpec((TM, ROUTE_W), lambda i, *_: (i, 0))],
            scratch_shapes=[pltpu.VMEM((LROWS, XS_W), BF16), pltpu.VMEM((MOE_RB, XS_W), BF16),
                            pltpu.SemaphoreType.DMA(()), pltpu.SemaphoreType.DMA(())],
        ),
        out_shape=[jax.ShapeDtypeStruct((n_blocks * MOE_RB, XS_W), BF16),
                   jax.ShapeDtypeStruct((n_tiles * TM, ROUTE_W), F32)],
        compiler_params=_params(1),
        name="moe_dispatch",
    )(meta, fill, a, route, tri)


def _expert_kernel(be_ref, fill_ref, xs_ref, w1_ref, w3_ref, w2_ref, ys_ref, t_ref):
    del be_ref
    used = pl.program_id(0) < fill_ref[2 * N_EXPERTS]

    @pl.when(used)
    def _ffn():
        y = _swiglu(xs_ref[:, :D], w1_ref, w3_ref, w2_ref, t_ref)
        gate = jnp.sum(xs_ref[:, D:].astype(F32), axis=-1, keepdims=True)
        ys_ref[...] = (gate * y).astype(BF16)

    @pl.when(jnp.logical_not(used))
    def _idle():
        ys_ref[...] = jnp.zeros(ys_ref.shape, BF16)


def _experts(xs, blk_exp, fill, layer, w1, w3, w2):
    def slab(d0, d1):
        return pl.BlockSpec((None, None, d0, d1), lambda i, be, fill: (layer, be[i], 0, 0),
                            pipeline_mode=pl.Buffered(1))

    n_blocks = blk_exp.shape[0]
    last_used = lambda i, fill: jnp.minimum(i, fill[2 * N_EXPERTS] - 1)
    return pl.pallas_call(
        _expert_kernel,
        grid_spec=pltpu.PrefetchScalarGridSpec(
            num_scalar_prefetch=2,
            grid=(n_blocks,),
            in_specs=[pl.BlockSpec((MOE_RB, XS_W), lambda i, be, fill: (last_used(i, fill), 0)),
                      slab(D, FFN_DIM), slab(D, FFN_DIM), slab(FFN_DIM, D)],
            out_specs=pl.BlockSpec((MOE_RB, D), lambda i, be, fill: (i, 0)),
            scratch_shapes=[pltpu.VMEM((MOE_RB, FFN_DIM), BF16)],
        ),
        out_shape=jax.ShapeDtypeStruct((n_blocks * MOE_RB, D), BF16),
        compiler_params=_params(1),
        name="moe_experts",
    )(blk_exp, fill, xs, w1, w3, w2)


def _combine_scratch():
    return [pltpu.VMEM((2, LROWS, D), BF16), pltpu.SemaphoreType.DMA((2,))]


def _combined_rows(meta_ref, h_ref, mod_ref, slot_ref, ys_hbm, local_ref, sem):
    i = pl.program_id(0)

    buf = i % 2

    def fetch(tile, b):
        return (meta_ref, tile, local_ref.at[b], ys_hbm, sem.at[b], False)

    @pl.when(i == 0)
    def _init():
        local_ref[...] = jnp.zeros(local_ref.shape, BF16)
        _start_segment_copies(*fetch(0, 0))

    @pl.when(i + 1 < pl.num_programs(0))
    def _prefetch():
        _start_segment_copies(*fetch(i + 1, 1 - buf))

    _wait_segment_copies(*fetch(i, buf))
    slots = slot_ref[...]
    lcol = lax.broadcasted_iota(jnp.int32, (TM, LROWS), 1).astype(F32)
    gather = jnp.where((lcol == slots[:, 0:1]) | (lcol == slots[:, 1:2]), 1.0, 0.0).astype(BF16)
    return h_ref[...] + mod_ref[0][5:6] * _dot(gather, local_ref[buf])


def _combine_kernel(meta_ref, h_ref, mod_ref, slot_ref, ys_hbm, o_ref, local_ref, sem):
    o_ref[...] = _combined_rows(meta_ref, h_ref, mod_ref, slot_ref, ys_hbm, local_ref, sem)


def _combine(rows, n_tiles, h, modg, slots, ys, meta):
    tps, batch = rows.tiles_per_seq, rows.batch
    return pl.pallas_call(
        _combine_kernel,
        grid_spec=pltpu.PrefetchScalarGridSpec(
            num_scalar_prefetch=1,
            grid=(n_tiles,),
            in_specs=[pl.BlockSpec((TM, D), lambda i, m: (i, 0)),
                      pl.BlockSpec((1, 8, D), lambda i, m: (jnp.minimum(i // tps, batch), 0, 0)),
                      pl.BlockSpec((TM, ROUTE_W), lambda i, m: (i, 0)),
                      pl.BlockSpec(memory_space=pl.ANY)],
            out_specs=pl.BlockSpec((TM, D), lambda i, m: (i, 0)),
            scratch_shapes=_combine_scratch(),
        ),
        out_shape=jax.ShapeDtypeStruct((n_tiles * TM, D), F32),
        compiler_params=_params(1),
        name="moe_combine",
    )(meta, h, modg, slots, ys)


def _mixer_out_moe(rows, n_tiles, h, modg, z_lat, z_ctx, w_out, w_r, b_r, layer, w1, w3, w2, defer_combine):
    h, a, route, cnt = _proj_route(rows, n_tiles, h, modg, z_lat, z_ctx, w_out, w_r, b_r)
    cnt = cnt[:, 0, :N_EXPERTS].astype(jnp.int32)
    seg = (cnt + SEG_ALIGN - 1) // SEG_ALIGN * SEG_ALIGN
    loff = jnp.cumsum(seg, axis=1) - seg
    total = jnp.sum(seg, axis=0)
    region = (total + MOE_RB - 1) // MOE_RB * MOE_RB
    region_end = jnp.cumsum(region)
    region_start = region_end - region
    goff = region_start[None, :] + jnp.cumsum(seg, axis=0) - seg
    meta = jnp.concatenate([loff, goff, seg], axis=1).reshape(-1)
    n_blocks = pl.cdiv(2 * n_tiles * TM + n_tiles * N_EXPERTS * (SEG_ALIGN - 1), MOE_RB) + N_EXPERTS
    n_used = region_end[-1] // MOE_RB
    fill = jnp.concatenate([region_start + total, region - total, n_used[None]]).astype(jnp.int32)
    blk = jnp.minimum(jnp.arange(n_blocks, dtype=jnp.int32), n_used - 1)
    blk_exp = jnp.sum((region_end[None, :] <= (blk * MOE_RB)[:, None]).astype(jnp.int32), axis=1)
    blk_exp = jnp.minimum(blk_exp, N_EXPERTS - 1)
    xs, slots = _dispatch(n_tiles, n_blocks, a, route, meta, fill)
    ys = _experts(xs, blk_exp, fill, layer, w1, w3, w2)
    if defer_combine:
        return h, modg, slots, ys, meta
    return _combine(rows, n_tiles, h, modg, slots, ys, meta)


def kernel(x, c, ctx, c_ctx, ada_w, ada_b, norm_mix_g, norm_ffn_g, sc_in_w, sc_conv_w, sc_out_w, da_qkv_w, da_out_w,
           da_q_norm_g, da_k_norm_g, da_lambda, da_sub_norm_g, cm_in_w, cm_in_b, cm_v_norm_g, cm_ws, cm_bs, cm_out_w,
           sw_qkv_w, sw_out_w, sw_q_norm_g, sw_k_norm_g, sw_sink, ffn_w1, ffn_w3, ffn_w2, moe_router_w, moe_router_b,
           moe_w1, moe_w3, moe_w2):
    batch, seq, _ = x.shape
    ctx_len = ctx.shape[1]
    depth = ada_w.shape[0]
    assert depth == 4 and batch + 1 <= 16
    rows = _Rows(batch, seq, ctx_len)

    cvec = jnp.concatenate([c, c_ctx[None, :], jnp.zeros((16 - batch - 1, D), F32)], axis=0)
    mod = _ada_all(cvec, ada_w, ada_b)[:, :batch + 1].reshape(depth, batch + 1, 6, D)
    gains = jnp.stack([norm_mix_g, norm_ffn_g], axis=1)[:, None]
    modg = jnp.concatenate([mod, jnp.broadcast_to(gains, (depth, batch + 1, 2, D))], axis=2)

    bf = lambda w: w.astype(BF16)
    x2, ctx2 = x.reshape(-1, D), ctx.reshape(-1, D)

    bg, y = _conv_in(rows, x2, ctx2, modg[0], bf(sc_in_w[0]))
    conv_w = jnp.pad(sc_conv_w[0], ((0, 5), (0, 0)))
    h = _conv_out(rows, x2, ctx2, modg[0], bg, y, conv_w, bf(sc_out_w[0]))
    h = _ffn(rows, rows.all_tiles, h, modg[0], 0, ffn_w1, ffn_w3, ffn_w2)

    nq_chunks = D // CN
    qkv = _qkv(rows, h, modg[1], bf(da_qkv_w[0]), da_q_norm_g[0], da_k_norm_g[0], nq_chunks, nq_chunks)
    lam_init = 0.8 - 0.6 * math.exp(-0.3 * 1)
    o_lat, o_ctx = _diff_attn(rows, qkv, da_lambda[0], da_sub_norm_g[0], lam_init,
                              _small_logits(da_q_norm_g[0], da_k_norm_g[0]))
    pending = _mixer_out_moe(rows, rows.all_tiles, h, modg[1], o_lat, o_ctx, bf(da_out_w[0]), moe_router_w[0],
                             moe_router_b[0], 0, moe_w1, moe_w3, moe_w2, defer_combine=True)

    h = _gmlp(rows, pending, modg[2], bf(cm_in_w[0]), cm_in_b[0].reshape(1, -1), cm_v_norm_g[0].reshape(1, -1),
              bf(cm_ws[0]), cm_bs[0].T, bf(cm_out_w[0]))
    h = _ffn(rows, rows.all_tiles, h, modg[2], 1, ffn_w1, ffn_w3, ffn_w2)

    kv_chunks = SWA_KV_HEADS * HEAD_DIM // CN
    qkv = _qkv(rows, h, modg[3], bf(sw_qkv_w[0]), sw_q_norm_g[0], sw_k_norm_g[0], nq_chunks, kv_chunks)
    o = _swa(rows, qkv, sw_sink[0], _small_logits(sw_q_norm_g[0], sw_k_norm_g[0], sw_sink[0]))
    h = _mixer_out_moe(rows, rows.lat_tiles, h, modg[3], o, o, bf(sw_out_w[0]), moe_router_w[1], moe_router_b[1], 1,
                       moe_w1, moe_w3, moe_w2, defer_combine=False)
    return h.reshape(batch, seq, D)
```

```python
import functools
import math

import jax
import jax.numpy as jnp
from jax import lax
from jax.experimental import pallas as pl
from jax.experimental.pallas import tpu as pltpu

D = 1024
HEAD_DIM = 64
GRID_W = 64
ROPE_HALF = HEAD_DIM // 2
ROPE_BASE = 10000.0
DIFF_HEADS = D // (2 * HEAD_DIM)
SWA_Q_HEADS = D // HEAD_DIM
SWA_KV_HEADS = 4
SWA_GROUP = SWA_Q_HEADS // SWA_KV_HEADS
SWA_WINDOW = 128
CHUNK = 128
CM_WIDTH = 2 * D
CM_GROUPS = 8
CM_GW = CM_WIDTH // CM_GROUPS
FFN_DIM = 2816
N_EXPERTS = 8
EPS = 1e-6

F32 = jnp.float32
BF16 = jnp.bfloat16
HIGHEST = lax.Precision.HIGHEST
LOG2E = math.log2(math.e)

TM = 512
CN = 256
MOE_RB = 512
ATT_TQ = 1024
ATT_SUB = 256
ATT_TK = 1024
SWA_TQ = 256
ROUTE_W = 128
MAX_UNSHIFTED_LOGIT = 60.0
VMEM_LIMIT = 56 << 20


def _params(n_grid):
    return pltpu.CompilerParams(dimension_semantics=("arbitrary",) * n_grid, vmem_limit_bytes=VMEM_LIMIT)


def _resident(shape):
    zeros = (0,) * len(shape)
    return pl.BlockSpec(shape, lambda *_: zeros, pipeline_mode=pl.Buffered(1))


def _sigmoid(x):
    return 1.0 / (1.0 + jnp.exp(-x))


def _gelu_tanh(x):
    return 0.5 * x * (1.0 + jnp.tanh(math.sqrt(2.0 / math.pi) * (x + 0.044715 * (x * x * x))))


def _modnorm(x, g, shift, scale):
    y = x * lax.rsqrt(jnp.mean(x * x, axis=-1, keepdims=True) + EPS)
    return (y * g) * (1.0 + scale) + shift


def _dot(a, b):
    return jnp.dot(a, b, preferred_element_type=F32)


def _mix_mod(m):
    return m[6:7], m[0:1], m[1:2]


def _ffn_mod(m):
    return m[7:8], m[3:4], m[4:5]


def _ada_kernel(c_ref, w_ref, b_ref, o_ref):
    c = c_ref[...]
    s = c * _sigmoid(c)
    o_ref[0] = jnp.dot(s, w_ref[0], precision=HIGHEST, preferred_element_type=F32) + b_ref[0]


def _ada_all(cvec, ada_w, ada_b):
    depth, _, n = ada_w.shape
    tn = 1536
    rows = cvec.shape[0]
    return pl.pallas_call(
        _ada_kernel,
        grid=(depth, n // tn),
        in_specs=[pl.BlockSpec((rows, D), lambda l, j: (0, 0)),
                  pl.BlockSpec((1, D, tn), lambda l, j: (l, 0, j)),
                  pl.BlockSpec((1, 1, tn), lambda l, j: (l, 0, j))],
        out_specs=pl.BlockSpec((1, rows, tn), lambda l, j: (l, 0, j)),
        out_shape=jax.ShapeDtypeStruct((depth, rows, n), F32),
        compiler_params=_params(2),
        name="adaln",
    )(cvec, ada_w, ada_b.reshape(depth, 1, n))


class _Rows:
    def __init__(self, batch, seq, ctx_len):
        self.batch, self.seq, self.ctx_len = batch, seq, ctx_len
        self.n_lat = batch * seq
        self.n_ctx = batch * ctx_len
        self.n_all = self.n_lat + self.n_ctx
        assert seq % TM == 0 and self.n_ctx % TM == 0 and TM % ctx_len == 0
        self.lat_tiles = self.n_lat // TM
        self.all_tiles = self.n_all // TM
        self.tiles_per_seq = seq // TM

    def mod_spec(self):
        tps, batch = self.tiles_per_seq, self.batch
        return pl.BlockSpec((1, 8, D), lambda i, *_: (jnp.minimum(i // tps, batch), 0, 0))

    def row_spec(self, width):
        return pl.BlockSpec((TM, width), lambda i, *_: (i, 0))

    def split_specs(self, width):
        lat_tiles = self.lat_tiles
        return [pl.BlockSpec((TM, width), lambda i, *_: (jnp.minimum(i, lat_tiles - 1), 0)),
                pl.BlockSpec((TM, width), lambda i, *_: (jnp.maximum(i - lat_tiles, 0), 0))]


def _conv_in_kernel(lat_tiles, x_ref, ctx_ref, mod_ref, w_ref, bg_ref, y_ref):
    h = jnp.where(pl.program_id(0) < lat_tiles, x_ref[...], ctx_ref[...])
    a = _modnorm(h, *_mix_mod(mod_ref[0])).astype(BF16)
    for c in range(D // CN):
        lo = c * CN
        bg_ref[:, lo:lo + CN] = _dot(a, w_ref[:, lo:lo + CN]).astype(BF16)
        cg = _dot(a, w_ref[:, D + lo:D + lo + CN])
        xv = _dot(a, w_ref[:, 2 * D + lo:2 * D + lo + CN])
        y_ref[:, lo:lo + CN] = (cg * xv).astype(BF16)


def _conv_in(rows, x, ctx, modg, w_in):
    out = jax.ShapeDtypeStruct((rows.n_all, D), BF16)
    return pl.pallas_call(
        functools.partial(_conv_in_kernel, rows.lat_tiles),
        grid=(rows.all_tiles,),
        in_specs=rows.split_specs(D) + [rows.mod_spec(), _resident((D, 3 * D))],
        out_specs=[rows.row_spec(D), rows.row_spec(D)],
        out_shape=[out, out],
        compiler_params=_params(1),
        name="conv_in",
    )(x, ctx, modg, w_in)


HALO = 16


def _conv_out_kernel(n_lat, seq, ctx_len, x_ref, ctx_ref, mod_ref, bg_ref, y_ref, yp_ref, yn_ref, cw_ref, w_ref,
                     o_ref):
    i = pl.program_id(0)
    h = jnp.where(i * TM < n_lat, x_ref[...], ctx_ref[...])
    m = mod_ref[0]
    row = lax.broadcasted_iota(jnp.int32, (TM, 1), 0)
    grow = row + i * TM
    seq_len = jnp.where(grow < n_lat, seq, ctx_len)
    pos = grow & (seq_len - 1)
    out = None
    for c in range(D // CN):
        cols = slice(c * CN, (c + 1) * CN)
        y = y_ref[:, cols].astype(F32)
        prev_row = yp_ref[HALO - 1:HALO, cols].astype(F32)
        next_row = yn_ref[0:1, cols].astype(F32)
        y_m1 = jnp.where(row == 0, prev_row, pltpu.roll(y, 1, 0))
        y_m1 = jnp.where(pos == 0, 0.0, y_m1)
        y_p1 = jnp.where(row == TM - 1, next_row, pltpu.roll(y, TM - 1, 0))
        y_p1 = jnp.where(pos == seq_len - 1, 0.0, y_p1)
        conv = cw_ref[0:1, cols] * y_m1 + cw_ref[1:2, cols] * y + cw_ref[2:3, cols] * y_p1
        z = (bg_ref[:, cols].astype(F32) * conv).astype(BF16)
        part = _dot(z, w_ref[cols, :])
        out = part if out is None else out + part
    o_ref[...] = h + m[2:3] * out


def _conv_out(rows, x, ctx, modg, bg, y, conv_w, w_out):
    hb = TM // HALO
    last = rows.n_all // HALO - 1
    return pl.pallas_call(
        functools.partial(_conv_out_kernel, rows.n_lat, rows.seq, rows.ctx_len),
        grid=(rows.all_tiles,),
        in_specs=rows.split_specs(D) + [rows.mod_spec(), rows.row_spec(D), rows.row_spec(D),
                  pl.BlockSpec((HALO, D), lambda i: (jnp.maximum(i * hb - 1, 0), 0)),
                  pl.BlockSpec((HALO, D), lambda i: (jnp.minimum((i + 1) * hb, last), 0)),
                  _resident((8, D)), _resident((D, D))],
        out_specs=rows.row_spec(D),
        out_shape=jax.ShapeDtypeStruct((rows.n_all, D), F32),
        compiler_params=_params(1),
        name="conv_out",
    )(x, ctx, modg, bg, y, y, y, conv_w, w_out)


W2_KC = 768


def _swiglu(a, w1_ref, w3_ref, w2_ref, t_ref):
    for c in range(FFN_DIM // CN):
        lo = c * CN
        h1 = _dot(a, w1_ref[:, lo:lo + CN].astype(BF16))
        h3 = _dot(a, w3_ref[:, lo:lo + CN].astype(BF16))
        t_ref[:, lo:lo + CN] = (h1 * _sigmoid(h1) * h3).astype(BF16)
    out = None
    for lo in range(0, FFN_DIM, W2_KC):
        hi = min(lo + W2_KC, FFN_DIM)
        part = _dot(t_ref[:, lo:hi], w2_ref[lo:hi, :].astype(BF16))
        out = part if out is None else out + part
    return out


def _ffn_kernel(h_ref, mod_ref, w1_ref, w3_ref, w2_ref, o_ref, t_ref):
    m = mod_ref[0]
    x = h_ref[...]
    a = _modnorm(x, *_ffn_mod(m)).astype(BF16)
    o_ref[...] = x + m[5:6] * _swiglu(a, w1_ref, w3_ref, w2_ref, t_ref)


def _ffn(rows, n_tiles, h, modg, layer, w1, w3, w2):
    def slab(d0, d1):
        return pl.BlockSpec((None, d0, d1), lambda i: (layer, 0, 0), pipeline_mode=pl.Buffered(1))

    return pl.pallas_call(
        _ffn_kernel,
        grid=(n_tiles,),
        in_specs=[rows.row_spec(D), rows.mod_spec(), slab(D, FFN_DIM), slab(D, FFN_DIM), slab(FFN_DIM, D)],
        out_specs=rows.row_spec(D),
        out_shape=jax.ShapeDtypeStruct((n_tiles * TM, D), F32),
        scratch_shapes=[pltpu.VMEM((TM, FFN_DIM), BF16)],
        compiler_params=_params(1),
        name="ffn_dense",
    )(h, modg, w1, w3, w2)


def _norm_rope(x, gmat_ref, cos, sin):
    ms = _dot((x * x).astype(BF16), gmat_ref[...]) * (1.0 / HEAD_DIM)
    xn = x * lax.rsqrt(ms + EPS)
    lane = lax.broadcasted_iota(jnp.int32, (1, CN), 1)
    first_half = (lane & (ROPE_HALF - 1)) < (ROPE_HALF // 2)
    partner = jnp.where(first_half, pltpu.roll(xn, CN - ROPE_HALF // 2, 1), pltpu.roll(xn, ROPE_HALF // 2, 1))
    return xn * cos + partner * sin


def _qkv_kernel(n_qk_chunks, n_q_chunks, h_ref, mod_ref, w_ref, gmat_ref, qcos_ref, qsin_ref, kcos_ref, ksin_ref,
                o_ref):
    a = _modnorm(h_ref[...], *_mix_mod(mod_ref[0])).astype(BF16)
    n_chunks = w_ref.shape[1] // CN

    def project(c):
        return _dot(a, w_ref[:, c * CN:(c + 1) * CN])

    nxt = project(0)
    for c in range(n_chunks):
        acc = nxt
        if c + 1 < n_chunks:
            nxt = project(c + 1)
        if c < n_q_chunks:
            acc = _norm_rope(acc, gmat_ref, qcos_ref[...], qsin_ref[...])
        elif c < n_qk_chunks:
            acc = _norm_rope(acc, gmat_ref, kcos_ref[...], ksin_ref[...])
        o_ref[:, c * CN:(c + 1) * CN] = acc.astype(BF16)


def _rope_tables(rows, gain, scale):
    seq = rows.seq
    pos = jnp.arange(seq)
    n_freq = ROPE_HALF // 2
    inv = ROPE_BASE ** (-jnp.arange(n_freq, dtype=F32) / n_freq)
    ang_r = (pos // GRID_W).astype(F32)[:, None] * inv
    ang_c = (pos % GRID_W).astype(F32)[:, None] * inv
    cos = jnp.concatenate([jnp.cos(ang_r)] * 2 + [jnp.cos(ang_c)] * 2, axis=-1)
    sin = jnp.concatenate([-jnp.sin(ang_r), jnp.sin(ang_r), -jnp.sin(ang_c), jnp.sin(ang_c)], axis=-1)
    cos = jnp.concatenate([cos, jnp.ones((TM, HEAD_DIM), F32)], axis=0)
    sin = jnp.concatenate([sin, jnp.zeros((TM, HEAD_DIM), F32)], axis=0)
    dim = jnp.arange(HEAD_DIM)
    partner = jnp.where((dim % ROPE_HALF) < n_freq, dim + n_freq, dim - n_freq)
    g = gain.astype(F32) * scale
    reps = CN // HEAD_DIM
    return jnp.tile(cos * g[None, :], (1, reps)), jnp.tile(sin * g[partner][None, :], (1, reps))


def _qkv(rows, h, modg, w, q_gain, k_gain, n_q_chunks, n_k_chunks):
    width = w.shape[1]
    head = jnp.arange(CN) // HEAD_DIM
    gmat = (head[:, None] == head[None, :]).astype(BF16)
    tables = _rope_tables(rows, q_gain, LOG2E * HEAD_DIM ** -0.5) + _rope_tables(rows, k_gain, 1.0)
    tps, lat_tiles = rows.tiles_per_seq, rows.lat_tiles
    tab_spec = pl.BlockSpec((TM, CN), lambda i: (jnp.where(i < lat_tiles, i % tps, tps), 0))
    return pl.pallas_call(
        functools.partial(_qkv_kernel, n_q_chunks + n_k_chunks, n_q_chunks),
        grid=(rows.all_tiles,),
        in_specs=[rows.row_spec(D), rows.mod_spec(), _resident((D, width)), _resident((CN, CN))] + [tab_spec] * 4,
        out_specs=rows.row_spec(width),
        out_shape=jax.ShapeDtypeStruct((rows.n_all, width), BF16),
        compiler_params=_params(1),
        name="qkv_proj",
    )(h, modg, w, gmat, *tables)


DH2 = 2 * HEAD_DIM


def _diff_attn_kernel(lam_init, n_lat_chunks, q_ref, kc_ref, vc_ref, kl_ref, vl_ref, lam_ref, subg_ref, small_ref,
                      o_ref, m_ref, acc_ref, vca_ref, vla_ref):
    small_logits = small_ref[0] != 0

    def online_setup():
        vca_ref[:, :DH2] = vc_ref[...]
        vca_ref[:, DH2:] = jnp.ones((vc_ref.shape[0], DH2), BF16)
        if n_lat_chunks:
            vla_ref[:, :DH2] = vl_ref[...]
            vla_ref[:, DH2:] = jnp.ones((vl_ref.shape[0], DH2), BF16)

    first_query_tile = pl.program_id(2) == 0 if n_lat_chunks else True
    pl.when(jnp.logical_and(first_query_tile, jnp.logical_not(small_logits)))(online_setup)

    q = q_ref[...]
    lane = lax.broadcasted_iota(jnp.int32, (1, DH2), 1)
    zero = jnp.zeros((), BF16)
    qs = (jnp.where(lane < HEAD_DIM, q, zero), jnp.where(lane >= HEAD_DIM, q, zero))
    nt = (((1,), (1,)), ((), ()))

    def over_keys(update):
        update(kc_ref[...], vca_ref[...])
        if n_lat_chunks:
            tk = kl_ref.shape[0] // n_lat_chunks

            def body(c, carry):
                start = pl.multiple_of(c * tk, tk)
                update(kl_ref[pl.ds(start, tk), :], vla_ref[pl.ds(start, tk), :])
                return carry
            lax.fori_loop(0, n_lat_chunks, body, 0)

    lp = lam_ref[...]
    lam = (jnp.exp(jnp.sum(lp[0:1] * lp[1:2], axis=-1, keepdims=True))
           - jnp.exp(jnp.sum(lp[2:3] * lp[3:4], axis=-1, keepdims=True)) + lam_init)

    def unshifted():
        sub = min(ATT_SUB, q.shape[0] // 2)
        n_sub = q.shape[0] // sub

        def logits(r):
            out = []
            for mi in range(2):
                qm = qs[mi][r * sub:(r + 1) * sub]
                s_c = lax.dot_general(qm, kc_ref[...], nt, preferred_element_type=F32)
                s_l = lax.dot_general(qm, kl_ref[...], nt, preferred_element_type=F32) if n_lat_chunks else None
                out.append((s_c, s_l))
            return out

        ahead = [logits(0), logits(1)]
        for r in range(n_sub):
            p_ctx, p_lat, sums = [], [], []
            for s_c, s_l in ahead.pop(0):
                e = jnp.exp2(s_c)
                total = jnp.sum(e, axis=-1, keepdims=True)
                p_ctx.append(e.astype(BF16))
                if s_l is not None:
                    e = jnp.exp2(s_l)
                    total = total + jnp.sum(e, axis=-1, keepdims=True)
                    p_lat.append(e.astype(BF16))
                sums.append(total)
            r0 = (1.0 / sums[0]).astype(BF16)
            r1 = (lam / sums[1]).astype(BF16)

            def weights(p):
                return p[0] * r0 - p[1] * r1

            o = _dot(weights(p_ctx), vc_ref[...])
            if n_lat_chunks:
                o = o + _dot(weights(p_lat), vl_ref[...])
            acc_ref[0, r * sub:(r + 1) * sub, :DH2] = o
            if r + 2 < n_sub:
                ahead.append(logits(r + 2))

    def update_online(k, va):
        reps = k.shape[0] // DH2
        for mi in range(2):
            s = lax.dot_general(qs[mi], k, nt, preferred_element_type=F32)
            m_old = m_ref[mi]
            m_new = jnp.maximum(m_old, jnp.max(s, axis=-1, keepdims=True))
            alpha = jnp.exp2(m_old - m_new)
            p = jnp.exp2(s - jnp.concatenate([m_new] * reps, axis=-1))
            acc_ref[mi] = jnp.concatenate([alpha, alpha], axis=-1) * acc_ref[mi] + _dot(p.astype(BF16), va)
            m_ref[mi] = m_new

    def online():
        m_ref[...] = jnp.full(m_ref.shape, -jnp.inf, F32)
        acc_ref[...] = jnp.zeros(acc_ref.shape, F32)
        over_keys(update_online)
        acc0 = acc_ref[0]
        acc1 = acc_ref[1]
        acc_ref[0, :, :DH2] = acc0[:, :DH2] / acc0[:, DH2:] - lam * (acc1[:, :DH2] / acc1[:, DH2:])

    pl.when(small_logits)(unshifted)
    pl.when(jnp.logical_not(small_logits))(online)

    o = acc_ref[0, :, :DH2]
    o = o * lax.rsqrt(jnp.mean(o * o, axis=-1, keepdims=True) + EPS) * subg_ref[...]
    o_ref[...] = (o * (1.0 - lam_init)).astype(BF16)


def _small_logits(q_gain, k_gain, sink=None):
    bound = 2 * HEAD_DIM * (LOG2E * HEAD_DIM ** -0.5) * jnp.max(jnp.abs(q_gain)) * jnp.max(jnp.abs(k_gain))
    ok = bound <= MAX_UNSHIFTED_LOGIT
    if sink is not None:
        ok = jnp.logical_and(ok, jnp.max(sink) * LOG2E <= MAX_UNSHIFTED_LOGIT)
    return ok.astype(jnp.int32).reshape(1)


def _diff_attn(rows, qkv, lam_p, sub_g, lam_init, small_logits):
    batch, seq, ctx_len = rows.batch, rows.seq, rows.ctx_len
    tq = min(ATT_TQ, seq)
    nq = seq // tq
    nh = DIFF_HEADS
    ctx0 = rows.n_lat // ctx_len
    sub_g = sub_g.reshape(1, DH2)
    small = [_resident((4, HEAD_DIM)), _resident((1, DH2)), pl.BlockSpec(memory_space=pltpu.SMEM)]
    kc_spec = pl.BlockSpec((ctx_len, DH2), lambda b, h, *_: (ctx0 + b, nh + h))
    vc_spec = pl.BlockSpec((ctx_len, DH2), lambda b, h, *_: (ctx0 + b, 2 * nh + h))

    def scratch(tq, n_lat_keys):
        return [pltpu.VMEM((2, tq, DH2), F32), pltpu.VMEM((2, tq, 2 * DH2), F32),
                pltpu.VMEM((ctx_len, 2 * DH2), BF16), pltpu.VMEM((n_lat_keys, 2 * DH2), BF16)]

    o_lat = pl.pallas_call(
        functools.partial(_diff_attn_kernel, lam_init, pl.cdiv(seq, ATT_TK)),
        grid=(batch, nh, nq),
        in_specs=[pl.BlockSpec((tq, DH2), lambda b, h, i: (b * nq + i, h)),
                  kc_spec, vc_spec,
                  pl.BlockSpec((seq, DH2), lambda b, h, i: (b, nh + h)),
                  pl.BlockSpec((seq, DH2), lambda b, h, i: (b, 2 * nh + h))] + small,
        out_specs=pl.BlockSpec((tq, DH2), lambda b, h, i: (b * nq + i, h)),
        out_shape=jax.ShapeDtypeStruct((rows.n_lat, D), BF16),
        scratch_shapes=scratch(tq, seq),
        compiler_params=_params(3),
        name="diff_attn_latent",
    )(qkv, qkv, qkv, qkv, qkv, lam_p, sub_g, small_logits)

    def ctx_kernel(q_ref, kc_ref, vc_ref, lam_ref, subg_ref, small_ref, o_ref, *scratch_refs):
        _diff_attn_kernel(lam_init, 0, q_ref, kc_ref, vc_ref, None, None, lam_ref, subg_ref, small_ref, o_ref,
                          *scratch_refs)

    o_ctx = pl.pallas_call(
        ctx_kernel,
        grid=(batch, nh),
        in_specs=[pl.BlockSpec((ctx_len, DH2), lambda b, h: (ctx0 + b, h)), kc_spec, vc_spec] + small,
        out_specs=pl.BlockSpec((ctx_len, DH2), lambda b, h: (b, h)),
        out_shape=jax.ShapeDtypeStruct((rows.n_ctx, D), BF16),
        scratch_shapes=scratch(ctx_len, 16),
        compiler_params=_params(2),
        name="diff_attn_context",
    )(qkv, qkv, qkv, lam_p, sub_g, small_logits)
    return o_lat, o_ctx


def _gmlp_kernel(meta_ref, h_ref, prev_mod_ref, slot_ref, ys_hbm, mod_ref, win_ref, bin_ref, vg_ref, ws_ref, bs_ref,
                 wout_ref, o_ref, u_ref, v_ref, t_ref, local_ref, sem):
    x = _combined_rows(meta_ref, h_ref, prev_mod_ref, slot_ref, ys_hbm, local_ref, sem)
    m = mod_ref[0]
    a = _modnorm(x, *_mix_mod(m)).astype(BF16)
    n_half = CM_WIDTH // CN

    def in_proj(c):
        lo = c * CN
        return _gelu_tanh(_dot(a, win_ref[:, lo:lo + CN]) + bin_ref[:, lo:lo + CN])

    ssq = jnp.zeros((TM, 1), F32)
    for c in range(n_half):
        z = in_proj(n_half + c)
        v_ref[:, c * CN:(c + 1) * CN] = z
        ssq = ssq + jnp.sum(z * z, axis=-1, keepdims=True)
    for c in range(n_half):
        u_ref[:, c * CN:(c + 1) * CN] = in_proj(c)
    inv = lax.rsqrt(ssq * (1.0 / CM_WIDTH) + EPS)

    def mix(g):
        lo = g * CM_GW
        vn = (v_ref[:, lo:lo + CM_GW] * inv * vg_ref[:, lo:lo + CM_GW]).astype(BF16)
        return [_dot(ws_ref[g], vn[r * CHUNK:(r + 1) * CHUNK]) + bs_ref[:, g:g + 1] for r in range(TM // CHUNK)]

    out = None
    nxt = mix(0)
    for g in range(CM_GROUPS):
        lo = g * CM_GW
        sv = nxt
        if g + 1 < CM_GROUPS:
            nxt = mix(g + 1)
        for r in range(TM // CHUNK):
            r0 = r * CHUNK
            t_ref[r0:r0 + CHUNK, lo:lo + CM_GW] = (u_ref[r0:r0 + CHUNK, lo:lo + CM_GW] * sv[r]).astype(BF16)
        part = _dot(t_ref[:, lo:lo + CM_GW], wout_ref[lo:lo + CM_GW, :])
        out = part if out is None else out + part
    o_ref[...] = x + m[2:3] * out


def _gmlp(rows, pending, modg, w_in, b_in, v_g, w_s, b_s, w_out):
    h, prev_modg, slots, ys, meta = pending
    return pl.pallas_call(
        _gmlp_kernel,
        grid_spec=pltpu.PrefetchScalarGridSpec(
            num_scalar_prefetch=1,
            grid=(rows.all_tiles,),
            in_specs=[rows.row_spec(D), rows.mod_spec(), rows.row_spec(ROUTE_W), pl.BlockSpec(memory_space=pl.ANY),
                      rows.mod_spec(), _resident((D, 2 * CM_WIDTH)), _resident((1, 2 * CM_WIDTH)),
                      _resident((1, CM_WIDTH)), _resident((CM_GROUPS, CHUNK, CHUNK)), _resident((CHUNK, CM_GROUPS)),
                      _resident((CM_WIDTH, D))],
            out_specs=rows.row_spec(D),
            scratch_shapes=[pltpu.VMEM((TM, CM_WIDTH), F32), pltpu.VMEM((TM, CM_WIDTH), F32),
                            pltpu.VMEM((TM, CM_WIDTH), BF16)] + _combine_scratch(),
        ),
        out_shape=jax.ShapeDtypeStruct((rows.n_all, D), F32),
        compiler_params=_params(1),
        name="moe_combine_gmlp",
    )(meta, h, prev_modg, slots, ys, modg, w_in, b_in, v_g, w_s, b_s, w_out)


SWA_BAND = SWA_TQ + 2 * SWA_WINDOW


SWA_PAIR = 2 * HEAD_DIM
N_KV_VARIANTS = 2 * SWA_KV_HEADS


def _swa_kernel(seq, sink_ref, small_ref, q_ref, kc_ref, vc_ref, kl_ref, vl_ref, o_ref, kcv_ref, vcv_ref, klv_ref,
                vlv_ref):
    qi = pl.program_id(1)
    lane = lax.broadcasted_iota(jnp.int32, (1, SWA_PAIR), 1)
    lo_half = lane < HEAD_DIM

    small_logits = small_ref[0] != 0

    @pl.when(jnp.logical_and(qi == 0, small_logits))
    def _per_batch_setup():
        for src_ref, dst_ref in ((kc_ref, kcv_ref), (vc_ref, vcv_ref), (kl_ref, klv_ref), (vl_ref, vlv_ref)):
            for half in range(SWA_KV_HEADS // 2):
                x = src_ref[:, half * SWA_PAIR:(half + 1) * SWA_PAIR].astype(F32)
                xr = pltpu.roll(x, HEAD_DIM, 1)
                j0, j1 = 2 * half, 2 * half + 1
                dst_ref[2 * j0] = jnp.where(lo_half, x, 0.0).astype(BF16)
                dst_ref[2 * j0 + 1] = jnp.where(lo_half, 0.0, xr).astype(BF16)
                dst_ref[2 * j1] = jnp.where(lo_half, xr, 0.0).astype(BF16)
                dst_ref[2 * j1 + 1] = jnp.where(lo_half, 0.0, x).astype(BF16)

    q0 = qi * SWA_TQ
    start = jnp.clip(q0 - SWA_WINDOW, 0, seq - SWA_BAND)
    start = pl.multiple_of(start, SWA_WINDOW)
    qpos = q0 + lax.broadcasted_iota(jnp.int32, (SWA_TQ, SWA_BAND), 0)
    kpos = start + lax.broadcasted_iota(jnp.int32, (SWA_TQ, SWA_BAND), 1)
    in_band = jnp.abs(qpos - kpos) <= SWA_WINDOW
    nt = (((1,), (1,)), ((), ()))

    def unshifted():
        in_band2 = jnp.concatenate([in_band, in_band], axis=0)
        top = lax.broadcasted_iota(jnp.int32, (2 * SWA_TQ, 1), 0) < SWA_TQ
        def logits(v):
            lo = (v // 2) * SWA_GROUP * HEAD_DIM
            q2 = jnp.concatenate([q_ref[:, lo:lo + SWA_PAIR], q_ref[:, lo + SWA_PAIR:lo + 2 * SWA_PAIR]], axis=0)
            return (lax.dot_general(q2, kcv_ref[v], nt, preferred_element_type=F32),
                    lax.dot_general(q2, klv_ref[v, pl.ds(start, SWA_BAND), :], nt, preferred_element_type=F32))

        nxt = logits(0)
        for j in range(SWA_KV_HEADS):
            lo = j * SWA_GROUP * HEAD_DIM
            out = jnp.zeros((2 * SWA_TQ, SWA_PAIR), F32)
            for var in range(2):
                v = 2 * j + var
                s_c, s_b = nxt
                if v + 1 < N_KV_VARIANTS:
                    nxt = logits(v + 1)
                p_c = jnp.exp2(s_c)
                p_b = jnp.where(in_band2, jnp.exp2(s_b), 0.0)
                hq = j * SWA_GROUP + var
                sink = jnp.where(top, sink_ref[hq] * LOG2E, sink_ref[hq + 2] * LOG2E)
                denom = (jnp.sum(p_c, axis=-1, keepdims=True) + jnp.sum(p_b, axis=-1, keepdims=True)
                         + jnp.exp2(sink))
                o = (_dot(p_c.astype(BF16), vcv_ref[v])
                     + _dot(p_b.astype(BF16), vlv_ref[v, pl.ds(start, SWA_BAND), :]))
                out = out + o / denom
            o_ref[:, lo:lo + SWA_PAIR] = out[:SWA_TQ].astype(BF16)
            o_ref[:, lo + SWA_PAIR:lo + 2 * SWA_PAIR] = out[SWA_TQ:].astype(BF16)

    def shifted():
        kb = kl_ref[pl.ds(start, SWA_BAND), :]
        vb = vl_ref[pl.ds(start, SWA_BAND), :]
        kc = kc_ref[...]
        vc = vc_ref[...]
        for j in range(SWA_KV_HEADS):
            kj = slice(j * HEAD_DIM, (j + 1) * HEAD_DIM)
            kbj, vbj, kcj, vcj = kb[:, kj], vb[:, kj], kc[:, kj], vc[:, kj]
            outs = []
            for g in range(SWA_GROUP):
                hq = j * SWA_GROUP + g
                qh = q_ref[:, hq * HEAD_DIM:(hq + 1) * HEAD_DIM]
                s_c = lax.dot_general(qh, kcj, nt, preferred_element_type=F32)
                s_b = lax.dot_general(qh, kbj, nt, preferred_element_type=F32)
                s_b = jnp.where(in_band, s_b, -jnp.inf)
                sink = sink_ref[hq] * LOG2E
                mx = jnp.maximum(jnp.maximum(jnp.max(s_c, axis=-1, keepdims=True),
                                             jnp.max(s_b, axis=-1, keepdims=True)), sink)
                p_c = jnp.exp2(s_c - mx)
                p_b = jnp.exp2(s_b - mx)
                denom = (jnp.sum(p_c, axis=-1, keepdims=True) + jnp.sum(p_b, axis=-1, keepdims=True)
                         + jnp.exp2(sink - mx))
                o = _dot(p_c.astype(BF16), vcj) + _dot(p_b.astype(BF16), vbj)
                outs.append(o / denom)
            lo = j * SWA_GROUP * HEAD_DIM
            o_ref[:, lo:lo + SWA_GROUP * HEAD_DIM] = jnp.concatenate(outs, axis=-1).astype(BF16)

    pl.when(small_logits)(unshifted)
    pl.when(jnp.logical_not(small_logits))(shifted)


def _swa(rows, qkv, sink, small_logits):
    batch, seq, ctx_len = rows.batch, rows.seq, rows.ctx_len
    nq = seq // SWA_TQ
    kvw = SWA_KV_HEADS * HEAD_DIM
    k_col = D // kvw
    ctx0 = rows.n_lat // ctx_len
    return pl.pallas_call(
        functools.partial(_swa_kernel, seq),
        grid=(batch, nq),
        in_specs=[pl.BlockSpec(memory_space=pltpu.SMEM), pl.BlockSpec(memory_space=pltpu.SMEM),
                  pl.BlockSpec((SWA_TQ, D), lambda b, i: (b * nq + i, 0)),
                  pl.BlockSpec((ctx_len, kvw), lambda b, i: (ctx0 + b, k_col)),
                  pl.BlockSpec((ctx_len, kvw), lambda b, i: (ctx0 + b, k_col + 1)),
                  pl.BlockSpec((seq, kvw), lambda b, i: (b, k_col)),
                  pl.BlockSpec((seq, kvw), lambda b, i: (b, k_col + 1))],
        out_specs=pl.BlockSpec((SWA_TQ, D), lambda b, i: (b * nq + i, 0)),
        out_shape=jax.ShapeDtypeStruct((rows.n_lat, D), BF16),
        scratch_shapes=[pltpu.VMEM((N_KV_VARIANTS, ctx_len, SWA_PAIR), BF16),
                        pltpu.VMEM((N_KV_VARIANTS, ctx_len, SWA_PAIR), BF16),
                        pltpu.VMEM((N_KV_VARIANTS, seq, SWA_PAIR), BF16),
                        pltpu.VMEM((N_KV_VARIANTS, seq, SWA_PAIR), BF16)],
        compiler_params=_params(2),
        name="swa_attn",
    )(sink, small_logits, qkv, qkv, qkv, qkv, qkv)


def _proj_route_kernel(lat_tiles, h_ref, mod_ref, zl_ref, zc_ref, w_ref, wr_ref, br_ref, hm_ref, a_ref, route_ref,
                       cnt_ref):
    i = pl.program_id(0)
    m = mod_ref[0]

    def project(z_ref):
        hm_ref[...] = h_ref[...] + m[2:3] * _dot(z_ref[...], w_ref[...])

    pl.when(i < lat_tiles)(lambda: project(zl_ref))
    pl.when(i >= lat_tiles)(lambda: project(zc_ref))
    a = _modnorm(hm_ref[...], *_ffn_mod(m))
    a_hi = a.astype(BF16)
    a_ref[...] = a_hi
    a_lo = (a - a_hi.astype(F32)).astype(BF16)
    hi_part = _dot(a_hi, wr_ref[...])
    logits = (hi_part[:, :ROUTE_W] + hi_part[:, ROUTE_W:]) + _dot(a_lo, wr_ref[:, :ROUTE_W]) + br_ref[...]
    lane = lax.broadcasted_iota(jnp.int32, (TM, ROUTE_W), 1)
    m1 = jnp.max(logits, axis=-1, keepdims=True)
    i1 = jnp.min(jnp.where(logits == m1, lane, ROUTE_W), axis=-1, keepdims=True)
    rest = jnp.where(lane == i1, -jnp.inf, logits)
    m2 = jnp.max(rest, axis=-1, keepdims=True)
    i2 = jnp.min(jnp.where(rest == m2, lane, ROUTE_W), axis=-1, keepdims=True)
    e2 = jnp.exp(m2 - m1)
    gate1 = 1.0 / (1.0 + e2)
    gate2 = e2 / (1.0 + e2)
    rec = jnp.where(lane == 0, i1.astype(F32), 0.0)
    rec = jnp.where(lane == 1, i2.astype(F32), rec)
    rec = jnp.where(lane == 2, gate1, rec)
    route_ref[...] = jnp.where(lane == 3, gate2, rec)
    chosen = jnp.where((lane == i1) | (lane == i2), 1.0, 0.0)
    cnt_ref[0] = jnp.broadcast_to(jnp.sum(chosen, axis=0, keepdims=True), (8, ROUTE_W))


def _proj_route(rows, n_tiles, h, modg, z_lat, z_ctx, w_out, w_r, b_r):
    lat_tiles = rows.lat_tiles
    pad = ROUTE_W - N_EXPERTS
    w_pad = jnp.pad(w_r, ((0, 0), (0, pad)))
    w_hi = w_pad.astype(BF16)
    w_lo = (w_pad - w_hi.astype(F32)).astype(BF16)
    w_split = jnp.concatenate([w_hi, w_lo], axis=1)
    b_pad = jnp.concatenate([b_r.astype(F32), jnp.full((pad,), -1e30, F32)]).reshape(1, ROUTE_W)
    return pl.pallas_call(
        functools.partial(_proj_route_kernel, lat_tiles),
        grid=(n_tiles,),
        in_specs=[rows.row_spec(D), rows.mod_spec(),
                  pl.BlockSpec((TM, D), lambda i: (jnp.minimum(i, lat_tiles - 1), 0)),
                  pl.BlockSpec((TM, D), lambda i: (jnp.maximum(i - lat_tiles, 0), 0)),
                  _resident((D, D)), _resident((D, 2 * ROUTE_W)), _resident((1, ROUTE_W))],
        out_specs=[rows.row_spec(D), rows.row_spec(D), rows.row_spec(ROUTE_W),
                   pl.BlockSpec((1, 8, ROUTE_W), lambda i: (i, 0, 0))],
        out_shape=[jax.ShapeDtypeStruct((n_tiles * TM, D), F32), jax.ShapeDtypeStruct((n_tiles * TM, D), BF16),
                   jax.ShapeDtypeStruct((n_tiles * TM, ROUTE_W), F32),
                   jax.ShapeDtypeStruct((n_tiles, 8, ROUTE_W), F32)],
        compiler_params=_params(1),
        name="out_proj_router",
    )(h, modg, z_lat, z_ctx, w_out, w_split, b_pad)


SEG_ALIGN = 16
SEG_BITS = tuple(1 << b for b in range(TM.bit_length() - 1, SEG_ALIGN.bit_length() - 2, -1))
LROWS = -(-(2 * TM + N_EXPERTS * (SEG_ALIGN - 1)) // 128) * 128
XS_W = D + 128
META_W = 3 * N_EXPERTS


def _segment_copies(meta_ref, tile, local_ref, slots_hbm, sem, to_slots):
    for e in range(N_EXPERTS):
        loff = meta_ref[tile * META_W + e]
        goff = meta_ref[tile * META_W + N_EXPERTS + e]
        seg = meta_ref[tile * META_W + 2 * N_EXPERTS + e]
        for bit in SEG_BITS:
            done = seg & ~(2 * bit - 1)
            lo = local_ref.at[pl.ds(pl.multiple_of(loff + done, SEG_ALIGN), bit)]
            gl = slots_hbm.at[pl.ds(pl.multiple_of(goff + done, SEG_ALIGN), bit)]
            copy = pltpu.make_async_copy(lo, gl, sem) if to_slots else pltpu.make_async_copy(gl, lo, sem)
            yield (seg & bit) != 0, copy


def _start_segment_copies(*args):
    for cond, copy in _segment_copies(*args):
        pl.when(cond)(copy.start)


def _wait_segment_copies(*args):
    for cond, copy in _segment_copies(*args):
        pl.when(cond)(copy.wait)


def _split3(x):
    hi = x.astype(BF16).astype(F32)
    mid = (x - hi).astype(BF16).astype(F32)
    lo = ((x - hi) - mid).astype(BF16).astype(F32)
    return hi, mid, lo


def _dispatch_kernel(n_blocks, meta_ref, fill_ref, a_ref, route_ref, tri_ref, xs_hbm, slot_ref, local_ref, zero_ref,
                     sem, zsem):
    i = pl.program_id(0)

    @pl.when(i == 0)
    def _zero_unwritten_slots():
        zero_ref[...] = jnp.zeros(zero_ref.shape, BF16)
        tails = []
        for e in range(N_EXPERTS):
            off, length = fill_ref[e], fill_ref[N_EXPERTS + e]
            for bit in (b for b in SEG_BITS if b < MOE_RB):
                done = length & ~(2 * bit - 1)
                dst = xs_hbm.at[pl.ds(pl.multiple_of(off + done, SEG_ALIGN), bit)]
                tails.append(((length & bit) != 0, pltpu.make_async_copy(zero_ref.at[pl.ds(0, bit)], dst, zsem)))
        for cond, copy in tails:
            pl.when(cond)(copy.start)

        def block_copy(blk):
            dst = xs_hbm.at[pl.ds(pl.multiple_of(blk * MOE_RB, MOE_RB), MOE_RB)]
            return pltpu.make_async_copy(zero_ref, dst, zsem)

        n_used = fill_ref[2 * N_EXPERTS]
        lax.fori_loop(n_used, n_blocks, lambda blk, c: (block_copy(blk).start(), c)[1], 0)
        for cond, copy in tails:
            pl.when(cond)(copy.wait)
        lax.fori_loop(n_used, n_blocks, lambda blk, c: (block_copy(blk).wait(), c)[1], 0)

    route = route_ref[...]
    lane = lax.broadcasted_iota(jnp.int32, (TM, ROUTE_W), 1)
    pick1 = lane == route[:, 0:1].astype(jnp.int32)
    pick2 = lane == route[:, 1:2].astype(jnp.int32)
    chosen = jnp.where(pick1 | pick2, 1.0, 0.0).astype(BF16)
    rank = _dot(tri_ref[...], chosen)
    lane1 = lax.broadcasted_iota(jnp.int32, (1, ROUTE_W), 1)
    loff = jnp.zeros((1, ROUTE_W), F32)
    for e in range(N_EXPERTS):
        loff = jnp.where(lane1 == e, meta_ref[i * META_W + e].astype(F32), loff)
    place = rank + loff
    slot1 = jnp.sum(jnp.where(pick1, place, 0.0), axis=-1, keepdims=True)
    slot2 = jnp.sum(jnp.where(pick2, place, 0.0), axis=-1, keepdims=True)
    slot_ref[...] = jnp.where(lane == 0, slot1, jnp.where(lane == 1, slot2, 0.0))

    row1 = jnp.transpose(jnp.broadcast_to(slot1, (TM, 128)))[0:1, :]
    row2 = jnp.transpose(jnp.broadcast_to(slot2, (TM, 128)))[0:1, :]
    lrow = lax.broadcasted_iota(jnp.int32, (LROWS, TM), 0).astype(F32)
    p1 = jnp.where(lrow == row1, 1.0, 0.0).astype(BF16)
    p2 = jnp.where(lrow == row2, 1.0, 0.0).astype(BF16)
    buf = i % 2
    local_ref.at[buf][:, :D] = _dot(p1 + p2, a_ref[...]).astype(BF16)
    gates = []
    for col in (2, 3):
        pieces = _split3(route[:, col:col + 1])
        g = jnp.zeros((TM, ROUTE_W), F32)
        for k, piece in enumerate(pieces):
            g = jnp.where(lane == k, piece, g)
        gates.append(g.astype(BF16))
    local_ref.at[buf][:, D:] = (_dot(p1, gates[0]) + _dot(p2, gates[1])).astype(BF16)

    def copies(tile, b):
        return (meta_ref, tile, local_ref.at[b], xs_hbm, sem.at[b], True)

    _start_segment_copies(*copies(i, buf))
    pl.when(i > 0)(lambda: _wait_segment_copies(*copies(i - 1, 1 - buf)))
    pl.when(i == pl.num_programs(0) - 1)(lambda: _wait_segment_copies(*copies(i, buf)))


def _dispatch(n_tiles, n_blocks, a, route, meta, fill):
    tri = (jnp.arange(TM)[:, None] > jnp.arange(TM)[None, :]).astype(BF16)
    return pl.pallas_call(
        functools.partial(_dispatch_kernel, n_blocks),
        grid_spec=pltpu.PrefetchScalarGridSpec(
            num_scalar_prefetch=2,
            grid=(n_tiles,),
            in_specs=[pl.BlockSpec((TM, D), lambda i, *_: (i, 0)),
                      pl.BlockSpec((TM, ROUTE_W), lambda i, *_: (i, 0)),
                      _resident((TM, TM))],
            out_specs=[pl.BlockSpec(memory_space=pl.ANY), pl.BlockSpec((TM, ROUTE_W), lambda i, *_: (i, 0))],
            scratch_shapes=[pltpu.VMEM((2, LROWS, XS_W), BF16), pltpu.VMEM((MOE_RB, XS_W), BF16),
                            pltpu.SemaphoreType.DMA((2,)), pltpu.SemaphoreType.DMA(())],
        ),
        out_shape=[jax.ShapeDtypeStruct((n_blocks * MOE_RB, XS_W), BF16),
                   jax.ShapeDtypeStruct((n_tiles * TM, ROUTE_W), F32)],
        compiler_params=_params(1),
        name="moe_dispatch",
    )(meta, fill, a, route, tri)


def _expert_kernel(be_ref, fill_ref, xs_ref, w1_ref, w3_ref, w2_ref, ys_ref, t_ref):
    del be_ref
    used = pl.program_id(0) < fill_ref[2 * N_EXPERTS]

    @pl.when(used)
    def _ffn():
        y = _swiglu(xs_ref[:, :D], w1_ref, w3_ref, w2_ref, t_ref)
        gate = jnp.sum(xs_ref[:, D:].astype(F32), axis=-1, keepdims=True)
        ys_ref[...] = (gate * y).astype(BF16)

    @pl.when(jnp.logical_not(used))
    def _idle():
        ys_ref[...] = jnp.zeros(ys_ref.shape, BF16)


def _experts(xs, blk_exp, fill, layer, w1, w3, w2):
    def slab(d0, d1):
        return pl.BlockSpec((None, None, d0, d1), lambda i, be, fill: (layer, be[i], 0, 0),
                            pipeline_mode=pl.Buffered(1))

    n_blocks = blk_exp.shape[0]
    last_used = lambda i, fill: jnp.minimum(i, fill[2 * N_EXPERTS] - 1)
    return pl.pallas_call(
        _expert_kernel,
        grid_spec=pltpu.PrefetchScalarGridSpec(
            num_scalar_prefetch=2,
            grid=(n_blocks,),
            in_specs=[pl.BlockSpec((MOE_RB, XS_W), lambda i, be, fill: (last_used(i, fill), 0)),
                      slab(D, FFN_DIM), slab(D, FFN_DIM), slab(FFN_DIM, D)],
            out_specs=pl.BlockSpec((MOE_RB, D), lambda i, be, fill: (i, 0)),
            scratch_shapes=[pltpu.VMEM((MOE_RB, FFN_DIM), BF16)],
        ),
        out_shape=jax.ShapeDtypeStruct((n_blocks * MOE_RB, D), BF16),
        compiler_params=_params(1),
        name="moe_experts",
    )(blk_exp, fill, xs, w1, w3, w2)


def _combine_scratch():
    return [pltpu.VMEM((2, LROWS, D), BF16), pltpu.SemaphoreType.DMA((2,))]


def _combined_rows(meta_ref, h_ref, mod_ref, slot_ref, ys_hbm, local_ref, sem):
    i = pl.program_id(0)

    buf = i % 2

    def fetch(tile, b):
        return (meta_ref, tile, local_ref.at[b], ys_hbm, sem.at[b], False)

    @pl.when(i == 0)
    def _init():
        local_ref[...] = jnp.zeros(local_ref.shape, BF16)
        _start_segment_copies(*fetch(0, 0))

    @pl.when(i + 1 < pl.num_programs(0))
    def _prefetch():
        _start_segment_copies(*fetch(i + 1, 1 - buf))

    _wait_segment_copies(*fetch(i, buf))
    slots = slot_ref[...]
    lcol = lax.broadcasted_iota(jnp.int32, (TM, LROWS), 1).astype(F32)
    gather = jnp.where((lcol == slots[:, 0:1]) | (lcol == slots[:, 1:2]), 1.0, 0.0).astype(BF16)
    return h_ref[...] + mod_ref[0][5:6] * _dot(gather, local_ref[buf])


def _combine_kernel(meta_ref, h_ref, mod_ref, slot_ref, ys_hbm, o_ref, local_ref, sem):
    o_ref[...] = _combined_rows(meta_ref, h_ref, mod_ref, slot_ref, ys_hbm, local_ref, sem)


def _combine(rows, n_tiles, h, modg, slots, ys, meta):
    tps, batch = rows.tiles_per_seq, rows.batch
    return pl.pallas_call(
        _combine_kernel,
        grid_spec=pltpu.PrefetchScalarGridSpec(
            num_scalar_prefetch=1,
            grid=(n_tiles,),
            in_specs=[pl.BlockSpec((TM, D), lambda i, m: (i, 0)),
                      pl.BlockSpec((1, 8, D), lambda i, m: (jnp.minimum(i // tps, batch), 0, 0)),
                      pl.BlockSpec((TM, ROUTE_W), lambda i, m: (i, 0)),
                      pl.BlockSpec(memory_space=pl.ANY)],
            out_specs=pl.BlockSpec((TM, D), lambda i, m: (i, 0)),
            scratch_shapes=_combine_scratch(),
        ),
        out_shape=jax.ShapeDtypeStruct((n_tiles * TM, D), F32),
        compiler_params=_params(1),
        name="moe_combine",
    )(meta, h, modg, slots, ys)


def _mixer_out_moe(rows, n_tiles, h, modg, z_lat, z_ctx, w_out, w_r, b_r, layer, w1, w3, w2, defer_combine):
    h, a, route, cnt = _proj_route(rows, n_tiles, h, modg, z_lat, z_ctx, w_out, w_r, b_r)
    cnt = cnt[:, 0, :N_EXPERTS].astype(jnp.int32)
    seg = (cnt + SEG_ALIGN - 1) // SEG_ALIGN * SEG_ALIGN
    loff = jnp.cumsum(seg, axis=1) - seg
    total = jnp.sum(seg, axis=0)
    region = (total + MOE_RB - 1) // MOE_RB * MOE_RB
    region_end = jnp.cumsum(region)
    region_start = region_end - region
    goff = region_start[None, :] + jnp.cumsum(seg, axis=0) - seg
    meta = jnp.concatenate([loff, goff, seg], axis=1).reshape(-1)
    n_blocks = pl.cdiv(2 * n_tiles * TM + n_tiles * N_EXPERTS * (SEG_ALIGN - 1), MOE_RB) + N_EXPERTS
    n_used = region_end[-1] // MOE_RB
    fill = jnp.concatenate([region_start + total, region - total, n_used[None]]).astype(jnp.int32)
    blk = jnp.minimum(jnp.arange(n_blocks, dtype=jnp.int32), n_used - 1)
    blk_exp = jnp.sum((region_end[None, :] <= (blk * MOE_RB)[:, None]).astype(jnp.int32), axis=1)
    blk_exp = jnp.minimum(blk_exp, N_EXPERTS - 1)
    xs, slots = _dispatch(n_tiles, n_blocks, a, route, meta, fill)
    ys = _experts(xs, blk_exp, fill, layer, w1, w3, w2)
    if defer_combine:
        return h, modg, slots, ys, meta
    return _combine(rows, n_tiles, h, modg, slots, ys, meta)


def kernel(x, c, ctx, c_ctx, ada_w, ada_b, norm_mix_g, norm_ffn_g, sc_in_w, sc_conv_w, sc_out_w, da_qkv_w, da_out_w,
           da_q_norm_g, da_k_norm_g, da_lambda, da_sub_norm_g, cm_in_w, cm_in_b, cm_v_norm_g, cm_ws, cm_bs, cm_out_w,
           sw_qkv_w, sw_out_w, sw_q_norm_g, sw_k_norm_g, sw_sink, ffn_w1, ffn_w3, ffn_w2, moe_router_w, moe_router_b,
           moe_w1, moe_w3, moe_w2):
    batch, seq, _ = x.shape
    ctx_len = ctx.shape[1]
    depth = ada_w.shape[0]
    assert depth == 4 and batch + 1 <= 16
    rows = _Rows(batch, seq, ctx_len)

    cvec = jnp.concatenate([c, c_ctx[None, :], jnp.zeros((16 - batch - 1, D), F32)], axis=0)
    mod = _ada_all(cvec, ada_w, ada_b)[:, :batch + 1].reshape(depth, batch + 1, 6, D)
    gains = jnp.stack([norm_mix_g, norm_ffn_g], axis=1)[:, None]
    modg = jnp.concatenate([mod, jnp.broadcast_to(gains, (depth, batch + 1, 2, D))], axis=2)

    bf = lambda w: w.astype(BF16)
    x2, ctx2 = x.reshape(-1, D), ctx.reshape(-1, D)

    bg, y = _conv_in(rows, x2, ctx2, modg[0], bf(sc_in_w[0]))
    conv_w = jnp.pad(sc_conv_w[0], ((0, 5), (0, 0)))
    h = _conv_out(rows, x2, ctx2, modg[0], bg, y, conv_w, bf(sc_out_w[0]))
    h = _ffn(rows, rows.all_tiles, h, modg[0], 0, ffn_w1, ffn_w3, ffn_w2)

    nq_chunks = D // CN
    qkv = _qkv(rows, h, modg[1], bf(da_qkv_w[0]), da_q_norm_g[0], da_k_norm_g[0], nq_chunks, nq_chunks)
    lam_init = 0.8 - 0.6 * math.exp(-0.3 * 1)
    o_lat, o_ctx = _diff_attn(rows, qkv, da_lambda[0], da_sub_norm_g[0], lam_init,
                              _small_logits(da_q_norm_g[0], da_k_norm_g[0]))
    pending = _mixer_out_moe(rows, rows.all_tiles, h, modg[1], o_lat, o_ctx, bf(da_out_w[0]), moe_router_w[0],
                             moe_router_b[0], 0, moe_w1, moe_w3, moe_w2, defer_combine=True)

    h = _gmlp(rows, pending, modg[2], bf(cm_in_w[0]), cm_in_b[0].reshape(1, -1), cm_v_norm_g[0].reshape(1, -1),
              bf(cm_ws[0]), cm_bs[0].T, bf(cm_out_w[0]))
    h = _ffn(rows, rows.all_tiles, h, modg[2], 1, ffn_w1, ffn_w3, ffn_w2)

    kv_chunks = SWA_KV_HEADS * HEAD_DIM // CN
    qkv = _qkv(rows, h, modg[3], bf(sw_qkv_w[0]), sw_q_norm_g[0], sw_k_norm_g[0], nq_chunks, kv_chunks)
    o = _swa(rows, qkv, sw_sink[0], _small_logits(sw_q_norm_g[0], sw_k_norm_g[0], sw_sink[0]))
    h = _mixer_out_moe(rows, rows.lat_tiles, h, modg[3], o, o, bf(sw_out_w[0]), moe_router_w[1], moe_router_b[1], 1,
                       moe_w1, moe_w3, moe_w2, defer_combine=False)
    return h.reshape(batch, seq, D)
```

```python
import functools
import math

import jax
import jax.numpy as jnp
from jax import lax
from jax.experimental import pallas as pl
from jax.experimental.pallas import tpu as pltpu

D = 1024
HEAD_DIM = 64
GRID_W = 64
ROPE_HALF = HEAD_DIM // 2
ROPE_BASE = 10000.0
DIFF_HEADS = D // (2 * HEAD_DIM)
SWA_Q_HEADS = D // HEAD_DIM
SWA_KV_HEADS = 4
SWA_GROUP = SWA_Q_HEADS // SWA_KV_HEADS
SWA_WINDOW = 128
CHUNK = 128
CM_WIDTH = 2 * D
CM_GROUPS = 8
CM_GW = CM_WIDTH // CM_GROUPS
FFN_DIM = 2816
N_EXPERTS = 8
EPS = 1e-6

F32 = jnp.float32
BF16 = jnp.bfloat16
HIGHEST = lax.Precision.HIGHEST
LOG2E = math.log2(math.e)

TM = 512
CN = 256
MOE_RB = 512
ATT_TQ = 1024
ATT_SUB = 256
ATT_TK = 1024
SWA_TQ = 256
ROUTE_W = 128
MAX_UNSHIFTED_LOGIT = 60.0
VMEM_LIMIT = 56 << 20


def _params(n_grid):
    return pltpu.CompilerParams(dimension_semantics=("arbitrary",) * n_grid, vmem_limit_bytes=VMEM_LIMIT)


def _resident(shape):
    zeros = (0,) * len(shape)
    return pl.BlockSpec(shape, lambda *_: zeros, pipeline_mode=pl.Buffered(1))


def _sigmoid(x):
    return 1.0 / (1.0 + jnp.exp(-x))


def _gelu_tanh(x):
    return 0.5 * x * (1.0 + jnp.tanh(math.sqrt(2.0 / math.pi) * (x + 0.044715 * (x * x * x))))


def _modnorm(x, g, shift, scale):
    y = x * lax.rsqrt(jnp.mean(x * x, axis=-1, keepdims=True) + EPS)
    return (y * g) * (1.0 + scale) + shift


def _dot(a, b):
    return jnp.dot(a, b, preferred_element_type=F32)


def _mix_mod(m):
    return m[6:7], m[0:1], m[1:2]


def _ffn_mod(m):
    return m[7:8], m[3:4], m[4:5]


def _ada_kernel(c_ref, w_ref, b_ref, o_ref):
    c = c_ref[...]
    s = c * _sigmoid(c)
    o_ref[0] = jnp.dot(s, w_ref[0], precision=HIGHEST, preferred_element_type=F32) + b_ref[0]


def _ada_all(cvec, ada_w, ada_b):
    depth, _, n = ada_w.shape
    tn = 1536
    rows = cvec.shape[0]
    return pl.pallas_call(
        _ada_kernel,
        grid=(depth, n // tn),
        in_specs=[pl.BlockSpec((rows, D), lambda l, j: (0, 0)),
                  pl.BlockSpec((1, D, tn), lambda l, j: (l, 0, j)),
                  pl.BlockSpec((1, 1, tn), lambda l, j: (l, 0, j))],
        out_specs=pl.BlockSpec((1, rows, tn), lambda l, j: (l, 0, j)),
        out_shape=jax.ShapeDtypeStruct((depth, rows, n), F32),
        compiler_params=_params(2),
        name="adaln",
    )(cvec, ada_w, ada_b.reshape(depth, 1, n))


class _Rows:
    def __init__(self, batch, seq, ctx_len):
        self.batch, self.seq, self.ctx_len = batch, seq, ctx_len
        self.n_lat = batch * seq
        self.n_ctx = batch * ctx_len
        self.n_all = self.n_lat + self.n_ctx
        assert seq % TM == 0 and self.n_ctx % TM == 0 and TM % ctx_len == 0
        self.lat_tiles = self.n_lat // TM
        self.all_tiles = self.n_all // TM
        self.tiles_per_seq = seq // TM

    def mod_spec(self):
        tps, batch = self.tiles_per_seq, self.batch
        return pl.BlockSpec((1, 8, D), lambda i, *_: (jnp.minimum(i // tps, batch), 0, 0))

    def row_spec(self, width):
        return pl.BlockSpec((TM, width), lambda i, *_: (i, 0))

    def split_specs(self, width):
        lat_tiles = self.lat_tiles
        return [pl.BlockSpec((TM, width), lambda i, *_: (jnp.minimum(i, lat_tiles - 1), 0)),
                pl.BlockSpec((TM, width), lambda i, *_: (jnp.maximum(i - lat_tiles, 0), 0))]


def _conv_in_kernel(lat_tiles, x_ref, ctx_ref, mod_ref, w_ref, bg_ref, y_ref):
    h = jnp.where(pl.program_id(0) < lat_tiles, x_ref[...], ctx_ref[...])
    a = _modnorm(h, *_mix_mod(mod_ref[0])).astype(BF16)
    for c in range(D // CN):
        lo = c * CN
        bg_ref[:, lo:lo + CN] = _dot(a, w_ref[:, lo:lo + CN]).astype(BF16)
        cg = _dot(a, w_ref[:, D + lo:D + lo + CN])
        xv = _dot(a, w_ref[:, 2 * D + lo:2 * D + lo + CN])
        y_ref[:, lo:lo + CN] = (cg * xv).astype(BF16)


def _conv_in(rows, x, ctx, modg, w_in):
    out = jax.ShapeDtypeStruct((rows.n_all, D), BF16)
    return pl.pallas_call(
        functools.partial(_conv_in_kernel, rows.lat_tiles),
        grid=(rows.all_tiles,),
        in_specs=rows.split_specs(D) + [rows.mod_spec(), _resident((D, 3 * D))],
        out_specs=[rows.row_spec(D), rows.row_spec(D)],
        out_shape=[out, out],
        compiler_params=_params(1),
        name="conv_in",
    )(x, ctx, modg, w_in)


HALO = 16


def _conv_out_kernel(n_lat, seq, ctx_len, x_ref, ctx_ref, mod_ref, bg_ref, y_ref, yp_ref, yn_ref, cw_ref, w_ref,
                     o_ref):
    i = pl.program_id(0)
    h = jnp.where(i * TM < n_lat, x_ref[...], ctx_ref[...])
    m = mod_ref[0]
    row = lax.broadcasted_iota(jnp.int32, (TM, 1), 0)
    grow = row + i * TM
    seq_len = jnp.where(grow < n_lat, seq, ctx_len)
    pos = grow & (seq_len - 1)
    out = None
    for c in range(D // CN):
        cols = slice(c * CN, (c + 1) * CN)
        y = y_ref[:, cols].astype(F32)
        prev_row = yp_ref[HALO - 1:HALO, cols].astype(F32)
        next_row = yn_ref[0:1, cols].astype(F32)
        y_m1 = jnp.where(row == 0, prev_row, pltpu.roll(y, 1, 0))
        y_m1 = jnp.where(pos == 0, 0.0, y_m1)
        y_p1 = jnp.where(row == TM - 1, next_row, pltpu.roll(y, TM - 1, 0))
        y_p1 = jnp.where(pos == seq_len - 1, 0.0, y_p1)
        conv = cw_ref[0:1, cols] * y_m1 + cw_ref[1:2, cols] * y + cw_ref[2:3, cols] * y_p1
        z = (bg_ref[:, cols].astype(F32) * conv).astype(BF16)
        part = _dot(z, w_ref[cols, :])
        out = part if out is None else out + part
    o_ref[...] = h + m[2:3] * out


def _conv_out(rows, x, ctx, modg, bg, y, conv_w, w_out):
    hb = TM // HALO
    last = rows.n_all // HALO - 1
    return pl.pallas_call(
        functools.partial(_conv_out_kernel, rows.n_lat, rows.seq, rows.ctx_len),
        grid=(rows.all_tiles,),
        in_specs=rows.split_specs(D) + [rows.mod_spec(), rows.row_spec(D), rows.row_spec(D),
                  pl.BlockSpec((HALO, D), lambda i: (jnp.maximum(i * hb - 1, 0), 0)),
                  pl.BlockSpec((HALO, D), lambda i: (jnp.minimum((i + 1) * hb, last), 0)),
                  _resident((8, D)), _resident((D, D))],
        out_specs=rows.row_spec(D),
        out_shape=jax.ShapeDtypeStruct((rows.n_all, D), F32),
        compiler_params=_params(1),
        name="conv_out",
    )(x, ctx, modg, bg, y, y, y, conv_w, w_out)


W2_KC = 768


def _swiglu(a, w1_ref, w3_ref, w2_ref, t_ref):
    for c in range(FFN_DIM // CN):
        lo = c * CN
        h1 = _dot(a, w1_ref[:, lo:lo + CN].astype(BF16))
        h3 = _dot(a, w3_ref[:, lo:lo + CN].astype(BF16))
        t_ref[:, lo:lo + CN] = (h1 * _sigmoid(h1) * h3).astype(BF16)
    out = None
    for lo in range(0, FFN_DIM, W2_KC):
        hi = min(lo + W2_KC, FFN_DIM)
        part = _dot(t_ref[:, lo:hi], w2_ref[lo:hi, :].astype(BF16))
        out = part if out is None else out + part
    return out


def _ffn_kernel(h_ref, mod_ref, w1_ref, w3_ref, w2_ref, o_ref, t_ref):
    m = mod_ref[0]
    x = h_ref[...]
    a = _modnorm(x, *_ffn_mod(m)).astype(BF16)
    o_ref[...] = x + m[5:6] * _swiglu(a, w1_ref, w3_ref, w2_ref, t_ref)


def _ffn(rows, n_tiles, h, modg, layer, w1, w3, w2):
    def slab(d0, d1):
        return pl.BlockSpec((None, d0, d1), lambda i: (layer, 0, 0), pipeline_mode=pl.Buffered(1))

    return pl.pallas_call(
        _ffn_kernel,
        grid=(n_tiles,),
        in_specs=[rows.row_spec(D), rows.mod_spec(), slab(D, FFN_DIM), slab(D, FFN_DIM), slab(FFN_DIM, D)],
        out_specs=rows.row_spec(D),
        out_shape=jax.ShapeDtypeStruct((n_tiles * TM, D), F32),
        scratch_shapes=[pltpu.VMEM((TM, FFN_DIM), BF16)],
        compiler_params=_params(1),
        name="ffn_dense",
    )(h, modg, w1, w3, w2)


def _norm_rope(x, gmat_ref, cos, sin):
    ms = _dot((x * x).astype(BF16), gmat_ref[...]) * (1.0 / HEAD_DIM)
    xn = x * lax.rsqrt(ms + EPS)
    lane = lax.broadcasted_iota(jnp.int32, (1, CN), 1)
    first_half = (lane & (ROPE_HALF - 1)) < (ROPE_HALF // 2)
    partner = jnp.where(first_half, pltpu.roll(xn, CN - ROPE_HALF // 2, 1), pltpu.roll(xn, ROPE_HALF // 2, 1))
    return xn * cos + partner * sin


def _qkv_kernel(n_qk_chunks, n_q_chunks, h_ref, mod_ref, w_ref, gmat_ref, qcos_ref, qsin_ref, kcos_ref, ksin_ref,
                o_ref):
    a = _modnorm(h_ref[...], *_mix_mod(mod_ref[0])).astype(BF16)
    n_chunks = w_ref.shape[1] // CN

    def project(c):
        return _dot(a, w_ref[:, c * CN:(c + 1) * CN])

    nxt = project(0)
    for c in range(n_chunks):
        acc = nxt
        if c + 1 < n_chunks:
            nxt = project(c + 1)
        if c < n_q_chunks:
            acc = _norm_rope(acc, gmat_ref, qcos_ref[...], qsin_ref[...])
        elif c < n_qk_chunks:
            acc = _norm_rope(acc, gmat_ref, kcos_ref[...], ksin_ref[...])
        o_ref[:, c * CN:(c + 1) * CN] = acc.astype(BF16)


def _rope_tables(rows, gain, scale):
    seq = rows.seq
    pos = jnp.arange(seq)
    n_freq = ROPE_HALF // 2
    inv = ROPE_BASE ** (-jnp.arange(n_freq, dtype=F32) / n_freq)
    ang_r = (pos // GRID_W).astype(F32)[:, None] * inv
    ang_c = (pos % GRID_W).astype(F32)[:, None] * inv
    cos = jnp.concatenate([jnp.cos(ang_r)] * 2 + [jnp.cos(ang_c)] * 2, axis=-1)
    sin = jnp.concatenate([-jnp.sin(ang_r), jnp.sin(ang_r), -jnp.sin(ang_c), jnp.sin(ang_c)], axis=-1)
    cos = jnp.concatenate([cos, jnp.ones((TM, HEAD_DIM), F32)], axis=0)
    sin = jnp.concatenate([sin, jnp.zeros((TM, HEAD_DIM), F32)], axis=0)
    dim = jnp.arange(HEAD_DIM)
    partner = jnp.where((dim % ROPE_HALF) < n_freq, dim + n_freq, dim - n_freq)
    g = gain.astype(F32) * scale
    reps = CN // HEAD_DIM
    return jnp.tile(cos * g[None, :], (1, reps)), jnp.tile(sin * g[partner][None, :], (1, reps))


def _qkv(rows, h, modg, w, q_gain, k_gain, n_q_chunks, n_k_chunks):
    width = w.shape[1]
    head = jnp.arange(CN) // HEAD_DIM
    gmat = (head[:, None] == head[None, :]).astype(BF16)
    tables = _rope_tables(rows, q_gain, LOG2E * HEAD_DIM ** -0.5) + _rope_tables(rows, k_gain, 1.0)
    tps, lat_tiles = rows.tiles_per_seq, rows.lat_tiles
    tab_spec = pl.BlockSpec((TM, CN), lambda i: (jnp.where(i < lat_tiles, i % tps, tps), 0))
    return pl.pallas_call(
        functools.partial(_qkv_kernel, n_q_chunks + n_k_chunks, n_q_chunks),
        grid=(rows.all_tiles,),
        in_specs=[rows.row_spec(D), rows.mod_spec(), _resident((D, width)), _resident((CN, CN))] + [tab_spec] * 4,
        out_specs=rows.row_spec(width),
        out_shape=jax.ShapeDtypeStruct((rows.n_all, width), BF16),
        compiler_params=_params(1),
        name="qkv_proj",
    )(h, modg, w, gmat, *tables)


DH2 = 2 * HEAD_DIM


def _diff_attn_kernel(lam_init, n_lat_chunks, q_ref, kc_ref, vc_ref, kl_ref, vl_ref, lam_ref, subg_ref, small_ref,
                      o_ref, m_ref, acc_ref, vca_ref, vla_ref):
    small_logits = small_ref[0] != 0

    def online_setup():
        vca_ref[:, :DH2] = vc_ref[...]
        vca_ref[:, DH2:] = jnp.ones((vc_ref.shape[0], DH2), BF16)
        if n_lat_chunks:
            vla_ref[:, :DH2] = vl_ref[...]
            vla_ref[:, DH2:] = jnp.ones((vl_ref.shape[0], DH2), BF16)

    first_query_tile = pl.program_id(2) == 0 if n_lat_chunks else True
    pl.when(jnp.logical_and(first_query_tile, jnp.logical_not(small_logits)))(online_setup)

    q = q_ref[...]
    lane = lax.broadcasted_iota(jnp.int32, (1, DH2), 1)
    zero = jnp.zeros((), BF16)
    qs = (jnp.where(lane < HEAD_DIM, q, zero), jnp.where(lane >= HEAD_DIM, q, zero))
    nt = (((1,), (1,)), ((), ()))

    def over_keys(update):
        update(kc_ref[...], vca_ref[...])
        if n_lat_chunks:
            tk = kl_ref.shape[0] // n_lat_chunks

            def body(c, carry):
                start = pl.multiple_of(c * tk, tk)
                update(kl_ref[pl.ds(start, tk), :], vla_ref[pl.ds(start, tk), :])
                return carry
            lax.fori_loop(0, n_lat_chunks, body, 0)

    lp = lam_ref[...]
    lam = (jnp.exp(jnp.sum(lp[0:1] * lp[1:2], axis=-1, keepdims=True))
           - jnp.exp(jnp.sum(lp[2:3] * lp[3:4], axis=-1, keepdims=True)) + lam_init)

    def unshifted():
        sub = min(ATT_SUB, q.shape[0] // 2)
        n_sub = q.shape[0] // sub

        def logits(r):
            out = []
            for mi in range(2):
                qm = qs[mi][r * sub:(r + 1) * sub]
                s_c = lax.dot_general(qm, kc_ref[...], nt, preferred_element_type=F32)
                s_l = lax.dot_general(qm, kl_ref[...], nt, preferred_element_type=F32) if n_lat_chunks else None
                out.append((s_c, s_l))
            return out

        ahead = [logits(0), logits(1)]
        for r in range(n_sub):
            p_ctx, p_lat, sums = [], [], []
            for s_c, s_l in ahead.pop(0):
                e = jnp.exp2(s_c)
                total = jnp.sum(e, axis=-1, keepdims=True)
                p_ctx.append(e.astype(BF16))
                if s_l is not None:
                    e = jnp.exp2(s_l)
                    total = total + jnp.sum(e, axis=-1, keepdims=True)
                    p_lat.append(e.astype(BF16))
                sums.append(total)
            r0 = (1.0 / sums[0]).astype(BF16)
            r1 = (lam / sums[1]).astype(BF16)

            def weights(p):
                return p[0] * r0 - p[1] * r1

            o = _dot(weights(p_ctx), vc_ref[...])
            if n_lat_chunks:
                o = o + _dot(weights(p_lat), vl_ref[...])
            acc_ref[0, r * sub:(r + 1) * sub, :DH2] = o
            if r + 2 < n_sub:
                ahead.append(logits(r + 2))

    def update_online(k, va):
        reps = k.shape[0] // DH2
        for mi in range(2):
            s = lax.dot_general(qs[mi], k, nt, preferred_element_type=F32)
            m_old = m_ref[mi]
            m_new = jnp.maximum(m_old, jnp.max(s, axis=-1, keepdims=True))
            alpha = jnp.exp2(m_old - m_new)
            p = jnp.exp2(s - jnp.concatenate([m_new] * reps, axis=-1))
            acc_ref[mi] = jnp.concatenate([alpha, alpha], axis=-1) * acc_ref[mi] + _dot(p.astype(BF16), va)
            m_ref[mi] = m_new

    def online():
        m_ref[...] = jnp.full(m_ref.shape, -jnp.inf, F32)
        acc_ref[...] = jnp.zeros(acc_ref.shape, F32)
        over_keys(update_online)
        acc0 = acc_ref[0]
        acc1 = acc_ref[1]
        acc_ref[0, :, :DH2] = acc0[:, :DH2] / acc0[:, DH2:] - lam * (acc1[:, :DH2] / acc1[:, DH2:])

    pl.when(small_logits)(unshifted)
    pl.when(jnp.logical_not(small_logits))(online)

    o = acc_ref[0, :, :DH2]
    o = o * lax.rsqrt(jnp.mean(o * o, axis=-1, keepdims=True) + EPS) * subg_ref[...]
    o_ref[...] = (o * (1.0 - lam_init)).astype(BF16)


def _small_logits(q_gain, k_gain, sink=None):
    bound = 2 * HEAD_DIM * (LOG2E * HEAD_DIM ** -0.5) * jnp.max(jnp.abs(q_gain)) * jnp.max(jnp.abs(k_gain))
    ok = bound <= MAX_UNSHIFTED_LOGIT
    if sink is not None:
        ok = jnp.logical_and(ok, jnp.max(sink) * LOG2E <= MAX_UNSHIFTED_LOGIT)
    return ok.astype(jnp.int32).reshape(1)


def _diff_attn(rows, qkv, lam_p, sub_g, lam_init, small_logits):
    batch, seq, ctx_len = rows.batch, rows.seq, rows.ctx_len
    tq = min(ATT_TQ, seq)
    nq = seq // tq
    nh = DIFF_HEADS
    ctx0 = rows.n_lat // ctx_len
    sub_g = sub_g.reshape(1, DH2)
    small = [_resident((4, HEAD_DIM)), _resident((1, DH2)), pl.BlockSpec(memory_space=pltpu.SMEM)]
    kc_spec = pl.BlockSpec((ctx_len, DH2), lambda b, h, *_: (ctx0 + b, nh + h))
    vc_spec = pl.BlockSpec((ctx_len, DH2), lambda b, h, *_: (ctx0 + b, 2 * nh + h))

    def scratch(tq, n_lat_keys):
        return [pltpu.VMEM((2, tq, DH2), F32), pltpu.VMEM((2, tq, 2 * DH2), F32),
                pltpu.VMEM((ctx_len, 2 * DH2), BF16), pltpu.VMEM((n_lat_keys, 2 * DH2), BF16)]

    o_lat = pl.pallas_call(
        functools.partial(_diff_attn_kernel, lam_init, pl.cdiv(seq, ATT_TK)),
        grid=(batch, nh, nq),
        in_specs=[pl.BlockSpec((tq, DH2), lambda b, h, i: (b * nq + i, h)),
                  kc_spec, vc_spec,
                  pl.BlockSpec((seq, DH2), lambda b, h, i: (b, nh + h)),
                  pl.BlockSpec((seq, DH2), lambda b, h, i: (b, 2 * nh + h))] + small,
        out_specs=pl.BlockSpec((tq, DH2), lambda b, h, i: (b * nq + i, h)),
        out_shape=jax.ShapeDtypeStruct((rows.n_lat, D), BF16),
        scratch_shapes=scratch(tq, seq),
        compiler_params=_params(3),
        name="diff_attn_latent",
    )(qkv, qkv, qkv, qkv, qkv, lam_p, sub_g, small_logits)

    def ctx_kernel(q_ref, kc_ref, vc_ref, lam_ref, subg_ref, small_ref, o_ref, *scratch_refs):
        _diff_attn_kernel(lam_init, 0, q_ref, kc_ref, vc_ref, None, None, lam_ref, subg_ref, small_ref, o_ref,
                          *scratch_refs)

    o_ctx = pl.pallas_call(
        ctx_kernel,
        grid=(batch, nh),
        in_specs=[pl.BlockSpec((ctx_len, DH2), lambda b, h: (ctx0 + b, h)), kc_spec, vc_spec] + small,
        out_specs=pl.BlockSpec((ctx_len, DH2), lambda b, h: (b, h)),
        out_shape=jax.ShapeDtypeStruct((rows.n_ctx, D), BF16),
        scratch_shapes=scratch(ctx_len, 16),
        compiler_params=_params(2),
        name="diff_attn_context",
    )(qkv, qkv, qkv, lam_p, sub_g, small_logits)
    return o_lat, o_ctx


def _gmlp_kernel(meta_ref, h_ref, prev_mod_ref, slot_ref, ys_hbm, mod_ref, win_ref, bin_ref, vg_ref, ws_ref, bs_ref,
                 wout_ref, o_ref, u_ref, v_ref, t_ref, local_ref, sem):
    x = _combined_rows(meta_ref, h_ref, prev_mod_ref, slot_ref, ys_hbm, local_ref, sem)
    m = mod_ref[0]
    a = _modnorm(x, *_mix_mod(m)).astype(BF16)
    n_half = CM_WIDTH // CN

    def in_proj(c):
        lo = c * CN
        return _gelu_tanh(_dot(a, win_ref[:, lo:lo + CN]) + bin_ref[:, lo:lo + CN])

    ssq = jnp.zeros((TM, 1), F32)
    for c in range(n_half):
        z = in_proj(n_half + c)
        v_ref[:, c * CN:(c + 1) * CN] = z
        ssq = ssq + jnp.sum(z * z, axis=-1, keepdims=True)
    for c in range(n_half):
        u_ref[:, c * CN:(c + 1) * CN] = in_proj(c)
    inv = lax.rsqrt(ssq * (1.0 / CM_WIDTH) + EPS)

    def mix(g):
        lo = g * CM_GW
        vn = (v_ref[:, lo:lo + CM_GW] * inv * vg_ref[:, lo:lo + CM_GW]).astype(BF16)
        return [_dot(ws_ref[g], vn[r * CHUNK:(r + 1) * CHUNK]) + bs_ref[:, g:g + 1] for r in range(TM // CHUNK)]

    out = None
    nxt = mix(0)
    for g in range(CM_GROUPS):
        lo = g * CM_GW
        sv = nxt
        if g + 1 < CM_GROUPS:
            nxt = mix(g + 1)
        for r in range(TM // CHUNK):
            r0 = r * CHUNK
            t_ref[r0:r0 + CHUNK, lo:lo + CM_GW] = (u_ref[r0:r0 + CHUNK, lo:lo + CM_GW] * sv[r]).astype(BF16)
        part = _dot(t_ref[:, lo:lo + CM_GW], wout_ref[lo:lo + CM_GW, :])
        out = part if out is None else out + part
    o_ref[...] = x + m[2:3] * out


def _gmlp(rows, pending, modg, w_in, b_in, v_g, w_s, b_s, w_out):
    h, prev_modg, slots, ys, meta = pending
    return pl.pallas_call(
        _gmlp_kernel,
        grid_spec=pltpu.PrefetchScalarGridSpec(
            num_scalar_prefetch=1,
            grid=(rows.all_tiles,),
            in_specs=[rows.row_spec(D), rows.mod_spec(), rows.row_spec(ROUTE_W), pl.BlockSpec(memory_space=pl.ANY),
                      rows.mod_spec(), _resident((D, 2 * CM_WIDTH)), _resident((1, 2 * CM_WIDTH)),
                      _resident((1, CM_WIDTH)), _resident((CM_GROUPS, CHUNK, CHUNK)), _resident((CHUNK, CM_GROUPS)),
                      _resident((CM_WIDTH, D))],
            out_specs=rows.row_spec(D),
            scratch_shapes=[pltpu.VMEM((TM, CM_WIDTH), F32), pltpu.VMEM((TM, CM_WIDTH), F32),
                            pltpu.VMEM((TM, CM_WIDTH), BF16)] + _combine_scratch(),
        ),
        out_shape=jax.ShapeDtypeStruct((rows.n_all, D), F32),
        compiler_params=_params(1),
        name="moe_combine_gmlp",
    )(meta, h, prev_modg, slots, ys, modg, w_in, b_in, v_g, w_s, b_s, w_out)


SWA_BAND = SWA_TQ + 2 * SWA_WINDOW


SWA_PAIR = 2 * HEAD_DIM
N_KV_VARIANTS = 2 * SWA_KV_HEADS


def _swa_kernel(seq, sink_ref, small_ref, q_ref, kc_ref, vc_ref, kl_ref, vl_ref, o_ref, kcv_ref, vcv_ref, klv_ref,
                vlv_ref):
    qi = pl.program_id(1)
    lane = lax.broadcasted_iota(jnp.int32, (1, SWA_PAIR), 1)
    lo_half = lane < HEAD_DIM

    small_logits = small_ref[0] != 0

    @pl.when(jnp.logical_and(qi == 0, small_logits))
    def _per_batch_setup():
        for src_ref, dst_ref in ((kc_ref, kcv_ref), (vc_ref, vcv_ref), (kl_ref, klv_ref), (vl_ref, vlv_ref)):
            for half in range(SWA_KV_HEADS // 2):
                x = src_ref[:, half * SWA_PAIR:(half + 1) * SWA_PAIR].astype(F32)
                xr = pltpu.roll(x, HEAD_DIM, 1)
                j0, j1 = 2 * half, 2 * half + 1
                dst_ref[2 * j0] = jnp.where(lo_half, x, 0.0).astype(BF16)
                dst_ref[2 * j0 + 1] = jnp.where(lo_half, 0.0, xr).astype(BF16)
                dst_ref[2 * j1] = jnp.where(lo_half, xr, 0.0).astype(BF16)
                dst_ref[2 * j1 + 1] = jnp.where(lo_half, 0.0, x).astype(BF16)

    q0 = qi * SWA_TQ
    start = jnp.clip(q0 - SWA_WINDOW, 0, seq - SWA_BAND)
    start = pl.multiple_of(start, SWA_WINDOW)
    qpos = q0 + lax.broadcasted_iota(jnp.int32, (SWA_TQ, SWA_BAND), 0)
    kpos = start + lax.broadcasted_iota(jnp.int32, (SWA_TQ, SWA_BAND), 1)
    in_band = jnp.abs(qpos - kpos) <= SWA_WINDOW
    nt = (((1,), (1,)), ((), ()))

    def unshifted():
        in_band2 = jnp.concatenate([in_band, in_band], axis=0)
        top = lax.broadcasted_iota(jnp.int32, (2 * SWA_TQ, 1), 0) < SWA_TQ
        def logits(v):
            lo = (v // 2) * SWA_GROUP * HEAD_DIM
            q2 = jnp.concatenate([q_ref[:, lo:lo + SWA_PAIR], q_ref[:, lo + SWA_PAIR:lo + 2 * SWA_PAIR]], axis=0)
            return (lax.dot_general(q2, kcv_ref[v], nt, preferred_element_type=F32),
                    lax.dot_general(q2, klv_ref[v, pl.ds(start, SWA_BAND), :], nt, preferred_element_type=F32))

        nxt = logits(0)
        for j in range(SWA_KV_HEADS):
            lo = j * SWA_GROUP * HEAD_DIM
            out = jnp.zeros((2 * SWA_TQ, SWA_PAIR), F32)
            for var in range(2):
                v = 2 * j + var
                s_c, s_b = nxt
                if v + 1 < N_KV_VARIANTS:
                    nxt = logits(v + 1)
                p_c = jnp.exp2(s_c)
                p_b = jnp.where(in_band2, jnp.exp2(s_b), 0.0)
                hq = j * SWA_GROUP + var
                sink = jnp.where(top, sink_ref[hq] * LOG2E, sink_ref[hq + 2] * LOG2E)
                denom = (jnp.sum(p_c, axis=-1, keepdims=True) + jnp.sum(p_b, axis=-1, keepdims=True)
                         + jnp.exp2(sink))
                o = (_dot(p_c.astype(BF16), vcv_ref[v])
                     + _dot(p_b.astype(BF16), vlv_ref[v, pl.ds(start, SWA_BAND), :]))
                out = out + o / denom
            o_ref[:, lo:lo + SWA_PAIR] = out[:SWA_TQ].astype(BF16)
            o_ref[:, lo + SWA_PAIR:lo + 2 * SWA_PAIR] = out[SWA_TQ:].astype(BF16)

    def shifted():
        kb = kl_ref[pl.ds(start, SWA_BAND), :]
        vb = vl_ref[pl.ds(start, SWA_BAND), :]
        kc = kc_ref[...]
        vc = vc_ref[...]
        for j in range(SWA_KV_HEADS):
            kj = slice(j * HEAD_DIM, (j + 1) * HEAD_DIM)
            kbj, vbj, kcj, vcj = kb[:, kj], vb[:, kj], kc[:, kj], vc[:, kj]
            outs = []
            for g in range(SWA_GROUP):
                hq = j * SWA_GROUP + g
                qh = q_ref[:, hq * HEAD_DIM:(hq + 1) * HEAD_DIM]
                s_c = lax.dot_general(qh, kcj, nt, preferred_element_type=F32)
                s_b = lax.dot_general(qh, kbj, nt, preferred_element_type=F32)
                s_b = jnp.where(in_band, s_b, -jnp.inf)
                sink = sink_ref[hq] * LOG2E
                mx = jnp.maximum(jnp.maximum(jnp.max(s_c, axis=-1, keepdims=True),
                                             jnp.max(s_b, axis=-1, keepdims=True)), sink)
                p_c = jnp.exp2(s_c - mx)
                p_b = jnp.exp2(s_b - mx)
                denom = (jnp.sum(p_c, axis=-1, keepdims=True) + jnp.sum(p_b, axis=-1, keepdims=True)
                         + jnp.exp2(sink - mx))
                o = _dot(p_c.astype(BF16), vcj) + _dot(p_b.astype(BF16), vbj)
                outs.append(o / denom)
            lo = j * SWA_GROUP * HEAD_DIM
            o_ref[:, lo:lo + SWA_GROUP * HEAD_DIM] = jnp.concatenate(outs, axis=-1).astype(BF16)

    pl.when(small_logits)(unshifted)
    pl.when(jnp.logical_not(small_logits))(shifted)


def _swa(rows, qkv, sink, small_logits):
    batch, seq, ctx_len = rows.batch, rows.seq, rows.ctx_len
    nq = seq // SWA_TQ
    kvw = SWA_KV_HEADS * HEAD_DIM
    k_col = D // kvw
    ctx0 = rows.n_lat // ctx_len
    return pl.pallas_call(
        functools.partial(_swa_kernel, seq),
        grid=(batch, nq),
        in_specs=[pl.BlockSpec(memory_space=pltpu.SMEM), pl.BlockSpec(memory_space=pltpu.SMEM),
                  pl.BlockSpec((SWA_TQ, D), lambda b, i: (b * nq + i, 0)),
                  pl.BlockSpec((ctx_len, kvw), lambda b, i: (ctx0 + b, k_col)),
                  pl.BlockSpec((ctx_len, kvw), lambda b, i: (ctx0 + b, k_col + 1)),
                  pl.BlockSpec((seq, kvw), lambda b, i: (b, k_col)),
                  pl.BlockSpec((seq, kvw), lambda b, i: (b, k_col + 1))],
        out_specs=pl.BlockSpec((SWA_TQ, D), lambda b, i: (b * nq + i, 0)),
        out_shape=jax.ShapeDtypeStruct((rows.n_lat, D), BF16),
        scratch_shapes=[pltpu.VMEM((N_KV_VARIANTS, ctx_len, SWA_PAIR), BF16),
                        pltpu.VMEM((N_KV_VARIANTS, ctx_len, SWA_PAIR), BF16),
                        pltpu.VMEM((N_KV_VARIANTS, seq, SWA_PAIR), BF16),
                        pltpu.VMEM((N_KV_VARIANTS, seq, SWA_PAIR), BF16)],
        compiler_params=_params(2),
        name="swa_attn",
    )(sink, small_logits, qkv, qkv, qkv, qkv, qkv)


def _proj_route_kernel(lat_tiles, h_ref, mod_ref, zl_ref, zc_ref, w_ref, wr_ref, br_ref, hm_ref, a_ref, route_ref,
                       cnt_ref):
    i = pl.program_id(0)
    m = mod_ref[0]

    def project(z_ref):
        hm_ref[...] = h_ref[...] + m[2:3] * _dot(z_ref[...], w_ref[...])

    pl.when(i < lat_tiles)(lambda: project(zl_ref))
    pl.when(i >= lat_tiles)(lambda: project(zc_ref))
    a = _modnorm(hm_ref[...], *_ffn_mod(m))
    a_hi = a.astype(BF16)
    a_ref[...] = a_hi
    a_lo = (a - a_hi.astype(F32)).astype(BF16)
    hi_part = _dot(a_hi, wr_ref[...])
    logits = (hi_part[:, :ROUTE_W] + hi_part[:, ROUTE_W:]) + _dot(a_lo, wr_ref[:, :ROUTE_W]) + br_ref[...]
    lane = lax.broadcasted_iota(jnp.int32, (TM, ROUTE_W), 1)
    m1 = jnp.max(logits, axis=-1, keepdims=True)
    i1 = jnp.min(jnp.where(logits == m1, lane, ROUTE_W), axis=-1, keepdims=True)
    rest = jnp.where(lane == i1, -jnp.inf, logits)
    m2 = jnp.max(rest, axis=-1, keepdims=True)
    i2 = jnp.min(jnp.where(rest == m2, lane, ROUTE_W), axis=-1, keepdims=True)
    e2 = jnp.exp(m2 - m1)
    gate1 = 1.0 / (1.0 + e2)
    gate2 = e2 / (1.0 + e2)
    rec = jnp.where(lane == 0, i1.astype(F32), 0.0)
    rec = jnp.where(lane == 1, i2.astype(F32), rec)
    rec = jnp.where(lane == 2, gate1, rec)
    route_ref[...] = jnp.where(lane == 3, gate2, rec)
    chosen = jnp.where((lane == i1) | (lane == i2), 1.0, 0.0)
    cnt_ref[0] = jnp.broadcast_to(jnp.sum(chosen, axis=0, keepdims=True), (8, ROUTE_W))


def _proj_route(rows, n_tiles, h, modg, z_lat, z_ctx, w_out, w_r, b_r):
    lat_tiles = rows.lat_tiles
    pad = ROUTE_W - N_EXPERTS
    w_pad = jnp.pad(w_r, ((0, 0), (0, pad)))
    w_hi = w_pad.astype(BF16)
    w_lo = (w_pad - w_hi.astype(F32)).astype(BF16)
    w_split = jnp.concatenate([w_hi, w_lo], axis=1)
    b_pad = jnp.concatenate([b_r.astype(F32), jnp.full((pad,), -1e30, F32)]).reshape(1, ROUTE_W)
    return pl.pallas_call(
        functools.partial(_proj_route_kernel, lat_tiles),
        grid=(n_tiles,),
        in_specs=[rows.row_spec(D), rows.mod_spec(),
                  pl.BlockSpec((TM, D), lambda i: (jnp.minimum(i, lat_tiles - 1), 0)),
                  pl.BlockSpec((TM, D), lambda i: (jnp.maximum(i - lat_tiles, 0), 0)),
                  _resident((D, D)), _resident((D, 2 * ROUTE_W)), _resident((1, ROUTE_W))],
        out_specs=[rows.row_spec(D), rows.row_spec(D), rows.row_spec(ROUTE_W),
                   pl.BlockSpec((1, 8, ROUTE_W), lambda i: (i, 0, 0))],
        out_shape=[jax.ShapeDtypeStruct((n_tiles * TM, D), F32), jax.ShapeDtypeStruct((n_tiles * TM, D), BF16),
                   jax.ShapeDtypeStruct((n_tiles * TM, ROUTE_W), F32),
                   jax.ShapeDtypeStruct((n_tiles, 8, ROUTE_W), F32)],
        compiler_params=_params(1),
        name="out_proj_router",
    )(h, modg, z_lat, z_ctx, w_out, w_split, b_pad)


SEG_ALIGN = 16
SEG_BITS = tuple(1 << b for b in range(TM.bit_length() - 1, SEG_ALIGN.bit_length() - 2, -1))
LROWS = -(-(2 * TM + N_EXPERTS * (SEG_ALIGN - 1)) // 128) * 128
XS_W = D + 128
META_W = 3 * N_EXPERTS


def _segment_copies(meta_ref, tile, local_ref, slots_hbm, sem, to_slots):
    for e in range(N_EXPERTS):
        loff = meta_ref[tile * META_W + e]
        goff = meta_ref[tile * META_W + N_EXPERTS + e]
        seg = meta_ref[tile * META_W + 2 * N_EXPERTS + e]
        for bit in SEG_BITS:
            done = seg & ~(2 * bit - 1)
            lo = local_ref.at[pl.ds(pl.multiple_of(loff + done, SEG_ALIGN), bit)]
            gl = slots_hbm.at[pl.ds(pl.multiple_of(goff + done, SEG_ALIGN), bit)]
            copy = pltpu.make_async_copy(lo, gl, sem) if to_slots else pltpu.make_async_copy(gl, lo, sem)
            yield (seg & bit) != 0, copy


def _start_segment_copies(*args):
    for cond, copy in _segment_copies(*args):
        pl.when(cond)(copy.start)


def _wait_segment_copies(*args):
    for cond, copy in _segment_copies(*args):
        pl.when(cond)(copy.wait)


def _split3(x):
    hi = x.astype(BF16).astype(F32)
    mid = (x - hi).astype(BF16).astype(F32)
    lo = ((x - hi) - mid).astype(BF16).astype(F32)
    return hi, mid, lo


def _dispatch_kernel(n_blocks, meta_ref, fill_ref, a_ref, route_ref, tri_ref, xs_hbm, slot_ref, local_ref, zero_ref,
                     sem, zsem):
    i = pl.program_id(0)

    @pl.when(i == 0)
    def _zero_unwritten_slots():
        zero_ref[...] = jnp.zeros(zero_ref.shape, BF16)
        tails = []
        for e in range(N_EXPERTS):
            off, length = fill_ref[e], fill_ref[N_EXPERTS + e]
            for bit in (b for b in SEG_BITS if b < MOE_RB):
                done = length & ~(2 * bit - 1)
                dst = xs_hbm.at[pl.ds(pl.multiple_of(off + done, SEG_ALIGN), bit)]
                tails.append(((length & bit) != 0, pltpu.make_async_copy(zero_ref.at[pl.ds(0, bit)], dst, zsem)))
        for cond, copy in tails:
            pl.when(cond)(copy.start)

        def block_copy(blk):
            dst = xs_hbm.at[pl.ds(pl.multiple_of(blk * MOE_RB, MOE_RB), MOE_RB)]
            return pltpu.make_async_copy(zero_ref, dst, zsem)

        n_used = fill_ref[2 * N_EXPERTS]
        lax.fori_loop(n_used, n_blocks, lambda blk, c: (block_copy(blk).start(), c)[1], 0)
        for cond, copy in tails:
            pl.when(cond)(copy.wait)
        lax.fori_loop(n_used, n_blocks, lambda blk, c: (block_copy(blk).wait(), c)[1], 0)

    route = route_ref[...]
    lane = lax.broadcasted_iota(jnp.int32, (TM, ROUTE_W), 1)
    pick1 = lane == route[:, 0:1].astype(jnp.int32)
    pick2 = lane == route[:, 1:2].astype(jnp.int32)
    chosen = jnp.where(pick1 | pick2, 1.0, 0.0).astype(BF16)
    rank = _dot(tri_ref[...], chosen)
    lane1 = lax.broadcasted_iota(jnp.int32, (1, ROUTE_W), 1)
    loff = jnp.zeros((1, ROUTE_W), F32)
    for e in range(N_EXPERTS):
        loff = jnp.where(lane1 == e, meta_ref[i * META_W + e].astype(F32), loff)
    place = rank + loff
    slot1 = jnp.sum(jnp.where(pick1, place, 0.0), axis=-1, keepdims=True)
    slot2 = jnp.sum(jnp.where(pick2, place, 0.0), axis=-1, keepdims=True)
    slot_ref[...] = jnp.where(lane == 0, slot1, jnp.where(lane == 1, slot2, 0.0))

    row1 = jnp.transpose(jnp.broadcast_to(slot1, (TM, 128)))[0:1, :]
    row2 = jnp.transpose(jnp.broadcast_to(slot2, (TM, 128)))[0:1, :]
    lrow = lax.broadcasted_iota(jnp.int32, (LROWS, TM), 0).astype(F32)
    p1 = jnp.where(lrow == row1, 1.0, 0.0).astype(BF16)
    p2 = jnp.where(lrow == row2, 1.0, 0.0).astype(BF16)
    buf = i % 2
    local_ref.at[buf][:, :D] = _dot(p1 + p2, a_ref[...]).astype(BF16)
    gates = []
    for col in (2, 3):
        pieces = _split3(route[:, col:col + 1])
        g = jnp.zeros((TM, ROUTE_W), F32)
        for k, piece in enumerate(pieces):
            g = jnp.where(lane == k, piece, g)
        gates.append(g.astype(BF16))
    local_ref.at[buf][:, D:] = (_dot(p1, gates[0]) + _dot(p2, gates[1])).astype(BF16)

    def copies(tile, b):
        return (meta_ref, tile, local_ref.at[b], xs_hbm, sem.at[b], True)

    _start_segment_copies(*copies(i, buf))
    pl.when(i > 0)(lambda: _wait_segment_copies(*copies(i - 1, 1 - buf)))
    pl.when(i == pl.num_programs(0) - 1)(lambda: _wait_segment_copies(*copies(i, buf)))


def _dispatch(n_tiles, n_blocks, a, route, meta, fill):
    tri = (jnp.arange(TM)[:, None] > jnp.arange(TM)[None, :]).astype(BF16)
    return pl.pallas_call(
        functools.partial(_dispatch_kernel, n_blocks),
        grid_spec=pltpu.PrefetchScalarGridSpec(
            num_scalar_prefetch=2,
            grid=(n_tiles,),
            in_specs=[pl.BlockSpec((TM, D), lambda i, *_: (i, 0)),
                      pl.BlockSpec((TM, ROUTE_W), lambda i, *_: (i, 0)),
                      _resident((TM, TM))],
            out_specs=[pl.BlockSpec(memory_space=pl.ANY), pl.BlockSpec((TM, ROUTE_W), lambda i, *_: (i, 0))],
            scratch_shapes=[pltpu.VMEM((2, LROWS, XS_W), BF16), pltpu.VMEM((MOE_RB, XS_W), BF16),
                            pltpu.SemaphoreType.DMA((2,)), pltpu.SemaphoreType.DMA(())],
        ),
        out_shape=[jax.ShapeDtypeStruct((n_blocks * MOE_RB, XS_W), BF16),
                   jax.ShapeDtypeStruct((n_tiles * TM, ROUTE_W), F32)],
        compiler_params=_params(1),
        name="moe_dispatch",
    )(meta, fill, a, route, tri)


def _expert_kernel(be_ref, fill_ref, xs_ref, w1_ref, w3_ref, w2_ref, ys_ref, t_ref):
    del be_ref
    used = pl.program_id(0) < fill_ref[2 * N_EXPERTS]

    @pl.when(used)
    def _ffn():
        y = _swiglu(xs_ref[:, :D], w1_ref, w3_ref, w2_ref, t_ref)
        gate = jnp.sum(xs_ref[:, D:].astype(F32), axis=-1, keepdims=True)
        ys_ref[...] = (gate * y).astype(BF16)

    @pl.when(jnp.logical_not(used))
    def _idle():
        ys_ref[...] = jnp.zeros(ys_ref.shape, BF16)


def _experts(xs, blk_exp, fill, layer, w1, w3, w2):
    def slab(d0, d1):
        return pl.BlockSpec((None, None, d0, d1), lambda i, be, fill: (layer, be[i], 0, 0),
                            pipeline_mode=pl.Buffered(1))

    n_blocks = blk_exp.shape[0]
    last_used = lambda i, fill: jnp.minimum(i, fill[2 * N_EXPERTS] - 1)
    return pl.pallas_call(
        _expert_kernel,
        grid_spec=pltpu.PrefetchScalarGridSpec(
            num_scalar_prefetch=2,
            grid=(n_blocks,),
            in_specs=[pl.BlockSpec((MOE_RB, XS_W), lambda i, be, fill: (last_used(i, fill), 0)),
                      slab(D, FFN_DIM), slab(D, FFN_DIM), slab(FFN_DIM, D)],
            out_specs=pl.BlockSpec((MOE_RB, D), lambda i, be, fill: (i, 0)),
            scratch_shapes=[pltpu.VMEM((MOE_RB, FFN_DIM), BF16)],
        ),
        out_shape=jax.ShapeDtypeStruct((n_blocks * MOE_RB, D), BF16),
        compiler_params=_params(1),
        name="moe_experts",
    )(blk_exp, fill, xs, w1, w3, w2)


def _combine_scratch():
    return [pltpu.VMEM((2, LROWS, D), BF16), pltpu.SemaphoreType.DMA((2,))]


def _combined_rows(meta_ref, h_ref, mod_ref, slot_ref, ys_hbm, local_ref, sem):
    i = pl.program_id(0)

    buf = i % 2

    def fetch(tile, b):
        return (meta_ref, tile, local_ref.at[b], ys_hbm, sem.at[b], False)

    @pl.when(i == 0)
    def _init():
        local_ref[...] = jnp.zeros(local_ref.shape, BF16)
        _start_segment_copies(*fetch(0, 0))

    @pl.when(i + 1 < pl.num_programs(0))
    def _prefetch():
        _start_segment_copies(*fetch(i + 1, 1 - buf))

    slots = slot_ref[...]
    lcol = lax.broadcasted_iota(jnp.int32, (TM, LROWS), 1).astype(F32)
    gather = jnp.where((lcol == slots[:, 0:1]) | (lcol == slots[:, 1:2]), 1.0, 0.0).astype(BF16)
    _wait_segment_copies(*fetch(i, buf))
    return h_ref[...] + mod_ref[0][5:6] * _dot(gather, local_ref[buf])


def _combine_kernel(meta_ref, h_ref, mod_ref, slot_ref, ys_hbm, o_ref, local_ref, sem):
    o_ref[...] = _combined_rows(meta_ref, h_ref, mod_ref, slot_ref, ys_hbm, local_ref, sem)


def _combine(rows, n_tiles, h, modg, slots, ys, meta):
    tps, batch = rows.tiles_per_seq, rows.batch
    return pl.pallas_call(
        _combine_kernel,
        grid_spec=pltpu.PrefetchScalarGridSpec(
            num_scalar_prefetch=1,
            grid=(n_tiles,),
            in_specs=[pl.BlockSpec((TM, D), lambda i, m: (i, 0)),
                      pl.BlockSpec((1, 8, D), lambda i, m: (jnp.minimum(i // tps, batch), 0, 0)),
                      pl.BlockSpec((TM, ROUTE_W), lambda i, m: (i, 0)),
                      pl.BlockSpec(memory_space=pl.ANY)],
            out_specs=pl.BlockSpec((TM, D), lambda i, m: (i, 0)),
            scratch_shapes=_combine_scratch(),
        ),
        out_shape=jax.ShapeDtypeStruct((n_tiles * TM, D), F32),
        compiler_params=_params(1),
        name="moe_combine",
    )(meta, h, modg, slots, ys)


def _mixer_out_moe(rows, n_tiles, h, modg, z_lat, z_ctx, w_out, w_r, b_r, layer, w1, w3, w2, defer_combine):
    h, a, route, cnt = _proj_route(rows, n_tiles, h, modg, z_lat, z_ctx, w_out, w_r, b_r)
    cnt = cnt[:, 0, :N_EXPERTS].astype(jnp.int32)
    seg = (cnt + SEG_ALIGN - 1) // SEG_ALIGN * SEG_ALIGN
    loff = jnp.cumsum(seg, axis=1) - seg
    total = jnp.sum(seg, axis=0)
    region = (total + MOE_RB - 1) // MOE_RB * MOE_RB
    region_end = jnp.cumsum(region)
    region_start = region_end - region
    goff = region_start[None, :] + jnp.cumsum(seg, axis=0) - seg
    meta = jnp.concatenate([loff, goff, seg], axis=1).reshape(-1)
    n_blocks = pl.cdiv(2 * n_tiles * TM + n_tiles * N_EXPERTS * (SEG_ALIGN - 1), MOE_RB) + N_EXPERTS
    n_used = region_end[-1] // MOE_RB
    fill = jnp.concatenate([region_start + total, region - total, n_used[None]]).astype(jnp.int32)
    blk = jnp.minimum(jnp.arange(n_blocks, dtype=jnp.int32), n_used - 1)
    blk_exp = jnp.sum((region_end[None, :] <= (blk * MOE_RB)[:, None]).astype(jnp.int32), axis=1)
    blk_exp = jnp.minimum(blk_exp, N_EXPERTS - 1)
    xs, slots = _dispatch(n_tiles, n_blocks, a, route, meta, fill)
    ys = _experts(xs, blk_exp, fill, layer, w1, w3, w2)
    if defer_combine:
        return h, modg, slots, ys, meta
    return _combine(rows, n_tiles, h, modg, slots, ys, meta)


def kernel(x, c, ctx, c_ctx, ada_w, ada_b, norm_mix_g, norm_ffn_g, sc_in_w, sc_conv_w, sc_out_w, da_qkv_w, da_out_w,
           da_q_norm_g, da_k_norm_g, da_lambda, da_sub_norm_g, cm_in_w, cm_in_b, cm_v_norm_g, cm_ws, cm_bs, cm_out_w,
           sw_qkv_w, sw_out_w, sw_q_norm_g, sw_k_norm_g, sw_sink, ffn_w1, ffn_w3, ffn_w2, moe_router_w, moe_router_b,
           moe_w1, moe_w3, moe_w2):
    batch, seq, _ = x.shape
    ctx_len = ctx.shape[1]
    depth = ada_w.shape[0]
    assert depth == 4 and batch + 1 <= 16
    rows = _Rows(batch, seq, ctx_len)

    cvec = jnp.concatenate([c, c_ctx[None, :], jnp.zeros((16 - batch - 1, D), F32)], axis=0)
    mod = _ada_all(cvec, ada_w, ada_b)[:, :batch + 1].reshape(depth, batch + 1, 6, D)
    gains = jnp.stack([norm_mix_g, norm_ffn_g], axis=1)[:, None]
    modg = jnp.concatenate([mod, jnp.broadcast_to(gains, (depth, batch + 1, 2, D))], axis=2)

    bf = lambda w: w.astype(BF16)
    x2, ctx2 = x.reshape(-1, D), ctx.reshape(-1, D)

    bg, y = _conv_in(rows, x2, ctx2, modg[0], bf(sc_in_w[0]))
    conv_w = jnp.pad(sc_conv_w[0], ((0, 5), (0, 0)))
    h = _conv_out(rows, x2, ctx2, modg[0], bg, y, conv_w, bf(sc_out_w[0]))
    h = _ffn(rows, rows.all_tiles, h, modg[0], 0, ffn_w1, ffn_w3, ffn_w2)

    nq_chunks = D // CN
    qkv = _qkv(rows, h, modg[1], bf(da_qkv_w[0]), da_q_norm_g[0], da_k_norm_g[0], nq_chunks, nq_chunks)
    lam_init = 0.8 - 0.6 * math.exp(-0.3 * 1)
    o_lat, o_ctx = _diff_attn(rows, qkv, da_lambda[0], da_sub_norm_g[0], lam_init,
                              _small_logits(da_q_norm_g[0], da_k_norm_g[0]))
    pending = _mixer_out_moe(rows, rows.all_tiles, h, modg[1], o_lat, o_ctx, bf(da_out_w[0]), moe_router_w[0],
                             moe_router_b[0], 0, moe_w1, moe_w3, moe_w2, defer_combine=True)

    h = _gmlp(rows, pending, modg[2], bf(cm_in_w[0]), cm_in_b[0].reshape(1, -1), cm_v_norm_g[0].reshape(1, -1),
              bf(cm_ws[0]), cm_bs[0].T, bf(cm_out_w[0]))
    h = _ffn(rows, rows.all_tiles, h, modg[2], 1, ffn_w1, ffn_w3, ffn_w2)

    kv_chunks = SWA_KV_HEADS * HEAD_DIM // CN
    qkv = _qkv(rows, h, modg[3], bf(sw_qkv_w[0]), sw_q_norm_g[0], sw_k_norm_g[0], nq_chunks, kv_chunks)
    o = _swa(rows, qkv, sw_sink[0], _small_logits(sw_q_norm_g[0], sw_k_norm_g[0], sw_sink[0]))
    h = _mixer_out_moe(rows, rows.lat_tiles, h, modg[3], o, o, bf(sw_out_w[0]), moe_router_w[1], moe_router_b[1], 1,
                       moe_w1, moe_w3, moe_w2, defer_combine=False)
    return h.reshape(batch, seq, D)
```
